```python
import math
import jax, jax.numpy as jnp
from jax import lax
import numpy as np

D_MODEL = 2048
BATCH = 4
SEQ = 2048
DEPTH = 4

N_MIXERS = 4
HEAD_DIM = 128
N_HEADS = 16
N_KV_HEADS = 4
GQA_GROUP = N_HEADS // N_KV_HEADS
GQA_IN = (N_HEADS + 2 * N_KV_HEADS) * HEAD_DIM
Q_BLOCK = 128
WINDOW = 128
ROPE_THETA = 500000.0
ROT_DIM = HEAD_DIM // 4
AXIAL_THETA = 10000.0
GRID_W = 64
MLA_Q_RANK = 512
MLA_KV_RANK = 512
MLA_NOPE_DIM = 128
MLA_ROPE_DIM = 64
MLA_V_DIM = 128
MLA_QK_DIM = MLA_NOPE_DIM + MLA_ROPE_DIM
MLA_IN = MLA_Q_RANK + MLA_KV_RANK + MLA_ROPE_DIM
NA_ROWS = 8
NA_COLS = 16
NA_QCOLS = 16
NA_KCOLS = 2 * NA_COLS
MEM_TOKENS = 256
XA_HEADS = 4
XA_HEAD_DIM = 128
N_EXPERTS = 16
EC_CAPACITY = 2
EXPERT_FF = 1024
EPS = 1e-6

kernel_name = "hybrid_interleaved_encoder_ec_moe"


def _n_layers_of(m):
    return len(range(m, DEPTH, N_MIXERS))


def _rms_norm(x, g):
    xf = x.astype(jnp.float32)
    y = xf * lax.rsqrt(jnp.mean(xf * xf, axis=-1, keepdims=True) + EPS)
    return (y * g.astype(jnp.float32)).astype(x.dtype)


def _rope_table(pos, dim, theta):
    inv = jnp.power(jnp.float32(theta), -(jnp.arange(0, dim, 2, dtype=jnp.float32) / dim))
    ang = pos.astype(jnp.float32)[:, None] * inv[None, :]
    return jnp.cos(ang), jnp.sin(ang)


def _rope(x, cos, sin):
    xf = x.astype(jnp.float32)
    half = x.shape[-1] // 2
    c, s = cos[None, :, None, :], sin[None, :, None, :]
    x1, x2 = xf[..., :half], xf[..., half:]
    return jnp.concatenate([x1 * c - x2 * s, x2 * c + x1 * s], axis=-1).astype(x.dtype)


def _partial_rope(x, cos, sin):
    return jnp.concatenate([_rope(x[..., :ROT_DIM], cos, sin), x[..., ROT_DIM:]], axis=-1)


def _split_gqa(xn, w_in):
    B, S, _ = xn.shape
    qkv = xn @ w_in
    nq, nk = N_HEADS * HEAD_DIM, N_KV_HEADS * HEAD_DIM
    q = qkv[..., :nq].reshape(B, S, N_HEADS, HEAD_DIM)
    k = qkv[..., nq:nq + nk].reshape(B, S, N_KV_HEADS, HEAD_DIM)
    v = qkv[..., nq + nk:].reshape(B, S, N_KV_HEADS, HEAD_DIM)
    return q, k, v


def _dense_blocked_attention(q, k, v):
    B, S, KV, G, dk = q.shape
    nb = S // Q_BLOCK
    scale = dk ** -0.5
    qb = q.reshape(B, nb, Q_BLOCK, KV, G, dk).transpose(1, 0, 2, 3, 4, 5)

    def one_block(qi):
        s = jnp.einsum('bqkgd,bskd->bkgqs', qi, k).astype(jnp.float32) * scale
        p = jax.nn.softmax(s, axis=-1).astype(v.dtype)
        return jnp.einsum('bkgqs,bskv->bqkgv', p, v)

    o = lax.map(one_block, qb)
    return o.transpose(1, 0, 2, 3, 4, 5).reshape(B, S, KV * G * v.shape[-1])


def _mixer_window_sink(xn, w_in, sink, w_out):
    B, S, _ = xn.shape
    q, k, v = _split_gqa(xn, w_in)
    cos, sin = _rope_table(jnp.arange(S), ROT_DIM, ROPE_THETA)
    q, k = _partial_rope(q, cos, sin), _partial_rope(k, cos, sin)
    nb = S // Q_BLOCK

    def bands(t):
        tp = jnp.pad(t, ((0, 0), (Q_BLOCK, Q_BLOCK), (0, 0), (0, 0)))
        tp = tp.reshape(B, nb + 2, Q_BLOCK, N_KV_HEADS, HEAD_DIM)
        return jnp.concatenate([tp[:, :-2], tp[:, 1:-1], tp[:, 2:]], axis=2)

    kb, vb = bands(k), bands(v)
    qb = q.reshape(B, nb, Q_BLOCK, N_KV_HEADS, GQA_GROUP, HEAD_DIM)
    s = jnp.einsum('bnqkgd,bnskd->bnkgqs', qb, kb).astype(jnp.float32) * HEAD_DIM ** -0.5
    qpos = jnp.arange(nb)[:, None] * Q_BLOCK + jnp.arange(Q_BLOCK)[None, :]
    kpos = jnp.arange(nb)[:, None] * Q_BLOCK - Q_BLOCK + jnp.arange(3 * Q_BLOCK)[None, :]
    kp = kpos[:, None, :]
    valid = (kp >= 0) & (kp < S) & (jnp.abs(qpos[:, :, None] - kp) <= WINDOW)
    s = jnp.where(valid[None, :, None, None], s, -jnp.inf)
    sk = sink.astype(jnp.float32).reshape(N_KV_HEADS, GQA_GROUP)[None, None, :, :, None, None]
    m = jnp.maximum(jnp.max(s, axis=-1, keepdims=True), sk)
    e = jnp.exp(s - m)
    p = e / (jnp.sum(e, axis=-1, keepdims=True) + jnp.exp(sk - m))
    o = jnp.einsum('bnkgqs,bnskd->bnqkgd', p.astype(vb.dtype), vb)
    return o.reshape(B, S, N_HEADS * HEAD_DIM) @ w_out


def _mixer_axial(xn, w_in, q_norm_g, k_norm_g, w_out):
    B, S, _ = xn.shape
    q, k, v = _split_gqa(xn, w_in)
    q, k = _rms_norm(q, q_norm_g), _rms_norm(k, k_norm_g)
    t = jnp.arange(S)
    half = HEAD_DIM // 2
    cr, sr = _rope_table(t // GRID_W, half, AXIAL_THETA)
    cc, sc = _rope_table(t % GRID_W, half, AXIAL_THETA)

    def axial(u):
        return jnp.concatenate([_rope(u[..., :half], cr, sr), _rope(u[..., half:], cc, sc)], axis=-1)

    q = axial(q).reshape(B, S, N_KV_HEADS, GQA_GROUP, HEAD_DIM)
    o = _dense_blocked_attention(q, axial(k), v)
    return o @ w_out


def _mixer_mla(xn, w_in, q_lat_g, kv_lat_g, w_uq, w_ukv, w_out):
    B, S, _ = xn.shape
    lat = xn @ w_in
    c_q = _rms_norm(lat[..., :MLA_Q_RANK], q_lat_g)
    c_kv = _rms_norm(lat[..., MLA_Q_RANK:MLA_Q_RANK + MLA_KV_RANK], kv_lat_g)
    k_rope = lat[..., MLA_Q_RANK + MLA_KV_RANK:].reshape(B, S, 1, MLA_ROPE_DIM)
    q = (c_q @ w_uq).reshape(B, S, N_HEADS, MLA_QK_DIM)
    kv = (c_kv @ w_ukv).reshape(B, S, N_HEADS, MLA_NOPE_DIM + MLA_V_DIM)
    cos, sin = _rope_table(jnp.arange(S), MLA_ROPE_DIM, ROPE_THETA)
    q = jnp.concatenate([q[..., :MLA_NOPE_DIM], _rope(q[..., MLA_NOPE_DIM:], cos, sin)], axis=-1)
    k_rope = jnp.broadcast_to(_rope(k_rope, cos, sin), (B, S, N_HEADS, MLA_ROPE_DIM))
    k = jnp.concatenate([kv[..., :MLA_NOPE_DIM], k_rope], axis=-1)
    v = kv[..., MLA_NOPE_DIM:]
    o = _dense_blocked_attention(q.reshape(B, S, N_HEADS, 1, MLA_QK_DIM), k, v)
    return o @ w_out


def _mixer_neighbourhood(xn, w_in, rpb, w_out):
    B, S, _ = xn.shape
    q, k, v = _split_gqa(xn, w_in)
    rows = S // GRID_W
    kr = min(NA_ROWS, rows)
    ncb = GRID_W // NA_QCOLS
    r = jnp.arange(rows)
    row_idx = jnp.clip(r - NA_ROWS // 2, 0, rows - kr)[:, None] + jnp.arange(kr)[None, :]
    qcol = jnp.arange(GRID_W).reshape(ncb, NA_QCOLS)
    c_start = jnp.clip(qcol - NA_COLS // 2, 0, GRID_W - NA_COLS)
    kcol = (jnp.clip(jnp.arange(ncb) * NA_QCOLS - NA_COLS // 2, 0, GRID_W - NA_KCOLS)[:, None]
            + jnp.arange(NA_KCOLS)[None, :])

    def gather(t):
        tg = t.reshape(B, rows, GRID_W, N_KV_HEADS, HEAD_DIM)[:, row_idx[:, None, :, None], kcol[None, :, None, :]]
        return tg.reshape(B, rows, ncb, kr * NA_KCOLS, N_KV_HEADS, HEAD_DIM)

    kg, vg = gather(k), gather(v)
    qg = q.reshape(B, rows, ncb, NA_QCOLS, N_KV_HEADS, GQA_GROUP, HEAD_DIM)
    s = jnp.einsum('brcqkgd,brcskd->brckgqs', qg, kg).astype(jnp.float32) * HEAD_DIM ** -0.5
    dr = row_idx - r[:, None] + NA_ROWS - 1
    dc = jnp.clip(kcol[:, None, :] - qcol[:, :, None] + NA_COLS - 1, 0, 2 * NA_COLS - 2)
    bias = rpb.astype(jnp.float32)[:, dr[:, None, None, :, None], dc[None, :, :, None, :]]
    bias = bias.reshape(N_KV_HEADS, GQA_GROUP, rows, ncb, NA_QCOLS, kr * NA_KCOLS).transpose(2, 3, 0, 1, 4, 5)
    in_win = (kcol[:, None, :] >= c_start[:, :, None]) & (kcol[:, None, :] < c_start[:, :, None] + NA_COLS)
    in_win = jnp.broadcast_to(in_win[:, :, None, :], (ncb, NA_QCOLS, kr, NA_KCOLS)).reshape(ncb, NA_QCOLS, kr * NA_KCOLS)
    s = jnp.where(in_win[None, None, :, None, None], s + bias[None], -jnp.inf)
    p = jax.nn.softmax(s, axis=-1).astype(vg.dtype)
    o = jnp.einsum('brckgqs,brcskd->brcqkgd', p, vg)
    return o.reshape(B, S, N_HEADS * HEAD_DIM) @ w_out


def _memory_cross_attention(xn, mem_n, wq, wkv, wo):
    B, S, _ = xn.shape
    M = mem_n.shape[1]
    inner = XA_HEADS * XA_HEAD_DIM
    q = (xn @ wq).reshape(B, S, XA_HEADS, XA_HEAD_DIM)
    kv = mem_n @ wkv
    k = kv[..., :inner].reshape(B, M, XA_HEADS, XA_HEAD_DIM)
    v = kv[..., inner:].reshape(B, M, XA_HEADS, XA_HEAD_DIM)
    s = jnp.einsum('bshd,bmhd->bhsm', q, k).astype(jnp.float32) * XA_HEAD_DIM ** -0.5
    p = jax.nn.softmax(s, axis=-1).astype(v.dtype)
    o = jnp.einsum('bhsm,bmhd->bshd', p, v).reshape(B, S, inner)
    return o @ wo


def _expert_choice_ffn(xn, router_w, w_gate, w_up, w_down):
    B, S, _ = xn.shape
    cap = EC_CAPACITY * S // N_EXPERTS
    aff = jax.nn.softmax((xn @ router_w).astype(jnp.float32), axis=-1)
    gate, idx = lax.top_k(aff.transpose(0, 2, 1), cap)
    bidx = jnp.arange(B)[:, None, None]
    xg = xn[bidx, idx]
    h = jax.nn.silu(jnp.einsum('becd,edf->becf', xg, w_gate)) * jnp.einsum('becd,edf->becf', xg, w_up)
    y = jnp.einsum('becf,efd->becd', h, w_down) * gate[..., None].astype(xn.dtype)
    return jnp.zeros_like(xn).at[bidx, idx].add(y)


def setup_inputs(seed: int = 0) -> dict:
    key = jax.random.key(seed)
    ks = iter(jax.random.split(key, 40))

    def dense(shape, fan_in):
        return jax.random.normal(next(ks), shape, jnp.float32) * fan_in ** -0.5

    def gain(shape):
        return 1.0 + 0.05 * jax.random.normal(next(ks), shape, jnp.float32)

    nA, nB, nC, nD = (_n_layers_of(m) for m in range(N_MIXERS))
    D = D_MODEL
    return {
        'x': jax.random.normal(next(ks), (BATCH, SEQ, D), jnp.float32),
        'mem': jax.random.normal(next(ks), (BATCH, MEM_TOKENS, D), jnp.float32),
        'norm_mix_g': gain((DEPTH, D)),
        'norm_xa_g': gain((DEPTH, D)),
        'norm_ffn_g': gain((DEPTH, D)),
        'a_w_in': dense((nA, D, GQA_IN), D),
        'a_sink': 0.5 * jax.random.normal(next(ks), (nA, N_HEADS), jnp.float32),
        'a_w_out': dense((nA, N_HEADS * HEAD_DIM, D), N_HEADS * HEAD_DIM),
        'b_w_in': dense((nB, D, GQA_IN), D),
        'b_q_norm_g': gain((nB, HEAD_DIM)),
        'b_k_norm_g': gain((nB, HEAD_DIM)),
        'b_w_out': dense((nB, N_HEADS * HEAD_DIM, D), N_HEADS * HEAD_DIM),
        'c_w_in': dense((nC, D, MLA_IN), D),
        'c_q_lat_norm_g': gain((nC, MLA_Q_RANK)),
        'c_kv_lat_norm_g': gain((nC, MLA_KV_RANK)),
        'c_w_uq': dense((nC, MLA_Q_RANK, N_HEADS * MLA_QK_DIM), MLA_Q_RANK),
        'c_w_ukv': dense((nC, MLA_KV_RANK, N_HEADS * (MLA_NOPE_DIM + MLA_V_DIM)), MLA_KV_RANK),
        'c_w_out': dense((nC, N_HEADS * MLA_V_DIM, D), N_HEADS * MLA_V_DIM),
        'd_w_in': dense((nD, D, GQA_IN), D),
        'd_rpb': 0.1 * jax.random.normal(next(ks), (nD, N_HEADS, 2 * NA_ROWS - 1, 2 * NA_COLS - 1), jnp.float32),
        'd_w_out': dense((nD, N_HEADS * HEAD_DIM, D), N_HEADS * HEAD_DIM),
        'mem_norm_g': gain((D,)),
        'xa_wq': dense((DEPTH, D, XA_HEADS * XA_HEAD_DIM), D),
        'xa_wkv': dense((DEPTH, D, 2 * XA_HEADS * XA_HEAD_DIM), D),
        'xa_wo': dense((DEPTH, XA_HEADS * XA_HEAD_DIM, D), XA_HEADS * XA_HEAD_DIM),
        'router_w': dense((DEPTH, D, N_EXPERTS), D),
        'moe_w_gate': dense((DEPTH, N_EXPERTS, D, EXPERT_FF), D),
        'moe_w_up': dense((DEPTH, N_EXPERTS, D, EXPERT_FF), D),
        'moe_w_down': dense((DEPTH, N_EXPERTS, EXPERT_FF, D), EXPERT_FF),
        'final_norm_g': gain((D,)),
    }


def reference(x, mem, norm_mix_g, norm_xa_g, norm_ffn_g,
              a_w_in, a_sink, a_w_out,
              b_w_in, b_q_norm_g, b_k_norm_g, b_w_out,
              c_w_in, c_q_lat_norm_g, c_kv_lat_norm_g, c_w_uq, c_w_ukv, c_w_out,
              d_w_in, d_rpb, d_w_out,
              mem_norm_g, xa_wq, xa_wkv, xa_wo,
              router_w, moe_w_gate, moe_w_up, moe_w_down, final_norm_g):
    mem_n = _rms_norm(mem, mem_norm_g)
    h = x
    for i in range(DEPTH):
        m, j = i % N_MIXERS, i // N_MIXERS
        xn = _rms_norm(h, norm_mix_g[i])
        if m == 0:
            mix = _mixer_window_sink(xn, a_w_in[j], a_sink[j], a_w_out[j])
        elif m == 1:
            mix = _mixer_axial(xn, b_w_in[j], b_q_norm_g[j], b_k_norm_g[j], b_w_out[j])
        elif m == 2:
            mix = _mixer_mla(xn, c_w_in[j], c_q_lat_norm_g[j], c_kv_lat_norm_g[j], c_w_uq[j], c_w_ukv[j], c_w_out[j])
        else:
            mix = _mixer_neighbourhood(xn, d_w_in[j], d_rpb[j], d_w_out[j])
        h = h + mix
        h = h + _memory_cross_attention(_rms_norm(h, norm_xa_g[i]), mem_n, xa_wq[i], xa_wkv[i], xa_wo[i])
        h = h + _expert_choice_ffn(_rms_norm(h, norm_ffn_g[i]), router_w[i], moe_w_gate[i], moe_w_up[i], moe_w_down[i])
    return _rms_norm(h, final_norm_g)
```

```python
import functools

import jax
import jax.numpy as jnp
from jax import lax
from jax.experimental import pallas as pl
from jax.experimental.pallas import tpu as pltpu

F32 = jnp.float32
BF16 = jnp.bfloat16

D_MODEL = 2048
BATCH = 4
SEQ = 2048
DEPTH = 4
N_MIXERS = 4
HEAD_DIM = 128
N_HEADS = 16
N_KV_HEADS = 4
GQA_GROUP = N_HEADS // N_KV_HEADS
GQA_IN = (N_HEADS + 2 * N_KV_HEADS) * HEAD_DIM
Q_BLOCK = 128
WINDOW = 128
ROPE_THETA = 500000.0
ROT_DIM = HEAD_DIM // 4
AXIAL_THETA = 10000.0
GRID_W = 64
GRID_H = SEQ // GRID_W
MLA_Q_RANK = 512
MLA_KV_RANK = 512
MLA_NOPE_DIM = 128
MLA_ROPE_DIM = 64
MLA_V_DIM = 128
MLA_QK_DIM = MLA_NOPE_DIM + MLA_ROPE_DIM
MLA_Q_PAD = 256
MLA_LAT_PAD = 1152
NA_ROWS = 8
NA_COLS = 16
MEM_TOKENS = 256
XA_HEADS = 4
XA_HEAD_DIM = 128
XA_INNER = XA_HEADS * XA_HEAD_DIM
N_EXPERTS = 16
EC_CAPACITY = 2
EXPERT_FF = 1024
CAP = EC_CAPACITY * SEQ // N_EXPERTS
EPS = 1e-6
MASKED = -1e30

VMEM_LIMIT_BYTES = 56 * 1024 * 1024


def _cparams(*sem):
    return pltpu.CompilerParams(dimension_semantics=sem, vmem_limit_bytes=VMEM_LIMIT_BYTES)


def _dot_nt(a, b):
    return lax.dot_general(a, b, (((1,), (1,)), ((), ())), preferred_element_type=F32)


def _dot(a, b):
    return jnp.dot(a, b, preferred_element_type=F32)


def _rms(x, g):
    ms = jnp.mean(x * x, axis=-1, keepdims=True)
    return x * lax.rsqrt(ms + EPS) * g


def _rot(x, cos, sneg, spos, shift):
    n = x.shape[-1]
    return x * cos + pltpu.roll(x, n - shift, 1) * sneg + pltpu.roll(x, shift, 1) * spos


def _norm_matmul_kernel(x_ref, g_ref, w_ref, o_ref, xn_ref):
    @pl.when(pl.program_id(1) == 0)
    def _():
        xn_ref[...] = _rms(x_ref[...].astype(F32), g_ref[...]).astype(BF16)

    o_ref[...] = _dot(xn_ref[...], w_ref[...]).astype(o_ref.dtype)


def _norm_matmul(x, g, w, *, col_block=0, tm, tn, out_dtype):
    t = x.shape[0]
    k, n = w.shape
    return pl.pallas_call(
        _norm_matmul_kernel,
        grid=(t // tm, n // tn),
        in_specs=[
            pl.BlockSpec((tm, k), lambda i, j: (i, col_block)),
            pl.BlockSpec((1, k), lambda i, j: (0, 0)),
            pl.BlockSpec((k, tn), lambda i, j: (0, j)),
        ],
        out_specs=pl.BlockSpec((tm, tn), lambda i, j: (i, j)),
        out_shape=jax.ShapeDtypeStruct((t, n), out_dtype),
        scratch_shapes=[pltpu.VMEM((tm, k), BF16)],
        compiler_params=_cparams("parallel", "arbitrary"),
        name="norm_matmul",
    )(x, g.reshape(1, k).astype(F32), w)


def _matmul_residual_kernel(a_ref, w_ref, h_ref, o_ref):
    o_ref[...] = h_ref[...] + _dot(a_ref[...], w_ref[...])


def _matmul_residual(a, w, h, *, tm, tn):
    t, k = a.shape
    n = w.shape[1]
    return pl.pallas_call(
        _matmul_residual_kernel,
        grid=(t // tm, n // tn),
        in_specs=[
            pl.BlockSpec((tm, k), lambda i, j: (i, 0)),
            pl.BlockSpec((k, tn), lambda i, j: (0, j)),
            pl.BlockSpec((tm, tn), lambda i, j: (i, j)),
        ],
        out_specs=pl.BlockSpec((tm, tn), lambda i, j: (i, j)),
        out_shape=jax.ShapeDtypeStruct((t, n), F32),
        compiler_params=_cparams("parallel", "arbitrary"),
        name="matmul_residual",
    )(a, w, h)


def _attn_window_kernel(sink_ref, q_ref, k_ref, v_ref, cos_ref, sneg_ref, spos_ref, o_ref, k_scr, v_scr):
    kvh = pl.program_id(1)
    n = pl.program_id(2)
    half = ROT_DIM // 2

    @pl.when(n == 0)
    def _():
        k_scr[...] = _rot(k_ref[0], cos_ref[...], sneg_ref[...], spos_ref[...], half).astype(BF16)
        v_scr[...] = v_ref[0].astype(BF16)

    nb = SEQ // Q_BLOCK
    q0 = pl.multiple_of(n * Q_BLOCK, Q_BLOCK)
    k0 = pl.multiple_of(jnp.clip(n - 1, 0, nb - 3) * Q_BLOCK, Q_BLOCK)
    cos = cos_ref[pl.ds(q0, Q_BLOCK), :]
    sneg = sneg_ref[pl.ds(q0, Q_BLOCK), :]
    spos = spos_ref[pl.ds(q0, Q_BLOCK), :]
    kw = k_scr[pl.ds(k0, 3 * Q_BLOCK), :]
    vw = v_scr[pl.ds(k0, 3 * Q_BLOCK), :]
    qpos = q0 + lax.broadcasted_iota(jnp.int32, (Q_BLOCK, 3 * Q_BLOCK), 0)
    kpos = k0 + lax.broadcasted_iota(jnp.int32, (Q_BLOCK, 3 * Q_BLOCK), 1)
    valid = jnp.abs(qpos - kpos) <= WINDOW
    scale = HEAD_DIM ** -0.5
    for g in range(GQA_GROUP):
        x = q_ref[0, :, g * HEAD_DIM:(g + 1) * HEAD_DIM]
        xr = (_rot(x, cos, sneg, spos, half) * scale).astype(BF16)
        s = jnp.where(valid, _dot_nt(xr, kw), MASKED)
        sk = sink_ref[kvh * GQA_GROUP + g]
        m = jnp.maximum(jnp.max(s, axis=-1, keepdims=True), sk)
        e = jnp.exp(s - m)
        den = jnp.sum(e, axis=-1, keepdims=True) + jnp.exp(sk - m)
        o = _dot(e.astype(BF16), vw) / den
        o_ref[0, :, g * HEAD_DIM:(g + 1) * HEAD_DIM] = o.astype(o_ref.dtype)


def _attn_window(qkv, sink, tables):
    cos, sneg, spos = tables
    gw = GQA_GROUP * HEAD_DIM
    tab = pl.BlockSpec((SEQ, HEAD_DIM), lambda b, h, n: (0, 0))
    return pl.pallas_call(
        _attn_window_kernel,
        grid=(BATCH, N_KV_HEADS, SEQ // Q_BLOCK),
        in_specs=[
            pl.BlockSpec(memory_space=pltpu.SMEM),
            pl.BlockSpec((1, Q_BLOCK, gw), lambda b, h, n: (b, n, h)),
            pl.BlockSpec((1, SEQ, HEAD_DIM), lambda b, h, n: (b, 0, N_HEADS + h)),
            pl.BlockSpec((1, SEQ, HEAD_DIM), lambda b, h, n: (b, 0, N_HEADS + N_KV_HEADS + h)),
            tab, tab, tab,
        ],
        out_specs=pl.BlockSpec((1, Q_BLOCK, gw), lambda b, h, n: (b, n, h)),
        out_shape=jax.ShapeDtypeStruct((BATCH, SEQ, N_HEADS * HEAD_DIM), BF16),
        scratch_shapes=[pltpu.VMEM((SEQ, HEAD_DIM), BF16), pltpu.VMEM((SEQ, HEAD_DIM), BF16)],
        compiler_params=_cparams("parallel", "parallel", "arbitrary"),
        name="attn_window",
    )(sink.astype(F32), qkv, qkv, qkv, cos, sneg, spos)


AX_TQ = 128


def _attn_axial_kernel(q_ref, k_ref, v_ref, qg_ref, kg_ref, cos_ref, sneg_ref, spos_ref, o_ref, k_scr, v_scr):
    qi = pl.program_id(2)
    quarter = HEAD_DIM // 4

    @pl.when(qi == 0)
    def _():
        kn = _rms(k_ref[0], kg_ref[...])
        k_scr[...] = _rot(kn, cos_ref[...], sneg_ref[...], spos_ref[...], quarter).astype(BF16)
        v_scr[...] = v_ref[0].astype(BF16)

    q0 = pl.multiple_of(qi * AX_TQ, AX_TQ)
    cos = cos_ref[pl.ds(q0, AX_TQ), :]
    sneg = sneg_ref[pl.ds(q0, AX_TQ), :]
    spos = spos_ref[pl.ds(q0, AX_TQ), :]
    kk = k_scr[...]
    vv = v_scr[...]
    scale = HEAD_DIM ** -0.5
    for g in range(GQA_GROUP):
        x = _rms(q_ref[0, :, g * HEAD_DIM:(g + 1) * HEAD_DIM], qg_ref[...])
        xr = (_rot(x, cos, sneg, spos, quarter) * scale).astype(BF16)
        s = _dot_nt(xr, kk)
        m = jnp.max(s, axis=-1, keepdims=True)
        e = jnp.exp(s - m)
        den = jnp.sum(e, axis=-1, keepdims=True)
        o = _dot(e.astype(BF16), vv) / den
        o_ref[0, :, g * HEAD_DIM:(g + 1) * HEAD_DIM] = o.astype(o_ref.dtype)


def _attn_axial(qkv, qg, kg, tables):
    cos, sneg, spos = tables
    gw = GQA_GROUP * HEAD_DIM
    tab = pl.BlockSpec((SEQ, HEAD_DIM), lambda b, h, n: (0, 0))
    gain = pl.BlockSpec((1, HEAD_DIM), lambda b, h, n: (0, 0))
    return pl.pallas_call(
        _attn_axial_kernel,
        grid=(BATCH, N_KV_HEADS, SEQ // AX_TQ),
        in_specs=[
            pl.BlockSpec((1, AX_TQ, gw), lambda b, h, n: (b, n, h)),
            pl.BlockSpec((1, SEQ, HEAD_DIM), lambda b, h, n: (b, 0, N_HEADS + h)),
            pl.BlockSpec((1, SEQ, HEAD_DIM), lambda b, h, n: (b, 0, N_HEADS + N_KV_HEADS + h)),
            gain, gain, tab, tab, tab,
        ],
        out_specs=pl.BlockSpec((1, AX_TQ, gw), lambda b, h, n: (b, n, h)),
        out_shape=jax.ShapeDtypeStruct((BATCH, SEQ, N_HEADS * HEAD_DIM), BF16),
        scratch_shapes=[pltpu.VMEM((SEQ, HEAD_DIM), BF16), pltpu.VMEM((SEQ, HEAD_DIM), BF16)],
        compiler_params=_cparams("parallel", "parallel", "arbitrary"),
        name="attn_axial",
    )(qkv, qkv, qkv, qg.reshape(1, HEAD_DIM).astype(F32), kg.reshape(1, HEAD_DIM).astype(F32), cos, sneg, spos)


MLA_TQ = 256


def _attn_mla_kernel(q_ref, kn_ref, v_ref, kr_ref, cos_ref, sneg_ref, spos_ref, o_ref, k_scr, v_scr):
    qi = pl.program_id(2)
    half = MLA_ROPE_DIM // 2

    @pl.when(qi == 0)
    def _():
        k_scr[:, :MLA_NOPE_DIM] = kn_ref[0].astype(BF16)
        k_scr[:, MLA_NOPE_DIM:] = _rot(kr_ref[0], cos_ref[...], sneg_ref[...], spos_ref[...], half).astype(BF16)
        v_scr[...] = v_ref[0].astype(BF16)

    q0 = pl.multiple_of(qi * MLA_TQ, MLA_TQ)
    cos = cos_ref[pl.ds(q0, MLA_TQ), :]
    sneg = sneg_ref[pl.ds(q0, MLA_TQ), :]
    spos = spos_ref[pl.ds(q0, MLA_TQ), :]
    scale = MLA_QK_DIM ** -0.5
    qn = (q_ref[0, :, :MLA_NOPE_DIM] * scale).astype(BF16)
    qr = (_rot(q_ref[0, :, MLA_NOPE_DIM:], cos, sneg, spos, half) * scale).astype(BF16)
    s = _dot_nt(jnp.concatenate([qn, qr], axis=1), k_scr[...])
    m = jnp.max(s, axis=-1, keepdims=True)
    e = jnp.exp(s - m)
    den = jnp.sum(e, axis=-1, keepdims=True)
    o_ref[0] = (_dot(e.astype(BF16), v_scr[...]) / den).astype(o_ref.dtype)


def _attn_mla(qfull, kvfull, lat, tables):
    cos, sneg, spos = tables
    tab = pl.BlockSpec((SEQ, HEAD_DIM), lambda b, h, n: (0, 0))
    return pl.pallas_call(
        _attn_mla_kernel,
        grid=(BATCH, N_HEADS, SEQ // MLA_TQ),
        in_specs=[
            pl.BlockSpec((1, MLA_TQ, MLA_Q_PAD), lambda b, h, n: (b, n, h)),
            pl.BlockSpec((1, SEQ, MLA_NOPE_DIM), lambda b, h, n: (b, 0, 2 * h)),
            pl.BlockSpec((1, SEQ, MLA_V_DIM), lambda b, h, n: (b, 0, 2 * h + 1)),
            pl.BlockSpec((1, SEQ, HEAD_DIM), lambda b, h, n: (b, 0, (MLA_Q_RANK + MLA_KV_RANK) // HEAD_DIM)),
            tab, tab, tab,
        ],
        out_specs=pl.BlockSpec((1, MLA_TQ, MLA_V_DIM), lambda b, h, n: (b, n, h)),
        out_shape=jax.ShapeDtypeStruct((BATCH, SEQ, N_HEADS * MLA_V_DIM), BF16),
        scratch_shapes=[pltpu.VMEM((SEQ, MLA_Q_PAD), BF16), pltpu.VMEM((SEQ, MLA_V_DIM), BF16)],
        compiler_params=_cparams("parallel", "parallel", "arbitrary"),
        name="attn_mla",
    )(qfull, kvfull, kvfull, lat, cos, sneg, spos)


NA_DELTAS = NA_ROWS
NA_KEYS = NA_ROWS * GRID_W
NA_DR = 2 * NA_ROWS - 1
NA_DC = 2 * NA_COLS - 1


def _na_bias_kernel(rpb_ref, o_ref):
    h = pl.program_id(0)
    shape = (GRID_W, 2 * GRID_W)
    qc = lax.broadcasted_iota(jnp.int32, shape, 0)
    lane = lax.broadcasted_iota(jnp.int32, shape, 1)
    kc = jnp.where(lane >= GRID_W, lane - GRID_W, lane)
    dc = kc - qc + (NA_COLS - 1)
    c_start = jnp.clip(qc - NA_COLS // 2, 0, GRID_W - NA_COLS)
    in_win = (kc >= c_start) & (kc < c_start + NA_COLS)
    low = lane < GRID_W
    tiles = [jnp.zeros(shape, F32) for _ in range(NA_DR)]
    for d in range(NA_DC):
        hit = dc == d
        for dr in range(NA_DR):
            tiles[dr] = jnp.where(hit, rpb_ref[(h * NA_DR + dr) * NA_DC + d], tiles[dr])
    for delta in range(NA_DELTAS):
        for j in range(NA_ROWS // 2):
            t = jnp.where(low, tiles[delta + 2 * j], tiles[delta + 2 * j + 1])
            o_ref[0, delta, :, j * 2 * GRID_W:(j + 1) * 2 * GRID_W] = jnp.where(in_win, t, MASKED)


def _na_bias(rpb):
    return pl.pallas_call(
        _na_bias_kernel,
        grid=(N_HEADS,),
        in_specs=[pl.BlockSpec(memory_space=pltpu.SMEM)],
        out_specs=pl.BlockSpec((1, NA_DELTAS, GRID_W, NA_KEYS), lambda h: (h, 0, 0, 0)),
        out_shape=jax.ShapeDtypeStruct((N_HEADS, NA_DELTAS, GRID_W, NA_KEYS), F32),
        compiler_params=_cparams("parallel"),
        name="na_bias",
    )(rpb.astype(F32).reshape(-1))


def _attn_na_kernel(q_ref, k_ref, v_ref, bias_ref, o_ref, k_scr, v_scr):
    k_scr[...] = k_ref[0].astype(BF16)
    v_scr[...] = v_ref[0].astype(BF16)
    scale = HEAD_DIM ** -0.5

    def row(r, carry):
        r0 = jnp.clip(r - NA_ROWS // 2, 0, GRID_H - NA_ROWS)
        delta = r0 - r + (NA_ROWS - 1)
        q0 = pl.multiple_of(r * GRID_W, GRID_W)
        k0 = pl.multiple_of(r0 * GRID_W, GRID_W)
        kw = k_scr[pl.ds(k0, NA_KEYS), :]
        vw = v_scr[pl.ds(k0, NA_KEYS), :]
        for g in range(GQA_GROUP):
            x = (q_ref[0, pl.ds(q0, GRID_W), g * HEAD_DIM:(g + 1) * HEAD_DIM] * scale).astype(BF16)
            s = _dot_nt(x, kw) + bias_ref[g, delta]
            m = jnp.max(s, axis=-1, keepdims=True)
            e = jnp.exp(s - m)
            den = jnp.sum(e, axis=-1, keepdims=True)
            o = _dot(e.astype(BF16), vw) / den
            o_ref[0, pl.ds(q0, GRID_W), g * HEAD_DIM:(g + 1) * HEAD_DIM] = o.astype(o_ref.dtype)
        return carry

    lax.fori_loop(0, GRID_H, row, 0)


def _attn_na(qkv, bias):
    gw = GQA_GROUP * HEAD_DIM
    return pl.pallas_call(
        _attn_na_kernel,
        grid=(BATCH, N_KV_HEADS),
        in_specs=[
            pl.BlockSpec((1, SEQ, gw), lambda b, h: (b, 0, h)),
            pl.BlockSpec((1, SEQ, HEAD_DIM), lambda b, h: (b, 0, N_HEADS + h)),
            pl.BlockSpec((1, SEQ, HEAD_DIM), lambda b, h: (b, 0, N_HEADS + N_KV_HEADS + h)),
            pl.BlockSpec((GQA_GROUP, NA_DELTAS, GRID_W, NA_KEYS), lambda b, h: (h, 0, 0, 0)),
        ],
        out_specs=pl.BlockSpec((1, SEQ, gw), lambda b, h: (b, 0, h)),
        out_shape=jax.ShapeDtypeStruct((BATCH, SEQ, N_HEADS * HEAD_DIM), BF16),
        scratch_shapes=[pltpu.VMEM((SEQ, HEAD_DIM), BF16), pltpu.VMEM((SEQ, HEAD_DIM), BF16)],
        compiler_params=_cparams("parallel", "parallel"),
        name="attn_na",
    )(qkv, qkv, qkv, bias)


XA_TM = 512


def _xattn_kernel(h_ref, g_ref, wq_ref, kv_ref, wo_ref, o_ref):
    x = h_ref[0]
    xn = _rms(x, g_ref[...]).astype(BF16)
    q = (_dot(xn, wq_ref[...]) * (XA_HEAD_DIM ** -0.5)).astype(BF16)
    outs = []
    for hh in range(XA_HEADS):
        k = kv_ref[0, :, hh * XA_HEAD_DIM:(hh + 1) * XA_HEAD_DIM]
        v = kv_ref[0, :, XA_INNER + hh * XA_HEAD_DIM:XA_INNER + (hh + 1) * XA_HEAD_DIM]
        s = _dot_nt(q[:, hh * XA_HEAD_DIM:(hh + 1) * XA_HEAD_DIM], k)
        m = jnp.max(s, axis=-1, keepdims=True)
        e = jnp.exp(s - m)
        den = jnp.sum(e, axis=-1, keepdims=True)
        outs.append((_dot(e.astype(BF16), v) / den).astype(BF16))
    o_ref[0] = x + _dot(jnp.concatenate(outs, axis=1), wo_ref[...])


def _xattn(h, g, wq, kv, wo):
    return pl.pallas_call(
        _xattn_kernel,
        grid=(BATCH, SEQ // XA_TM),
        in_specs=[
            pl.BlockSpec((1, XA_TM, D_MODEL), lambda b, i: (b, i, 0)),
            pl.BlockSpec((1, D_MODEL), lambda b, i: (0, 0)),
            pl.BlockSpec((D_MODEL, XA_INNER), lambda b, i: (0, 0)),
            pl.BlockSpec((1, MEM_TOKENS, 2 * XA_INNER), lambda b, i: (b, 0, 0)),
            pl.BlockSpec((XA_INNER, D_MODEL), lambda b, i: (0, 0)),
        ],
        out_specs=pl.BlockSpec((1, XA_TM, D_MODEL), lambda b, i: (b, i, 0)),
        out_shape=jax.ShapeDtypeStruct((BATCH, SEQ, D_MODEL), F32),
        compiler_params=_cparams("parallel", "parallel"),
        name="xattn",
    )(h, g.reshape(1, D_MODEL).astype(F32), wq, kv, wo)


RT_TM = 512
PREFIX_BLOCK = 256
TOPK_MAX_ITERS = 256


def _router_kernel(h_ref, g_ref, rw_ref, xn_ref, selpos_ref, selpos_t_ref, gate_t_ref, aff_scr):
    i = pl.program_id(1)
    xn = _rms(h_ref[0], g_ref[...]).astype(BF16)
    xn_ref[0] = xn
    logits = _dot(xn, rw_ref[...])
    m = jnp.max(logits, axis=-1, keepdims=True)
    e = jnp.exp(logits - m)
    aff_scr[pl.ds(pl.multiple_of(i * RT_TM, RT_TM), RT_TM), :] = e / jnp.sum(e, axis=-1, keepdims=True)

    @pl.when(i == pl.num_programs(1) - 1)
    def _():
        a = aff_scr[...]
        kf = jnp.float32(CAP)

        def count(mask):
            return jnp.sum(jnp.where(mask, 1.0, 0.0), axis=0, keepdims=True)

        def cond(c):
            it, _, _, _, done = c
            return jnp.logical_and(it < TOPK_MAX_ITERS, done == 0)

        def body(c):
            it, lo, hi, _, _ = c
            mid = 0.5 * (lo + hi)
            take = count(a >= mid) >= kf
            lo = jnp.where(take, mid, lo)
            hi = jnp.where(take, hi, mid)
            top = jnp.max(jnp.where(a < hi, a, -1.0), axis=0, keepdims=True)
            bot = jnp.min(jnp.where(a >= lo, a, 3.0), axis=0, keepdims=True)
            done = jnp.min(jnp.where(top == bot, 1, 0))
            return it + 1, lo, hi, top, done

        init = (jnp.int32(0), jnp.zeros((1, N_EXPERTS), F32), jnp.full((1, N_EXPERTS), 2.0, F32),
                jnp.zeros((1, N_EXPERTS), F32), jnp.int32(0))
        _, _, _, kth, _ = lax.while_loop(cond, body, init)

        gt = a > kth
        eq = a == kth
        need = kf - count(gt)
        tok = lax.broadcasted_iota(jnp.int32, (SEQ, N_EXPERTS), 0).astype(F32)
        cut = jnp.zeros((1, N_EXPERTS), F32)
        step = SEQ // 2
        while step >= 1:
            cand = cut + step
            cut = jnp.where(count(eq & (tok < cand)) < need, cand, cut)
            step //= 2
        sel = jnp.where(gt, 1.0, jnp.where(eq & (tok <= cut), 1.0, 0.0))

        r = lax.broadcasted_iota(jnp.int32, (PREFIX_BLOCK, PREFIX_BLOCK), 0)
        c = lax.broadcasted_iota(jnp.int32, (PREFIX_BLOCK, PREFIX_BLOCK), 1)
        ltri = jnp.where(c < r, 1.0, 0.0).astype(BF16)
        off = jnp.zeros((1, N_EXPERTS), F32)
        for j in range(SEQ // PREFIX_BLOCK):
            blk = sel[j * PREFIX_BLOCK:(j + 1) * PREFIX_BLOCK]
            pos = _dot(ltri, blk.astype(BF16)) + off
            selpos_ref[0, j * PREFIX_BLOCK:(j + 1) * PREFIX_BLOCK, :] = jnp.where(blk > 0.5, pos, -1.0)
            off = off + jnp.sum(blk, axis=0, keepdims=True)

        er = lax.broadcasted_iota(jnp.int32, (N_EXPERTS, N_EXPERTS), 0)
        ec = lax.broadcasted_iota(jnp.int32, (N_EXPERTS, N_EXPERTS), 1)
        eye = jnp.where(er == ec, 1.0, 0.0).astype(BF16)
        selpos_t_ref[0] = _dot_nt(eye, selpos_ref[0].astype(BF16))
        a1 = a.astype(BF16)
        r1 = a - a1.astype(F32)
        a2 = r1.astype(BF16)
        a3 = (r1 - a2.astype(F32)).astype(BF16)
        gate_t_ref[0] = (_dot_nt(eye, a1) + _dot_nt(eye, a2)) + _dot_nt(eye, a3)


def _router(h, g, rw):
    return pl.pallas_call(
        _router_kernel,
        grid=(BATCH, SEQ // RT_TM),
        in_specs=[
            pl.BlockSpec((1, RT_TM, D_MODEL), lambda b, i: (b, i, 0)),
            pl.BlockSpec((1, D_MODEL), lambda b, i: (0, 0)),
            pl.BlockSpec((D_MODEL, N_EXPERTS), lambda b, i: (0, 0)),
        ],
        out_specs=[
            pl.BlockSpec((1, RT_TM, D_MODEL), lambda b, i: (b, i, 0)),
            pl.BlockSpec((1, SEQ, N_EXPERTS), lambda b, i: (b, 0, 0)),
            pl.BlockSpec((1, N_EXPERTS, SEQ), lambda b, i: (b, 0, 0)),
            pl.BlockSpec((1, N_EXPERTS, SEQ), lambda b, i: (b, 0, 0)),
        ],
        out_shape=[
            jax.ShapeDtypeStruct((BATCH, SEQ, D_MODEL), BF16),
            jax.ShapeDtypeStruct((BATCH, SEQ, N_EXPERTS), F32),
            jax.ShapeDtypeStruct((BATCH, N_EXPERTS, SEQ), F32),
            jax.ShapeDtypeStruct((BATCH, N_EXPERTS, SEQ), F32),
        ],
        scratch_shapes=[pltpu.VMEM((SEQ, N_EXPERTS), F32)],
        compiler_params=_cparams("parallel", "arbitrary"),
        name="router_topk",
    )(h, g.reshape(1, D_MODEL).astype(F32), rw)


def _gather_kernel(xn_ref, selpos_t_ref, gate_t_ref, xg_ref, gs_ref):
    e = pl.program_id(1)
    slot_of_token = selpos_t_ref[0, pl.ds(e, 1), :]
    gate_row = gate_t_ref[0, pl.ds(e, 1), :]
    slot = lax.broadcasted_iota(jnp.int32, (CAP, SEQ), 0).astype(F32)
    hit = slot == slot_of_token
    xg_ref[0, 0] = _dot(jnp.where(hit, 1.0, 0.0).astype(BF16), xn_ref[0]).astype(BF16)
    gs_ref[0, 0] = jnp.sum(jnp.where(hit, gate_row, 0.0), axis=1, keepdims=True)


def _gather(xn, selpos_t, gate_t):
    return pl.pallas_call(
        _gather_kernel,
        grid=(BATCH, N_EXPERTS),
        in_specs=[
            pl.BlockSpec((1, SEQ, D_MODEL), lambda b, e: (b, 0, 0)),
            pl.BlockSpec((1, N_EXPERTS, SEQ), lambda b, e: (b, 0, 0)),
            pl.BlockSpec((1, N_EXPERTS, SEQ), lambda b, e: (b, 0, 0)),
        ],
        out_specs=[
            pl.BlockSpec((1, 1, CAP, D_MODEL), lambda b, e: (e, b, 0, 0)),
            pl.BlockSpec((1, 1, CAP, 1), lambda b, e: (e, b, 0, 0)),
        ],
        out_shape=[
            jax.ShapeDtypeStruct((N_EXPERTS, BATCH, CAP, D_MODEL), BF16),
            jax.ShapeDtypeStruct((N_EXPERTS, BATCH, CAP, 1), F32),
        ],
        compiler_params=_cparams("parallel", "arbitrary"),
        name="moe_gather",
    )(xn, selpos_t, gate_t)


FF_TF = 256


def _expert_ffn_kernel(xg_ref, wg_ref, wu_ref, wd_ref, gs_ref, y_ref, acc_ref):
    f = pl.program_id(1)
    x = xg_ref[0].reshape(BATCH * CAP, D_MODEL)
    gate = _dot(x, wg_ref[0, 0].astype(BF16))
    up = _dot(x, wu_ref[0, 0].astype(BF16))
    act = (gate * (1.0 / (1.0 + jnp.exp(-gate))) * up).astype(BF16)
    part = _dot(act, wd_ref[0, 0].astype(BF16))

    @pl.when(f == 0)
    def _():
        acc_ref[...] = part

    @pl.when(f > 0)
    def _():
        acc_ref[...] += part

    @pl.when(f == pl.num_programs(1) - 1)
    def _():
        y = acc_ref[...] * gs_ref[0].reshape(BATCH * CAP, 1)
        y_ref[0] = y.astype(BF16).reshape(BATCH, CAP, D_MODEL)


def _expert_ffn(xg, w_gate, w_up, w_down, gs, layer):
    return pl.pallas_call(
        _expert_ffn_kernel,
        grid=(N_EXPERTS, EXPERT_FF // FF_TF),
        in_specs=[
            pl.BlockSpec((1, BATCH, CAP, D_MODEL), lambda e, f: (e, 0, 0, 0)),
            pl.BlockSpec((1, 1, D_MODEL, FF_TF), lambda e, f: (layer, e, 0, f)),
            pl.BlockSpec((1, 1, D_MODEL, FF_TF), lambda e, f: (layer, e, 0, f)),
            pl.BlockSpec((1, 1, FF_TF, D_MODEL), lambda e, f: (layer, e, f, 0)),
            pl.BlockSpec((1, BATCH, CAP, 1), lambda e, f: (e, 0, 0, 0)),
        ],
        out_specs=pl.BlockSpec((1, BATCH, CAP, D_MODEL), lambda e, f: (e, 0, 0, 0)),
        out_shape=jax.ShapeDtypeStruct((N_EXPERTS, BATCH, CAP, D_MODEL), BF16),
        scratch_shapes=[pltpu.VMEM((BATCH * CAP, D_MODEL), F32)],
        compiler_params=_cparams("parallel", "arbitrary"),
        name="moe_ffn",
    )(xg, w_gate, w_up, w_down, gs)


CB_TM = 512
CB_EXPERTS = 4


def _combine_kernel(h_ref, selpos_ref, y_ref, o_ref):
    c = pl.program_id(2)

    @pl.when(c == 0)
    def _():
        o_ref[...] = h_ref[...]

    sp = selpos_ref[0]
    lane = lax.broadcasted_iota(jnp.int32, sp.shape, 1)
    slot = lax.broadcasted_iota(jnp.int32, (CB_TM, CAP), 1).astype(F32)
    pieces = []
    for k in range(CB_EXPERTS):
        col = jnp.sum(jnp.where(lane == c * CB_EXPERTS + k, sp, 0.0), axis=1, keepdims=True)
        pieces.append(jnp.where(col == slot, 1.0, 0.0).astype(BF16))
    onehot = jnp.concatenate(pieces, axis=1)
    o_ref[0] += _dot(onehot, y_ref[...].reshape(CB_EXPERTS * CAP, D_MODEL))


def _combine(h, selpos, y):
    return pl.pallas_call(
        _combine_kernel,
        grid=(BATCH, SEQ // CB_TM, N_EXPERTS // CB_EXPERTS),
        in_specs=[
            pl.BlockSpec((1, CB_TM, D_MODEL), lambda b, i, c: (b, i, 0)),
            pl.BlockSpec((1, CB_TM, N_EXPERTS), lambda b, i, c: (b, i, 0)),
            pl.BlockSpec((CB_EXPERTS, 1, CAP, D_MODEL), lambda b, i, c: (c, b, 0, 0)),
        ],
        out_specs=pl.BlockSpec((1, CB_TM, D_MODEL), lambda b, i, c: (b, i, 0)),
        out_shape=jax.ShapeDtypeStruct((BATCH, SEQ, D_MODEL), F32),
        compiler_params=_cparams("parallel", "parallel", "arbitrary"),
        name="moe_combine",
    )(h, selpos, y)


def _final_norm_kernel(x_ref, g_ref, o_ref):
    o_ref[...] = _rms(x_ref[...], g_ref[...])


def _final_norm(x, g, *, tm=1024):
    t, d = x.shape
    return pl.pallas_call(
        _final_norm_kernel,
        grid=(t // tm,),
        in_specs=[pl.BlockSpec((tm, d), lambda i: (i, 0)), pl.BlockSpec((1, d), lambda i: (0, 0))],
        out_specs=pl.BlockSpec((tm, d), lambda i: (i, 0)),
        out_shape=jax.ShapeDtypeStruct((t, d), F32),
        compiler_params=_cparams("parallel"),
        name="final_norm",
    )(x, g.reshape(1, d).astype(F32))


def _angles(pos, dim, theta):
    inv = jnp.power(jnp.float32(theta), -(jnp.arange(0, dim, 2, dtype=jnp.float32) / dim))
    ang = pos.astype(jnp.float32)[:, None] * inv[None, :]
    return jnp.cos(ang), jnp.sin(ang)


def _rot_tables(groups, width=HEAD_DIM):
    cos_parts, sneg_parts, spos_parts = [], [], []
    used = 0
    for c, s in groups:
        z = jnp.zeros_like(s)
        cos_parts += [c, c]
        sneg_parts += [-s, z]
        spos_parts += [z, s]
        used += 2 * c.shape[1]
    rest = width - used
    if rest:
        cos_parts.append(jnp.ones((SEQ, rest), F32))
        sneg_parts.append(jnp.zeros((SEQ, rest), F32))
        spos_parts.append(jnp.zeros((SEQ, rest), F32))
    return (jnp.concatenate(cos_parts, axis=1), jnp.concatenate(sneg_parts, axis=1),
            jnp.concatenate(spos_parts, axis=1))


def _mixer_gqa_in(h2, g, w_in):
    return _norm_matmul(h2, g, w_in.astype(BF16), tm=1024, tn=512, out_dtype=F32).reshape(BATCH, SEQ, GQA_IN)


def kernel(x, mem, norm_mix_g, norm_xa_g, norm_ffn_g, a_w_in, a_sink, a_w_out, b_w_in, b_q_norm_g, b_k_norm_g, b_w_out, c_w_in, c_q_lat_norm_g, c_kv_lat_norm_g, c_w_uq, c_w_ukv, c_w_out, d_w_in, d_rpb, d_w_out, mem_norm_g, xa_wq, xa_wkv, xa_wo, router_w, moe_w_gate, moe_w_up, moe_w_down, final_norm_g):
    t = BATCH * SEQ
    pos = jnp.arange(SEQ)
    tab_a = _rot_tables([_angles(pos, ROT_DIM, ROPE_THETA)])
    tab_b = _rot_tables([_angles(pos // GRID_W, HEAD_DIM // 2, AXIAL_THETA),
                         _angles(pos % GRID_W, HEAD_DIM // 2, AXIAL_THETA)])
    tab_c = _rot_tables([_angles(pos, MLA_ROPE_DIM, ROPE_THETA)])
    mem2 = mem.reshape(BATCH * MEM_TOKENS, D_MODEL)

    h = x
    for i in range(DEPTH):
        m, j = i % N_MIXERS, i // N_MIXERS
        h2 = h.reshape(t, D_MODEL)
        if m == 0:
            qkv = _mixer_gqa_in(h2, norm_mix_g[i], a_w_in[j])
            att = _attn_window(qkv, a_sink[j], tab_a)
            w_out = a_w_out[j]
        elif m == 1:
            qkv = _mixer_gqa_in(h2, norm_mix_g[i], b_w_in[j])
            att = _attn_axial(qkv, b_q_norm_g[j], b_k_norm_g[j], tab_b)
            w_out = b_w_out[j]
        elif m == 2:
            w_in = jnp.pad(c_w_in[j], ((0, 0), (0, MLA_LAT_PAD - c_w_in.shape[-1]))).astype(BF16)
            lat = _norm_matmul(h2, norm_mix_g[i], w_in, tm=1024, tn=MLA_LAT_PAD // 3, out_dtype=F32)
            w_uq = jnp.pad(c_w_uq[j].reshape(MLA_Q_RANK, N_HEADS, MLA_QK_DIM),
                           ((0, 0), (0, 0), (0, MLA_Q_PAD - MLA_QK_DIM))).reshape(MLA_Q_RANK, N_HEADS * MLA_Q_PAD)
            qfull = _norm_matmul(lat, c_q_lat_norm_g[j], w_uq.astype(BF16), col_block=0, tm=1024, tn=1024, out_dtype=F32)
            kvfull = _norm_matmul(lat, c_kv_lat_norm_g[j], c_w_ukv[j].astype(BF16), col_block=1, tm=1024, tn=1024,
                                  out_dtype=F32)
            att = _attn_mla(qfull.reshape(BATCH, SEQ, -1), kvfull.reshape(BATCH, SEQ, -1),
                            lat.reshape(BATCH, SEQ, MLA_LAT_PAD), tab_c)
            w_out = c_w_out[j]
        else:
            qkv = _mixer_gqa_in(h2, norm_mix_g[i], d_w_in[j])
            att = _attn_na(qkv, _na_bias(d_rpb[j]))
            w_out = d_w_out[j]
        h = _matmul_residual(att.reshape(t, -1), w_out.astype(BF16), h2, tm=1024, tn=512).reshape(BATCH, SEQ, D_MODEL)

        kv = _norm_matmul(mem2, mem_norm_g, xa_wkv[i].astype(BF16), tm=512, tn=512, out_dtype=BF16)
        h = _xattn(h, norm_xa_g[i], xa_wq[i].astype(BF16), kv.reshape(BATCH, MEM_TOKENS, 2 * XA_INNER),
                   xa_wo[i].astype(BF16))

        xn, selpos, selpos_t, gate_t = _router(h, norm_ffn_g[i], router_w[i].astype(BF16))
        xg, gs = _gather(xn, selpos_t, gate_t)
        y = _expert_ffn(xg, moe_w_gate, moe_w_up, moe_w_down, gs, i)
        h = _combine(h, selpos, y)

    return _final_norm(h.reshape(t, D_MODEL), final_norm_g).reshape(BATCH, SEQ, D_MODEL)
```

```python
import jax
import jax.numpy as jnp
from jax import lax
from jax.experimental import pallas as pl
from jax.experimental.pallas import tpu as pltpu

F32 = jnp.float32
BF16 = jnp.bfloat16

D_MODEL = 2048
BATCH = 4
SEQ = 2048
DEPTH = 4
N_MIXERS = 4
HEAD_DIM = 128
N_HEADS = 16
N_KV_HEADS = 4
GQA_GROUP = N_HEADS // N_KV_HEADS
GQA_IN = (N_HEADS + 2 * N_KV_HEADS) * HEAD_DIM
Q_BLOCK = 128
WINDOW = 128
ROPE_THETA = 500000.0
ROT_DIM = HEAD_DIM // 4
AXIAL_THETA = 10000.0
GRID_W = 64
GRID_H = SEQ // GRID_W
MLA_Q_RANK = 512
MLA_KV_RANK = 512
MLA_NOPE_DIM = 128
MLA_ROPE_DIM = 64
MLA_V_DIM = 128
MLA_QK_DIM = MLA_NOPE_DIM + MLA_ROPE_DIM
MLA_Q_PAD = 256
MLA_LAT_PAD = 1152
NA_ROWS = 8
NA_COLS = 16
MEM_TOKENS = 256
XA_HEADS = 4
XA_HEAD_DIM = 128
XA_INNER = XA_HEADS * XA_HEAD_DIM
N_EXPERTS = 16
EC_CAPACITY = 2
EXPERT_FF = 1024
CAP = EC_CAPACITY * SEQ // N_EXPERTS
EPS = 1e-6
MASKED = -1e30
LOG2E = 1.4426950408889634
LANES = 128

VMEM_LIMIT_BYTES = 56 * 1024 * 1024


def _cparams(*sem):
    return pltpu.CompilerParams(dimension_semantics=sem, vmem_limit_bytes=VMEM_LIMIT_BYTES)


def _dot_nt(a, b):
    return lax.dot_general(a, b, (((1,), (1,)), ((), ())), preferred_element_type=F32)


def _dot(a, b):
    return jnp.dot(a, b, preferred_element_type=F32)


def _rms(x, g):
    ms = jnp.mean(x * x, axis=-1, keepdims=True)
    return x * lax.rsqrt(ms + EPS) * g


def _rot(x, cos, sneg, spos, shift):
    n = x.shape[-1]
    return x * cos + pltpu.roll(x, n - shift, 1) * sneg + pltpu.roll(x, shift, 1) * spos


def _softmax_pv(s, v_ones, sink=None):
    dv = v_ones.shape[1] // 2
    m = jnp.max(s, axis=-1, keepdims=True)
    if sink is not None:
        m = jnp.maximum(m, sink)
    r = _dot(jnp.exp2(s - m).astype(BF16), v_ones)
    den = r[:, dv:]
    if sink is not None:
        den = den + jnp.exp2(sink - m)
    return r[:, :dv] * (1.0 / den)


def _fill_v_ones(v_scr, v):
    dv = v.shape[1]
    v_scr[:, :dv] = v
    v_scr[:, dv:] = jnp.ones_like(v)


def _pairs(n_blocks, scores, finish, buf_a, buf_b):
    buf_b[...] = scores(0)

    def pair(i, carry):
        buf_a[...] = scores(2 * i + 1)
        finish(2 * i, buf_b[...])
        buf_b[...] = scores(jnp.minimum(2 * i + 2, n_blocks - 1))
        finish(2 * i + 1, buf_a[...])
        return carry

    lax.fori_loop(0, n_blocks // 2, pair, 0)


def _score_bufs(rows, keys):
    return [pltpu.VMEM((rows, keys), F32), pltpu.VMEM((rows, keys), F32)]


def _norm_matmul_kernel(x_ref, g_ref, w_ref, o_ref, xn_ref):
    @pl.when(pl.program_id(1) == 0)
    def _():
        xn_ref[...] = _rms(x_ref[...].astype(F32), g_ref[...]).astype(BF16)

    o_ref[...] = _dot(xn_ref[...], w_ref[...]).astype(o_ref.dtype)


def _norm_matmul(x, g, w, *, col_block=0, tm, tn):
    t = x.shape[0]
    k, n = w.shape
    return pl.pallas_call(
        _norm_matmul_kernel,
        grid=(t // tm, n // tn),
        in_specs=[
            pl.BlockSpec((tm, k), lambda i, j: (i, col_block)),
            pl.BlockSpec((1, k), lambda i, j: (0, 0)),
            pl.BlockSpec((k, tn), lambda i, j: (0, j)),
        ],
        out_specs=pl.BlockSpec((tm, tn), lambda i, j: (i, j)),
        out_shape=jax.ShapeDtypeStruct((t, n), BF16),
        scratch_shapes=[pltpu.VMEM((tm, k), BF16)],
        compiler_params=_cparams("parallel", "arbitrary"),
        name="norm_matmul",
    )(x, g.reshape(1, k).astype(F32), w)


def _attn_window_kernel(sink_ref, q_ref, k_ref, v_ref, cos_ref, sneg_ref, spos_ref, o_ref, k_scr, v_scr,
                        buf_a, buf_b):
    kvh = pl.program_id(1)
    half = ROT_DIM // 2
    nb = SEQ // Q_BLOCK
    rows, keys = GQA_GROUP * Q_BLOCK, 3 * Q_BLOCK
    k_scr[...] = _rot(k_ref[0].astype(F32), cos_ref[...], sneg_ref[...], spos_ref[...], half).astype(BF16)
    _fill_v_ones(v_scr, v_ref[0])
    r = lax.broadcasted_iota(jnp.int32, (rows, keys), 0)
    c = lax.broadcasted_iota(jnp.int32, (rows, keys), 1)
    base = (r & (Q_BLOCK - 1)) - c
    sink = jnp.concatenate(
        [jnp.full((Q_BLOCK, 1), sink_ref[kvh * GQA_GROUP + g], F32) for g in range(GQA_GROUP)], axis=0)

    def window_start(n):
        return pl.multiple_of(jnp.clip(n - 1, 0, nb - 3) * Q_BLOCK, Q_BLOCK)

    def scores(n):
        q0 = pl.multiple_of(n * Q_BLOCK, Q_BLOCK)
        cos = cos_ref[pl.ds(q0, Q_BLOCK), :]
        sneg = sneg_ref[pl.ds(q0, Q_BLOCK), :]
        spos = spos_ref[pl.ds(q0, Q_BLOCK), :]
        q4 = jnp.concatenate(
            [_rot(q_ref[0, pl.ds(q0, Q_BLOCK), g * HEAD_DIM:(g + 1) * HEAD_DIM].astype(F32),
                  cos, sneg, spos, half).astype(BF16) for g in range(GQA_GROUP)], axis=0)
        return _dot_nt(q4, k_scr[pl.ds(window_start(n), keys), :])

    def finish(n, s):
        q0 = pl.multiple_of(n * Q_BLOCK, Q_BLOCK)
        k0 = window_start(n)
        s = jnp.where(jnp.abs(base + (q0 - k0)) <= WINDOW, s, MASKED)
        o = _softmax_pv(s, v_scr[pl.ds(k0, keys), :], sink)
        for g in range(GQA_GROUP):
            o_ref[0, pl.ds(q0, Q_BLOCK), g * HEAD_DIM:(g + 1) * HEAD_DIM] = (
                o[g * Q_BLOCK:(g + 1) * Q_BLOCK].astype(o_ref.dtype))

    _pairs(nb, scores, finish, buf_a, buf_b)


def _gqa_specs():
    gw = GQA_GROUP * HEAD_DIM
    return [
        pl.BlockSpec((1, SEQ, gw), lambda b, h: (b, 0, h)),
        pl.BlockSpec((1, SEQ, HEAD_DIM), lambda b, h: (b, 0, N_HEADS + h)),
        pl.BlockSpec((1, SEQ, HEAD_DIM), lambda b, h: (b, 0, N_HEADS + N_KV_HEADS + h)),
    ], pl.BlockSpec((1, SEQ, gw), lambda b, h: (b, 0, h))


def _attn_window(qkv, sink, tables):
    cos, sneg, spos = tables
    qkv_specs, out_spec = _gqa_specs()
    tab = pl.BlockSpec((SEQ, HEAD_DIM), lambda b, h: (0, 0))
    return pl.pallas_call(
        _attn_window_kernel,
        grid=(BATCH, N_KV_HEADS),
        in_specs=[pl.BlockSpec(memory_space=pltpu.SMEM)] + qkv_specs + [tab, tab, tab],
        out_specs=out_spec,
        out_shape=jax.ShapeDtypeStruct((BATCH, SEQ, N_HEADS * HEAD_DIM), BF16),
        scratch_shapes=[pltpu.VMEM((SEQ, HEAD_DIM), BF16), pltpu.VMEM((SEQ, 2 * HEAD_DIM), BF16)]
        + _score_bufs(GQA_GROUP * Q_BLOCK, 3 * Q_BLOCK),
        compiler_params=_cparams("parallel", "parallel"),
        name="attn_window",
    )(sink.astype(F32), qkv, qkv, qkv, cos, sneg, spos)


AX_TQ = 128


def _attn_axial_kernel(q_ref, k_ref, v_ref, qg_ref, kg_ref, cos_ref, sneg_ref, spos_ref, o_ref, k_scr, v_scr,
                       buf_a, buf_b):
    quarter = HEAD_DIM // 4
    kn = _rms(k_ref[0].astype(F32), kg_ref[...])
    k_scr[...] = _rot(kn, cos_ref[...], sneg_ref[...], spos_ref[...], quarter).astype(BF16)
    _fill_v_ones(v_scr, v_ref[0])

    def scores(n):
        q0 = pl.multiple_of(n * AX_TQ, AX_TQ)
        cos = cos_ref[pl.ds(q0, AX_TQ), :]
        sneg = sneg_ref[pl.ds(q0, AX_TQ), :]
        spos = spos_ref[pl.ds(q0, AX_TQ), :]
        q4 = jnp.concatenate(
            [_rot(_rms(q_ref[0, pl.ds(q0, AX_TQ), g * HEAD_DIM:(g + 1) * HEAD_DIM].astype(F32), qg_ref[...]),
                  cos, sneg, spos, quarter).astype(BF16) for g in range(GQA_GROUP)], axis=0)
        return _dot_nt(q4, k_scr[...])

    def finish(n, s):
        q0 = pl.multiple_of(n * AX_TQ, AX_TQ)
        o = _softmax_pv(s, v_scr[...])
        for g in range(GQA_GROUP):
            o_ref[0, pl.ds(q0, AX_TQ), g * HEAD_DIM:(g + 1) * HEAD_DIM] = (
                o[g * AX_TQ:(g + 1) * AX_TQ].astype(o_ref.dtype))

    _pairs(SEQ // AX_TQ, scores, finish, buf_a, buf_b)


def _attn_axial(qkv, qg, kg, tables):
    cos, sneg, spos = tables
    qkv_specs, out_spec = _gqa_specs()
    tab = pl.BlockSpec((SEQ, HEAD_DIM), lambda b, h: (0, 0))
    gain = pl.BlockSpec((1, HEAD_DIM), lambda b, h: (0, 0))
    return pl.pallas_call(
        _attn_axial_kernel,
        grid=(BATCH, N_KV_HEADS),
        in_specs=qkv_specs + [gain, gain, tab, tab, tab],
        out_specs=out_spec,
        out_shape=jax.ShapeDtypeStruct((BATCH, SEQ, N_HEADS * HEAD_DIM), BF16),
        scratch_shapes=[pltpu.VMEM((SEQ, HEAD_DIM), BF16), pltpu.VMEM((SEQ, 2 * HEAD_DIM), BF16)]
        + _score_bufs(GQA_GROUP * AX_TQ, SEQ),
        compiler_params=_cparams("parallel", "parallel"),
        name="attn_axial",
    )(qkv, qkv, qkv, qg.reshape(1, HEAD_DIM).astype(F32), kg.reshape(1, HEAD_DIM).astype(F32), cos, sneg, spos)


MLA_TQ = 512


def _attn_mla_kernel(q_ref, kn_ref, v_ref, kr_ref, cos_ref, sneg_ref, spos_ref, o_ref, k_scr, v_scr, buf_a, buf_b):
    half = MLA_ROPE_DIM // 2
    k_scr[:, :MLA_NOPE_DIM] = kn_ref[0]
    k_scr[:, MLA_NOPE_DIM:] = _rot(kr_ref[0].astype(F32), cos_ref[...], sneg_ref[...], spos_ref[...],
                                   half).astype(BF16)
    _fill_v_ones(v_scr, v_ref[0])

    def scores(n):
        q0 = pl.multiple_of(n * MLA_TQ, MLA_TQ)
        cos = cos_ref[pl.ds(q0, MLA_TQ), :]
        sneg = sneg_ref[pl.ds(q0, MLA_TQ), :]
        spos = spos_ref[pl.ds(q0, MLA_TQ), :]
        qn = q_ref[0, pl.ds(q0, MLA_TQ), :MLA_NOPE_DIM]
        qr = _rot(q_ref[0, pl.ds(q0, MLA_TQ), MLA_NOPE_DIM:].astype(F32), cos, sneg, spos, half).astype(BF16)
        return _dot_nt(jnp.concatenate([qn, qr], axis=1), k_scr[...])

    def finish(n, s):
        q0 = pl.multiple_of(n * MLA_TQ, MLA_TQ)
        o_ref[0, pl.ds(q0, MLA_TQ), :] = _softmax_pv(s, v_scr[...]).astype(o_ref.dtype)

    _pairs(SEQ // MLA_TQ, scores, finish, buf_a, buf_b)


def _attn_mla(qfull, kvfull, lat, tables):
    cos, sneg, spos = tables
    tab = pl.BlockSpec((SEQ, HEAD_DIM), lambda b, h: (0, 0))
    return pl.pallas_call(
        _attn_mla_kernel,
        grid=(BATCH, N_HEADS),
        in_specs=[
            pl.BlockSpec((1, SEQ, MLA_Q_PAD), lambda b, h: (b, 0, h)),
            pl.BlockSpec((1, SEQ, MLA_NOPE_DIM), lambda b, h: (b, 0, 2 * h)),
            pl.BlockSpec((1, SEQ, MLA_V_DIM), lambda b, h: (b, 0, 2 * h + 1)),
            pl.BlockSpec((1, SEQ, HEAD_DIM), lambda b, h: (b, 0, (MLA_Q_RANK + MLA_KV_RANK) // HEAD_DIM)),
            tab, tab, tab,
        ],
        out_specs=pl.BlockSpec((1, SEQ, MLA_V_DIM), lambda b, h: (b, 0, h)),
        out_shape=jax.ShapeDtypeStruct((BATCH, SEQ, N_HEADS * MLA_V_DIM), BF16),
        scratch_shapes=[pltpu.VMEM((SEQ, MLA_Q_PAD), BF16), pltpu.VMEM((SEQ, 2 * MLA_V_DIM), BF16)]
        + _score_bufs(MLA_TQ, SEQ),
        compiler_params=_cparams("parallel", "parallel"),
        name="attn_mla",
    )(qfull, kvfull, kvfull, lat, cos, sneg, spos)


NA_DELTAS = NA_ROWS
NA_KEYS = NA_ROWS * GRID_W
NA_DR = 2 * NA_ROWS - 1
NA_DC = 2 * NA_COLS - 1


def _na_bias_kernel(rpb_ref, o_ref):
    h = pl.program_id(0)
    shape = (GRID_W, 2 * GRID_W)
    qc = lax.broadcasted_iota(jnp.int32, shape, 0)
    lane = lax.broadcasted_iota(jnp.int32, shape, 1)
    kc = jnp.where(lane >= GRID_W, lane - GRID_W, lane)
    dc = kc - qc + (NA_COLS - 1)
    c_start = jnp.clip(qc - NA_COLS // 2, 0, GRID_W - NA_COLS)
    in_win = (kc >= c_start) & (kc < c_start + NA_COLS)
    low = lane < GRID_W
    tiles = [jnp.zeros(shape, F32) for _ in range(NA_DR)]
    for d in range(NA_DC):
        hit = dc == d
        for dr in range(NA_DR):
            tiles[dr] = jnp.where(hit, rpb_ref[(h * NA_DR + dr) * NA_DC + d], tiles[dr])
    for delta in range(NA_DELTAS):
        for j in range(NA_ROWS // 2):
            t = jnp.where(low, tiles[delta + 2 * j], tiles[delta + 2 * j + 1])
            o_ref[0, delta, :, j * 2 * GRID_W:(j + 1) * 2 * GRID_W] = jnp.where(in_win, t, MASKED)


def _na_bias(rpb):
    return pl.pallas_call(
        _na_bias_kernel,
        grid=(N_HEADS,),
        in_specs=[pl.BlockSpec(memory_space=pltpu.SMEM)],
        out_specs=pl.BlockSpec((1, NA_DELTAS, GRID_W, NA_KEYS), lambda h: (h, 0, 0, 0)),
        out_shape=jax.ShapeDtypeStruct((N_HEADS, NA_DELTAS, GRID_W, NA_KEYS), F32),
        compiler_params=_cparams("parallel"),
        name="na_bias",
    )(rpb.astype(F32).reshape(-1))


def _attn_na_kernel(q_ref, k_ref, v_ref, bias_ref, o_ref, v_scr, buf_a, buf_b):
    _fill_v_ones(v_scr, v_ref[0])

    def first_key_row(r):
        return jnp.clip(r - NA_ROWS // 2, 0, GRID_H - NA_ROWS)

    def scores(r):
        q0 = pl.multiple_of(r * GRID_W, GRID_W)
        k0 = pl.multiple_of(first_key_row(r) * GRID_W, GRID_W)
        q4 = jnp.concatenate(
            [q_ref[0, pl.ds(q0, GRID_W), g * HEAD_DIM:(g + 1) * HEAD_DIM] for g in range(GQA_GROUP)], axis=0)
        return _dot_nt(q4, k_ref[0, pl.ds(k0, NA_KEYS), :])

    def finish(r, s):
        r0 = first_key_row(r)
        delta = r0 - r + (NA_ROWS - 1)
        q0 = pl.multiple_of(r * GRID_W, GRID_W)
        k0 = pl.multiple_of(r0 * GRID_W, GRID_W)
        bias = jnp.concatenate([bias_ref[g, delta] for g in range(GQA_GROUP)], axis=0)
        o = _softmax_pv(s + bias, v_scr[pl.ds(k0, NA_KEYS), :])
        for g in range(GQA_GROUP):
            o_ref[0, pl.ds(q0, GRID_W), g * HEAD_DIM:(g + 1) * HEAD_DIM] = (
                o[g * GRID_W:(g + 1) * GRID_W].astype(o_ref.dtype))

    _pairs(GRID_H, scores, finish, buf_a, buf_b)


def _attn_na(qkv, bias):
    qkv_specs, out_spec = _gqa_specs()
    return pl.pallas_call(
        _attn_na_kernel,
        grid=(BATCH, N_KV_HEADS),
        in_specs=qkv_specs + [pl.BlockSpec((GQA_GROUP, NA_DELTAS, GRID_W, NA_KEYS), lambda b, h: (h, 0, 0, 0))],
        out_specs=out_spec,
        out_shape=jax.ShapeDtypeStruct((BATCH, SEQ, N_HEADS * HEAD_DIM), BF16),
        scratch_shapes=[pltpu.VMEM((SEQ, 2 * HEAD_DIM), BF16)] + _score_bufs(GQA_GROUP * GRID_W, NA_KEYS),
        compiler_params=_cparams("parallel", "parallel"),
        name="attn_na",
    )(qkv, qkv, qkv, bias)


XA_TM = 256


def _xattn_kernel(h_ref, att_ref, wout_ref, g_ref, wq_ref, kv_ref, wo_ref, o_ref):
    x = h_ref[0] + _dot(att_ref[0], wout_ref[...])
    xn = _rms(x, g_ref[...]).astype(BF16)
    q = _dot(xn, wq_ref[...]).astype(BF16)
    outs = []
    ones = jnp.ones((MEM_TOKENS, XA_HEAD_DIM), BF16)
    for hh in range(XA_HEADS):
        k = kv_ref[0, :, hh * XA_HEAD_DIM:(hh + 1) * XA_HEAD_DIM]
        v = kv_ref[0, :, XA_INNER + hh * XA_HEAD_DIM:XA_INNER + (hh + 1) * XA_HEAD_DIM]
        s = _dot_nt(q[:, hh * XA_HEAD_DIM:(hh + 1) * XA_HEAD_DIM], k)
        outs.append(_softmax_pv(s, jnp.concatenate([v, ones], axis=1)).astype(BF16))
    o_ref[0] = x + _dot(jnp.concatenate(outs, axis=1), wo_ref[...])


def _xattn(h, att, w_out, g, wq, kv, wo):
    full = lambda b, i: (0, 0)
    return pl.pallas_call(
        _xattn_kernel,
        grid=(BATCH, SEQ // XA_TM),
        in_specs=[
            pl.BlockSpec((1, XA_TM, D_MODEL), lambda b, i: (b, i, 0)),
            pl.BlockSpec((1, XA_TM, att.shape[-1]), lambda b, i: (b, i, 0)),
            pl.BlockSpec(w_out.shape, full),
            pl.BlockSpec((1, D_MODEL), full),
            pl.BlockSpec((D_MODEL, XA_INNER), full),
            pl.BlockSpec((1, MEM_TOKENS, 2 * XA_INNER), lambda b, i: (b, 0, 0)),
            pl.BlockSpec((XA_INNER, D_MODEL), full),
        ],
        out_specs=pl.BlockSpec((1, XA_TM, D_MODEL), lambda b, i: (b, i, 0)),
        out_shape=jax.ShapeDtypeStruct((BATCH, SEQ, D_MODEL), F32),
        compiler_params=_cparams("parallel", "parallel"),
        name="outproj_xattn",
    )(h, att, w_out, g.reshape(1, D_MODEL).astype(F32), wq, kv, wo)


RT_TM = 512
PREFIX_BLOCK = 256
TOPK_MAX_ITERS = 256


def _router_kernel(h_ref, g_ref, rw_ref, xn_ref, selpos_ref, selpos_t_ref, gate_t_ref, aff_scr):
    i = pl.program_id(1)

    @pl.when(i == 0)
    def _():
        aff_scr[...] = jnp.zeros_like(aff_scr)

    xn = _rms(h_ref[0], g_ref[...]).astype(BF16)
    xn_ref[0] = xn
    logits = _dot(xn, rw_ref[...])
    m = jnp.max(logits, axis=-1, keepdims=True)
    e = jnp.exp(logits - m)
    aff_scr[pl.ds(pl.multiple_of(i * RT_TM, RT_TM), RT_TM), :N_EXPERTS] = e / jnp.sum(e, axis=-1, keepdims=True)

    @pl.when(i == pl.num_programs(1) - 1)
    def _():
        a = jnp.transpose(aff_scr[...])[:N_EXPERTS]
        gate_t_ref[0] = a
        kf = jnp.float32(CAP)

        def count(mask):
            return jnp.sum(jnp.where(mask, 1.0, 0.0), axis=1, keepdims=True)

        def cond(c):
            it, _, _, _, done = c
            return jnp.logical_and(it < TOPK_MAX_ITERS, done == 0)

        def body(c):
            it, lo, hi, _, _ = c
            mid = 0.5 * (lo + hi)
            take = count(a >= mid) >= kf
            lo = jnp.where(take, mid, lo)
            hi = jnp.where(take, hi, mid)
            top = jnp.max(jnp.where(a < hi, a, -1.0), axis=1, keepdims=True)
            bot = jnp.min(jnp.where(a >= lo, a, 3.0), axis=1, keepdims=True)
            done = jnp.min(jnp.where(top == bot, 1, 0))
            return it + 1, lo, hi, top, done

        init = (jnp.int32(0), jnp.zeros((N_EXPERTS, 1), F32), jnp.full((N_EXPERTS, 1), 2.0, F32),
                jnp.zeros((N_EXPERTS, 1), F32), jnp.int32(0))
        _, _, _, kth, _ = lax.while_loop(cond, body, init)

        gt = a > kth
        eq = a == kth
        need = kf - count(gt)
        tok = lax.broadcasted_iota(jnp.int32, (N_EXPERTS, SEQ), 1).astype(F32)
        cut = jnp.zeros((N_EXPERTS, 1), F32)
        step = SEQ // 2
        while step >= 1:
            cand = cut + step
            cut = jnp.where(count(eq & (tok < cand)) < need, cand, cut)
            step //= 2
        sel = jnp.where(gt, 1.0, jnp.where(eq & (tok <= cut), 1.0, 0.0))

        r = lax.broadcasted_iota(jnp.int32, (PREFIX_BLOCK, PREFIX_BLOCK), 0)
        c = lax.broadcasted_iota(jnp.int32, (PREFIX_BLOCK, PREFIX_BLOCK), 1)
        utri = jnp.where(r < c, 1.0, 0.0).astype(BF16)
        off = jnp.zeros((N_EXPERTS, 1), F32)
        for j in range(SEQ // PREFIX_BLOCK):
            blk = sel[:, j * PREFIX_BLOCK:(j + 1) * PREFIX_BLOCK]
            pos = _dot(blk.astype(BF16), utri) + off
            selpos_t_ref[0, :, j * PREFIX_BLOCK:(j + 1) * PREFIX_BLOCK] = jnp.where(blk > 0.5, pos, -1.0)
            off = off + jnp.sum(blk, axis=1, keepdims=True)

        padded = jnp.concatenate([selpos_t_ref[0], jnp.zeros((LANES - N_EXPERTS, SEQ), F32)], axis=0)
        selpos_ref[0] = jnp.transpose(padded)[:, :N_EXPERTS]


def _router(h, g, rw):
    return pl.pallas_call(
        _router_kernel,
        grid=(BATCH, SEQ // RT_TM),
        in_specs=[
            pl.BlockSpec((1, RT_TM, D_MODEL), lambda b, i: (b, i, 0)),
            pl.BlockSpec((1, D_MODEL), lambda b, i: (0, 0)),
            pl.BlockSpec((D_MODEL, N_EXPERTS), lambda b, i: (0, 0)),
        ],
        out_specs=[
            pl.BlockSpec((1, RT_TM, D_MODEL), lambda b, i: (b, i, 0)),
            pl.BlockSpec((1, SEQ, N_EXPERTS), lambda b, i: (b, 0, 0)),
            pl.BlockSpec((1, N_EXPERTS, SEQ), lambda b, i: (b, 0, 0)),
            pl.BlockSpec((1, N_EXPERTS, SEQ), lambda b, i: (b, 0, 0)),
        ],
        out_shape=[
            jax.ShapeDtypeStruct((BATCH, SEQ, D_MODEL), BF16),
            jax.ShapeDtypeStruct((BATCH, SEQ, N_EXPERTS), F32),
            jax.ShapeDtypeStruct((BATCH, N_EXPERTS, SEQ), F32),
            jax.ShapeDtypeStruct((BATCH, N_EXPERTS, SEQ), F32),
        ],
        scratch_shapes=[pltpu.VMEM((SEQ, LANES), F32)],
        compiler_params=_cparams("parallel", "arbitrary"),
        name="router_topk",
    )(h, g.reshape(1, D_MODEL).astype(F32), rw)


def _gather_kernel(xn_ref, selpos_t_ref, gate_t_ref, xg_ref, gs_ref):
    e = pl.program_id(1)
    slot_of_token = selpos_t_ref[0, pl.ds(e, 1), :]
    gate_row = gate_t_ref[0, pl.ds(e, 1), :]
    slot = lax.broadcasted_iota(jnp.int32, (CAP, SEQ), 0).astype(F32)
    hit = slot == slot_of_token
    xg_ref[0, 0] = _dot(jnp.where(hit, 1.0, 0.0).astype(BF16), xn_ref[0]).astype(BF16)
    gs_ref[0, 0] = jnp.sum(jnp.where(hit, gate_row, 0.0), axis=1, keepdims=True)


def _gather(xn, selpos_t, gate_t):
    return pl.pallas_call(
        _gather_kernel,
        grid=(BATCH, N_EXPERTS),
        in_specs=[
            pl.BlockSpec((1, SEQ, D_MODEL), lambda b, e: (b, 0, 0)),
            pl.BlockSpec((1, N_EXPERTS, SEQ), lambda b, e: (b, 0, 0)),
            pl.BlockSpec((1, N_EXPERTS, SEQ), lambda b, e: (b, 0, 0)),
        ],
        out_specs=[
            pl.BlockSpec((1, 1, CAP, D_MODEL), lambda b, e: (e, b, 0, 0)),
            pl.BlockSpec((1, 1, CAP, 1), lambda b, e: (e, b, 0, 0)),
        ],
        out_shape=[
            jax.ShapeDtypeStruct((N_EXPERTS, BATCH, CAP, D_MODEL), BF16),
            jax.ShapeDtypeStruct((N_EXPERTS, BATCH, CAP, 1), F32),
        ],
        compiler_params=_cparams("parallel", "arbitrary"),
        name="moe_gather",
    )(xn, selpos_t, gate_t)


FF_TF = 256


def _expert_ffn_kernel(xg_ref, wg_ref, wu_ref, wd_ref, gs_ref, y_ref, acc_ref):
    f = pl.program_id(1)

    @pl.when(f == 0)
    def _():
        acc_ref[...] = jnp.zeros_like(acc_ref)

    x = xg_ref[0].reshape(BATCH * CAP, D_MODEL)
    gate = _dot(x, wg_ref[0, 0].astype(BF16))
    up = _dot(x, wu_ref[0, 0].astype(BF16))
    act = (gate * (1.0 / (1.0 + jnp.exp(-gate))) * up).astype(BF16)
    acc_ref[...] += _dot(act, wd_ref[0, 0].astype(BF16))

    @pl.when(f == pl.num_programs(1) - 1)
    def _():
        y = acc_ref[...] * gs_ref[0].reshape(BATCH * CAP, 1)
        y_ref[0] = y.astype(BF16).reshape(BATCH, CAP, D_MODEL)


def _expert_ffn(xg, w_gate, w_up, w_down, gs, layer):
    return pl.pallas_call(
        _expert_ffn_kernel,
        grid=(N_EXPERTS, EXPERT_FF // FF_TF),
        in_specs=[
            pl.BlockSpec((1, BATCH, CAP, D_MODEL), lambda e, f: (e, 0, 0, 0)),
            pl.BlockSpec((1, 1, D_MODEL, FF_TF), lambda e, f: (layer, e, 0, f)),
            pl.BlockSpec((1, 1, D_MODEL, FF_TF), lambda e, f: (layer, e, 0, f)),
            pl.BlockSpec((1, 1, FF_TF, D_MODEL), lambda e, f: (layer, e, f, 0)),
            pl.BlockSpec((1, BATCH, CAP, 1), lambda e, f: (e, 0, 0, 0)),
        ],
        out_specs=pl.BlockSpec((1, BATCH, CAP, D_MODEL), lambda e, f: (e, 0, 0, 0)),
        out_shape=jax.ShapeDtypeStruct((N_EXPERTS, BATCH, CAP, D_MODEL), BF16),
        scratch_shapes=[pltpu.VMEM((BATCH * CAP, D_MODEL), F32)],
        compiler_params=_cparams("parallel", "arbitrary"),
        name="moe_ffn",
    )(xg, w_gate, w_up, w_down, gs)


CB_TM = 512
CB_EXPERTS = 4


def _combine_kernel(h_ref, selpos_ref, y_ref, *rest):
    o_ref = rest[-1]
    c = pl.program_id(2)

    @pl.when(c == 0)
    def _():
        o_ref[...] = h_ref[...]

    sp = selpos_ref[0]
    lane = lax.broadcasted_iota(jnp.int32, sp.shape, 1)
    slot = lax.broadcasted_iota(jnp.int32, (CB_TM, CAP), 1).astype(F32)
    pieces = []
    for k in range(CB_EXPERTS):
        col = jnp.sum(jnp.where(lane == c * CB_EXPERTS + k, sp, 0.0), axis=1, keepdims=True)
        pieces.append(jnp.where(col == slot, 1.0, 0.0).astype(BF16))
    onehot = jnp.concatenate(pieces, axis=1)
    o_ref[0] += _dot(onehot, y_ref[...].reshape(CB_EXPERTS * CAP, D_MODEL))

    if len(rest) == 2:
        @pl.when(c == pl.num_programs(2) - 1)
        def _():
            o_ref[0] = _rms(o_ref[0], rest[0][...])


def _combine(h, selpos, y, final_g=None):
    in_specs = [
        pl.BlockSpec((1, CB_TM, D_MODEL), lambda b, i, c: (b, i, 0)),
        pl.BlockSpec((1, CB_TM, N_EXPERTS), lambda b, i, c: (b, i, 0)),
        pl.BlockSpec((CB_EXPERTS, 1, CAP, D_MODEL), lambda b, i, c: (c, b, 0, 0)),
    ]
    args = [h, selpos, y]
    if final_g is not None:
        in_specs.append(pl.BlockSpec((1, D_MODEL), lambda b, i, c: (0, 0)))
        args.append(final_g.reshape(1, D_MODEL).astype(F32))
    return pl.pallas_call(
        _combine_kernel,
        grid=(BATCH, SEQ // CB_TM, N_EXPERTS // CB_EXPERTS),
        in_specs=in_specs,
        out_specs=pl.BlockSpec((1, CB_TM, D_MODEL), lambda b, i, c: (b, i, 0)),
        out_shape=jax.ShapeDtypeStruct((BATCH, SEQ, D_MODEL), F32),
        compiler_params=_cparams("parallel", "parallel", "arbitrary"),
        name="moe_combine",
    )(*args)


def _angles(pos, dim, theta):
    inv = jnp.power(jnp.float32(theta), -(jnp.arange(0, dim, 2, dtype=jnp.float32) / dim))
    ang = pos.astype(jnp.float32)[:, None] * inv[None, :]
    return jnp.cos(ang), jnp.sin(ang)


def _rot_tables(groups, width=HEAD_DIM):
    cos_parts, sneg_parts, spos_parts = [], [], []
    used = 0
    for c, s in groups:
        z = jnp.zeros_like(s)
        cos_parts += [c, c]
        sneg_parts += [-s, z]
        spos_parts += [z, s]
        used += 2 * c.shape[1]
    rest = width - used
    if rest:
        cos_parts.append(jnp.ones((SEQ, rest), F32))
        sneg_parts.append(jnp.zeros((SEQ, rest), F32))
        spos_parts.append(jnp.zeros((SEQ, rest), F32))
    return (jnp.concatenate(cos_parts, axis=1), jnp.concatenate(sneg_parts, axis=1),
            jnp.concatenate(spos_parts, axis=1))


def _gqa_w_in(w_in, scale_q):
    if scale_q:
        col = jnp.concatenate([jnp.full((N_HEADS * HEAD_DIM,), HEAD_DIM ** -0.5 * LOG2E, F32),
                               jnp.ones((2 * N_KV_HEADS * HEAD_DIM,), F32)])
        w_in = w_in * col[None, :]
    return w_in.astype(BF16)


def _mixer_gqa_in(h2, g, w_in, scale_q):
    return _norm_matmul(h2, g, _gqa_w_in(w_in, scale_q), tm=1024, tn=512).reshape(BATCH, SEQ, GQA_IN)


def _mla_weights(w_in, w_uq):
    w_in = jnp.pad(w_in, ((0, 0), (0, MLA_LAT_PAD - w_in.shape[-1]))).astype(BF16)
    w_uq = jnp.pad((w_uq * (MLA_QK_DIM ** -0.5 * LOG2E)).reshape(MLA_Q_RANK, N_HEADS, MLA_QK_DIM),
                   ((0, 0), (0, 0), (0, MLA_Q_PAD - MLA_QK_DIM))).reshape(MLA_Q_RANK, N_HEADS * MLA_Q_PAD)
    return w_in, w_uq.astype(BF16)


def kernel(x, mem, norm_mix_g, norm_xa_g, norm_ffn_g, a_w_in, a_sink, a_w_out, b_w_in, b_q_norm_g, b_k_norm_g, b_w_out, c_w_in, c_q_lat_norm_g, c_kv_lat_norm_g, c_w_uq, c_w_ukv, c_w_out, d_w_in, d_rpb, d_w_out, mem_norm_g, xa_wq, xa_wkv, xa_wo, router_w, moe_w_gate, moe_w_up, moe_w_down, final_norm_g):
    t = BATCH * SEQ
    pos = jnp.arange(SEQ)
    tab_a = _rot_tables([_angles(pos, ROT_DIM, ROPE_THETA)])
    tab_b = _rot_tables([_angles(pos // GRID_W, HEAD_DIM // 2, AXIAL_THETA),
                         _angles(pos % GRID_W, HEAD_DIM // 2, AXIAL_THETA)])
    tab_c = _rot_tables([_angles(pos, MLA_ROPE_DIM, ROPE_THETA)])
    mem2 = mem.reshape(BATCH * MEM_TOKENS, D_MODEL)

    h = x
    for i in range(DEPTH):
        m, j = i % N_MIXERS, i // N_MIXERS
        h2 = h.reshape(t, D_MODEL)
        if m == 0:
            qkv = _mixer_gqa_in(h2, norm_mix_g[i], a_w_in[j], True)
            att = _attn_window(qkv, a_sink[j] * LOG2E, tab_a)
            w_out = a_w_out[j]
        elif m == 1:
            qkv = _mixer_gqa_in(h2, norm_mix_g[i], b_w_in[j], False)
            att = _attn_axial(qkv, b_q_norm_g[j] * (HEAD_DIM ** -0.5 * LOG2E), b_k_norm_g[j], tab_b)
            w_out = b_w_out[j]
        elif m == 2:
            w_in, w_uq = _mla_weights(c_w_in[j], c_w_uq[j])
            lat = _norm_matmul(h2, norm_mix_g[i], w_in, tm=1024, tn=MLA_LAT_PAD // 3)
            qfull = _norm_matmul(lat, c_q_lat_norm_g[j], w_uq, col_block=0, tm=1024, tn=1024)
            kvfull = _norm_matmul(lat, c_kv_lat_norm_g[j], c_w_ukv[j].astype(BF16), col_block=1, tm=1024, tn=1024)
            att = _attn_mla(qfull.reshape(BATCH, SEQ, -1), kvfull.reshape(BATCH, SEQ, -1),
                            lat.reshape(BATCH, SEQ, MLA_LAT_PAD), tab_c)
            w_out = c_w_out[j]
        else:
            qkv = _mixer_gqa_in(h2, norm_mix_g[i], d_w_in[j], True)
            att = _attn_na(qkv, _na_bias(d_rpb[j] * LOG2E))
            w_out = d_w_out[j]

        kv = _norm_matmul(mem2, mem_norm_g, xa_wkv[i].astype(BF16), tm=512, tn=512)
        h = _xattn(h, att, w_out.astype(BF16), norm_xa_g[i], (xa_wq[i] * (XA_HEAD_DIM ** -0.5 * LOG2E)).astype(BF16),
                   kv.reshape(BATCH, MEM_TOKENS, 2 * XA_INNER), xa_wo[i].astype(BF16))

        xn, selpos, selpos_t, gate_t = _router(h, norm_ffn_g[i], router_w[i].astype(BF16))
        xg, gs = _gather(xn, selpos_t, gate_t)
        y = _expert_ffn(xg, moe_w_gate, moe_w_up, moe_w_down, gs, i)
        h = _combine(h, selpos, y, final_norm_g if i == DEPTH - 1 else None)

    return h
```

```python
import jax
import jax.numpy as jnp
from jax import lax
from jax.experimental import pallas as pl
from jax.experimental.pallas import tpu as pltpu

F32 = jnp.float32
BF16 = jnp.bfloat16

D_MODEL = 2048
BATCH = 4
SEQ = 2048
DEPTH = 4
N_MIXERS = 4
HEAD_DIM = 128
N_HEADS = 16
N_KV_HEADS = 4
GQA_GROUP = N_HEADS // N_KV_HEADS
GQA_IN = (N_HEADS + 2 * N_KV_HEADS) * HEAD_DIM
Q_BLOCK = 128
WINDOW = 128
ROPE_THETA = 500000.0
ROT_DIM = HEAD_DIM // 4
AXIAL_THETA = 10000.0
GRID_W = 64
GRID_H = SEQ // GRID_W
MLA_Q_RANK = 512
MLA_KV_RANK = 512
MLA_NOPE_DIM = 128
MLA_ROPE_DIM = 64
MLA_V_DIM = 128
MLA_QK_DIM = MLA_NOPE_DIM + MLA_ROPE_DIM
MLA_Q_PAD = 256
MLA_LAT_PAD = 1152
NA_ROWS = 8
NA_COLS = 16
MEM_TOKENS = 256
XA_HEADS = 4
XA_HEAD_DIM = 128
XA_INNER = XA_HEADS * XA_HEAD_DIM
N_EXPERTS = 16
EC_CAPACITY = 2
EXPERT_FF = 1024
CAP = EC_CAPACITY * SEQ // N_EXPERTS
EPS = 1e-6
MASKED = -1e30
LOG2E = 1.4426950408889634
LANES = 128

VMEM_LIMIT_BYTES = 56 * 1024 * 1024


def _cparams(*sem):
    return pltpu.CompilerParams(dimension_semantics=sem, vmem_limit_bytes=VMEM_LIMIT_BYTES)


def _dot_nt(a, b):
    return lax.dot_general(a, b, (((1,), (1,)), ((), ())), preferred_element_type=F32)


def _dot(a, b):
    return jnp.dot(a, b, preferred_element_type=F32)


def _rms(x, g):
    ms = jnp.mean(x * x, axis=-1, keepdims=True)
    return x * lax.rsqrt(ms + EPS) * g


def _rot(x, cos, sneg, spos, shift):
    n = x.shape[-1]
    return x * cos + pltpu.roll(x, n - shift, 1) * sneg + pltpu.roll(x, shift, 1) * spos


def _softmax_pv(s, v_ones, sink=None):
    dv = v_ones.shape[1] // 2
    m = jnp.max(s, axis=-1, keepdims=True)
    if sink is not None:
        m = jnp.maximum(m, sink)
    r = _dot(jnp.exp2(s - m).astype(BF16), v_ones)
    den = r[:, dv:]
    if sink is not None:
        den = den + jnp.exp2(sink - m)
    return r[:, :dv] * (1.0 / den)


def _fill_v_ones(v_scr, v):
    dv = v.shape[1]
    v_scr[:, :dv] = v
    v_scr[:, dv:] = jnp.ones_like(v)


def _pairs(n_blocks, scores, finish, buf_a, buf_b):
    buf_b[...] = scores(0)

    def pair(i, carry):
        buf_a[...] = scores(2 * i + 1)
        finish(2 * i, buf_b[...])
        buf_b[...] = scores(jnp.minimum(2 * i + 2, n_blocks - 1))
        finish(2 * i + 1, buf_a[...])
        return carry

    lax.fori_loop(0, n_blocks // 2, pair, 0)


def _score_bufs(rows, keys):
    return [pltpu.VMEM((rows, keys), F32), pltpu.VMEM((rows, keys), F32)]


def _norm_matmul_kernel(x_ref, g_ref, w_ref, o_ref, xn_ref):
    @pl.when(pl.program_id(1) == 0)
    def _():
        xn_ref[...] = _rms(x_ref[...].astype(F32), g_ref[...]).astype(BF16)

    res = _dot(xn_ref[...], w_ref[...])
    if len(o_ref.shape) == 3:
        head_w = o_ref.shape[2]
        for hh in range(o_ref.shape[0]):
            o_ref[hh] = res[:, hh * head_w:(hh + 1) * head_w].astype(o_ref.dtype)
    else:
        o_ref[...] = res.astype(o_ref.dtype)


def _norm_matmul(x, g, w, *, col_block=0, tm, tn, head_w=None):
    t = x.shape[0]
    k, n = w.shape
    if head_w is None:
        out_spec = pl.BlockSpec((tm, tn), lambda i, j: (i, j))
        out_shape = jax.ShapeDtypeStruct((t, n), BF16)
    else:
        out_spec = pl.BlockSpec((tn // head_w, tm, head_w), lambda i, j: (j, i, 0))
        out_shape = jax.ShapeDtypeStruct((n // head_w, t, head_w), BF16)
    return pl.pallas_call(
        _norm_matmul_kernel,
        grid=(t // tm, n // tn),
        in_specs=[
            pl.BlockSpec((tm, k), lambda i, j: (i, col_block)),
            pl.BlockSpec((1, k), lambda i, j: (0, 0)),
            pl.BlockSpec((k, tn), lambda i, j: (0, j)),
        ],
        out_specs=out_spec,
        out_shape=out_shape,
        scratch_shapes=[pltpu.VMEM((tm, k), BF16)],
        compiler_params=_cparams("parallel", "arbitrary"),
        name="norm_matmul",
    )(x, g.reshape(1, k).astype(F32), w)


def _attn_window_kernel(sink_ref, q_ref, k_ref, v_ref, cos_ref, sneg_ref, spos_ref, o_ref, k_scr, v_scr,
                        buf_a, buf_b):
    kvh = pl.program_id(1)
    half = ROT_DIM // 2
    nb = SEQ // Q_BLOCK
    rows, keys = GQA_GROUP * Q_BLOCK, 3 * Q_BLOCK
    k_scr[...] = _rot(k_ref[0, 0].astype(F32), cos_ref[...], sneg_ref[...], spos_ref[...], half).astype(BF16)
    _fill_v_ones(v_scr, v_ref[0, 0])
    r = lax.broadcasted_iota(jnp.int32, (rows, keys), 0)
    c = lax.broadcasted_iota(jnp.int32, (rows, keys), 1)
    base = (r & (Q_BLOCK - 1)) - c
    sink = jnp.concatenate(
        [jnp.full((Q_BLOCK, 1), sink_ref[kvh * GQA_GROUP + g], F32) for g in range(GQA_GROUP)], axis=0)

    def window_start(n):
        return pl.multiple_of(jnp.clip(n - 1, 0, nb - 3) * Q_BLOCK, Q_BLOCK)

    def scores(n):
        q0 = pl.multiple_of(n * Q_BLOCK, Q_BLOCK)
        cos = cos_ref[pl.ds(q0, Q_BLOCK), :]
        sneg = sneg_ref[pl.ds(q0, Q_BLOCK), :]
        spos = spos_ref[pl.ds(q0, Q_BLOCK), :]
        q4 = jnp.concatenate(
            [_rot(q_ref[g, 0, pl.ds(q0, Q_BLOCK), :].astype(F32),
                  cos, sneg, spos, half).astype(BF16) for g in range(GQA_GROUP)], axis=0)
        return _dot_nt(q4, k_scr[pl.ds(window_start(n), keys), :])

    def finish(n, s):
        q0 = pl.multiple_of(n * Q_BLOCK, Q_BLOCK)
        k0 = window_start(n)
        s = jnp.where(jnp.abs(base + (q0 - k0)) <= WINDOW, s, MASKED)
        o = _softmax_pv(s, v_scr[pl.ds(k0, keys), :], sink)
        for g in range(GQA_GROUP):
            o_ref[g, 0, pl.ds(q0, Q_BLOCK), :] = (
                o[g * Q_BLOCK:(g + 1) * Q_BLOCK].astype(o_ref.dtype))

    _pairs(nb, scores, finish, buf_a, buf_b)


def _gqa_specs():
    head = (1, 1, SEQ, HEAD_DIM)
    group = (GQA_GROUP, 1, SEQ, HEAD_DIM)
    return [
        pl.BlockSpec(group, lambda b, h: (h, b, 0, 0)),
        pl.BlockSpec(head, lambda b, h: (N_HEADS + h, b, 0, 0)),
        pl.BlockSpec(head, lambda b, h: (N_HEADS + N_KV_HEADS + h, b, 0, 0)),
    ], pl.BlockSpec(group, lambda b, h: (h, b, 0, 0))


_ATT_SHAPE = jax.ShapeDtypeStruct((N_HEADS, BATCH, SEQ, HEAD_DIM), BF16)


def _attn_window(qkv, sink, tables):
    cos, sneg, spos = tables
    qkv_specs, out_spec = _gqa_specs()
    tab = pl.BlockSpec((SEQ, HEAD_DIM), lambda b, h: (0, 0))
    return pl.pallas_call(
        _attn_window_kernel,
        grid=(BATCH, N_KV_HEADS),
        in_specs=[pl.BlockSpec(memory_space=pltpu.SMEM)] + qkv_specs + [tab, tab, tab],
        out_specs=out_spec,
        out_shape=_ATT_SHAPE,
        scratch_shapes=[pltpu.VMEM((SEQ, HEAD_DIM), BF16), pltpu.VMEM((SEQ, 2 * HEAD_DIM), BF16)]
        + _score_bufs(GQA_GROUP * Q_BLOCK, 3 * Q_BLOCK),
        compiler_params=_cparams("parallel", "parallel"),
        name="attn_window",
    )(sink.astype(F32), qkv, qkv, qkv, cos, sneg, spos)


AX_TQ = 128


def _attn_axial_kernel(q_ref, k_ref, v_ref, qg_ref, kg_ref, cos_ref, sneg_ref, spos_ref, o_ref, k_scr, v_scr,
                       buf_a, buf_b):
    quarter = HEAD_DIM // 4
    kn = _rms(k_ref[0, 0].astype(F32), kg_ref[...])
    k_scr[...] = _rot(kn, cos_ref[...], sneg_ref[...], spos_ref[...], quarter).astype(BF16)
    _fill_v_ones(v_scr, v_ref[0, 0])

    def scores(n):
        q0 = pl.multiple_of(n * AX_TQ, AX_TQ)
        cos = cos_ref[pl.ds(q0, AX_TQ), :]
        sneg = sneg_ref[pl.ds(q0, AX_TQ), :]
        spos = spos_ref[pl.ds(q0, AX_TQ), :]
        q4 = jnp.concatenate(
            [_rot(_rms(q_ref[g, 0, pl.ds(q0, AX_TQ), :].astype(F32), qg_ref[...]),
                  cos, sneg, spos, quarter).astype(BF16) for g in range(GQA_GROUP)], axis=0)
        return _dot_nt(q4, k_scr[...])

    def finish(n, s):
        q0 = pl.multiple_of(n * AX_TQ, AX_TQ)
        o = _softmax_pv(s, v_scr[...])
        for g in range(GQA_GROUP):
            o_ref[g, 0, pl.ds(q0, AX_TQ), :] = (
                o[g * AX_TQ:(g + 1) * AX_TQ].astype(o_ref.dtype))

    _pairs(SEQ // AX_TQ, scores, finish, buf_a, buf_b)


def _attn_axial(qkv, qg, kg, tables):
    cos, sneg, spos = tables
    qkv_specs, out_spec = _gqa_specs()
    tab = pl.BlockSpec((SEQ, HEAD_DIM), lambda b, h: (0, 0))
    gain = pl.BlockSpec((1, HEAD_DIM), lambda b, h: (0, 0))
    return pl.pallas_call(
        _attn_axial_kernel,
        grid=(BATCH, N_KV_HEADS),
        in_specs=qkv_specs + [gain, gain, tab, tab, tab],
        out_specs=out_spec,
        out_shape=_ATT_SHAPE,
        scratch_shapes=[pltpu.VMEM((SEQ, HEAD_DIM), BF16), pltpu.VMEM((SEQ, 2 * HEAD_DIM), BF16)]
        + _score_bufs(GQA_GROUP * AX_TQ, SEQ),
        compiler_params=_cparams("parallel", "parallel"),
        name="attn_axial",
    )(qkv, qkv, qkv, qg.reshape(1, HEAD_DIM).astype(F32), kg.reshape(1, HEAD_DIM).astype(F32), cos, sneg, spos)


MLA_TQ = 512


def _attn_mla_kernel(q_ref, kn_ref, v_ref, kr_ref, cos_ref, sneg_ref, spos_ref, o_ref, k_scr, v_scr, buf_a, buf_b):
    half = MLA_ROPE_DIM // 2
    k_scr[:, :MLA_NOPE_DIM] = kn_ref[0, 0]
    k_scr[:, MLA_NOPE_DIM:] = _rot(kr_ref[0].astype(F32), cos_ref[...], sneg_ref[...], spos_ref[...],
                                   half).astype(BF16)
    _fill_v_ones(v_scr, v_ref[0, 0])

    def scores(n):
        q0 = pl.multiple_of(n * MLA_TQ, MLA_TQ)
        cos = cos_ref[pl.ds(q0, MLA_TQ), :]
        sneg = sneg_ref[pl.ds(q0, MLA_TQ), :]
        spos = spos_ref[pl.ds(q0, MLA_TQ), :]
        qn = q_ref[0, 0, pl.ds(q0, MLA_TQ), :MLA_NOPE_DIM]
        qr = _rot(q_ref[0, 0, pl.ds(q0, MLA_TQ), MLA_NOPE_DIM:].astype(F32), cos, sneg, spos, half).astype(BF16)
        return _dot_nt(jnp.concatenate([qn, qr], axis=1), k_scr[...])

    def finish(n, s):
        q0 = pl.multiple_of(n * MLA_TQ, MLA_TQ)
        o_ref[0, 0, pl.ds(q0, MLA_TQ), :] = _softmax_pv(s, v_scr[...]).astype(o_ref.dtype)

    _pairs(SEQ // MLA_TQ, scores, finish, buf_a, buf_b)


def _attn_mla(qfull, kvfull, k_rope, tables):
    cos, sneg, spos = tables
    tab = pl.BlockSpec((SEQ, HEAD_DIM), lambda b, h: (0, 0))
    return pl.pallas_call(
        _attn_mla_kernel,
        grid=(BATCH, N_HEADS),
        in_specs=[
            pl.BlockSpec((1, 1, SEQ, MLA_Q_PAD), lambda b, h: (h, b, 0, 0)),
            pl.BlockSpec((1, 1, SEQ, MLA_NOPE_DIM), lambda b, h: (2 * h, b, 0, 0)),
            pl.BlockSpec((1, 1, SEQ, MLA_V_DIM), lambda b, h: (2 * h + 1, b, 0, 0)),
            pl.BlockSpec((1, SEQ, HEAD_DIM), lambda b, h: (b, 0, 0)),
            tab, tab, tab,
        ],
        out_specs=pl.BlockSpec((1, 1, SEQ, MLA_V_DIM), lambda b, h: (h, b, 0, 0)),
        out_shape=_ATT_SHAPE,
        scratch_shapes=[pltpu.VMEM((SEQ, MLA_Q_PAD), BF16), pltpu.VMEM((SEQ, 2 * MLA_V_DIM), BF16)]
        + _score_bufs(MLA_TQ, SEQ),
        compiler_params=_cparams("parallel", "parallel"),
        name="attn_mla",
    )(qfull, kvfull, kvfull, k_rope, cos, sneg, spos)


NA_DELTAS = NA_ROWS
NA_KEYS = NA_ROWS * GRID_W
NA_DR = 2 * NA_ROWS - 1
NA_DC = 2 * NA_COLS - 1


def _na_bias_kernel(rpb_ref, o_ref):
    h = pl.program_id(0)
    shape = (GRID_W, 2 * GRID_W)
    qc = lax.broadcasted_iota(jnp.int32, shape, 0)
    lane = lax.broadcasted_iota(jnp.int32, shape, 1)
    kc = jnp.where(lane >= GRID_W, lane - GRID_W, lane)
    dc = kc - qc + (NA_COLS - 1)
    c_start = jnp.clip(qc - NA_COLS // 2, 0, GRID_W - NA_COLS)
    in_win = (kc >= c_start) & (kc < c_start + NA_COLS)
    low = lane < GRID_W
    tiles = [jnp.zeros(shape, F32) for _ in range(NA_DR)]
    for d in range(NA_DC):
        hit = dc == d
        for dr in range(NA_DR):
            tiles[dr] = jnp.where(hit, rpb_ref[(h * NA_DR + dr) * NA_DC + d], tiles[dr])
    for delta in range(NA_DELTAS):
        for j in range(NA_ROWS // 2):
            t = jnp.where(low, tiles[delta + 2 * j], tiles[delta + 2 * j + 1])
            o_ref[0, delta, :, j * 2 * GRID_W:(j + 1) * 2 * GRID_W] = jnp.where(in_win, t, MASKED)


def _na_bias(rpb):
    return pl.pallas_call(
        _na_bias_kernel,
        grid=(N_HEADS,),
        in_specs=[pl.BlockSpec(memory_space=pltpu.SMEM)],
        out_specs=pl.BlockSpec((1, NA_DELTAS, GRID_W, NA_KEYS), lambda h: (h, 0, 0, 0)),
        out_shape=jax.ShapeDtypeStruct((N_HEADS, NA_DELTAS, GRID_W, NA_KEYS), F32),
        compiler_params=_cparams("parallel"),
        name="na_bias",
    )(rpb.astype(F32).reshape(-1))


def _attn_na_kernel(q_ref, k_ref, v_ref, bias_ref, o_ref, v_scr, buf_a, buf_b):
    _fill_v_ones(v_scr, v_ref[0, 0])

    def first_key_row(r):
        return jnp.clip(r - NA_ROWS // 2, 0, GRID_H - NA_ROWS)

    def scores(r):
        q0 = pl.multiple_of(r * GRID_W, GRID_W)
        k0 = pl.multiple_of(first_key_row(r) * GRID_W, GRID_W)
        q4 = jnp.concatenate(
            [q_ref[g, 0, pl.ds(q0, GRID_W), :] for g in range(GQA_GROUP)], axis=0)
        return _dot_nt(q4, k_ref[0, 0, pl.ds(k0, NA_KEYS), :])

    def finish(r, s):
        r0 = first_key_row(r)
        delta = r0 - r + (NA_ROWS - 1)
        q0 = pl.multiple_of(r * GRID_W, GRID_W)
        k0 = pl.multiple_of(r0 * GRID_W, GRID_W)
        bias = jnp.concatenate([bias_ref[g, delta] for g in range(GQA_GROUP)], axis=0)
        o = _softmax_pv(s + bias, v_scr[pl.ds(k0, NA_KEYS), :])
        for g in range(GQA_GROUP):
            o_ref[g, 0, pl.ds(q0, GRID_W), :] = (
                o[g * GRID_W:(g + 1) * GRID_W].astype(o_ref.dtype))

    _pairs(GRID_H, scores, finish, buf_a, buf_b)


def _attn_na(qkv, bias):
    qkv_specs, out_spec = _gqa_specs()
    return pl.pallas_call(
        _attn_na_kernel,
        grid=(BATCH, N_KV_HEADS),
        in_specs=qkv_specs + [pl.BlockSpec((GQA_GROUP, NA_DELTAS, GRID_W, NA_KEYS), lambda b, h: (h, 0, 0, 0))],
        out_specs=out_spec,
        out_shape=_ATT_SHAPE,
        scratch_shapes=[pltpu.VMEM((SEQ, 2 * HEAD_DIM), BF16)] + _score_bufs(GQA_GROUP * GRID_W, NA_KEYS),
        compiler_params=_cparams("parallel", "parallel"),
        name="attn_na",
    )(qkv, qkv, qkv, bias)


XA_TM = 256


def _xattn_kernel(h_ref, att_ref, wout_ref, g_ref, wq_ref, kv_ref, wo_ref, o_ref):
    att = jnp.concatenate([att_ref[hh, 0] for hh in range(N_HEADS)], axis=1)
    x = h_ref[0] + _dot(att, wout_ref[...])
    xn = _rms(x, g_ref[...]).astype(BF16)
    q = _dot(xn, wq_ref[...]).astype(BF16)
    outs = []
    ones = jnp.ones((MEM_TOKENS, XA_HEAD_DIM), BF16)
    for hh in range(XA_HEADS):
        k = kv_ref[0, :, hh * XA_HEAD_DIM:(hh + 1) * XA_HEAD_DIM]
        v = kv_ref[0, :, XA_INNER + hh * XA_HEAD_DIM:XA_INNER + (hh + 1) * XA_HEAD_DIM]
        s = _dot_nt(q[:, hh * XA_HEAD_DIM:(hh + 1) * XA_HEAD_DIM], k)
        outs.append(_softmax_pv(s, jnp.concatenate([v, ones], axis=1)).astype(BF16))
    o_ref[0] = x + _dot(jnp.concatenate(outs, axis=1), wo_ref[...])


def _xattn(h, att, w_out, g, wq, kv, wo):
    full = lambda b, i: (0, 0)
    return pl.pallas_call(
        _xattn_kernel,
        grid=(BATCH, SEQ // XA_TM),
        in_specs=[
            pl.BlockSpec((1, XA_TM, D_MODEL), lambda b, i: (b, i, 0)),
            pl.BlockSpec((N_HEADS, 1, XA_TM, HEAD_DIM), lambda b, i: (0, b, i, 0)),
            pl.BlockSpec(w_out.shape, full),
            pl.BlockSpec((1, D_MODEL), full),
            pl.BlockSpec((D_MODEL, XA_INNER), full),
            pl.BlockSpec((1, MEM_TOKENS, 2 * XA_INNER), lambda b, i: (b, 0, 0)),
            pl.BlockSpec((XA_INNER, D_MODEL), full),
        ],
        out_specs=pl.BlockSpec((1, XA_TM, D_MODEL), lambda b, i: (b, i, 0)),
        out_shape=jax.ShapeDtypeStruct((BATCH, SEQ, D_MODEL), F32),
        compiler_params=_cparams("parallel", "parallel"),
        name="outproj_xattn",
    )(h, att, w_out, g.reshape(1, D_MODEL).astype(F32), wq, kv, wo)


RT_TM = 512
PREFIX_BLOCK = 256
TOPK_MAX_ITERS = 256


def _router_kernel(h_ref, g_ref, rw_ref, xn_ref, selpos_ref, selpos_t_ref, gate_t_ref, aff_scr):
    i = pl.program_id(1)

    @pl.when(i == 0)
    def _():
        aff_scr[...] = jnp.zeros_like(aff_scr)

    xn = _rms(h_ref[0], g_ref[...]).astype(BF16)
    xn_ref[0] = xn
    logits = _dot(xn, rw_ref[...])
    m = jnp.max(logits, axis=-1, keepdims=True)
    e = jnp.exp(logits - m)
    aff_scr[pl.ds(pl.multiple_of(i * RT_TM, RT_TM), RT_TM), :N_EXPERTS] = e / jnp.sum(e, axis=-1, keepdims=True)

    @pl.when(i == pl.num_programs(1) - 1)
    def _():
        a = jnp.transpose(aff_scr[...])[:N_EXPERTS]
        gate_t_ref[0] = a
        kf = jnp.float32(CAP)

        def count(mask):
            return jnp.sum(jnp.where(mask, 1.0, 0.0), axis=1, keepdims=True)

        def cond(c):
            it, _, _, _, done = c
            return jnp.logical_and(it < TOPK_MAX_ITERS, done == 0)

        def body(c):
            it, lo, hi, _, _ = c
            mid = 0.5 * (lo + hi)
            take = count(a >= mid) >= kf
            lo = jnp.where(take, mid, lo)
            hi = jnp.where(take, hi, mid)
            top = jnp.max(jnp.where(a < hi, a, -1.0), axis=1, keepdims=True)
            bot = jnp.min(jnp.where(a >= lo, a, 3.0), axis=1, keepdims=True)
            done = jnp.min(jnp.where(top == bot, 1, 0))
            return it + 1, lo, hi, top, done

        init = (jnp.int32(0), jnp.zeros((N_EXPERTS, 1), F32), jnp.full((N_EXPERTS, 1), 2.0, F32),
                jnp.zeros((N_EXPERTS, 1), F32), jnp.int32(0))
        _, _, _, kth, _ = lax.while_loop(cond, body, init)

        gt = a > kth
        eq = a == kth
        need = kf - count(gt)
        tok = lax.broadcasted_iota(jnp.int32, (N_EXPERTS, SEQ), 1).astype(F32)
        cut = jnp.zeros((N_EXPERTS, 1), F32)
        step = SEQ // 2
        while step >= 1:
            cand = cut + step
            cut = jnp.where(count(eq & (tok < cand)) < need, cand, cut)
            step //= 2
        sel = jnp.where(gt, 1.0, jnp.where(eq & (tok <= cut), 1.0, 0.0))

        r = lax.broadcasted_iota(jnp.int32, (PREFIX_BLOCK, PREFIX_BLOCK), 0)
        c = lax.broadcasted_iota(jnp.int32, (PREFIX_BLOCK, PREFIX_BLOCK), 1)
        utri = jnp.where(r < c, 1.0, 0.0).astype(BF16)
        off = jnp.zeros((N_EXPERTS, 1), F32)
        for j in range(SEQ // PREFIX_BLOCK):
            blk = sel[:, j * PREFIX_BLOCK:(j + 1) * PREFIX_BLOCK]
            pos = _dot(blk.astype(BF16), utri) + off
            selpos_t_ref[0, :, j * PREFIX_BLOCK:(j + 1) * PREFIX_BLOCK] = jnp.where(blk > 0.5, pos, -1.0)
            off = off + jnp.sum(blk, axis=1, keepdims=True)

        padded = jnp.concatenate([selpos_t_ref[0], jnp.zeros((LANES - N_EXPERTS, SEQ), F32)], axis=0)
        selpos_ref[0] = jnp.transpose(padded)[:, :N_EXPERTS]


def _router(h, g, rw):
    return pl.pallas_call(
        _router_kernel,
        grid=(BATCH, SEQ // RT_TM),
        in_specs=[
            pl.BlockSpec((1, RT_TM, D_MODEL), lambda b, i: (b, i, 0)),
            pl.BlockSpec((1, D_MODEL), lambda b, i: (0, 0)),
            pl.BlockSpec((D_MODEL, N_EXPERTS), lambda b, i: (0, 0)),
        ],
        out_specs=[
            pl.BlockSpec((1, RT_TM, D_MODEL), lambda b, i: (b, i, 0)),
            pl.BlockSpec((1, SEQ, N_EXPERTS), lambda b, i: (b, 0, 0)),
            pl.BlockSpec((1, N_EXPERTS, SEQ), lambda b, i: (b, 0, 0)),
            pl.BlockSpec((1, N_EXPERTS, SEQ), lambda b, i: (b, 0, 0)),
        ],
        out_shape=[
            jax.ShapeDtypeStruct((BATCH, SEQ, D_MODEL), BF16),
            jax.ShapeDtypeStruct((BATCH, SEQ, N_EXPERTS), F32),
            jax.ShapeDtypeStruct((BATCH, N_EXPERTS, SEQ), F32),
            jax.ShapeDtypeStruct((BATCH, N_EXPERTS, SEQ), F32),
        ],
        scratch_shapes=[pltpu.VMEM((SEQ, LANES), F32)],
        compiler_params=_cparams("parallel", "arbitrary"),
        name="router_topk",
    )(h, g.reshape(1, D_MODEL).astype(F32), rw)


def _gather_kernel(xn_ref, selpos_t_ref, gate_t_ref, xg_ref, gs_ref):
    e = pl.program_id(1)
    slot_of_token = selpos_t_ref[0, pl.ds(e, 1), :]
    gate_row = gate_t_ref[0, pl.ds(e, 1), :]
    slot = lax.broadcasted_iota(jnp.int32, (CAP, SEQ), 0).astype(F32)
    hit = slot == slot_of_token
    xg_ref[0, 0] = _dot(jnp.where(hit, 1.0, 0.0).astype(BF16), xn_ref[0]).astype(BF16)
    gs_ref[0, 0] = jnp.sum(jnp.where(hit, gate_row, 0.0), axis=1, keepdims=True)


def _gather(xn, selpos_t, gate_t):
    return pl.pallas_call(
        _gather_kernel,
        grid=(BATCH, N_EXPERTS),
        in_specs=[
            pl.BlockSpec((1, SEQ, D_MODEL), lambda b, e: (b, 0, 0)),
            pl.BlockSpec((1, N_EXPERTS, SEQ), lambda b, e: (b, 0, 0)),
            pl.BlockSpec((1, N_EXPERTS, SEQ), lambda b, e: (b, 0, 0)),
        ],
        out_specs=[
            pl.BlockSpec((1, 1, CAP, D_MODEL), lambda b, e: (e, b, 0, 0)),
            pl.BlockSpec((1, 1, CAP, 1), lambda b, e: (e, b, 0, 0)),
        ],
        out_shape=[
            jax.ShapeDtypeStruct((N_EXPERTS, BATCH, CAP, D_MODEL), BF16),
            jax.ShapeDtypeStruct((N_EXPERTS, BATCH, CAP, 1), F32),
        ],
        compiler_params=_cparams("parallel", "arbitrary"),
        name="moe_gather",
    )(xn, selpos_t, gate_t)


FF_TF = 256


def _expert_ffn_kernel(xg_ref, wg_ref, wu_ref, wd_ref, gs_ref, y_ref, acc_ref):
    f = pl.program_id(1)

    @pl.when(f == 0)
    def _():
        acc_ref[...] = jnp.zeros_like(acc_ref)

    x = xg_ref[0].reshape(BATCH * CAP, D_MODEL)
    gate = _dot(x, wg_ref[0, 0].astype(BF16))
    up = _dot(x, wu_ref[0, 0].astype(BF16))
    act = (gate * (1.0 / (1.0 + jnp.exp(-gate))) * up).astype(BF16)
    acc_ref[...] += _dot(act, wd_ref[0, 0].astype(BF16))

    @pl.when(f == pl.num_programs(1) - 1)
    def _():
        y = acc_ref[...] * gs_ref[0].reshape(BATCH * CAP, 1)
        y_ref[0] = y.astype(BF16).reshape(BATCH, CAP, D_MODEL)


def _expert_ffn(xg, w_gate, w_up, w_down, gs, layer):
    return pl.pallas_call(
        _expert_ffn_kernel,
        grid=(N_EXPERTS, EXPERT_FF // FF_TF),
        in_specs=[
            pl.BlockSpec((1, BATCH, CAP, D_MODEL), lambda e, f: (e, 0, 0, 0)),
            pl.BlockSpec((1, 1, D_MODEL, FF_TF), lambda e, f: (layer, e, 0, f)),
            pl.BlockSpec((1, 1, D_MODEL, FF_TF), lambda e, f: (layer, e, 0, f)),
            pl.BlockSpec((1, 1, FF_TF, D_MODEL), lambda e, f: (layer, e, f, 0)),
            pl.BlockSpec((1, BATCH, CAP, 1), lambda e, f: (e, 0, 0, 0)),
        ],
        out_specs=pl.BlockSpec((1, BATCH, CAP, D_MODEL), lambda e, f: (e, 0, 0, 0)),
        out_shape=jax.ShapeDtypeStruct((N_EXPERTS, BATCH, CAP, D_MODEL), BF16),
        scratch_shapes=[pltpu.VMEM((BATCH * CAP, D_MODEL), F32)],
        compiler_params=_cparams("parallel", "arbitrary"),
        name="moe_ffn",
    )(xg, w_gate, w_up, w_down, gs)


CB_TM = 512
CB_EXPERTS = 4


def _combine_kernel(h_ref, selpos_ref, y_ref, *rest):
    o_ref = rest[-1]
    c = pl.program_id(2)

    @pl.when(c == 0)
    def _():
        o_ref[...] = h_ref[...]

    sp = selpos_ref[0]
    lane = lax.broadcasted_iota(jnp.int32, sp.shape, 1)
    slot = lax.broadcasted_iota(jnp.int32, (CB_TM, CAP), 1).astype(F32)
    pieces = []
    for k in range(CB_EXPERTS):
        col = jnp.sum(jnp.where(lane == c * CB_EXPERTS + k, sp, 0.0), axis=1, keepdims=True)
        pieces.append(jnp.where(col == slot, 1.0, 0.0).astype(BF16))
    onehot = jnp.concatenate(pieces, axis=1)
    o_ref[0] += _dot(onehot, y_ref[...].reshape(CB_EXPERTS * CAP, D_MODEL))

    if len(rest) == 2:
        @pl.when(c == pl.num_programs(2) - 1)
        def _():
            o_ref[0] = _rms(o_ref[0], rest[0][...])


def _combine(h, selpos, y, final_g=None):
    in_specs = [
        pl.BlockSpec((1, CB_TM, D_MODEL), lambda b, i, c: (b, i, 0)),
        pl.BlockSpec((1, CB_TM, N_EXPERTS), lambda b, i, c: (b, i, 0)),
        pl.BlockSpec((CB_EXPERTS, 1, CAP, D_MODEL), lambda b, i, c: (c, b, 0, 0)),
    ]
    args = [h, selpos, y]
    if final_g is not None:
        in_specs.append(pl.BlockSpec((1, D_MODEL), lambda b, i, c: (0, 0)))
        args.append(final_g.reshape(1, D_MODEL).astype(F32))
    return pl.pallas_call(
        _combine_kernel,
        grid=(BATCH, SEQ // CB_TM, N_EXPERTS // CB_EXPERTS),
        in_specs=in_specs,
        out_specs=pl.BlockSpec((1, CB_TM, D_MODEL), lambda b, i, c: (b, i, 0)),
        out_shape=jax.ShapeDtypeStruct((BATCH, SEQ, D_MODEL), F32),
        compiler_params=_cparams("parallel", "parallel", "arbitrary"),
        name="moe_combine",
    )(*args)


def _angles(pos, dim, theta):
    inv = jnp.power(jnp.float32(theta), -(jnp.arange(0, dim, 2, dtype=jnp.float32) / dim))
    ang = pos.astype(jnp.float32)[:, None] * inv[None, :]
    return jnp.cos(ang), jnp.sin(ang)


def _rot_tables(groups, width=HEAD_DIM):
    cos_parts, sneg_parts, spos_parts = [], [], []
    used = 0
    for c, s in groups:
        z = jnp.zeros_like(s)
        cos_parts += [c, c]
        sneg_parts += [-s, z]
        spos_parts += [z, s]
        used += 2 * c.shape[1]
    rest = width - used
    if rest:
        cos_parts.append(jnp.ones((SEQ, rest), F32))
        sneg_parts.append(jnp.zeros((SEQ, rest), F32))
        spos_parts.append(jnp.zeros((SEQ, rest), F32))
    return (jnp.concatenate(cos_parts, axis=1), jnp.concatenate(sneg_parts, axis=1),
            jnp.concatenate(spos_parts, axis=1))


def _gqa_w_in(w_in, scale_q):
    if scale_q:
        col = jnp.concatenate([jnp.full((N_HEADS * HEAD_DIM,), HEAD_DIM ** -0.5 * LOG2E, F32),
                               jnp.ones((2 * N_KV_HEADS * HEAD_DIM,), F32)])
        w_in = w_in * col[None, :]
    return w_in.astype(BF16)


def _mixer_gqa_in(h2, g, w_in, scale_q):
    qkv = _norm_matmul(h2, g, _gqa_w_in(w_in, scale_q), tm=1024, tn=512, head_w=HEAD_DIM)
    return qkv.reshape(N_HEADS + 2 * N_KV_HEADS, BATCH, SEQ, HEAD_DIM)


def _mla_weights(w_in, w_uq):
    w_in = jnp.pad(w_in, ((0, 0), (0, MLA_LAT_PAD - w_in.shape[-1]))).astype(BF16)
    w_uq = jnp.pad((w_uq * (MLA_QK_DIM ** -0.5 * LOG2E)).reshape(MLA_Q_RANK, N_HEADS, MLA_QK_DIM),
                   ((0, 0), (0, 0), (0, MLA_Q_PAD - MLA_QK_DIM))).reshape(MLA_Q_RANK, N_HEADS * MLA_Q_PAD)
    return w_in, w_uq.astype(BF16)


def kernel(x, mem, norm_mix_g, norm_xa_g, norm_ffn_g, a_w_in, a_sink, a_w_out, b_w_in, b_q_norm_g, b_k_norm_g, b_w_out, c_w_in, c_q_lat_norm_g, c_kv_lat_norm_g, c_w_uq, c_w_ukv, c_w_out, d_w_in, d_rpb, d_w_out, mem_norm_g, xa_wq, xa_wkv, xa_wo, router_w, moe_w_gate, moe_w_up, moe_w_down, final_norm_g):
    t = BATCH * SEQ
    pos = jnp.arange(SEQ)
    tab_a = _rot_tables([_angles(pos, ROT_DIM, ROPE_THETA)])
    tab_b = _rot_tables([_angles(pos // GRID_W, HEAD_DIM // 2, AXIAL_THETA),
                         _angles(pos % GRID_W, HEAD_DIM // 2, AXIAL_THETA)])
    tab_c = _rot_tables([_angles(pos, MLA_ROPE_DIM, ROPE_THETA)])
    mem2 = mem.reshape(BATCH * MEM_TOKENS, D_MODEL)

    h = x
    for i in range(DEPTH):
        m, j = i % N_MIXERS, i // N_MIXERS
        h2 = h.reshape(t, D_MODEL)
        if m == 0:
            qkv = _mixer_gqa_in(h2, norm_mix_g[i], a_w_in[j], True)
            att = _attn_window(qkv, a_sink[j] * LOG2E, tab_a)
            w_out = a_w_out[j]
        elif m == 1:
            qkv = _mixer_gqa_in(h2, norm_mix_g[i], b_w_in[j], False)
            att = _attn_axial(qkv, b_q_norm_g[j] * (HEAD_DIM ** -0.5 * LOG2E), b_k_norm_g[j], tab_b)
            w_out = b_w_out[j]
        elif m == 2:
            w_in, w_uq = _mla_weights(c_w_in[j], c_w_uq[j])
            lat = _norm_matmul(h2, norm_mix_g[i], w_in, tm=1024, tn=MLA_LAT_PAD // 3)
            qfull = _norm_matmul(lat, c_q_lat_norm_g[j], w_uq, col_block=0, tm=1024, tn=1024, head_w=MLA_Q_PAD)
            kvfull = _norm_matmul(lat, c_kv_lat_norm_g[j], c_w_ukv[j].astype(BF16), col_block=1, tm=1024, tn=1024,
                                  head_w=MLA_NOPE_DIM)
            k_rope = lat[:, MLA_Q_RANK + MLA_KV_RANK:].reshape(BATCH, SEQ, HEAD_DIM)
            att = _attn_mla(qfull.reshape(N_HEADS, BATCH, SEQ, MLA_Q_PAD),
                            kvfull.reshape(2 * N_HEADS, BATCH, SEQ, MLA_NOPE_DIM), k_rope, tab_c)
            w_out = c_w_out[j]
        else:
            qkv = _mixer_gqa_in(h2, norm_mix_g[i], d_w_in[j], True)
            att = _attn_na(qkv, _na_bias(d_rpb[j] * LOG2E))
            w_out = d_w_out[j]

        kv = _norm_matmul(mem2, mem_norm_g, xa_wkv[i].astype(BF16), tm=512, tn=512)
        h = _xattn(h, att, w_out.astype(BF16), norm_xa_g[i], (xa_wq[i] * (XA_HEAD_DIM ** -0.5 * LOG2E)).astype(BF16),
                   kv.reshape(BATCH, MEM_TOKENS, 2 * XA_INNER), xa_wo[i].astype(BF16))

        xn, selpos, selpos_t, gate_t = _router(h, norm_ffn_g[i], router_w[i].astype(BF16))
        xg, gs = _gather(xn, selpos_t, gate_t)
        y = _expert_ffn(xg, moe_w_gate, moe_w_up, moe_w_down, gs, i)
        h = _combine(h, selpos, y, final_norm_g if i == DEPTH - 1 else None)

    return h
```

```python
import jax
import jax.numpy as jnp
from jax import lax
from jax.experimental import pallas as pl
from jax.experimental.pallas import tpu as pltpu

F32 = jnp.float32
BF16 = jnp.bfloat16

D_MODEL = 2048
BATCH = 4
SEQ = 2048
DEPTH = 4
N_MIXERS = 4
HEAD_DIM = 128
N_HEADS = 16
N_KV_HEADS = 4
GQA_GROUP = N_HEADS // N_KV_HEADS
GQA_IN = (N_HEADS + 2 * N_KV_HEADS) * HEAD_DIM
Q_BLOCK = 128
WINDOW = 128
ROPE_THETA = 500000.0
ROT_DIM = HEAD_DIM // 4
AXIAL_THETA = 10000.0
GRID_W = 64
GRID_H = SEQ // GRID_W
MLA_Q_RANK = 512
MLA_KV_RANK = 512
MLA_NOPE_DIM = 128
MLA_ROPE_DIM = 64
MLA_V_DIM = 128
MLA_QK_DIM = MLA_NOPE_DIM + MLA_ROPE_DIM
MLA_Q_PAD = 256
MLA_LAT_PAD = 1152
NA_ROWS = 8
NA_COLS = 16
MEM_TOKENS = 256
XA_HEADS = 4
XA_HEAD_DIM = 128
XA_INNER = XA_HEADS * XA_HEAD_DIM
N_EXPERTS = 16
EC_CAPACITY = 2
EXPERT_FF = 1024
CAP = EC_CAPACITY * SEQ // N_EXPERTS
EPS = 1e-6
MASKED = -1e30
LOG2E = 1.4426950408889634
LANES = 128

VMEM_LIMIT_BYTES = 56 * 1024 * 1024


def _cparams(*sem):
    return pltpu.CompilerParams(dimension_semantics=sem, vmem_limit_bytes=VMEM_LIMIT_BYTES)


def _dot_nt(a, b):
    return lax.dot_general(a, b, (((1,), (1,)), ((), ())), preferred_element_type=F32)


def _dot(a, b):
    return jnp.dot(a, b, preferred_element_type=F32)


def _rms(x, g):
    ms = jnp.mean(x * x, axis=-1, keepdims=True)
    return x * lax.rsqrt(ms + EPS) * g


def _rot(x, cos, sneg, spos, shift):
    n = x.shape[-1]
    return x * cos + pltpu.roll(x, n - shift, 1) * sneg + pltpu.roll(x, shift, 1) * spos


PV_CHUNK = 512


def _softmax_pv(s, v_ones, sink=None):
    dv = v_ones.shape[1] // 2
    keys = s.shape[1]
    m = jnp.max(s, axis=-1, keepdims=True)
    if sink is not None:
        m = jnp.maximum(m, sink)
    r = None
    for j in range(0, keys, PV_CHUNK):
        part = _dot(jnp.exp2(s[:, j:j + PV_CHUNK] - m).astype(BF16), v_ones[j:j + PV_CHUNK])
        r = part if r is None else r + part
    den = r[:, dv:]
    if sink is not None:
        den = den + jnp.exp2(sink - m)
    return r[:, :dv] * (1.0 / den)


def _fill_v_ones(v_scr, v):
    dv = v.shape[1]
    v_scr[:, :dv] = v
    v_scr[:, dv:] = jnp.ones_like(v)


def _pairs(n_blocks, prep, scores, finish, bufs):
    q_a, q_b, s_a, s_b = bufs
    last = n_blocks - 1
    q_b[...] = prep(0)
    s_b[...] = scores(0, q_b[...])
    q_a[...] = prep(1)

    def pair(i, carry):
        n = 2 * i
        s_a[...] = scores(n + 1, q_a[...])
        q_b[...] = prep(jnp.minimum(n + 2, last))
        finish(n, s_b[...])
        s_b[...] = scores(jnp.minimum(n + 2, last), q_b[...])
        q_a[...] = prep(jnp.minimum(n + 3, last))
        finish(n + 1, s_a[...])
        return carry

    lax.fori_loop(0, n_blocks // 2, pair, 0)


def _pipeline_bufs(rows, dq, keys):
    return [pltpu.VMEM((rows, dq), BF16), pltpu.VMEM((rows, dq), BF16),
            pltpu.VMEM((rows, keys), F32), pltpu.VMEM((rows, keys), F32)]


def _norm_matmul_kernel(x_ref, g_ref, w_ref, o_ref, xn_ref):
    @pl.when(pl.program_id(1) == 0)
    def _():
        xn_ref[...] = _rms(x_ref[...].astype(F32), g_ref[...]).astype(BF16)

    res = _dot(xn_ref[...], w_ref[...])
    if len(o_ref.shape) == 3:
        head_w = o_ref.shape[2]
        for hh in range(o_ref.shape[0]):
            o_ref[hh] = res[:, hh * head_w:(hh + 1) * head_w].astype(o_ref.dtype)
    else:
        o_ref[...] = res.astype(o_ref.dtype)


def _norm_matmul(x, g, w, *, col_block=0, tm, tn, head_w=None):
    t = x.shape[0]
    k, n = w.shape
    if head_w is None:
        out_spec = pl.BlockSpec((tm, tn), lambda i, j: (i, j))
        out_shape = jax.ShapeDtypeStruct((t, n), BF16)
    else:
        out_spec = pl.BlockSpec((tn // head_w, tm, head_w), lambda i, j: (j, i, 0))
        out_shape = jax.ShapeDtypeStruct((n // head_w, t, head_w), BF16)
    return pl.pallas_call(
        _norm_matmul_kernel,
        grid=(t // tm, n // tn),
        in_specs=[
            pl.BlockSpec((tm, k), lambda i, j: (i, col_block)),
            pl.BlockSpec((1, k), lambda i, j: (0, 0)),
            pl.BlockSpec((k, tn), lambda i, j: (0, j)),
        ],
        out_specs=out_spec,
        out_shape=out_shape,
        scratch_shapes=[pltpu.VMEM((tm, k), BF16)],
        compiler_params=_cparams("parallel", "arbitrary"),
        name="norm_matmul",
    )(x, g.reshape(1, k).astype(F32), w)


def _attn_window_kernel(sink_ref, q_ref, k_ref, v_ref, cos_ref, sneg_ref, spos_ref, o_ref, k_scr, v_scr, *bufs):
    kvh = pl.program_id(1)
    half = ROT_DIM // 2
    nb = SEQ // Q_BLOCK
    rows, keys = GQA_GROUP * Q_BLOCK, 3 * Q_BLOCK
    k_scr[...] = _rot(k_ref[0, 0].astype(F32), cos_ref[...], sneg_ref[...], spos_ref[...], half).astype(BF16)
    _fill_v_ones(v_scr, v_ref[0, 0])
    r = lax.broadcasted_iota(jnp.int32, (rows, keys), 0)
    c = lax.broadcasted_iota(jnp.int32, (rows, keys), 1)
    base = (r & (Q_BLOCK - 1)) - c
    sink = jnp.concatenate(
        [jnp.full((Q_BLOCK, 1), sink_ref[kvh * GQA_GROUP + g], F32) for g in range(GQA_GROUP)], axis=0)

    def window_start(n):
        return pl.multiple_of(jnp.clip(n - 1, 0, nb - 3) * Q_BLOCK, Q_BLOCK)

    def prep(n):
        q0 = pl.multiple_of(n * Q_BLOCK, Q_BLOCK)
        cos = cos_ref[pl.ds(q0, Q_BLOCK), :]
        sneg = sneg_ref[pl.ds(q0, Q_BLOCK), :]
        spos = spos_ref[pl.ds(q0, Q_BLOCK), :]
        return jnp.concatenate(
            [_rot(q_ref[g, 0, pl.ds(q0, Q_BLOCK), :].astype(F32),
                  cos, sneg, spos, half).astype(BF16) for g in range(GQA_GROUP)], axis=0)

    def scores(n, q4):
        return _dot_nt(q4, k_scr[pl.ds(window_start(n), keys), :])

    def finish(n, s):
        q0 = pl.multiple_of(n * Q_BLOCK, Q_BLOCK)
        k0 = window_start(n)
        s = jnp.where(jnp.abs(base + (q0 - k0)) <= WINDOW, s, MASKED)
        o = _softmax_pv(s, v_scr[pl.ds(k0, keys), :], sink)
        for g in range(GQA_GROUP):
            o_ref[g, 0, pl.ds(q0, Q_BLOCK), :] = (
                o[g * Q_BLOCK:(g + 1) * Q_BLOCK].astype(o_ref.dtype))

    _pairs(nb, prep, scores, finish, bufs)


def _gqa_specs():
    head = (1, 1, SEQ, HEAD_DIM)
    group = (GQA_GROUP, 1, SEQ, HEAD_DIM)
    return [
        pl.BlockSpec(group, lambda b, h: (h, b, 0, 0)),
        pl.BlockSpec(head, lambda b, h: (N_HEADS + h, b, 0, 0)),
        pl.BlockSpec(head, lambda b, h: (N_HEADS + N_KV_HEADS + h, b, 0, 0)),
    ], pl.BlockSpec(group, lambda b, h: (h, b, 0, 0))


_ATT_SHAPE = jax.ShapeDtypeStruct((N_HEADS, BATCH, SEQ, HEAD_DIM), BF16)


def _attn_window(qkv, sink, tables):
    cos, sneg, spos = tables
    qkv_specs, out_spec = _gqa_specs()
    tab = pl.BlockSpec((SEQ, HEAD_DIM), lambda b, h: (0, 0))
    return pl.pallas_call(
        _attn_window_kernel,
        grid=(BATCH, N_KV_HEADS),
        in_specs=[pl.BlockSpec(memory_space=pltpu.SMEM)] + qkv_specs + [tab, tab, tab],
        out_specs=out_spec,
        out_shape=_ATT_SHAPE,
        scratch_shapes=[pltpu.VMEM((SEQ, HEAD_DIM), BF16), pltpu.VMEM((SEQ, 2 * HEAD_DIM), BF16)]
        + _pipeline_bufs(GQA_GROUP * Q_BLOCK, HEAD_DIM, 3 * Q_BLOCK),
        compiler_params=_cparams("parallel", "parallel"),
        name="attn_window",
    )(sink.astype(F32), qkv, qkv, qkv, cos, sneg, spos)


AX_TQ = 128


def _attn_axial_kernel(q_ref, k_ref, v_ref, qg_ref, kg_ref, cos_ref, sneg_ref, spos_ref, o_ref, k_scr, v_scr, *bufs):
    quarter = HEAD_DIM // 4
    kn = _rms(k_ref[0, 0].astype(F32), kg_ref[...])
    k_scr[...] = _rot(kn, cos_ref[...], sneg_ref[...], spos_ref[...], quarter).astype(BF16)
    _fill_v_ones(v_scr, v_ref[0, 0])

    def prep(n):
        q0 = pl.multiple_of(n * AX_TQ, AX_TQ)
        cos = cos_ref[pl.ds(q0, AX_TQ), :]
        sneg = sneg_ref[pl.ds(q0, AX_TQ), :]
        spos = spos_ref[pl.ds(q0, AX_TQ), :]
        return jnp.concatenate(
            [_rot(_rms(q_ref[g, 0, pl.ds(q0, AX_TQ), :].astype(F32), qg_ref[...]),
                  cos, sneg, spos, quarter).astype(BF16) for g in range(GQA_GROUP)], axis=0)

    def scores(n, q4):
        return _dot_nt(q4, k_scr[...])

    def finish(n, s):
        q0 = pl.multiple_of(n * AX_TQ, AX_TQ)
        o = _softmax_pv(s, v_scr[...])
        for g in range(GQA_GROUP):
            o_ref[g, 0, pl.ds(q0, AX_TQ), :] = (
                o[g * AX_TQ:(g + 1) * AX_TQ].astype(o_ref.dtype))

    _pairs(SEQ // AX_TQ, prep, scores, finish, bufs)


def _attn_axial(qkv, qg, kg, tables):
    cos, sneg, spos = tables
    qkv_specs, out_spec = _gqa_specs()
    tab = pl.BlockSpec((SEQ, HEAD_DIM), lambda b, h: (0, 0))
    gain = pl.BlockSpec((1, HEAD_DIM), lambda b, h: (0, 0))
    return pl.pallas_call(
        _attn_axial_kernel,
        grid=(BATCH, N_KV_HEADS),
        in_specs=qkv_specs + [gain, gain, tab, tab, tab],
        out_specs=out_spec,
        out_shape=_ATT_SHAPE,
        scratch_shapes=[pltpu.VMEM((SEQ, HEAD_DIM), BF16), pltpu.VMEM((SEQ, 2 * HEAD_DIM), BF16)]
        + _pipeline_bufs(GQA_GROUP * AX_TQ, HEAD_DIM, SEQ),
        compiler_params=_cparams("parallel", "parallel"),
        name="attn_axial",
    )(qkv, qkv, qkv, qg.reshape(1, HEAD_DIM).astype(F32), kg.reshape(1, HEAD_DIM).astype(F32), cos, sneg, spos)


MLA_TQ = 512


def _attn_mla_kernel(q_ref, kn_ref, v_ref, kr_ref, cos_ref, sneg_ref, spos_ref, o_ref, k_scr, v_scr, *bufs):
    half = MLA_ROPE_DIM // 2
    k_scr[:, :MLA_NOPE_DIM] = kn_ref[0, 0]
    k_scr[:, MLA_NOPE_DIM:] = _rot(kr_ref[0].astype(F32), cos_ref[...], sneg_ref[...], spos_ref[...],
                                   half).astype(BF16)
    _fill_v_ones(v_scr, v_ref[0, 0])

    def prep(n):
        q0 = pl.multiple_of(n * MLA_TQ, MLA_TQ)
        cos = cos_ref[pl.ds(q0, MLA_TQ), :]
        sneg = sneg_ref[pl.ds(q0, MLA_TQ), :]
        spos = spos_ref[pl.ds(q0, MLA_TQ), :]
        qn = q_ref[0, 0, pl.ds(q0, MLA_TQ), :MLA_NOPE_DIM]
        qr = _rot(q_ref[0, 0, pl.ds(q0, MLA_TQ), MLA_NOPE_DIM:].astype(F32), cos, sneg, spos, half).astype(BF16)
        return jnp.concatenate([qn, qr], axis=1)

    def scores(n, q):
        return _dot_nt(q, k_scr[...])

    def finish(n, s):
        q0 = pl.multiple_of(n * MLA_TQ, MLA_TQ)
        o_ref[0, 0, pl.ds(q0, MLA_TQ), :] = _softmax_pv(s, v_scr[...]).astype(o_ref.dtype)

    _pairs(SEQ // MLA_TQ, prep, scores, finish, bufs)


def _attn_mla(qfull, kvfull, k_rope, tables):
    cos, sneg, spos = tables
    tab = pl.BlockSpec((SEQ, HEAD_DIM), lambda b, h: (0, 0))
    return pl.pallas_call(
        _attn_mla_kernel,
        grid=(BATCH, N_HEADS),
        in_specs=[
            pl.BlockSpec((1, 1, SEQ, MLA_Q_PAD), lambda b, h: (h, b, 0, 0)),
            pl.BlockSpec((1, 1, SEQ, MLA_NOPE_DIM), lambda b, h: (2 * h, b, 0, 0)),
            pl.BlockSpec((1, 1, SEQ, MLA_V_DIM), lambda b, h: (2 * h + 1, b, 0, 0)),
            pl.BlockSpec((1, SEQ, HEAD_DIM), lambda b, h: (b, 0, 0)),
            tab, tab, tab,
        ],
        out_specs=pl.BlockSpec((1, 1, SEQ, MLA_V_DIM), lambda b, h: (h, b, 0, 0)),
        out_shape=_ATT_SHAPE,
        scratch_shapes=[pltpu.VMEM((SEQ, MLA_Q_PAD), BF16), pltpu.VMEM((SEQ, 2 * MLA_V_DIM), BF16)]
        + _pipeline_bufs(MLA_TQ, MLA_Q_PAD, SEQ),
        compiler_params=_cparams("parallel", "parallel"),
        name="attn_mla",
    )(qfull, kvfull, kvfull, k_rope, cos, sneg, spos)


NA_DELTAS = NA_ROWS
NA_KEYS = NA_ROWS * GRID_W
NA_DR = 2 * NA_ROWS - 1
NA_DC = 2 * NA_COLS - 1


def _na_bias_kernel(rpb_ref, o_ref):
    h = pl.program_id(0)
    shape = (GRID_W, 2 * GRID_W)
    qc = lax.broadcasted_iota(jnp.int32, shape, 0)
    lane = lax.broadcasted_iota(jnp.int32, shape, 1)
    kc = jnp.where(lane >= GRID_W, lane - GRID_W, lane)
    dc = kc - qc + (NA_COLS - 1)
    c_start = jnp.clip(qc - NA_COLS // 2, 0, GRID_W - NA_COLS)
    in_win = (kc >= c_start) & (kc < c_start + NA_COLS)
    low = lane < GRID_W
    tiles = [jnp.zeros(shape, F32) for _ in range(NA_DR)]
    for d in range(NA_DC):
        hit = dc == d
        for dr in range(NA_DR):
            tiles[dr] = jnp.where(hit, rpb_ref[(h * NA_DR + dr) * NA_DC + d], tiles[dr])
    for delta in range(NA_DELTAS):
        for j in range(NA_ROWS // 2):
            t = jnp.where(low, tiles[delta + 2 * j], tiles[delta + 2 * j + 1])
            o_ref[0, delta, :, j * 2 * GRID_W:(j + 1) * 2 * GRID_W] = jnp.where(in_win, t, MASKED)


def _na_bias(rpb):
    return pl.pallas_call(
        _na_bias_kernel,
        grid=(N_HEADS,),
        in_specs=[pl.BlockSpec(memory_space=pltpu.SMEM)],
        out_specs=pl.BlockSpec((1, NA_DELTAS, GRID_W, NA_KEYS), lambda h: (h, 0, 0, 0)),
        out_shape=jax.ShapeDtypeStruct((N_HEADS, NA_DELTAS, GRID_W, NA_KEYS), F32),
        compiler_params=_cparams("parallel"),
        name="na_bias",
    )(rpb.astype(F32).reshape(-1))


def _attn_na_kernel(q_ref, k_ref, v_ref, bias_ref, o_ref, v_scr, *bufs):
    _fill_v_ones(v_scr, v_ref[0, 0])

    def first_key_row(r):
        return jnp.clip(r - NA_ROWS // 2, 0, GRID_H - NA_ROWS)

    def prep(r):
        q0 = pl.multiple_of(r * GRID_W, GRID_W)
        return jnp.concatenate([q_ref[g, 0, pl.ds(q0, GRID_W), :] for g in range(GQA_GROUP)], axis=0)

    def scores(r, q4):
        k0 = pl.multiple_of(first_key_row(r) * GRID_W, GRID_W)
        return _dot_nt(q4, k_ref[0, 0, pl.ds(k0, NA_KEYS), :])

    def finish(r, s):
        r0 = first_key_row(r)
        delta = r0 - r + (NA_ROWS - 1)
        q0 = pl.multiple_of(r * GRID_W, GRID_W)
        k0 = pl.multiple_of(r0 * GRID_W, GRID_W)
        bias = jnp.concatenate([bias_ref[g, delta] for g in range(GQA_GROUP)], axis=0)
        o = _softmax_pv(s + bias, v_scr[pl.ds(k0, NA_KEYS), :])
        for g in range(GQA_GROUP):
            o_ref[g, 0, pl.ds(q0, GRID_W), :] = (
                o[g * GRID_W:(g + 1) * GRID_W].astype(o_ref.dtype))

    _pairs(GRID_H, prep, scores, finish, bufs)


def _attn_na(qkv, bias):
    qkv_specs, out_spec = _gqa_specs()
    return pl.pallas_call(
        _attn_na_kernel,
        grid=(BATCH, N_KV_HEADS),
        in_specs=qkv_specs + [pl.BlockSpec((GQA_GROUP, NA_DELTAS, GRID_W, NA_KEYS), lambda b, h: (h, 0, 0, 0))],
        out_specs=out_spec,
        out_shape=_ATT_SHAPE,
        scratch_shapes=[pltpu.VMEM((SEQ, 2 * HEAD_DIM), BF16)]
        + _pipeline_bufs(GQA_GROUP * GRID_W, HEAD_DIM, NA_KEYS),
        compiler_params=_cparams("parallel", "parallel"),
        name="attn_na",
    )(qkv, qkv, qkv, bias)


XA_TM = 256


def _xattn_kernel(h_ref, att_ref, wout_ref, g_ref, wq_ref, kv_ref, wo_ref, o_ref):
    att = jnp.concatenate([att_ref[hh, 0] for hh in range(N_HEADS)], axis=1)
    x = h_ref[0] + _dot(att, wout_ref[...])
    xn = _rms(x, g_ref[...]).astype(BF16)
    q = _dot(xn, wq_ref[...]).astype(BF16)
    outs = []
    ones = jnp.ones((MEM_TOKENS, XA_HEAD_DIM), BF16)
    for hh in range(XA_HEADS):
        k = kv_ref[0, :, hh * XA_HEAD_DIM:(hh + 1) * XA_HEAD_DIM]
        v = kv_ref[0, :, XA_INNER + hh * XA_HEAD_DIM:XA_INNER + (hh + 1) * XA_HEAD_DIM]
        s = _dot_nt(q[:, hh * XA_HEAD_DIM:(hh + 1) * XA_HEAD_DIM], k)
        outs.append(_softmax_pv(s, jnp.concatenate([v, ones], axis=1)).astype(BF16))
    o_ref[0] = x + _dot(jnp.concatenate(outs, axis=1), wo_ref[...])


def _xattn(h, att, w_out, g, wq, kv, wo):
    full = lambda b, i: (0, 0)
    return pl.pallas_call(
        _xattn_kernel,
        grid=(BATCH, SEQ // XA_TM),
        in_specs=[
            pl.BlockSpec((1, XA_TM, D_MODEL), lambda b, i: (b, i, 0)),
            pl.BlockSpec((N_HEADS, 1, XA_TM, HEAD_DIM), lambda b, i: (0, b, i, 0)),
            pl.BlockSpec(w_out.shape, full),
            pl.BlockSpec((1, D_MODEL), full),
            pl.BlockSpec((D_MODEL, XA_INNER), full),
            pl.BlockSpec((1, MEM_TOKENS, 2 * XA_INNER), lambda b, i: (b, 0, 0)),
            pl.BlockSpec((XA_INNER, D_MODEL), full),
        ],
        out_specs=pl.BlockSpec((1, XA_TM, D_MODEL), lambda b, i: (b, i, 0)),
        out_shape=jax.ShapeDtypeStruct((BATCH, SEQ, D_MODEL), F32),
        compiler_params=_cparams("parallel", "parallel"),
        name="outproj_xattn",
    )(h, att, w_out, g.reshape(1, D_MODEL).astype(F32), wq, kv, wo)


RT_TM = 512
PREFIX_BLOCK = 256
TOPK_MAX_ITERS = 256


def _router_kernel(h_ref, g_ref, rw_ref, xn_ref, selpos_ref, selpos_t_ref, gate_t_ref, aff_scr):
    i = pl.program_id(1)

    @pl.when(i == 0)
    def _():
        aff_scr[...] = jnp.zeros_like(aff_scr)

    xn = _rms(h_ref[0], g_ref[...]).astype(BF16)
    xn_ref[0] = xn
    logits = _dot(xn, rw_ref[...])
    m = jnp.max(logits, axis=-1, keepdims=True)
    e = jnp.exp(logits - m)
    aff_scr[pl.ds(pl.multiple_of(i * RT_TM, RT_TM), RT_TM), :N_EXPERTS] = e / jnp.sum(e, axis=-1, keepdims=True)

    @pl.when(i == pl.num_programs(1) - 1)
    def _():
        a = jnp.transpose(aff_scr[...])[:N_EXPERTS]
        gate_t_ref[0] = a
        kf = jnp.float32(CAP)

        def count(mask):
            return jnp.sum(jnp.where(mask, 1.0, 0.0), axis=1, keepdims=True)

        def cond(c):
            it, _, _, _, done = c
            return jnp.logical_and(it < TOPK_MAX_ITERS, done == 0)

        def body(c):
            it, lo, hi, _, _ = c
            mid = 0.5 * (lo + hi)
            take = count(a >= mid) >= kf
            lo = jnp.where(take, mid, lo)
            hi = jnp.where(take, hi, mid)
            top = jnp.max(jnp.where(a < hi, a, -1.0), axis=1, keepdims=True)
            bot = jnp.min(jnp.where(a >= lo, a, 3.0), axis=1, keepdims=True)
            done = jnp.min(jnp.where(top == bot, 1, 0))
            return it + 1, lo, hi, top, done

        init = (jnp.int32(0), jnp.zeros((N_EXPERTS, 1), F32), jnp.full((N_EXPERTS, 1), 2.0, F32),
                jnp.zeros((N_EXPERTS, 1), F32), jnp.int32(0))
        _, _, _, kth, _ = lax.while_loop(cond, body, init)

        gt = a > kth
        eq = a == kth
        need = kf - count(gt)
        tok = lax.broadcasted_iota(jnp.int32, (N_EXPERTS, SEQ), 1).astype(F32)
        cut = jnp.zeros((N_EXPERTS, 1), F32)
        step = SEQ // 2
        while step >= 1:
            cand = cut + step
            cut = jnp.where(count(eq & (tok < cand)) < need, cand, cut)
            step //= 2
        sel = jnp.where(gt, 1.0, jnp.where(eq & (tok <= cut), 1.0, 0.0))

        r = lax.broadcasted_iota(jnp.int32, (PREFIX_BLOCK, PREFIX_BLOCK), 0)
        c = lax.broadcasted_iota(jnp.int32, (PREFIX_BLOCK, PREFIX_BLOCK), 1)
        utri = jnp.where(r < c, 1.0, 0.0).astype(BF16)
        off = jnp.zeros((N_EXPERTS, 1), F32)
        for j in range(SEQ // PREFIX_BLOCK):
            blk = sel[:, j * PREFIX_BLOCK:(j + 1) * PREFIX_BLOCK]
            pos = _dot(blk.astype(BF16), utri) + off
            selpos_t_ref[0, :, j * PREFIX_BLOCK:(j + 1) * PREFIX_BLOCK] = jnp.where(blk > 0.5, pos, -1.0)
            off = off + jnp.sum(blk, axis=1, keepdims=True)

        padded = jnp.concatenate([selpos_t_ref[0], jnp.zeros((LANES - N_EXPERTS, SEQ), F32)], axis=0)
        selpos_ref[0] = jnp.transpose(padded)[:, :N_EXPERTS]


def _router(h, g, rw):
    return pl.pallas_call(
        _router_kernel,
        grid=(BATCH, SEQ // RT_TM),
        in_specs=[
            pl.BlockSpec((1, RT_TM, D_MODEL), lambda b, i: (b, i, 0)),
            pl.BlockSpec((1, D_MODEL), lambda b, i: (0, 0)),
            pl.BlockSpec((D_MODEL, N_EXPERTS), lambda b, i: (0, 0)),
        ],
        out_specs=[
            pl.BlockSpec((1, RT_TM, D_MODEL), lambda b, i: (b, i, 0)),
            pl.BlockSpec((1, SEQ, N_EXPERTS), lambda b, i: (b, 0, 0)),
            pl.BlockSpec((1, N_EXPERTS, SEQ), lambda b, i: (b, 0, 0)),
            pl.BlockSpec((1, N_EXPERTS, SEQ), lambda b, i: (b, 0, 0)),
        ],
        out_shape=[
            jax.ShapeDtypeStruct((BATCH, SEQ, D_MODEL), BF16),
            jax.ShapeDtypeStruct((BATCH, SEQ, N_EXPERTS), F32),
            jax.ShapeDtypeStruct((BATCH, N_EXPERTS, SEQ), F32),
            jax.ShapeDtypeStruct((BATCH, N_EXPERTS, SEQ), F32),
        ],
        scratch_shapes=[pltpu.VMEM((SEQ, LANES), F32)],
        compiler_params=_cparams("parallel", "arbitrary"),
        name="router_topk",
    )(h, g.reshape(1, D_MODEL).astype(F32), rw)


def _gather_kernel(xn_ref, selpos_t_ref, gate_t_ref, xg_ref, gs_ref):
    e = pl.program_id(1)
    slot_of_token = selpos_t_ref[0, pl.ds(e, 1), :]
    gate_row = gate_t_ref[0, pl.ds(e, 1), :]
    slot = lax.broadcasted_iota(jnp.int32, (CAP, SEQ), 0).astype(F32)
    hit = slot == slot_of_token
    xg_ref[0, 0] = _dot(jnp.where(hit, 1.0, 0.0).astype(BF16), xn_ref[0]).astype(BF16)
    gs_ref[0, 0] = jnp.sum(jnp.where(hit, gate_row, 0.0), axis=1, keepdims=True)


def _gather(xn, selpos_t, gate_t):
    return pl.pallas_call(
        _gather_kernel,
        grid=(BATCH, N_EXPERTS),
        in_specs=[
            pl.BlockSpec((1, SEQ, D_MODEL), lambda b, e: (b, 0, 0)),
            pl.BlockSpec((1, N_EXPERTS, SEQ), lambda b, e: (b, 0, 0)),
            pl.BlockSpec((1, N_EXPERTS, SEQ), lambda b, e: (b, 0, 0)),
        ],
        out_specs=[
            pl.BlockSpec((1, 1, CAP, D_MODEL), lambda b, e: (e, b, 0, 0)),
            pl.BlockSpec((1, 1, CAP, 1), lambda b, e: (e, b, 0, 0)),
        ],
        out_shape=[
            jax.ShapeDtypeStruct((N_EXPERTS, BATCH, CAP, D_MODEL), BF16),
            jax.ShapeDtypeStruct((N_EXPERTS, BATCH, CAP, 1), F32),
        ],
        compiler_params=_cparams("parallel", "arbitrary"),
        name="moe_gather",
    )(xn, selpos_t, gate_t)


FF_TF = 256


def _expert_ffn_kernel(xg_ref, wg_ref, wu_ref, wd_ref, gs_ref, y_ref, acc_ref):
    f = pl.program_id(1)

    @pl.when(f == 0)
    def _():
        acc_ref[...] = jnp.zeros_like(acc_ref)

    x = xg_ref[0].reshape(BATCH * CAP, D_MODEL)
    gate = _dot(x, wg_ref[0, 0].astype(BF16))
    up = _dot(x, wu_ref[0, 0].astype(BF16))
    act = (gate * (1.0 / (1.0 + jnp.exp(-gate))) * up).astype(BF16)
    acc_ref[...] += _dot(act, wd_ref[0, 0].astype(BF16))

    @pl.when(f == pl.num_programs(1) - 1)
    def _():
        y = acc_ref[...] * gs_ref[0].reshape(BATCH * CAP, 1)
        y_ref[0] = y.astype(BF16).reshape(BATCH, CAP, D_MODEL)


def _expert_ffn(xg, w_gate, w_up, w_down, gs, layer):
    return pl.pallas_call(
        _expert_ffn_kernel,
        grid=(N_EXPERTS, EXPERT_FF // FF_TF),
        in_specs=[
            pl.BlockSpec((1, BATCH, CAP, D_MODEL), lambda e, f: (e, 0, 0, 0)),
            pl.BlockSpec((1, 1, D_MODEL, FF_TF), lambda e, f: (layer, e, 0, f)),
            pl.BlockSpec((1, 1, D_MODEL, FF_TF), lambda e, f: (layer, e, 0, f)),
            pl.BlockSpec((1, 1, FF_TF, D_MODEL), lambda e, f: (layer, e, f, 0)),
            pl.BlockSpec((1, BATCH, CAP, 1), lambda e, f: (e, 0, 0, 0)),
        ],
        out_specs=pl.BlockSpec((1, BATCH, CAP, D_MODEL), lambda e, f: (e, 0, 0, 0)),
        out_shape=jax.ShapeDtypeStruct((N_EXPERTS, BATCH, CAP, D_MODEL), BF16),
        scratch_shapes=[pltpu.VMEM((BATCH * CAP, D_MODEL), F32)],
        compiler_params=_cparams("parallel", "arbitrary"),
        name="moe_ffn",
    )(xg, w_gate, w_up, w_down, gs)


CB_TM = 512
CB_EXPERTS = 4


def _combine_kernel(h_ref, selpos_ref, y_ref, *rest):
    o_ref = rest[-1]
    c = pl.program_id(2)

    @pl.when(c == 0)
    def _():
        o_ref[...] = h_ref[...]

    sp = selpos_ref[0]
    lane = lax.broadcasted_iota(jnp.int32, sp.shape, 1)
    slot = lax.broadcasted_iota(jnp.int32, (CB_TM, CAP), 1).astype(F32)
    pieces = []
    for k in range(CB_EXPERTS):
        col = jnp.sum(jnp.where(lane == c * CB_EXPERTS + k, sp, 0.0), axis=1, keepdims=True)
        pieces.append(jnp.where(col == slot, 1.0, 0.0).astype(BF16))
    onehot = jnp.concatenate(pieces, axis=1)
    o_ref[0] += _dot(onehot, y_ref[...].reshape(CB_EXPERTS * CAP, D_MODEL))

    if len(rest) == 2:
        @pl.when(c == pl.num_programs(2) - 1)
        def _():
            o_ref[0] = _rms(o_ref[0], rest[0][...])


def _combine(h, selpos, y, final_g=None):
    in_specs = [
        pl.BlockSpec((1, CB_TM, D_MODEL), lambda b, i, c: (b, i, 0)),
        pl.BlockSpec((1, CB_TM, N_EXPERTS), lambda b, i, c: (b, i, 0)),
        pl.BlockSpec((CB_EXPERTS, 1, CAP, D_MODEL), lambda b, i, c: (c, b, 0, 0)),
    ]
    args = [h, selpos, y]
    if final_g is not None:
        in_specs.append(pl.BlockSpec((1, D_MODEL), lambda b, i, c: (0, 0)))
        args.append(final_g.reshape(1, D_MODEL).astype(F32))
    return pl.pallas_call(
        _combine_kernel,
        grid=(BATCH, SEQ // CB_TM, N_EXPERTS // CB_EXPERTS),
        in_specs=in_specs,
        out_specs=pl.BlockSpec((1, CB_TM, D_MODEL), lambda b, i, c: (b, i, 0)),
        out_shape=jax.ShapeDtypeStruct((BATCH, SEQ, D_MODEL), F32),
        compiler_params=_cparams("parallel", "parallel", "arbitrary"),
        name="moe_combine",
    )(*args)


def _angles(pos, dim, theta):
    inv = jnp.power(jnp.float32(theta), -(jnp.arange(0, dim, 2, dtype=jnp.float32) / dim))
    ang = pos.astype(jnp.float32)[:, None] * inv[None, :]
    return jnp.cos(ang), jnp.sin(ang)


def _rot_tables(groups, width=HEAD_DIM):
    cos_parts, sneg_parts, spos_parts = [], [], []
    used = 0
    for c, s in groups:
        z = jnp.zeros_like(s)
        cos_parts += [c, c]
        sneg_parts += [-s, z]
        spos_parts += [z, s]
        used += 2 * c.shape[1]
    rest = width - used
    if rest:
        cos_parts.append(jnp.ones((SEQ, rest), F32))
        sneg_parts.append(jnp.zeros((SEQ, rest), F32))
        spos_parts.append(jnp.zeros((SEQ, rest), F32))
    return (jnp.concatenate(cos_parts, axis=1), jnp.concatenate(sneg_parts, axis=1),
            jnp.concatenate(spos_parts, axis=1))


def _gqa_w_in(w_in, scale_q):
    if scale_q:
        col = jnp.concatenate([jnp.full((N_HEADS * HEAD_DIM,), HEAD_DIM ** -0.5 * LOG2E, F32),
                               jnp.ones((2 * N_KV_HEADS * HEAD_DIM,), F32)])
        w_in = w_in * col[None, :]
    return w_in.astype(BF16)


def _mixer_gqa_in(h2, g, w_in, scale_q):
    qkv = _norm_matmul(h2, g, _gqa_w_in(w_in, scale_q), tm=1024, tn=512, head_w=HEAD_DIM)
    return qkv.reshape(N_HEADS + 2 * N_KV_HEADS, BATCH, SEQ, HEAD_DIM)


def _mla_weights(w_in, w_uq):
    w_in = jnp.pad(w_in, ((0, 0), (0, MLA_LAT_PAD - w_in.shape[-1]))).astype(BF16)
    w_uq = jnp.pad((w_uq * (MLA_QK_DIM ** -0.5 * LOG2E)).reshape(MLA_Q_RANK, N_HEADS, MLA_QK_DIM),
                   ((0, 0), (0, 0), (0, MLA_Q_PAD - MLA_QK_DIM))).reshape(MLA_Q_RANK, N_HEADS * MLA_Q_PAD)
    return w_in, w_uq.astype(BF16)


def kernel(x, mem, norm_mix_g, norm_xa_g, norm_ffn_g, a_w_in, a_sink, a_w_out, b_w_in, b_q_norm_g, b_k_norm_g, b_w_out, c_w_in, c_q_lat_norm_g, c_kv_lat_norm_g, c_w_uq, c_w_ukv, c_w_out, d_w_in, d_rpb, d_w_out, mem_norm_g, xa_wq, xa_wkv, xa_wo, router_w, moe_w_gate, moe_w_up, moe_w_down, final_norm_g):
    t = BATCH * SEQ
    pos = jnp.arange(SEQ)
    tab_a = _rot_tables([_angles(pos, ROT_DIM, ROPE_THETA)])
    tab_b = _rot_tables([_angles(pos // GRID_W, HEAD_DIM // 2, AXIAL_THETA),
                         _angles(pos % GRID_W, HEAD_DIM // 2, AXIAL_THETA)])
    tab_c = _rot_tables([_angles(pos, MLA_ROPE_DIM, ROPE_THETA)])
    mem2 = mem.reshape(BATCH * MEM_TOKENS, D_MODEL)

    h = x
    for i in range(DEPTH):
        m, j = i % N_MIXERS, i // N_MIXERS
        h2 = h.reshape(t, D_MODEL)
        if m == 0:
            qkv = _mixer_gqa_in(h2, norm_mix_g[i], a_w_in[j], True)
            att = _attn_window(qkv, a_sink[j] * LOG2E, tab_a)
            w_out = a_w_out[j]
        elif m == 1:
            qkv = _mixer_gqa_in(h2, norm_mix_g[i], b_w_in[j], False)
            att = _attn_axial(qkv, b_q_norm_g[j] * (HEAD_DIM ** -0.5 * LOG2E), b_k_norm_g[j], tab_b)
            w_out = b_w_out[j]
        elif m == 2:
            w_in, w_uq = _mla_weights(c_w_in[j], c_w_uq[j])
            lat = _norm_matmul(h2, norm_mix_g[i], w_in, tm=1024, tn=MLA_LAT_PAD // 3)
            qfull = _norm_matmul(lat, c_q_lat_norm_g[j], w_uq, col_block=0, tm=1024, tn=1024, head_w=MLA_Q_PAD)
            kvfull = _norm_matmul(lat, c_kv_lat_norm_g[j], c_w_ukv[j].astype(BF16), col_block=1, tm=1024, tn=1024,
                                  head_w=MLA_NOPE_DIM)
            k_rope = lat[:, MLA_Q_RANK + MLA_KV_RANK:].reshape(BATCH, SEQ, HEAD_DIM)
            att = _attn_mla(qfull.reshape(N_HEADS, BATCH, SEQ, MLA_Q_PAD),
                            kvfull.reshape(2 * N_HEADS, BATCH, SEQ, MLA_NOPE_DIM), k_rope, tab_c)
            w_out = c_w_out[j]
        else:
            qkv = _mixer_gqa_in(h2, norm_mix_g[i], d_w_in[j], True)
            att = _attn_na(qkv, _na_bias(d_rpb[j] * LOG2E))
            w_out = d_w_out[j]

        kv = _norm_matmul(mem2, mem_norm_g, xa_wkv[i].astype(BF16), tm=512, tn=512)
        h = _xattn(h, att, w_out.astype(BF16), norm_xa_g[i], (xa_wq[i] * (XA_HEAD_DIM ** -0.5 * LOG2E)).astype(BF16),
                   kv.reshape(BATCH, MEM_TOKENS, 2 * XA_INNER), xa_wo[i].astype(BF16))

        xn, selpos, selpos_t, gate_t = _router(h, norm_ffn_g[i], router_w[i].astype(BF16))
        xg, gs = _gather(xn, selpos_t, gate_t)
        y = _expert_ffn(xg, moe_w_gate, moe_w_up, moe_w_down, gs, i)
        h = _combine(h, selpos, y, final_norm_g if i == DEPTH - 1 else None)

    return h
```

```python
import jax
import jax.numpy as jnp
from jax import lax
from jax.experimental import pallas as pl
from jax.experimental.pallas import tpu as pltpu

F32 = jnp.float32
BF16 = jnp.bfloat16

D_MODEL = 2048
BATCH = 4
SEQ = 2048
DEPTH = 4
N_MIXERS = 4
HEAD_DIM = 128
N_HEADS = 16
N_KV_HEADS = 4
GQA_GROUP = N_HEADS // N_KV_HEADS
GQA_IN = (N_HEADS + 2 * N_KV_HEADS) * HEAD_DIM
Q_BLOCK = 128
WINDOW = 128
ROPE_THETA = 500000.0
ROT_DIM = HEAD_DIM // 4
AXIAL_THETA = 10000.0
GRID_W = 64
GRID_H = SEQ // GRID_W
MLA_Q_RANK = 512
MLA_KV_RANK = 512
MLA_NOPE_DIM = 128
MLA_ROPE_DIM = 64
MLA_V_DIM = 128
MLA_QK_DIM = MLA_NOPE_DIM + MLA_ROPE_DIM
MLA_Q_PAD = 256
MLA_LAT_PAD = 1152
NA_ROWS = 8
NA_COLS = 16
MEM_TOKENS = 256
XA_HEADS = 4
XA_HEAD_DIM = 128
XA_INNER = XA_HEADS * XA_HEAD_DIM
N_EXPERTS = 16
EC_CAPACITY = 2
EXPERT_FF = 1024
CAP = EC_CAPACITY * SEQ // N_EXPERTS
EPS = 1e-6
MASKED = -1e30
LOG2E = 1.4426950408889634
LANES = 128

VMEM_LIMIT_BYTES = 56 * 1024 * 1024


def _cparams(*sem):
    return pltpu.CompilerParams(dimension_semantics=sem, vmem_limit_bytes=VMEM_LIMIT_BYTES)


def _dot_nt(a, b):
    return lax.dot_general(a, b, (((1,), (1,)), ((), ())), preferred_element_type=F32)


def _dot(a, b):
    return jnp.dot(a, b, preferred_element_type=F32)


def _rms(x, g):
    ms = jnp.mean(x * x, axis=-1, keepdims=True)
    return x * lax.rsqrt(ms + EPS) * g


def _rot(x, cos, sneg, spos, shift):
    n = x.shape[-1]
    return x * cos + pltpu.roll(x, n - shift, 1) * sneg + pltpu.roll(x, shift, 1) * spos


PV_CHUNK = 512


def _softmax_pv(s, v_ones, sink=None):
    dv = v_ones.shape[1] // 2
    keys = s.shape[1]
    m = jnp.max(s, axis=-1, keepdims=True)
    if sink is not None:
        m = jnp.maximum(m, sink)
    r = None
    for j in range(0, keys, PV_CHUNK):
        part = _dot(jnp.exp2(s[:, j:j + PV_CHUNK] - m).astype(BF16), v_ones[j:j + PV_CHUNK])
        r = part if r is None else r + part
    den = r[:, dv:]
    if sink is not None:
        den = den + jnp.exp2(sink - m)
    return r[:, :dv] * (1.0 / den)


def _fill_v_ones(v_scr, v):
    dv = v.shape[1]
    v_scr[:, :dv] = v
    v_scr[:, dv:] = jnp.ones_like(v)


def _pairs(n_blocks, prep, scores, finish, bufs):
    q_a, q_b, s_a, s_b = bufs
    last = n_blocks - 1
    q_b[...] = prep(0)
    s_b[...] = scores(0, q_b[...])
    q_a[...] = prep(1)

    def pair(i, carry):
        n = 2 * i
        s_a[...] = scores(n + 1, q_a[...])
        q_b[...] = prep(jnp.minimum(n + 2, last))
        finish(n, s_b[...])
        s_b[...] = scores(jnp.minimum(n + 2, last), q_b[...])
        q_a[...] = prep(jnp.minimum(n + 3, last))
        finish(n + 1, s_a[...])
        return carry

    lax.fori_loop(0, n_blocks // 2, pair, 0)


def _pipeline_bufs(rows, dq, keys):
    return [pltpu.VMEM((rows, dq), BF16), pltpu.VMEM((rows, dq), BF16),
            pltpu.VMEM((rows, keys), F32), pltpu.VMEM((rows, keys), F32)]


def _norm_matmul_kernel(x_ref, g_ref, w_ref, o_ref, xn_ref):
    @pl.when(pl.program_id(1) == 0)
    def _():
        xn_ref[...] = _rms(x_ref[...].astype(F32), g_ref[...]).astype(BF16)

    res = _dot(xn_ref[...], w_ref[...])
    if len(o_ref.shape) == 3:
        head_w = o_ref.shape[2]
        for hh in range(o_ref.shape[0]):
            o_ref[hh] = res[:, hh * head_w:(hh + 1) * head_w].astype(o_ref.dtype)
    else:
        o_ref[...] = res.astype(o_ref.dtype)


def _norm_matmul(x, g, w, *, col_block=0, tm, tn, head_w=None):
    t = x.shape[0]
    k, n = w.shape
    if head_w is None:
        out_spec = pl.BlockSpec((tm, tn), lambda i, j: (i, j))
        out_shape = jax.ShapeDtypeStruct((t, n), BF16)
    else:
        out_spec = pl.BlockSpec((tn // head_w, tm, head_w), lambda i, j: (j, i, 0))
        out_shape = jax.ShapeDtypeStruct((n // head_w, t, head_w), BF16)
    return pl.pallas_call(
        _norm_matmul_kernel,
        grid=(t // tm, n // tn),
        in_specs=[
            pl.BlockSpec((tm, k), lambda i, j: (i, col_block)),
            pl.BlockSpec((1, k), lambda i, j: (0, 0)),
            pl.BlockSpec((k, tn), lambda i, j: (0, j)),
        ],
        out_specs=out_spec,
        out_shape=out_shape,
        scratch_shapes=[pltpu.VMEM((tm, k), BF16)],
        compiler_params=_cparams("parallel", "arbitrary"),
        name="norm_matmul",
    )(x, g.reshape(1, k).astype(F32), w)


def _attn_window_kernel(sink_ref, q_ref, k_ref, v_ref, cos_ref, sneg_ref, spos_ref, o_ref, k_scr, v_scr, *bufs):
    kvh = pl.program_id(1)
    half = ROT_DIM // 2
    nb = SEQ // Q_BLOCK
    rows, keys = GQA_GROUP * Q_BLOCK, 3 * Q_BLOCK
    k_scr[...] = _rot(k_ref[0, 0].astype(F32), cos_ref[...], sneg_ref[...], spos_ref[...], half).astype(BF16)
    _fill_v_ones(v_scr, v_ref[0, 0])
    r = lax.broadcasted_iota(jnp.int32, (rows, keys), 0)
    c = lax.broadcasted_iota(jnp.int32, (rows, keys), 1)
    base = (r & (Q_BLOCK - 1)) - c
    sink = jnp.concatenate(
        [jnp.full((Q_BLOCK, 1), sink_ref[kvh * GQA_GROUP + g], F32) for g in range(GQA_GROUP)], axis=0)

    def window_start(n):
        return pl.multiple_of(jnp.clip(n - 1, 0, nb - 3) * Q_BLOCK, Q_BLOCK)

    def prep(n):
        q0 = pl.multiple_of(n * Q_BLOCK, Q_BLOCK)
        cos = cos_ref[pl.ds(q0, Q_BLOCK), :]
        sneg = sneg_ref[pl.ds(q0, Q_BLOCK), :]
        spos = spos_ref[pl.ds(q0, Q_BLOCK), :]
        return jnp.concatenate(
            [_rot(q_ref[g, 0, pl.ds(q0, Q_BLOCK), :].astype(F32),
                  cos, sneg, spos, half).astype(BF16) for g in range(GQA_GROUP)], axis=0)

    def scores(n, q4):
        return _dot_nt(q4, k_scr[pl.ds(window_start(n), keys), :])

    def finish(n, s):
        q0 = pl.multiple_of(n * Q_BLOCK, Q_BLOCK)
        k0 = window_start(n)
        s = jnp.where(jnp.abs(base + (q0 - k0)) <= WINDOW, s, MASKED)
        o = _softmax_pv(s, v_scr[pl.ds(k0, keys), :], sink)
        for g in range(GQA_GROUP):
            o_ref[g, 0, pl.ds(q0, Q_BLOCK), :] = (
                o[g * Q_BLOCK:(g + 1) * Q_BLOCK].astype(o_ref.dtype))

    _pairs(nb, prep, scores, finish, bufs)


def _gqa_specs():
    head = (1, 1, SEQ, HEAD_DIM)
    group = (GQA_GROUP, 1, SEQ, HEAD_DIM)
    return [
        pl.BlockSpec(group, lambda b, h: (h, b, 0, 0)),
        pl.BlockSpec(head, lambda b, h: (N_HEADS + h, b, 0, 0)),
        pl.BlockSpec(head, lambda b, h: (N_HEADS + N_KV_HEADS + h, b, 0, 0)),
    ], pl.BlockSpec(group, lambda b, h: (h, b, 0, 0))


_ATT_SHAPE = jax.ShapeDtypeStruct((N_HEADS, BATCH, SEQ, HEAD_DIM), BF16)


def _attn_window(qkv, sink, tables):
    cos, sneg, spos = tables
    qkv_specs, out_spec = _gqa_specs()
    tab = pl.BlockSpec((SEQ, HEAD_DIM), lambda b, h: (0, 0))
    return pl.pallas_call(
        _attn_window_kernel,
        grid=(BATCH, N_KV_HEADS),
        in_specs=[pl.BlockSpec(memory_space=pltpu.SMEM)] + qkv_specs + [tab, tab, tab],
        out_specs=out_spec,
        out_shape=_ATT_SHAPE,
        scratch_shapes=[pltpu.VMEM((SEQ, HEAD_DIM), BF16), pltpu.VMEM((SEQ, 2 * HEAD_DIM), BF16)]
        + _pipeline_bufs(GQA_GROUP * Q_BLOCK, HEAD_DIM, 3 * Q_BLOCK),
        compiler_params=_cparams("parallel", "parallel"),
        name="attn_window",
    )(sink.astype(F32), qkv, qkv, qkv, cos, sneg, spos)


AX_TQ = 128


def _attn_axial_kernel(q_ref, k_ref, v_ref, qg_ref, kg_ref, cos_ref, sneg_ref, spos_ref, o_ref, k_scr, v_scr, *bufs):
    quarter = HEAD_DIM // 4
    kn = _rms(k_ref[0, 0].astype(F32), kg_ref[...])
    k_scr[...] = _rot(kn, cos_ref[...], sneg_ref[...], spos_ref[...], quarter).astype(BF16)
    _fill_v_ones(v_scr, v_ref[0, 0])

    def prep(n):
        q0 = pl.multiple_of(n * AX_TQ, AX_TQ)
        cos = cos_ref[pl.ds(q0, AX_TQ), :]
        sneg = sneg_ref[pl.ds(q0, AX_TQ), :]
        spos = spos_ref[pl.ds(q0, AX_TQ), :]
        return jnp.concatenate(
            [_rot(_rms(q_ref[g, 0, pl.ds(q0, AX_TQ), :].astype(F32), qg_ref[...]),
                  cos, sneg, spos, quarter).astype(BF16) for g in range(GQA_GROUP)], axis=0)

    def scores(n, q4):
        return _dot_nt(q4, k_scr[...])

    def finish(n, s):
        q0 = pl.multiple_of(n * AX_TQ, AX_TQ)
        o = _softmax_pv(s, v_scr[...])
        for g in range(GQA_GROUP):
            o_ref[g, 0, pl.ds(q0, AX_TQ), :] = (
                o[g * AX_TQ:(g + 1) * AX_TQ].astype(o_ref.dtype))

    _pairs(SEQ // AX_TQ, prep, scores, finish, bufs)


def _attn_axial(qkv, qg, kg, tables):
    cos, sneg, spos = tables
    qkv_specs, out_spec = _gqa_specs()
    tab = pl.BlockSpec((SEQ, HEAD_DIM), lambda b, h: (0, 0))
    gain = pl.BlockSpec((1, HEAD_DIM), lambda b, h: (0, 0))
    return pl.pallas_call(
        _attn_axial_kernel,
        grid=(BATCH, N_KV_HEADS),
        in_specs=qkv_specs + [gain, gain, tab, tab, tab],
        out_specs=out_spec,
        out_shape=_ATT_SHAPE,
        scratch_shapes=[pltpu.VMEM((SEQ, HEAD_DIM), BF16), pltpu.VMEM((SEQ, 2 * HEAD_DIM), BF16)]
        + _pipeline_bufs(GQA_GROUP * AX_TQ, HEAD_DIM, SEQ),
        compiler_params=_cparams("parallel", "parallel"),
        name="attn_axial",
    )(qkv, qkv, qkv, qg.reshape(1, HEAD_DIM).astype(F32), kg.reshape(1, HEAD_DIM).astype(F32), cos, sneg, spos)


MLA_TQ = 512


def _attn_mla_kernel(q_ref, kn_ref, v_ref, kr_ref, cos_ref, sneg_ref, spos_ref, o_ref, k_scr, v_scr, *bufs):
    half = MLA_ROPE_DIM // 2
    k_scr[:, :MLA_NOPE_DIM] = kn_ref[0, 0]

    @pl.when(pl.program_id(1) == 0)
    def _():
        k_scr[:, MLA_NOPE_DIM:] = _rot(kr_ref[0].astype(F32), cos_ref[...], sneg_ref[...], spos_ref[...],
                                       half).astype(BF16)

    _fill_v_ones(v_scr, v_ref[0, 0])

    def prep(n):
        q0 = pl.multiple_of(n * MLA_TQ, MLA_TQ)
        cos = cos_ref[pl.ds(q0, MLA_TQ), :]
        sneg = sneg_ref[pl.ds(q0, MLA_TQ), :]
        spos = spos_ref[pl.ds(q0, MLA_TQ), :]
        qn = q_ref[0, 0, pl.ds(q0, MLA_TQ), :MLA_NOPE_DIM]
        qr = _rot(q_ref[0, 0, pl.ds(q0, MLA_TQ), MLA_NOPE_DIM:].astype(F32), cos, sneg, spos, half).astype(BF16)
        return jnp.concatenate([qn, qr], axis=1)

    def scores(n, q):
        return _dot_nt(q, k_scr[...])

    def finish(n, s):
        q0 = pl.multiple_of(n * MLA_TQ, MLA_TQ)
        o_ref[0, 0, pl.ds(q0, MLA_TQ), :] = _softmax_pv(s, v_scr[...]).astype(o_ref.dtype)

    _pairs(SEQ // MLA_TQ, prep, scores, finish, bufs)


def _attn_mla(qfull, kvfull, k_rope, tables):
    cos, sneg, spos = tables
    tab = pl.BlockSpec((SEQ, HEAD_DIM), lambda b, h: (0, 0))
    return pl.pallas_call(
        _attn_mla_kernel,
        grid=(BATCH, N_HEADS),
        in_specs=[
            pl.BlockSpec((1, 1, SEQ, MLA_Q_PAD), lambda b, h: (h, b, 0, 0)),
            pl.BlockSpec((1, 1, SEQ, MLA_NOPE_DIM), lambda b, h: (2 * h, b, 0, 0)),
            pl.BlockSpec((1, 1, SEQ, MLA_V_DIM), lambda b, h: (2 * h + 1, b, 0, 0)),
            pl.BlockSpec((1, SEQ, HEAD_DIM), lambda b, h: (b, 0, 0)),
            tab, tab, tab,
        ],
        out_specs=pl.BlockSpec((1, 1, SEQ, MLA_V_DIM), lambda b, h: (h, b, 0, 0)),
        out_shape=_ATT_SHAPE,
        scratch_shapes=[pltpu.VMEM((SEQ, MLA_Q_PAD), BF16), pltpu.VMEM((SEQ, 2 * MLA_V_DIM), BF16)]
        + _pipeline_bufs(MLA_TQ, MLA_Q_PAD, SEQ),
        compiler_params=_cparams("parallel", "arbitrary"),
        name="attn_mla",
    )(qfull, kvfull, kvfull, k_rope, cos, sneg, spos)


NA_DELTAS = NA_ROWS
NA_KEYS = NA_ROWS * GRID_W
NA_DR = 2 * NA_ROWS - 1
NA_DC = 2 * NA_COLS - 1


def _na_bias_kernel(rpb_ref, o_ref):
    h = pl.program_id(0)
    shape = (GRID_W, 2 * GRID_W)
    qc = lax.broadcasted_iota(jnp.int32, shape, 0)
    lane = lax.broadcasted_iota(jnp.int32, shape, 1)
    kc = jnp.where(lane >= GRID_W, lane - GRID_W, lane)
    dc = kc - qc + (NA_COLS - 1)
    c_start = jnp.clip(qc - NA_COLS // 2, 0, GRID_W - NA_COLS)
    in_win = (kc >= c_start) & (kc < c_start + NA_COLS)
    low = lane < GRID_W
    tiles = [jnp.zeros(shape, F32) for _ in range(NA_DR)]
    for d in range(NA_DC):
        hit = dc == d
        for dr in range(NA_DR):
            tiles[dr] = jnp.where(hit, rpb_ref[(h * NA_DR + dr) * NA_DC + d], tiles[dr])
    for delta in range(NA_DELTAS):
        for j in range(NA_ROWS // 2):
            t = jnp.where(low, tiles[delta + 2 * j], tiles[delta + 2 * j + 1])
            o_ref[0, delta, :, j * 2 * GRID_W:(j + 1) * 2 * GRID_W] = jnp.where(in_win, t, MASKED)


def _na_bias(rpb):
    return pl.pallas_call(
        _na_bias_kernel,
        grid=(N_HEADS,),
        in_specs=[pl.BlockSpec(memory_space=pltpu.SMEM)],
        out_specs=pl.BlockSpec((1, NA_DELTAS, GRID_W, NA_KEYS), lambda h: (h, 0, 0, 0)),
        out_shape=jax.ShapeDtypeStruct((N_HEADS, NA_DELTAS, GRID_W, NA_KEYS), F32),
        compiler_params=_cparams("parallel"),
        name="na_bias",
    )(rpb.astype(F32).reshape(-1))


def _attn_na_kernel(q_ref, k_ref, v_ref, bias_ref, o_ref, v_scr, *bufs):
    _fill_v_ones(v_scr, v_ref[0, 0])

    def first_key_row(r):
        return jnp.clip(r - NA_ROWS // 2, 0, GRID_H - NA_ROWS)

    def prep(r):
        q0 = pl.multiple_of(r * GRID_W, GRID_W)
        return jnp.concatenate([q_ref[g, 0, pl.ds(q0, GRID_W), :] for g in range(GQA_GROUP)], axis=0)

    def scores(r, q4):
        k0 = pl.multiple_of(first_key_row(r) * GRID_W, GRID_W)
        return _dot_nt(q4, k_ref[0, 0, pl.ds(k0, NA_KEYS), :])

    def finish(r, s):
        r0 = first_key_row(r)
        delta = r0 - r + (NA_ROWS - 1)
        q0 = pl.multiple_of(r * GRID_W, GRID_W)
        k0 = pl.multiple_of(r0 * GRID_W, GRID_W)
        bias = jnp.concatenate([bias_ref[g, delta] for g in range(GQA_GROUP)], axis=0)
        o = _softmax_pv(s + bias, v_scr[pl.ds(k0, NA_KEYS), :])
        for g in range(GQA_GROUP):
            o_ref[g, 0, pl.ds(q0, GRID_W), :] = (
                o[g * GRID_W:(g + 1) * GRID_W].astype(o_ref.dtype))

    _pairs(GRID_H, prep, scores, finish, bufs)


def _attn_na(qkv, bias):
    qkv_specs, out_spec = _gqa_specs()
    return pl.pallas_call(
        _attn_na_kernel,
        grid=(BATCH, N_KV_HEADS),
        in_specs=qkv_specs + [pl.BlockSpec((GQA_GROUP, NA_DELTAS, GRID_W, NA_KEYS), lambda b, h: (h, 0, 0, 0))],
        out_specs=out_spec,
        out_shape=_ATT_SHAPE,
        scratch_shapes=[pltpu.VMEM((SEQ, 2 * HEAD_DIM), BF16)]
        + _pipeline_bufs(GQA_GROUP * GRID_W, HEAD_DIM, NA_KEYS),
        compiler_params=_cparams("parallel", "parallel"),
        name="attn_na",
    )(qkv, qkv, qkv, bias)


XA_TM = 256


def _xattn_kernel(h_ref, att_ref, wout_ref, g_ref, wq_ref, kv_ref, wo_ref, gf_ref, rw_ref, rb_ref,
                  o_ref, xn_ref, aff_ref):
    att = jnp.concatenate([att_ref[hh, 0] for hh in range(N_HEADS)], axis=1)
    x = h_ref[0] + _dot(att, wout_ref[...])
    xn = _rms(x, g_ref[...]).astype(BF16)
    q = _dot(xn, wq_ref[...]).astype(BF16)
    outs = []
    ones = jnp.ones((MEM_TOKENS, XA_HEAD_DIM), BF16)
    for hh in range(XA_HEADS):
        k = kv_ref[0, :, hh * XA_HEAD_DIM:(hh + 1) * XA_HEAD_DIM]
        v = kv_ref[0, :, XA_INNER + hh * XA_HEAD_DIM:XA_INNER + (hh + 1) * XA_HEAD_DIM]
        s = _dot_nt(q[:, hh * XA_HEAD_DIM:(hh + 1) * XA_HEAD_DIM], k)
        outs.append(_softmax_pv(s, jnp.concatenate([v, ones], axis=1)).astype(BF16))
    y = x + _dot(jnp.concatenate(outs, axis=1), wo_ref[...])
    o_ref[0] = y
    yn = _rms(y, gf_ref[...]).astype(BF16)
    xn_ref[0] = yn
    logits = _dot(yn, rw_ref[...]) + rb_ref[...]
    e = jnp.exp(logits - jnp.max(logits, axis=-1, keepdims=True))
    aff_ref[0] = e / jnp.sum(e, axis=-1, keepdims=True)


def _xattn(h, att, w_out, g, wq, kv, wo, g_ffn, router_w):
    full = lambda b, i: (0, 0)
    tile = lambda b, i: (b, i, 0)
    rw = jnp.pad(router_w, ((0, 0), (0, LANES - N_EXPERTS))).astype(BF16)
    rb = jnp.where(jnp.arange(LANES) < N_EXPERTS, 0.0, MASKED).astype(F32).reshape(1, LANES)
    return pl.pallas_call(
        _xattn_kernel,
        grid=(BATCH, SEQ // XA_TM),
        in_specs=[
            pl.BlockSpec((1, XA_TM, D_MODEL), lambda b, i: (b, i, 0)),
            pl.BlockSpec((N_HEADS, 1, XA_TM, HEAD_DIM), lambda b, i: (0, b, i, 0)),
            pl.BlockSpec(w_out.shape, full),
            pl.BlockSpec((1, D_MODEL), full),
            pl.BlockSpec((D_MODEL, XA_INNER), full),
            pl.BlockSpec((1, MEM_TOKENS, 2 * XA_INNER), lambda b, i: (b, 0, 0)),
            pl.BlockSpec((XA_INNER, D_MODEL), full),
            pl.BlockSpec((1, D_MODEL), full),
            pl.BlockSpec((D_MODEL, LANES), full),
            pl.BlockSpec((1, LANES), full),
        ],
        out_specs=[
            pl.BlockSpec((1, XA_TM, D_MODEL), tile),
            pl.BlockSpec((1, XA_TM, D_MODEL), tile),
            pl.BlockSpec((1, XA_TM, LANES), tile),
        ],
        out_shape=[
            jax.ShapeDtypeStruct((BATCH, SEQ, D_MODEL), F32),
            jax.ShapeDtypeStruct((BATCH, SEQ, D_MODEL), BF16),
            jax.ShapeDtypeStruct((BATCH, SEQ, LANES), F32),
        ],
        compiler_params=_cparams("parallel", "parallel"),
        name="outproj_xattn",
    )(h, att, w_out, g.reshape(1, D_MODEL).astype(F32), wq, kv, wo, g_ffn.reshape(1, D_MODEL).astype(F32), rw, rb)


PREFIX_BLOCK = 256
TOPK_MAX_ITERS = 256


def _topk_kernel(aff_ref, selpos_ref, selpos_t_ref, gate_t_ref):
    a = jnp.transpose(aff_ref[0])[:N_EXPERTS]
    gate_t_ref[0] = a
    kf = jnp.float32(CAP)

    def count(mask):
        return jnp.sum(jnp.where(mask, 1.0, 0.0), axis=1, keepdims=True)

    def cond(c):
        it, _, _, _, done = c
        return jnp.logical_and(it < TOPK_MAX_ITERS, done == 0)

    def body(c):
        it, lo, hi, _, _ = c
        mid = 0.5 * (lo + hi)
        take = count(a >= mid) >= kf
        lo = jnp.where(take, mid, lo)
        hi = jnp.where(take, hi, mid)
        top = jnp.max(jnp.where(a < hi, a, -1.0), axis=1, keepdims=True)
        bot = jnp.min(jnp.where(a >= lo, a, 3.0), axis=1, keepdims=True)
        done = jnp.min(jnp.where(top == bot, 1, 0))
        return it + 1, lo, hi, top, done

    init = (jnp.int32(0), jnp.zeros((N_EXPERTS, 1), F32), jnp.full((N_EXPERTS, 1), 2.0, F32),
            jnp.zeros((N_EXPERTS, 1), F32), jnp.int32(0))
    _, _, _, kth, _ = lax.while_loop(cond, body, init)

    gt = a > kth
    eq = a == kth
    need = kf - count(gt)
    tok = lax.broadcasted_iota(jnp.int32, (N_EXPERTS, SEQ), 1).astype(F32)
    cut = jnp.zeros((N_EXPERTS, 1), F32)
    step = SEQ // 2
    while step >= 1:
        cand = cut + step
        cut = jnp.where(count(eq & (tok < cand)) < need, cand, cut)
        step //= 2
    sel = jnp.where(gt, 1.0, jnp.where(eq & (tok <= cut), 1.0, 0.0))

    r = lax.broadcasted_iota(jnp.int32, (PREFIX_BLOCK, PREFIX_BLOCK), 0)
    c = lax.broadcasted_iota(jnp.int32, (PREFIX_BLOCK, PREFIX_BLOCK), 1)
    utri = jnp.where(r < c, 1.0, 0.0).astype(BF16)
    off = jnp.zeros((N_EXPERTS, 1), F32)
    for j in range(SEQ // PREFIX_BLOCK):
        blk = sel[:, j * PREFIX_BLOCK:(j + 1) * PREFIX_BLOCK]
        pos = _dot(blk.astype(BF16), utri) + off
        selpos_t_ref[0, :, j * PREFIX_BLOCK:(j + 1) * PREFIX_BLOCK] = jnp.where(blk > 0.5, pos, -1.0)
        off = off + jnp.sum(blk, axis=1, keepdims=True)

    padded = jnp.concatenate([selpos_t_ref[0], jnp.zeros((LANES - N_EXPERTS, SEQ), F32)], axis=0)
    selpos_ref[0] = jnp.transpose(padded)[:, :N_EXPERTS]


def _topk(aff):
    return pl.pallas_call(
        _topk_kernel,
        grid=(BATCH,),
        in_specs=[pl.BlockSpec((1, SEQ, LANES), lambda b: (b, 0, 0))],
        out_specs=[
            pl.BlockSpec((1, SEQ, N_EXPERTS), lambda b: (b, 0, 0)),
            pl.BlockSpec((1, N_EXPERTS, SEQ), lambda b: (b, 0, 0)),
            pl.BlockSpec((1, N_EXPERTS, SEQ), lambda b: (b, 0, 0)),
        ],
        out_shape=[
            jax.ShapeDtypeStruct((BATCH, SEQ, N_EXPERTS), F32),
            jax.ShapeDtypeStruct((BATCH, N_EXPERTS, SEQ), F32),
            jax.ShapeDtypeStruct((BATCH, N_EXPERTS, SEQ), F32),
        ],
        compiler_params=_cparams("parallel"),
        name="expert_topk",
    )(aff)


def _gather_kernel(xn_ref, selpos_t_ref, gate_t_ref, xg_ref, gs_ref):
    e = pl.program_id(1)
    slot_of_token = selpos_t_ref[0, pl.ds(e, 1), :]
    gate_row = gate_t_ref[0, pl.ds(e, 1), :]
    slot = lax.broadcasted_iota(jnp.int32, (CAP, SEQ), 0).astype(F32)
    hit = slot == slot_of_token
    xg_ref[0, 0] = _dot(jnp.where(hit, 1.0, 0.0).astype(BF16), xn_ref[0]).astype(BF16)
    gs_ref[0, 0] = jnp.sum(jnp.where(hit, gate_row, 0.0), axis=1, keepdims=True)


def _gather(xn, selpos_t, gate_t):
    return pl.pallas_call(
        _gather_kernel,
        grid=(BATCH, N_EXPERTS),
        in_specs=[
            pl.BlockSpec((1, SEQ, D_MODEL), lambda b, e: (b, 0, 0)),
            pl.BlockSpec((1, N_EXPERTS, SEQ), lambda b, e: (b, 0, 0)),
            pl.BlockSpec((1, N_EXPERTS, SEQ), lambda b, e: (b, 0, 0)),
        ],
        out_specs=[
            pl.BlockSpec((1, 1, CAP, D_MODEL), lambda b, e: (e, b, 0, 0)),
            pl.BlockSpec((1, 1, CAP, 1), lambda b, e: (e, b, 0, 0)),
        ],
        out_shape=[
            jax.ShapeDtypeStruct((N_EXPERTS, BATCH, CAP, D_MODEL), BF16),
            jax.ShapeDtypeStruct((N_EXPERTS, BATCH, CAP, 1), F32),
        ],
        compiler_params=_cparams("parallel", "arbitrary"),
        name="moe_gather",
    )(xn, selpos_t, gate_t)


FF_TF = 256


def _expert_ffn_kernel(xg_ref, wg_ref, wu_ref, wd_ref, gs_ref, y_ref, acc_ref):
    f = pl.program_id(1)

    @pl.when(f == 0)
    def _():
        acc_ref[...] = jnp.zeros_like(acc_ref)

    x = xg_ref[0].reshape(BATCH * CAP, D_MODEL)
    gate = _dot(x, wg_ref[0, 0].astype(BF16))
    up = _dot(x, wu_ref[0, 0].astype(BF16))
    act = (gate * (1.0 / (1.0 + jnp.exp(-gate))) * up).astype(BF16)
    acc_ref[...] += _dot(act, wd_ref[0, 0].astype(BF16))

    @pl.when(f == pl.num_programs(1) - 1)
    def _():
        y = acc_ref[...] * gs_ref[0].reshape(BATCH * CAP, 1)
        y_ref[0] = y.astype(BF16).reshape(BATCH, CAP, D_MODEL)


def _expert_ffn(xg, w_gate, w_up, w_down, gs, layer):
    return pl.pallas_call(
        _expert_ffn_kernel,
        grid=(N_EXPERTS, EXPERT_FF // FF_TF),
        in_specs=[
            pl.BlockSpec((1, BATCH, CAP, D_MODEL), lambda e, f: (e, 0, 0, 0)),
            pl.BlockSpec((1, 1, D_MODEL, FF_TF), lambda e, f: (layer, e, 0, f)),
            pl.BlockSpec((1, 1, D_MODEL, FF_TF), lambda e, f: (layer, e, 0, f)),
            pl.BlockSpec((1, 1, FF_TF, D_MODEL), lambda e, f: (layer, e, f, 0)),
            pl.BlockSpec((1, BATCH, CAP, 1), lambda e, f: (e, 0, 0, 0)),
        ],
        out_specs=pl.BlockSpec((1, BATCH, CAP, D_MODEL), lambda e, f: (e, 0, 0, 0)),
        out_shape=jax.ShapeDtypeStruct((N_EXPERTS, BATCH, CAP, D_MODEL), BF16),
        scratch_shapes=[pltpu.VMEM((BATCH * CAP, D_MODEL), F32)],
        compiler_params=_cparams("parallel", "arbitrary"),
        name="moe_ffn",
    )(xg, w_gate, w_up, w_down, gs)


CB_TM = 512
CB_EXPERTS = 4


def _combine_kernel(h_ref, selpos_ref, y_ref, *rest):
    o_ref = rest[-1]
    c = pl.program_id(2)

    @pl.when(c == 0)
    def _():
        o_ref[...] = h_ref[...]

    sp = selpos_ref[0]
    lane = lax.broadcasted_iota(jnp.int32, sp.shape, 1)
    slot = lax.broadcasted_iota(jnp.int32, (CB_TM, CAP), 1).astype(F32)
    pieces = []
    for k in range(CB_EXPERTS):
        col = jnp.sum(jnp.where(lane == c * CB_EXPERTS + k, sp, 0.0), axis=1, keepdims=True)
        pieces.append(jnp.where(col == slot, 1.0, 0.0).astype(BF16))
    onehot = jnp.concatenate(pieces, axis=1)
    o_ref[0] += _dot(onehot, y_ref[...].reshape(CB_EXPERTS * CAP, D_MODEL))

    if len(rest) == 2:
        @pl.when(c == pl.num_programs(2) - 1)
        def _():
            o_ref[0] = _rms(o_ref[0], rest[0][...])


def _combine(h, selpos, y, final_g=None):
    in_specs = [
        pl.BlockSpec((1, CB_TM, D_MODEL), lambda b, i, c: (b, i, 0)),
        pl.BlockSpec((1, CB_TM, N_EXPERTS), lambda b, i, c: (b, i, 0)),
        pl.BlockSpec((CB_EXPERTS, 1, CAP, D_MODEL), lambda b, i, c: (c, b, 0, 0)),
    ]
    args = [h, selpos, y]
    if final_g is not None:
        in_specs.append(pl.BlockSpec((1, D_MODEL), lambda b, i, c: (0, 0)))
        args.append(final_g.reshape(1, D_MODEL).astype(F32))
    return pl.pallas_call(
        _combine_kernel,
        grid=(BATCH, SEQ // CB_TM, N_EXPERTS // CB_EXPERTS),
        in_specs=in_specs,
        out_specs=pl.BlockSpec((1, CB_TM, D_MODEL), lambda b, i, c: (b, i, 0)),
        out_shape=jax.ShapeDtypeStruct((BATCH, SEQ, D_MODEL), F32),
        compiler_params=_cparams("parallel", "parallel", "arbitrary"),
        name="moe_combine",
    )(*args)


def _angles(pos, dim, theta):
    inv = jnp.power(jnp.float32(theta), -(jnp.arange(0, dim, 2, dtype=jnp.float32) / dim))
    ang = pos.astype(jnp.float32)[:, None] * inv[None, :]
    return jnp.cos(ang), jnp.sin(ang)


def _rot_tables(groups, width=HEAD_DIM):
    cos_parts, sneg_parts, spos_parts = [], [], []
    used = 0
    for c, s in groups:
        z = jnp.zeros_like(s)
        cos_parts += [c, c]
        sneg_parts += [-s, z]
        spos_parts += [z, s]
        used += 2 * c.shape[1]
    rest = width - used
    if rest:
        cos_parts.append(jnp.ones((SEQ, rest), F32))
        sneg_parts.append(jnp.zeros((SEQ, rest), F32))
        spos_parts.append(jnp.zeros((SEQ, rest), F32))
    return (jnp.concatenate(cos_parts, axis=1), jnp.concatenate(sneg_parts, axis=1),
            jnp.concatenate(spos_parts, axis=1))


def _gqa_w_in(w_in, scale_q):
    if scale_q:
        col = jnp.concatenate([jnp.full((N_HEADS * HEAD_DIM,), HEAD_DIM ** -0.5 * LOG2E, F32),
                               jnp.ones((2 * N_KV_HEADS * HEAD_DIM,), F32)])
        w_in = w_in * col[None, :]
    return w_in.astype(BF16)


def _mixer_gqa_in(h2, g, w_in, scale_q):
    qkv = _norm_matmul(h2, g, _gqa_w_in(w_in, scale_q), tm=1024, tn=1024, head_w=HEAD_DIM)
    return qkv.reshape(N_HEADS + 2 * N_KV_HEADS, BATCH, SEQ, HEAD_DIM)


def _mla_weights(w_in, w_uq):
    w_in = jnp.pad(w_in, ((0, 0), (0, MLA_LAT_PAD - w_in.shape[-1]))).astype(BF16)
    w_uq = jnp.pad((w_uq * (MLA_QK_DIM ** -0.5 * LOG2E)).reshape(MLA_Q_RANK, N_HEADS, MLA_QK_DIM),
                   ((0, 0), (0, 0), (0, MLA_Q_PAD - MLA_QK_DIM))).reshape(MLA_Q_RANK, N_HEADS * MLA_Q_PAD)
    return w_in, w_uq.astype(BF16)


def kernel(x, mem, norm_mix_g, norm_xa_g, norm_ffn_g, a_w_in, a_sink, a_w_out, b_w_in, b_q_norm_g, b_k_norm_g, b_w_out, c_w_in, c_q_lat_norm_g, c_kv_lat_norm_g, c_w_uq, c_w_ukv, c_w_out, d_w_in, d_rpb, d_w_out, mem_norm_g, xa_wq, xa_wkv, xa_wo, router_w, moe_w_gate, moe_w_up, moe_w_down, final_norm_g):
    t = BATCH * SEQ
    pos = jnp.arange(SEQ)
    tab_a = _rot_tables([_angles(pos, ROT_DIM, ROPE_THETA)])
    tab_b = _rot_tables([_angles(pos // GRID_W, HEAD_DIM // 2, AXIAL_THETA),
                         _angles(pos % GRID_W, HEAD_DIM // 2, AXIAL_THETA)])
    tab_c = _rot_tables([_angles(pos, MLA_ROPE_DIM, ROPE_THETA)])
    mem2 = mem.reshape(BATCH * MEM_TOKENS, D_MODEL)

    h = x
    for i in range(DEPTH):
        m, j = i % N_MIXERS, i // N_MIXERS
        h2 = h.reshape(t, D_MODEL)
        if m == 0:
            qkv = _mixer_gqa_in(h2, norm_mix_g[i], a_w_in[j], True)
            att = _attn_window(qkv, a_sink[j] * LOG2E, tab_a)
            w_out = a_w_out[j]
        elif m == 1:
            qkv = _mixer_gqa_in(h2, norm_mix_g[i], b_w_in[j], False)
            att = _attn_axial(qkv, b_q_norm_g[j] * (HEAD_DIM ** -0.5 * LOG2E), b_k_norm_g[j], tab_b)
            w_out = b_w_out[j]
        elif m == 2:
            w_in, w_uq = _mla_weights(c_w_in[j], c_w_uq[j])
            lat = _norm_matmul(h2, norm_mix_g[i], w_in, tm=1024, tn=MLA_LAT_PAD // 3)
            qfull = _norm_matmul(lat, c_q_lat_norm_g[j], w_uq, col_block=0, tm=1024, tn=1024, head_w=MLA_Q_PAD)
            kvfull = _norm_matmul(lat, c_kv_lat_norm_g[j], c_w_ukv[j].astype(BF16), col_block=1, tm=1024, tn=1024,
                                  head_w=MLA_NOPE_DIM)
            k_rope = lat[:, MLA_Q_RANK + MLA_KV_RANK:].reshape(BATCH, SEQ, HEAD_DIM)
            att = _attn_mla(qfull.reshape(N_HEADS, BATCH, SEQ, MLA_Q_PAD),
                            kvfull.reshape(2 * N_HEADS, BATCH, SEQ, MLA_NOPE_DIM), k_rope, tab_c)
            w_out = c_w_out[j]
        else:
            qkv = _mixer_gqa_in(h2, norm_mix_g[i], d_w_in[j], True)
            att = _attn_na(qkv, _na_bias(d_rpb[j] * LOG2E))
            w_out = d_w_out[j]

        kv = _norm_matmul(mem2, mem_norm_g, xa_wkv[i].astype(BF16), tm=512, tn=512)
        h, xn, aff = _xattn(h, att, w_out.astype(BF16), norm_xa_g[i],
                            (xa_wq[i] * (XA_HEAD_DIM ** -0.5 * LOG2E)).astype(BF16),
                            kv.reshape(BATCH, MEM_TOKENS, 2 * XA_INNER), xa_wo[i].astype(BF16),
                            norm_ffn_g[i], router_w[i])

        selpos, selpos_t, gate_t = _topk(aff)
        xg, gs = _gather(xn, selpos_t, gate_t)
        y = _expert_ffn(xg, moe_w_gate, moe_w_up, moe_w_down, gs, i)
        h = _combine(h, selpos, y, final_norm_g if i == DEPTH - 1 else None)

    return h
```

```python
import jax
import jax.numpy as jnp
from jax import lax
from jax.experimental import pallas as pl
from jax.experimental.pallas import tpu as pltpu

F32 = jnp.float32
BF16 = jnp.bfloat16

D_MODEL = 2048
BATCH = 4
SEQ = 2048
DEPTH = 4
N_MIXERS = 4
HEAD_DIM = 128
N_HEADS = 16
N_KV_HEADS = 4
GQA_GROUP = N_HEADS // N_KV_HEADS
GQA_IN = (N_HEADS + 2 * N_KV_HEADS) * HEAD_DIM
Q_BLOCK = 128
WINDOW = 128
ROPE_THETA = 500000.0
ROT_DIM = HEAD_DIM // 4
AXIAL_THETA = 10000.0
GRID_W = 64
GRID_H = SEQ // GRID_W
MLA_Q_RANK = 512
MLA_KV_RANK = 512
MLA_NOPE_DIM = 128
MLA_ROPE_DIM = 64
MLA_V_DIM = 128
MLA_QK_DIM = MLA_NOPE_DIM + MLA_ROPE_DIM
MLA_Q_PAD = 256
MLA_LAT_PAD = 1152
NA_ROWS = 8
NA_COLS = 16
MEM_TOKENS = 256
XA_HEADS = 4
XA_HEAD_DIM = 128
XA_INNER = XA_HEADS * XA_HEAD_DIM
N_EXPERTS = 16
EC_CAPACITY = 2
EXPERT_FF = 1024
CAP = EC_CAPACITY * SEQ // N_EXPERTS
EPS = 1e-6
MASKED = -1e30
LOG2E = 1.4426950408889634
LANES = 128

VMEM_LIMIT_BYTES = 56 * 1024 * 1024


def _cparams(*sem):
    return pltpu.CompilerParams(dimension_semantics=sem, vmem_limit_bytes=VMEM_LIMIT_BYTES)


def _dot_nt(a, b):
    return lax.dot_general(a, b, (((1,), (1,)), ((), ())), preferred_element_type=F32)


def _dot(a, b):
    return jnp.dot(a, b, preferred_element_type=F32)


def _rms(x, g):
    ms = jnp.mean(x * x, axis=-1, keepdims=True)
    return x * lax.rsqrt(ms + EPS) * g


def _rot(x, cos, sneg, spos, shift):
    n = x.shape[-1]
    return x * cos + pltpu.roll(x, n - shift, 1) * sneg + pltpu.roll(x, shift, 1) * spos


PV_CHUNK = 512


def _softmax_pv(s, v_ones, sink=None):
    dv = v_ones.shape[1] // 2
    keys = s.shape[1]
    m = jnp.max(s, axis=-1, keepdims=True)
    if sink is not None:
        m = jnp.maximum(m, sink)
    r = None
    for j in range(0, keys, PV_CHUNK):
        part = _dot(jnp.exp2(s[:, j:j + PV_CHUNK] - m).astype(BF16), v_ones[j:j + PV_CHUNK])
        r = part if r is None else r + part
    den = r[:, dv:]
    if sink is not None:
        den = den + jnp.exp2(sink - m)
    return r[:, :dv] * (1.0 / den)


def _fill_v_ones(v_scr, v):
    dv = v.shape[1]
    v_scr[:, :dv] = v
    v_scr[:, dv:] = jnp.ones_like(v)


def _pairs(n_blocks, prep, scores, finish, bufs):
    q_a, q_b, s_a, s_b = bufs
    last = n_blocks - 1
    q_b[...] = prep(0)
    s_b[...] = scores(0, q_b[...])
    q_a[...] = prep(1)

    def pair(i, carry):
        n = 2 * i
        s_a[...] = scores(n + 1, q_a[...])
        q_b[...] = prep(jnp.minimum(n + 2, last))
        finish(n, s_b[...])
        s_b[...] = scores(jnp.minimum(n + 2, last), q_b[...])
        q_a[...] = prep(jnp.minimum(n + 3, last))
        finish(n + 1, s_a[...])
        return carry

    lax.fori_loop(0, n_blocks // 2, pair, 0)


def _pipeline_bufs(rows, dq, keys):
    return [pltpu.VMEM((rows, dq), BF16), pltpu.VMEM((rows, dq), BF16),
            pltpu.VMEM((rows, keys), F32), pltpu.VMEM((rows, keys), F32)]


def _norm_matmul_kernel(x_ref, g_ref, w_ref, o_ref, xn_ref):
    @pl.when(pl.program_id(1) == 0)
    def _():
        xn_ref[...] = _rms(x_ref[...].astype(F32), g_ref[...]).astype(BF16)

    res = _dot(xn_ref[...], w_ref[...])
    if len(o_ref.shape) == 3:
        head_w = o_ref.shape[2]
        for hh in range(o_ref.shape[0]):
            o_ref[hh] = res[:, hh * head_w:(hh + 1) * head_w].astype(o_ref.dtype)
    else:
        o_ref[...] = res.astype(o_ref.dtype)


def _norm_matmul(x, g, w, *, col_block=0, tm, tn, head_w=None):
    t = x.shape[0]
    k, n = w.shape
    if head_w is None:
        out_spec = pl.BlockSpec((tm, tn), lambda i, j: (i, j))
        out_shape = jax.ShapeDtypeStruct((t, n), BF16)
    else:
        out_spec = pl.BlockSpec((tn // head_w, tm, head_w), lambda i, j: (j, i, 0))
        out_shape = jax.ShapeDtypeStruct((n // head_w, t, head_w), BF16)
    return pl.pallas_call(
        _norm_matmul_kernel,
        grid=(t // tm, n // tn),
        in_specs=[
            pl.BlockSpec((tm, k), lambda i, j: (i, col_block)),
            pl.BlockSpec((1, k), lambda i, j: (0, 0)),
            pl.BlockSpec((k, tn), lambda i, j: (0, j)),
        ],
        out_specs=out_spec,
        out_shape=out_shape,
        scratch_shapes=[pltpu.VMEM((tm, k), BF16)],
        compiler_params=_cparams("parallel", "arbitrary"),
        name="norm_matmul",
    )(x, g.reshape(1, k).astype(F32), w)


def _attn_window_kernel(sink_ref, q_ref, k_ref, v_ref, cos_ref, sneg_ref, spos_ref, o_ref, k_scr, v_scr, *bufs):
    kvh = pl.program_id(1)
    half = ROT_DIM // 2
    nb = SEQ // Q_BLOCK
    rows, keys = GQA_GROUP * Q_BLOCK, 3 * Q_BLOCK
    k_scr[...] = _rot(k_ref[0, 0].astype(F32), cos_ref[...], sneg_ref[...], spos_ref[...], half).astype(BF16)
    _fill_v_ones(v_scr, v_ref[0, 0])
    r = lax.broadcasted_iota(jnp.int32, (rows, keys), 0)
    c = lax.broadcasted_iota(jnp.int32, (rows, keys), 1)
    base = (r & (Q_BLOCK - 1)) - c
    sink = jnp.concatenate(
        [jnp.full((Q_BLOCK, 1), sink_ref[kvh * GQA_GROUP + g], F32) for g in range(GQA_GROUP)], axis=0)

    def window_start(n):
        return pl.multiple_of(jnp.clip(n - 1, 0, nb - 3) * Q_BLOCK, Q_BLOCK)

    def prep(n):
        q0 = pl.multiple_of(n * Q_BLOCK, Q_BLOCK)
        cos = cos_ref[pl.ds(q0, Q_BLOCK), :]
        sneg = sneg_ref[pl.ds(q0, Q_BLOCK), :]
        spos = spos_ref[pl.ds(q0, Q_BLOCK), :]
        return jnp.concatenate(
            [_rot(q_ref[g, 0, pl.ds(q0, Q_BLOCK), :].astype(F32),
                  cos, sneg, spos, half).astype(BF16) for g in range(GQA_GROUP)], axis=0)

    def scores(n, q4):
        return _dot_nt(q4, k_scr[pl.ds(window_start(n), keys), :])

    def finish(n, s):
        q0 = pl.multiple_of(n * Q_BLOCK, Q_BLOCK)
        k0 = window_start(n)
        s = jnp.where(jnp.abs(base + (q0 - k0)) <= WINDOW, s, MASKED)
        o = _softmax_pv(s, v_scr[pl.ds(k0, keys), :], sink)
        for g in range(GQA_GROUP):
            o_ref[g, 0, pl.ds(q0, Q_BLOCK), :] = (
                o[g * Q_BLOCK:(g + 1) * Q_BLOCK].astype(o_ref.dtype))

    _pairs(nb, prep, scores, finish, bufs)


def _gqa_specs():
    head = (1, 1, SEQ, HEAD_DIM)
    group = (GQA_GROUP, 1, SEQ, HEAD_DIM)
    return [
        pl.BlockSpec(group, lambda b, h: (h, b, 0, 0)),
        pl.BlockSpec(head, lambda b, h: (N_HEADS + h, b, 0, 0)),
        pl.BlockSpec(head, lambda b, h: (N_HEADS + N_KV_HEADS + h, b, 0, 0)),
    ], pl.BlockSpec(group, lambda b, h: (h, b, 0, 0))


_ATT_SHAPE = jax.ShapeDtypeStruct((N_HEADS, BATCH, SEQ, HEAD_DIM), BF16)


def _attn_window(qkv, sink, tables):
    cos, sneg, spos = tables
    qkv_specs, out_spec = _gqa_specs()
    tab = pl.BlockSpec((SEQ, HEAD_DIM), lambda b, h: (0, 0))
    return pl.pallas_call(
        _attn_window_kernel,
        grid=(BATCH, N_KV_HEADS),
        in_specs=[pl.BlockSpec(memory_space=pltpu.SMEM)] + qkv_specs + [tab, tab, tab],
        out_specs=out_spec,
        out_shape=_ATT_SHAPE,
        scratch_shapes=[pltpu.VMEM((SEQ, HEAD_DIM), BF16), pltpu.VMEM((SEQ, 2 * HEAD_DIM), BF16)]
        + _pipeline_bufs(GQA_GROUP * Q_BLOCK, HEAD_DIM, 3 * Q_BLOCK),
        compiler_params=_cparams("parallel", "parallel"),
        name="attn_window",
    )(sink.astype(F32), qkv, qkv, qkv, cos, sneg, spos)


AX_TQ = 128


def _attn_axial_kernel(q_ref, k_ref, v_ref, qg_ref, kg_ref, cos_ref, sneg_ref, spos_ref, o_ref, k_scr, v_scr, *bufs):
    quarter = HEAD_DIM // 4
    kn = _rms(k_ref[0, 0].astype(F32), kg_ref[...])
    k_scr[...] = _rot(kn, cos_ref[...], sneg_ref[...], spos_ref[...], quarter).astype(BF16)
    _fill_v_ones(v_scr, v_ref[0, 0])

    def prep(n):
        q0 = pl.multiple_of(n * AX_TQ, AX_TQ)
        cos = cos_ref[pl.ds(q0, AX_TQ), :]
        sneg = sneg_ref[pl.ds(q0, AX_TQ), :]
        spos = spos_ref[pl.ds(q0, AX_TQ), :]
        return jnp.concatenate(
            [_rot(_rms(q_ref[g, 0, pl.ds(q0, AX_TQ), :].astype(F32), qg_ref[...]),
                  cos, sneg, spos, quarter).astype(BF16) for g in range(GQA_GROUP)], axis=0)

    def scores(n, q4):
        return _dot_nt(q4, k_scr[...])

    def finish(n, s):
        q0 = pl.multiple_of(n * AX_TQ, AX_TQ)
        o = _softmax_pv(s, v_scr[...])
        for g in range(GQA_GROUP):
            o_ref[g, 0, pl.ds(q0, AX_TQ), :] = (
                o[g * AX_TQ:(g + 1) * AX_TQ].astype(o_ref.dtype))

    _pairs(SEQ // AX_TQ, prep, scores, finish, bufs)


def _attn_axial(qkv, qg, kg, tables):
    cos, sneg, spos = tables
    qkv_specs, out_spec = _gqa_specs()
    tab = pl.BlockSpec((SEQ, HEAD_DIM), lambda b, h: (0, 0))
    gain = pl.BlockSpec((1, HEAD_DIM), lambda b, h: (0, 0))
    return pl.pallas_call(
        _attn_axial_kernel,
        grid=(BATCH, N_KV_HEADS),
        in_specs=qkv_specs + [gain, gain, tab, tab, tab],
        out_specs=out_spec,
        out_shape=_ATT_SHAPE,
        scratch_shapes=[pltpu.VMEM((SEQ, HEAD_DIM), BF16), pltpu.VMEM((SEQ, 2 * HEAD_DIM), BF16)]
        + _pipeline_bufs(GQA_GROUP * AX_TQ, HEAD_DIM, SEQ),
        compiler_params=_cparams("parallel", "parallel"),
        name="attn_axial",
    )(qkv, qkv, qkv, qg.reshape(1, HEAD_DIM).astype(F32), kg.reshape(1, HEAD_DIM).astype(F32), cos, sneg, spos)


MLA_TQ = 512


def _attn_mla_kernel(q_ref, kn_ref, v_ref, kr_ref, cos_ref, sneg_ref, spos_ref, o_ref, k_scr, v_scr, *bufs):
    half = MLA_ROPE_DIM // 2
    k_scr[:, :MLA_NOPE_DIM] = kn_ref[0, 0]

    @pl.when(pl.program_id(1) == 0)
    def _():
        k_scr[:, MLA_NOPE_DIM:] = _rot(kr_ref[0].astype(F32), cos_ref[...], sneg_ref[...], spos_ref[...],
                                       half).astype(BF16)

    _fill_v_ones(v_scr, v_ref[0, 0])

    def prep(n):
        q0 = pl.multiple_of(n * MLA_TQ, MLA_TQ)
        cos = cos_ref[pl.ds(q0, MLA_TQ), :]
        sneg = sneg_ref[pl.ds(q0, MLA_TQ), :]
        spos = spos_ref[pl.ds(q0, MLA_TQ), :]
        qn = q_ref[0, 0, pl.ds(q0, MLA_TQ), :MLA_NOPE_DIM]
        qr = _rot(q_ref[0, 0, pl.ds(q0, MLA_TQ), MLA_NOPE_DIM:].astype(F32), cos, sneg, spos, half).astype(BF16)
        return jnp.concatenate([qn, qr], axis=1)

    def scores(n, q):
        return _dot_nt(q, k_scr[...])

    def finish(n, s):
        q0 = pl.multiple_of(n * MLA_TQ, MLA_TQ)
        o_ref[0, 0, pl.ds(q0, MLA_TQ), :] = _softmax_pv(s, v_scr[...]).astype(o_ref.dtype)

    _pairs(SEQ // MLA_TQ, prep, scores, finish, bufs)


def _attn_mla(qfull, kvfull, k_rope, tables):
    cos, sneg, spos = tables
    tab = pl.BlockSpec((SEQ, HEAD_DIM), lambda b, h: (0, 0))
    return pl.pallas_call(
        _attn_mla_kernel,
        grid=(BATCH, N_HEADS),
        in_specs=[
            pl.BlockSpec((1, 1, SEQ, MLA_Q_PAD), lambda b, h: (h, b, 0, 0)),
            pl.BlockSpec((1, 1, SEQ, MLA_NOPE_DIM), lambda b, h: (2 * h, b, 0, 0)),
            pl.BlockSpec((1, 1, SEQ, MLA_V_DIM), lambda b, h: (2 * h + 1, b, 0, 0)),
            pl.BlockSpec((1, SEQ, HEAD_DIM), lambda b, h: (b, 0, 0)),
            tab, tab, tab,
        ],
        out_specs=pl.BlockSpec((1, 1, SEQ, MLA_V_DIM), lambda b, h: (h, b, 0, 0)),
        out_shape=_ATT_SHAPE,
        scratch_shapes=[pltpu.VMEM((SEQ, MLA_Q_PAD), BF16), pltpu.VMEM((SEQ, 2 * MLA_V_DIM), BF16)]
        + _pipeline_bufs(MLA_TQ, MLA_Q_PAD, SEQ),
        compiler_params=_cparams("parallel", "arbitrary"),
        name="attn_mla",
    )(qfull, kvfull, kvfull, k_rope, cos, sneg, spos)


NA_DELTAS = NA_ROWS
NA_KEYS = NA_ROWS * GRID_W
NA_DR = 2 * NA_ROWS - 1
NA_DC = 2 * NA_COLS - 1


def _na_bias_kernel(rpb_ref, o_ref):
    h = pl.program_id(0)
    shape = (GRID_W, 2 * GRID_W)
    qc = lax.broadcasted_iota(jnp.int32, shape, 0)
    lane = lax.broadcasted_iota(jnp.int32, shape, 1)
    kc = jnp.where(lane >= GRID_W, lane - GRID_W, lane)
    dc = kc - qc + (NA_COLS - 1)
    c_start = jnp.clip(qc - NA_COLS // 2, 0, GRID_W - NA_COLS)
    in_win = (kc >= c_start) & (kc < c_start + NA_COLS)
    low = lane < GRID_W
    tiles = [jnp.zeros(shape, F32) for _ in range(NA_DR)]
    for d in range(NA_DC):
        hit = dc == d
        for dr in range(NA_DR):
            tiles[dr] = jnp.where(hit, rpb_ref[(h * NA_DR + dr) * NA_DC + d], tiles[dr])
    for delta in range(NA_DELTAS):
        for j in range(NA_ROWS // 2):
            t = jnp.where(low, tiles[delta + 2 * j], tiles[delta + 2 * j + 1])
            o_ref[0, delta, :, j * 2 * GRID_W:(j + 1) * 2 * GRID_W] = jnp.where(in_win, t, MASKED)


def _na_bias(rpb):
    return pl.pallas_call(
        _na_bias_kernel,
        grid=(N_HEADS,),
        in_specs=[pl.BlockSpec(memory_space=pltpu.SMEM)],
        out_specs=pl.BlockSpec((1, NA_DELTAS, GRID_W, NA_KEYS), lambda h: (h, 0, 0, 0)),
        out_shape=jax.ShapeDtypeStruct((N_HEADS, NA_DELTAS, GRID_W, NA_KEYS), F32),
        compiler_params=_cparams("parallel"),
        name="na_bias",
    )(rpb.astype(F32).reshape(-1))


def _attn_na_kernel(q_ref, k_ref, v_ref, bias_ref, o_ref, v_scr, *bufs):
    _fill_v_ones(v_scr, v_ref[0, 0])

    def first_key_row(r):
        return jnp.clip(r - NA_ROWS // 2, 0, GRID_H - NA_ROWS)

    def prep(r):
        q0 = pl.multiple_of(r * GRID_W, GRID_W)
        return jnp.concatenate([q_ref[g, 0, pl.ds(q0, GRID_W), :] for g in range(GQA_GROUP)], axis=0)

    def scores(r, q4):
        k0 = pl.multiple_of(first_key_row(r) * GRID_W, GRID_W)
        return _dot_nt(q4, k_ref[0, 0, pl.ds(k0, NA_KEYS), :])

    def finish(r, s):
        r0 = first_key_row(r)
        delta = r0 - r + (NA_ROWS - 1)
        q0 = pl.multiple_of(r * GRID_W, GRID_W)
        k0 = pl.multiple_of(r0 * GRID_W, GRID_W)
        bias = jnp.concatenate([bias_ref[g, delta] for g in range(GQA_GROUP)], axis=0)
        o = _softmax_pv(s + bias, v_scr[pl.ds(k0, NA_KEYS), :])
        for g in range(GQA_GROUP):
            o_ref[g, 0, pl.ds(q0, GRID_W), :] = (
                o[g * GRID_W:(g + 1) * GRID_W].astype(o_ref.dtype))

    _pairs(GRID_H, prep, scores, finish, bufs)


def _attn_na(qkv, bias):
    qkv_specs, out_spec = _gqa_specs()
    return pl.pallas_call(
        _attn_na_kernel,
        grid=(BATCH, N_KV_HEADS),
        in_specs=qkv_specs + [pl.BlockSpec((GQA_GROUP, NA_DELTAS, GRID_W, NA_KEYS), lambda b, h: (h, 0, 0, 0))],
        out_specs=out_spec,
        out_shape=_ATT_SHAPE,
        scratch_shapes=[pltpu.VMEM((SEQ, 2 * HEAD_DIM), BF16)]
        + _pipeline_bufs(GQA_GROUP * GRID_W, HEAD_DIM, NA_KEYS),
        compiler_params=_cparams("parallel", "parallel"),
        name="attn_na",
    )(qkv, qkv, qkv, bias)


XA_TM = 256


def _xattn_kernel(h_ref, att_ref, wout_ref, g_ref, wq_ref, kv_ref, wo_ref, gf_ref, rw_ref, rb_ref,
                  o_ref, xn_ref, aff_ref):
    att = jnp.concatenate([att_ref[hh, 0] for hh in range(N_HEADS)], axis=1)
    x = h_ref[0] + _dot(att, wout_ref[...])
    xn = _rms(x, g_ref[...]).astype(BF16)
    q = _dot(xn, wq_ref[...]).astype(BF16)
    outs = []
    ones = jnp.ones((MEM_TOKENS, XA_HEAD_DIM), BF16)
    for hh in range(XA_HEADS):
        k = kv_ref[0, :, hh * XA_HEAD_DIM:(hh + 1) * XA_HEAD_DIM]
        v = kv_ref[0, :, XA_INNER + hh * XA_HEAD_DIM:XA_INNER + (hh + 1) * XA_HEAD_DIM]
        s = _dot_nt(q[:, hh * XA_HEAD_DIM:(hh + 1) * XA_HEAD_DIM], k)
        outs.append(_softmax_pv(s, jnp.concatenate([v, ones], axis=1)).astype(BF16))
    y = x + _dot(jnp.concatenate(outs, axis=1), wo_ref[...])
    o_ref[0] = y
    yn = _rms(y, gf_ref[...]).astype(BF16)
    xn_ref[0] = yn
    logits = _dot(yn, rw_ref[...]) + rb_ref[...]
    e = jnp.exp(logits - jnp.max(logits, axis=-1, keepdims=True))
    aff_ref[0] = e / jnp.sum(e, axis=-1, keepdims=True)


def _xattn(h, att, w_out, g, wq, kv, layer, wo, g_ffn, router_w):
    full = lambda b, i: (0, 0)
    tile = lambda b, i: (b, i, 0)
    rw = jnp.pad(router_w, ((0, 0), (0, LANES - N_EXPERTS))).astype(BF16)
    rb = jnp.where(jnp.arange(LANES) < N_EXPERTS, 0.0, MASKED).astype(F32).reshape(1, LANES)
    return pl.pallas_call(
        _xattn_kernel,
        grid=(BATCH, SEQ // XA_TM),
        in_specs=[
            pl.BlockSpec((1, XA_TM, D_MODEL), lambda b, i: (b, i, 0)),
            pl.BlockSpec((N_HEADS, 1, XA_TM, HEAD_DIM), lambda b, i: (0, b, i, 0)),
            pl.BlockSpec(w_out.shape, full),
            pl.BlockSpec((1, D_MODEL), full),
            pl.BlockSpec((D_MODEL, XA_INNER), full),
            pl.BlockSpec((1, MEM_TOKENS, 2 * XA_INNER), lambda b, i: (b, 0, layer)),
            pl.BlockSpec((XA_INNER, D_MODEL), full),
            pl.BlockSpec((1, D_MODEL), full),
            pl.BlockSpec((D_MODEL, LANES), full),
            pl.BlockSpec((1, LANES), full),
        ],
        out_specs=[
            pl.BlockSpec((1, XA_TM, D_MODEL), tile),
            pl.BlockSpec((1, XA_TM, D_MODEL), tile),
            pl.BlockSpec((1, XA_TM, LANES), tile),
        ],
        out_shape=[
            jax.ShapeDtypeStruct((BATCH, SEQ, D_MODEL), F32),
            jax.ShapeDtypeStruct((BATCH, SEQ, D_MODEL), BF16),
            jax.ShapeDtypeStruct((BATCH, SEQ, LANES), F32),
        ],
        compiler_params=_cparams("parallel", "parallel"),
        name="outproj_xattn",
    )(h, att, w_out, g.reshape(1, D_MODEL).astype(F32), wq, kv, wo, g_ffn.reshape(1, D_MODEL).astype(F32), rw, rb)


PREFIX_BLOCK = 256
TOPK_MAX_ITERS = 256


def _topk_kernel(aff_ref, selpos_ref, selpos_t_ref, gate_t_ref):
    a = jnp.transpose(aff_ref[0])[:N_EXPERTS]
    gate_t_ref[0] = a
    kf = jnp.float32(CAP)

    def count(mask):
        return jnp.sum(jnp.where(mask, 1.0, 0.0), axis=1, keepdims=True)

    def cond(c):
        it, _, _, _, done = c
        return jnp.logical_and(it < TOPK_MAX_ITERS, done == 0)

    def body(c):
        it, lo, hi, _, _ = c
        mid = 0.5 * (lo + hi)
        take = count(a >= mid) >= kf
        lo = jnp.where(take, mid, lo)
        hi = jnp.where(take, hi, mid)
        top = jnp.max(jnp.where(a < hi, a, -1.0), axis=1, keepdims=True)
        bot = jnp.min(jnp.where(a >= lo, a, 3.0), axis=1, keepdims=True)
        done = jnp.min(jnp.where(top == bot, 1, 0))
        return it + 1, lo, hi, top, done

    init = (jnp.int32(0), jnp.zeros((N_EXPERTS, 1), F32), jnp.full((N_EXPERTS, 1), 2.0, F32),
            jnp.zeros((N_EXPERTS, 1), F32), jnp.int32(0))
    _, _, _, kth, _ = lax.while_loop(cond, body, init)

    gt = a > kth
    eq = a == kth
    need = kf - count(gt)
    tok = lax.broadcasted_iota(jnp.int32, (N_EXPERTS, SEQ), 1).astype(F32)
    cut = jnp.zeros((N_EXPERTS, 1), F32)
    step = SEQ // 2
    while step >= 1:
        cand = cut + step
        cut = jnp.where(count(eq & (tok < cand)) < need, cand, cut)
        step //= 2
    sel = jnp.where(gt, 1.0, jnp.where(eq & (tok <= cut), 1.0, 0.0))

    r = lax.broadcasted_iota(jnp.int32, (PREFIX_BLOCK, PREFIX_BLOCK), 0)
    c = lax.broadcasted_iota(jnp.int32, (PREFIX_BLOCK, PREFIX_BLOCK), 1)
    utri = jnp.where(r < c, 1.0, 0.0).astype(BF16)
    off = jnp.zeros((N_EXPERTS, 1), F32)
    for j in range(SEQ // PREFIX_BLOCK):
        blk = sel[:, j * PREFIX_BLOCK:(j + 1) * PREFIX_BLOCK]
        pos = _dot(blk.astype(BF16), utri) + off
        selpos_t_ref[0, :, j * PREFIX_BLOCK:(j + 1) * PREFIX_BLOCK] = jnp.where(blk > 0.5, pos, -1.0)
        off = off + jnp.sum(blk, axis=1, keepdims=True)

    padded = jnp.concatenate([selpos_t_ref[0], jnp.zeros((LANES - N_EXPERTS, SEQ), F32)], axis=0)
    selpos_ref[0] = jnp.transpose(padded)[:, :N_EXPERTS]


def _topk(aff):
    return pl.pallas_call(
        _topk_kernel,
        grid=(BATCH,),
        in_specs=[pl.BlockSpec((1, SEQ, LANES), lambda b: (b, 0, 0))],
        out_specs=[
            pl.BlockSpec((1, SEQ, N_EXPERTS), lambda b: (b, 0, 0)),
            pl.BlockSpec((1, N_EXPERTS, SEQ), lambda b: (b, 0, 0)),
            pl.BlockSpec((1, N_EXPERTS, SEQ), lambda b: (b, 0, 0)),
        ],
        out_shape=[
            jax.ShapeDtypeStruct((BATCH, SEQ, N_EXPERTS), F32),
            jax.ShapeDtypeStruct((BATCH, N_EXPERTS, SEQ), F32),
            jax.ShapeDtypeStruct((BATCH, N_EXPERTS, SEQ), F32),
        ],
        compiler_params=_cparams("parallel"),
        name="expert_topk",
    )(aff)


def _gather_kernel(xn_ref, selpos_t_ref, gate_t_ref, xg_ref, gs_ref):
    e = pl.program_id(1)
    slot_of_token = selpos_t_ref[0, pl.ds(e, 1), :]
    gate_row = gate_t_ref[0, pl.ds(e, 1), :]
    slot = lax.broadcasted_iota(jnp.int32, (CAP, SEQ), 0).astype(F32)
    hit = slot == slot_of_token
    xg_ref[0, 0] = _dot(jnp.where(hit, 1.0, 0.0).astype(BF16), xn_ref[0]).astype(BF16)
    gs_ref[0, 0] = jnp.sum(jnp.where(hit, gate_row, 0.0), axis=1, keepdims=True)


def _gather(xn, selpos_t, gate_t):
    return pl.pallas_call(
        _gather_kernel,
        grid=(BATCH, N_EXPERTS),
        in_specs=[
            pl.BlockSpec((1, SEQ, D_MODEL), lambda b, e: (b, 0, 0)),
            pl.BlockSpec((1, N_EXPERTS, SEQ), lambda b, e: (b, 0, 0)),
            pl.BlockSpec((1, N_EXPERTS, SEQ), lambda b, e: (b, 0, 0)),
        ],
        out_specs=[
            pl.BlockSpec((1, 1, CAP, D_MODEL), lambda b, e: (e, b, 0, 0)),
            pl.BlockSpec((1, 1, CAP, 1), lambda b, e: (e, b, 0, 0)),
        ],
        out_shape=[
            jax.ShapeDtypeStruct((N_EXPERTS, BATCH, CAP, D_MODEL), BF16),
            jax.ShapeDtypeStruct((N_EXPERTS, BATCH, CAP, 1), F32),
        ],
        compiler_params=_cparams("parallel", "arbitrary"),
        name="moe_gather",
    )(xn, selpos_t, gate_t)


FF_TF = 256


def _expert_ffn_kernel(xg_ref, wg_ref, wu_ref, wd_ref, gs_ref, y_ref, acc_ref):
    f = pl.program_id(1)

    @pl.when(f == 0)
    def _():
        acc_ref[...] = jnp.zeros_like(acc_ref)

    x = xg_ref[0].reshape(BATCH * CAP, D_MODEL)
    gate = _dot(x, wg_ref[0, 0].astype(BF16))
    up = _dot(x, wu_ref[0, 0].astype(BF16))
    act = (gate * (1.0 / (1.0 + jnp.exp(-gate))) * up).astype(BF16)
    acc_ref[...] += _dot(act, wd_ref[0, 0].astype(BF16))

    @pl.when(f == pl.num_programs(1) - 1)
    def _():
        y = acc_ref[...] * gs_ref[0].reshape(BATCH * CAP, 1)
        y_ref[0] = y.astype(BF16).reshape(BATCH, CAP, D_MODEL)


def _expert_ffn(xg, w_gate, w_up, w_down, gs, layer):
    return pl.pallas_call(
        _expert_ffn_kernel,
        grid=(N_EXPERTS, EXPERT_FF // FF_TF),
        in_specs=[
            pl.BlockSpec((1, BATCH, CAP, D_MODEL), lambda e, f: (e, 0, 0, 0)),
            pl.BlockSpec((1, 1, D_MODEL, FF_TF), lambda e, f: (layer, e, 0, f)),
            pl.BlockSpec((1, 1, D_MODEL, FF_TF), lambda e, f: (layer, e, 0, f)),
            pl.BlockSpec((1, 1, FF_TF, D_MODEL), lambda e, f: (layer, e, f, 0)),
            pl.BlockSpec((1, BATCH, CAP, 1), lambda e, f: (e, 0, 0, 0)),
        ],
        out_specs=pl.BlockSpec((1, BATCH, CAP, D_MODEL), lambda e, f: (e, 0, 0, 0)),
        out_shape=jax.ShapeDtypeStruct((N_EXPERTS, BATCH, CAP, D_MODEL), BF16),
        scratch_shapes=[pltpu.VMEM((BATCH * CAP, D_MODEL), F32)],
        compiler_params=_cparams("parallel", "arbitrary"),
        name="moe_ffn",
    )(xg, w_gate, w_up, w_down, gs)


CB_TM = 512
CB_EXPERTS = 8


def _combine_kernel(h_ref, selpos_ref, y_ref, *rest):
    o_ref = rest[-1]
    c = pl.program_id(2)

    @pl.when(c == 0)
    def _():
        o_ref[...] = h_ref[...]

    sp = selpos_ref[0]
    lane = lax.broadcasted_iota(jnp.int32, sp.shape, 1)
    slot = lax.broadcasted_iota(jnp.int32, (CB_TM, CAP), 1).astype(F32)
    pieces = []
    for k in range(CB_EXPERTS):
        col = jnp.sum(jnp.where(lane == c * CB_EXPERTS + k, sp, 0.0), axis=1, keepdims=True)
        pieces.append(jnp.where(col == slot, 1.0, 0.0).astype(BF16))
    onehot = jnp.concatenate(pieces, axis=1)
    o_ref[0] += _dot(onehot, y_ref[...].reshape(CB_EXPERTS * CAP, D_MODEL))

    if len(rest) == 2:
        @pl.when(c == pl.num_programs(2) - 1)
        def _():
            o_ref[0] = _rms(o_ref[0], rest[0][...])


def _combine(h, selpos, y, final_g=None):
    in_specs = [
        pl.BlockSpec((1, CB_TM, D_MODEL), lambda b, i, c: (b, i, 0)),
        pl.BlockSpec((1, CB_TM, N_EXPERTS), lambda b, i, c: (b, i, 0)),
        pl.BlockSpec((CB_EXPERTS, 1, CAP, D_MODEL), lambda b, i, c: (c, b, 0, 0)),
    ]
    args = [h, selpos, y]
    if final_g is not None:
        in_specs.append(pl.BlockSpec((1, D_MODEL), lambda b, i, c: (0, 0)))
        args.append(final_g.reshape(1, D_MODEL).astype(F32))
    return pl.pallas_call(
        _combine_kernel,
        grid=(BATCH, SEQ // CB_TM, N_EXPERTS // CB_EXPERTS),
        in_specs=in_specs,
        out_specs=pl.BlockSpec((1, CB_TM, D_MODEL), lambda b, i, c: (b, i, 0)),
        out_shape=jax.ShapeDtypeStruct((BATCH, SEQ, D_MODEL), F32),
        compiler_params=_cparams("parallel", "parallel", "arbitrary"),
        name="moe_combine",
    )(*args)


def _angles(pos, dim, theta):
    inv = jnp.power(jnp.float32(theta), -(jnp.arange(0, dim, 2, dtype=jnp.float32) / dim))
    ang = pos.astype(jnp.float32)[:, None] * inv[None, :]
    return jnp.cos(ang), jnp.sin(ang)


def _rot_tables(groups, width=HEAD_DIM):
    cos_parts, sneg_parts, spos_parts = [], [], []
    used = 0
    for c, s in groups:
        z = jnp.zeros_like(s)
        cos_parts += [c, c]
        sneg_parts += [-s, z]
        spos_parts += [z, s]
        used += 2 * c.shape[1]
    rest = width - used
    if rest:
        cos_parts.append(jnp.ones((SEQ, rest), F32))
        sneg_parts.append(jnp.zeros((SEQ, rest), F32))
        spos_parts.append(jnp.zeros((SEQ, rest), F32))
    return (jnp.concatenate(cos_parts, axis=1), jnp.concatenate(sneg_parts, axis=1),
            jnp.concatenate(spos_parts, axis=1))


def _gqa_w_in(w_in, scale_q):
    if scale_q:
        col = jnp.concatenate([jnp.full((N_HEADS * HEAD_DIM,), HEAD_DIM ** -0.5 * LOG2E, F32),
                               jnp.ones((2 * N_KV_HEADS * HEAD_DIM,), F32)])
        w_in = w_in * col[None, :]
    return w_in.astype(BF16)


def _mixer_gqa_in(h2, g, w_in, scale_q):
    qkv = _norm_matmul(h2, g, _gqa_w_in(w_in, scale_q), tm=1024, tn=1024, head_w=HEAD_DIM)
    return qkv.reshape(N_HEADS + 2 * N_KV_HEADS, BATCH, SEQ, HEAD_DIM)


def _mla_weights(w_in, w_uq):
    w_in = jnp.pad(w_in, ((0, 0), (0, MLA_LAT_PAD - w_in.shape[-1]))).astype(BF16)
    w_uq = jnp.pad((w_uq * (MLA_QK_DIM ** -0.5 * LOG2E)).reshape(MLA_Q_RANK, N_HEADS, MLA_QK_DIM),
                   ((0, 0), (0, 0), (0, MLA_Q_PAD - MLA_QK_DIM))).reshape(MLA_Q_RANK, N_HEADS * MLA_Q_PAD)
    return w_in, w_uq.astype(BF16)


def kernel(x, mem, norm_mix_g, norm_xa_g, norm_ffn_g, a_w_in, a_sink, a_w_out, b_w_in, b_q_norm_g, b_k_norm_g, b_w_out, c_w_in, c_q_lat_norm_g, c_kv_lat_norm_g, c_w_uq, c_w_ukv, c_w_out, d_w_in, d_rpb, d_w_out, mem_norm_g, xa_wq, xa_wkv, xa_wo, router_w, moe_w_gate, moe_w_up, moe_w_down, final_norm_g):
    t = BATCH * SEQ
    pos = jnp.arange(SEQ)
    tab_a = _rot_tables([_angles(pos, ROT_DIM, ROPE_THETA)])
    tab_b = _rot_tables([_angles(pos // GRID_W, HEAD_DIM // 2, AXIAL_THETA),
                         _angles(pos % GRID_W, HEAD_DIM // 2, AXIAL_THETA)])
    tab_c = _rot_tables([_angles(pos, MLA_ROPE_DIM, ROPE_THETA)])
    wkv_all = jnp.transpose(xa_wkv, (1, 0, 2)).reshape(D_MODEL, DEPTH * 2 * XA_INNER).astype(BF16)
    kv_all = _norm_matmul(mem.reshape(BATCH * MEM_TOKENS, D_MODEL), mem_norm_g, wkv_all, tm=BATCH * MEM_TOKENS,
                          tn=2 * XA_INNER).reshape(BATCH, MEM_TOKENS, DEPTH * 2 * XA_INNER)

    h = x
    for i in range(DEPTH):
        m, j = i % N_MIXERS, i // N_MIXERS
        h2 = h.reshape(t, D_MODEL)
        if m == 0:
            qkv = _mixer_gqa_in(h2, norm_mix_g[i], a_w_in[j], True)
            att = _attn_window(qkv, a_sink[j] * LOG2E, tab_a)
            w_out = a_w_out[j]
        elif m == 1:
            qkv = _mixer_gqa_in(h2, norm_mix_g[i], b_w_in[j], False)
            att = _attn_axial(qkv, b_q_norm_g[j] * (HEAD_DIM ** -0.5 * LOG2E), b_k_norm_g[j], tab_b)
            w_out = b_w_out[j]
        elif m == 2:
            w_in, w_uq = _mla_weights(c_w_in[j], c_w_uq[j])
            lat = _norm_matmul(h2, norm_mix_g[i], w_in, tm=1024, tn=MLA_LAT_PAD)
            qfull = _norm_matmul(lat, c_q_lat_norm_g[j], w_uq, col_block=0, tm=2048, tn=1024, head_w=MLA_Q_PAD)
            kvfull = _norm_matmul(lat, c_kv_lat_norm_g[j], c_w_ukv[j].astype(BF16), col_block=1, tm=2048, tn=1024,
                                  head_w=MLA_NOPE_DIM)
            k_rope = lat[:, MLA_Q_RANK + MLA_KV_RANK:].reshape(BATCH, SEQ, HEAD_DIM)
            att = _attn_mla(qfull.reshape(N_HEADS, BATCH, SEQ, MLA_Q_PAD),
                            kvfull.reshape(2 * N_HEADS, BATCH, SEQ, MLA_NOPE_DIM), k_rope, tab_c)
            w_out = c_w_out[j]
        else:
            qkv = _mixer_gqa_in(h2, norm_mix_g[i], d_w_in[j], True)
            att = _attn_na(qkv, _na_bias(d_rpb[j] * LOG2E))
            w_out = d_w_out[j]

        h, xn, aff = _xattn(h, att, w_out.astype(BF16), norm_xa_g[i],
                            (xa_wq[i] * (XA_HEAD_DIM ** -0.5 * LOG2E)).astype(BF16),
                            kv_all, i, xa_wo[i].astype(BF16), norm_ffn_g[i], router_w[i])

        selpos, selpos_t, gate_t = _topk(aff)
        xg, gs = _gather(xn, selpos_t, gate_t)
        y = _expert_ffn(xg, moe_w_gate, moe_w_up, moe_w_down, gs, i)
        h = _combine(h, selpos, y, final_norm_g if i == DEPTH - 1 else None)

    return h
```

```python
import jax
import jax.numpy as jnp
from jax import lax
from jax.experimental import pallas as pl
from jax.experimental.pallas import tpu as pltpu

F32 = jnp.float32
BF16 = jnp.bfloat16

D_MODEL = 2048
BATCH = 4
SEQ = 2048
DEPTH = 4
N_MIXERS = 4
HEAD_DIM = 128
N_HEADS = 16
N_KV_HEADS = 4
GQA_GROUP = N_HEADS // N_KV_HEADS
GQA_IN = (N_HEADS + 2 * N_KV_HEADS) * HEAD_DIM
Q_BLOCK = 128
WINDOW = 128
ROPE_THETA = 500000.0
ROT_DIM = HEAD_DIM // 4
AXIAL_THETA = 10000.0
GRID_W = 64
GRID_H = SEQ // GRID_W
MLA_Q_RANK = 512
MLA_KV_RANK = 512
MLA_NOPE_DIM = 128
MLA_ROPE_DIM = 64
MLA_V_DIM = 128
MLA_QK_DIM = MLA_NOPE_DIM + MLA_ROPE_DIM
MLA_Q_PAD = 256
MLA_LAT_PAD = 1152
NA_ROWS = 8
NA_COLS = 16
MEM_TOKENS = 256
XA_HEADS = 4
XA_HEAD_DIM = 128
XA_INNER = XA_HEADS * XA_HEAD_DIM
N_EXPERTS = 16
EC_CAPACITY = 2
EXPERT_FF = 1024
CAP = EC_CAPACITY * SEQ // N_EXPERTS
EPS = 1e-6
MASKED = -1e30
LOG2E = 1.4426950408889634
LANES = 128

VMEM_LIMIT_BYTES = 56 * 1024 * 1024


def _cparams(*sem):
    return pltpu.CompilerParams(dimension_semantics=sem, vmem_limit_bytes=VMEM_LIMIT_BYTES)


def _dot_nt(a, b):
    return lax.dot_general(a, b, (((1,), (1,)), ((), ())), preferred_element_type=F32)


def _dot(a, b):
    return jnp.dot(a, b, preferred_element_type=F32)


def _rms(x, g):
    ms = jnp.mean(x * x, axis=-1, keepdims=True)
    return x * lax.rsqrt(ms + EPS) * g


def _rot(x, cos, sneg, spos, shift):
    n = x.shape[-1]
    return x * cos + pltpu.roll(x, n - shift, 1) * sneg + pltpu.roll(x, shift, 1) * spos


PV_CHUNK = 512


def _softmax_pv(s, v_ones, sink=None):
    dv = v_ones.shape[1] // 2
    keys = s.shape[1]
    m = jnp.max(s, axis=-1, keepdims=True)
    if sink is not None:
        m = jnp.maximum(m, sink)
    r = None
    for j in range(0, keys, PV_CHUNK):
        part = _dot(jnp.exp2(s[:, j:j + PV_CHUNK] - m).astype(BF16), v_ones[j:j + PV_CHUNK])
        r = part if r is None else r + part
    den = r[:, dv:]
    if sink is not None:
        den = den + jnp.exp2(sink - m)
    return r[:, :dv] * (1.0 / den)


def _fill_v_ones(v_scr, v):
    dv = v.shape[1]
    v_scr[:, :dv] = v
    v_scr[:, dv:] = jnp.ones_like(v)


def _pairs(n_blocks, prep, scores, finish, bufs):
    q_a, q_b, s_a, s_b = bufs
    last = n_blocks - 1
    q_b[...] = prep(0)
    s_b[...] = scores(0, q_b[...])
    q_a[...] = prep(1)

    def pair(i, carry):
        n = 2 * i
        s_a[...] = scores(n + 1, q_a[...])
        q_b[...] = prep(jnp.minimum(n + 2, last))
        finish(n, s_b[...])
        s_b[...] = scores(jnp.minimum(n + 2, last), q_b[...])
        q_a[...] = prep(jnp.minimum(n + 3, last))
        finish(n + 1, s_a[...])
        return carry

    lax.fori_loop(0, n_blocks // 2, pair, 0)


WAVE = 4


def _waves(n_blocks, prep, scores, finish, bufs):
    last = n_blocks - 1
    for k in range(WAVE):
        bufs[k][...] = scores(k, prep(k))

    def wave(i, carry):
        n = i * WAVE
        for k in range(WAVE):
            finish(n + k, bufs[k][...])
        for k in range(WAVE):
            nxt = jnp.minimum(n + WAVE + k, last)
            bufs[k][...] = scores(nxt, prep(nxt))
        return carry

    lax.fori_loop(0, n_blocks // WAVE, wave, 0)


def _wave_bufs(rows, keys):
    return [pltpu.VMEM((rows, keys), F32) for _ in range(WAVE)]


def _pipeline_bufs(rows, dq, keys):
    return [pltpu.VMEM((rows, dq), BF16), pltpu.VMEM((rows, dq), BF16),
            pltpu.VMEM((rows, keys), F32), pltpu.VMEM((rows, keys), F32)]


def _norm_matmul_kernel(x_ref, g_ref, w_ref, o_ref, xn_ref):
    @pl.when(pl.program_id(1) == 0)
    def _():
        xn_ref[...] = _rms(x_ref[...].astype(F32), g_ref[...]).astype(BF16)

    res = _dot(xn_ref[...], w_ref[...])
    if len(o_ref.shape) == 3:
        head_w = o_ref.shape[2]
        for hh in range(o_ref.shape[0]):
            o_ref[hh] = res[:, hh * head_w:(hh + 1) * head_w].astype(o_ref.dtype)
    else:
        o_ref[...] = res.astype(o_ref.dtype)


def _norm_matmul(x, g, w, *, col_block=0, tm, tn, head_w=None):
    t = x.shape[0]
    k, n = w.shape
    if head_w is None:
        out_spec = pl.BlockSpec((tm, tn), lambda i, j: (i, j))
        out_shape = jax.ShapeDtypeStruct((t, n), BF16)
    else:
        out_spec = pl.BlockSpec((tn // head_w, tm, head_w), lambda i, j: (j, i, 0))
        out_shape = jax.ShapeDtypeStruct((n // head_w, t, head_w), BF16)
    return pl.pallas_call(
        _norm_matmul_kernel,
        grid=(t // tm, n // tn),
        in_specs=[
            pl.BlockSpec((tm, k), lambda i, j: (i, col_block)),
            pl.BlockSpec((1, k), lambda i, j: (0, 0)),
            pl.BlockSpec((k, tn), lambda i, j: (0, j)),
        ],
        out_specs=out_spec,
        out_shape=out_shape,
        scratch_shapes=[pltpu.VMEM((tm, k), BF16)],
        compiler_params=_cparams("parallel", "arbitrary"),
        name="norm_matmul",
    )(x, g.reshape(1, k).astype(F32), w)


def _attn_window_kernel(sink_ref, q_ref, k_ref, v_ref, cos_ref, sneg_ref, spos_ref, o_ref, k_scr, v_scr, *bufs):
    kvh = pl.program_id(1)
    half = ROT_DIM // 2
    nb = SEQ // Q_BLOCK
    rows, keys = GQA_GROUP * Q_BLOCK, 3 * Q_BLOCK
    k_scr[...] = _rot(k_ref[0, 0].astype(F32), cos_ref[...], sneg_ref[...], spos_ref[...], half).astype(BF16)
    _fill_v_ones(v_scr, v_ref[0, 0])
    r = lax.broadcasted_iota(jnp.int32, (rows, keys), 0)
    c = lax.broadcasted_iota(jnp.int32, (rows, keys), 1)
    base = (r & (Q_BLOCK - 1)) - c
    sink = jnp.concatenate(
        [jnp.full((Q_BLOCK, 1), sink_ref[kvh * GQA_GROUP + g], F32) for g in range(GQA_GROUP)], axis=0)

    def window_start(n):
        return pl.multiple_of(jnp.clip(n - 1, 0, nb - 3) * Q_BLOCK, Q_BLOCK)

    def prep(n):
        q0 = pl.multiple_of(n * Q_BLOCK, Q_BLOCK)
        cos = cos_ref[pl.ds(q0, Q_BLOCK), :]
        sneg = sneg_ref[pl.ds(q0, Q_BLOCK), :]
        spos = spos_ref[pl.ds(q0, Q_BLOCK), :]
        return jnp.concatenate(
            [_rot(q_ref[g, 0, pl.ds(q0, Q_BLOCK), :].astype(F32),
                  cos, sneg, spos, half).astype(BF16) for g in range(GQA_GROUP)], axis=0)

    def scores(n, q4):
        return _dot_nt(q4, k_scr[pl.ds(window_start(n), keys), :])

    def finish(n, s):
        q0 = pl.multiple_of(n * Q_BLOCK, Q_BLOCK)
        k0 = window_start(n)
        s = jnp.where(jnp.abs(base + (q0 - k0)) <= WINDOW, s, MASKED)
        o = _softmax_pv(s, v_scr[pl.ds(k0, keys), :], sink)
        for g in range(GQA_GROUP):
            o_ref[g, 0, pl.ds(q0, Q_BLOCK), :] = (
                o[g * Q_BLOCK:(g + 1) * Q_BLOCK].astype(o_ref.dtype))

    _waves(nb, prep, scores, finish, bufs)


def _gqa_specs():
    head = (1, 1, SEQ, HEAD_DIM)
    group = (GQA_GROUP, 1, SEQ, HEAD_DIM)
    return [
        pl.BlockSpec(group, lambda b, h: (h, b, 0, 0)),
        pl.BlockSpec(head, lambda b, h: (N_HEADS + h, b, 0, 0)),
        pl.BlockSpec(head, lambda b, h: (N_HEADS + N_KV_HEADS + h, b, 0, 0)),
    ], pl.BlockSpec(group, lambda b, h: (h, b, 0, 0))


_ATT_SHAPE = jax.ShapeDtypeStruct((N_HEADS, BATCH, SEQ, HEAD_DIM), BF16)


def _attn_window(qkv, sink, tables):
    cos, sneg, spos = tables
    qkv_specs, out_spec = _gqa_specs()
    tab = pl.BlockSpec((SEQ, HEAD_DIM), lambda b, h: (0, 0))
    return pl.pallas_call(
        _attn_window_kernel,
        grid=(BATCH, N_KV_HEADS),
        in_specs=[pl.BlockSpec(memory_space=pltpu.SMEM)] + qkv_specs + [tab, tab, tab],
        out_specs=out_spec,
        out_shape=_ATT_SHAPE,
        scratch_shapes=[pltpu.VMEM((SEQ, HEAD_DIM), BF16), pltpu.VMEM((SEQ, 2 * HEAD_DIM), BF16)]
        + _wave_bufs(GQA_GROUP * Q_BLOCK, 3 * Q_BLOCK),
        compiler_params=_cparams("parallel", "parallel"),
        name="attn_window",
    )(sink.astype(F32), qkv, qkv, qkv, cos, sneg, spos)


AX_TQ = 128


def _attn_axial_kernel(q_ref, k_ref, v_ref, qg_ref, kg_ref, cos_ref, sneg_ref, spos_ref, o_ref, k_scr, v_scr, *bufs):
    quarter = HEAD_DIM // 4
    kn = _rms(k_ref[0, 0].astype(F32), kg_ref[...])
    k_scr[...] = _rot(kn, cos_ref[...], sneg_ref[...], spos_ref[...], quarter).astype(BF16)
    _fill_v_ones(v_scr, v_ref[0, 0])

    def prep(n):
        q0 = pl.multiple_of(n * AX_TQ, AX_TQ)
        cos = cos_ref[pl.ds(q0, AX_TQ), :]
        sneg = sneg_ref[pl.ds(q0, AX_TQ), :]
        spos = spos_ref[pl.ds(q0, AX_TQ), :]
        return jnp.concatenate(
            [_rot(_rms(q_ref[g, 0, pl.ds(q0, AX_TQ), :].astype(F32), qg_ref[...]),
                  cos, sneg, spos, quarter).astype(BF16) for g in range(GQA_GROUP)], axis=0)

    def scores(n, q4):
        return _dot_nt(q4, k_scr[...])

    def finish(n, s):
        q0 = pl.multiple_of(n * AX_TQ, AX_TQ)
        o = _softmax_pv(s, v_scr[...])
        for g in range(GQA_GROUP):
            o_ref[g, 0, pl.ds(q0, AX_TQ), :] = (
                o[g * AX_TQ:(g + 1) * AX_TQ].astype(o_ref.dtype))

    _pairs(SEQ // AX_TQ, prep, scores, finish, bufs)


def _attn_axial(qkv, qg, kg, tables):
    cos, sneg, spos = tables
    qkv_specs, out_spec = _gqa_specs()
    tab = pl.BlockSpec((SEQ, HEAD_DIM), lambda b, h: (0, 0))
    gain = pl.BlockSpec((1, HEAD_DIM), lambda b, h: (0, 0))
    return pl.pallas_call(
        _attn_axial_kernel,
        grid=(BATCH, N_KV_HEADS),
        in_specs=qkv_specs + [gain, gain, tab, tab, tab],
        out_specs=out_spec,
        out_shape=_ATT_SHAPE,
        scratch_shapes=[pltpu.VMEM((SEQ, HEAD_DIM), BF16), pltpu.VMEM((SEQ, 2 * HEAD_DIM), BF16)]
        + _pipeline_bufs(GQA_GROUP * AX_TQ, HEAD_DIM, SEQ),
        compiler_params=_cparams("parallel", "parallel"),
        name="attn_axial",
    )(qkv, qkv, qkv, qg.reshape(1, HEAD_DIM).astype(F32), kg.reshape(1, HEAD_DIM).astype(F32), cos, sneg, spos)


MLA_TQ = 512


def _attn_mla_kernel(q_ref, kn_ref, v_ref, kr_ref, cos_ref, sneg_ref, spos_ref, o_ref, k_scr, v_scr, *bufs):
    half = MLA_ROPE_DIM // 2
    k_scr[:, :MLA_NOPE_DIM] = kn_ref[0, 0]

    @pl.when(pl.program_id(1) == 0)
    def _():
        k_scr[:, MLA_NOPE_DIM:] = _rot(kr_ref[0].astype(F32), cos_ref[...], sneg_ref[...], spos_ref[...],
                                       half).astype(BF16)

    _fill_v_ones(v_scr, v_ref[0, 0])

    def prep(n):
        q0 = pl.multiple_of(n * MLA_TQ, MLA_TQ)
        cos = cos_ref[pl.ds(q0, MLA_TQ), :]
        sneg = sneg_ref[pl.ds(q0, MLA_TQ), :]
        spos = spos_ref[pl.ds(q0, MLA_TQ), :]
        qn = q_ref[0, 0, pl.ds(q0, MLA_TQ), :MLA_NOPE_DIM]
        qr = _rot(q_ref[0, 0, pl.ds(q0, MLA_TQ), MLA_NOPE_DIM:].astype(F32), cos, sneg, spos, half).astype(BF16)
        return jnp.concatenate([qn, qr], axis=1)

    def scores(n, q):
        return _dot_nt(q, k_scr[...])

    def finish(n, s):
        q0 = pl.multiple_of(n * MLA_TQ, MLA_TQ)
        o_ref[0, 0, pl.ds(q0, MLA_TQ), :] = _softmax_pv(s, v_scr[...]).astype(o_ref.dtype)

    _pairs(SEQ // MLA_TQ, prep, scores, finish, bufs)


def _attn_mla(qfull, kvfull, k_rope, tables):
    cos, sneg, spos = tables
    tab = pl.BlockSpec((SEQ, HEAD_DIM), lambda b, h: (0, 0))
    return pl.pallas_call(
        _attn_mla_kernel,
        grid=(BATCH, N_HEADS),
        in_specs=[
            pl.BlockSpec((1, 1, SEQ, MLA_Q_PAD), lambda b, h: (h, b, 0, 0)),
            pl.BlockSpec((1, 1, SEQ, MLA_NOPE_DIM), lambda b, h: (2 * h, b, 0, 0)),
            pl.BlockSpec((1, 1, SEQ, MLA_V_DIM), lambda b, h: (2 * h + 1, b, 0, 0)),
            pl.BlockSpec((1, SEQ, HEAD_DIM), lambda b, h: (b, 0, 0)),
            tab, tab, tab,
        ],
        out_specs=pl.BlockSpec((1, 1, SEQ, MLA_V_DIM), lambda b, h: (h, b, 0, 0)),
        out_shape=_ATT_SHAPE,
        scratch_shapes=[pltpu.VMEM((SEQ, MLA_Q_PAD), BF16), pltpu.VMEM((SEQ, 2 * MLA_V_DIM), BF16)]
        + _pipeline_bufs(MLA_TQ, MLA_Q_PAD, SEQ),
        compiler_params=_cparams("parallel", "arbitrary"),
        name="attn_mla",
    )(qfull, kvfull, kvfull, k_rope, cos, sneg, spos)


NA_DELTAS = NA_ROWS
NA_KEYS = NA_ROWS * GRID_W
NA_DR = 2 * NA_ROWS - 1
NA_DC = 2 * NA_COLS - 1


def _na_bias_kernel(rpb_ref, o_ref):
    h = pl.program_id(0)
    shape = (GRID_W, 2 * GRID_W)
    qc = lax.broadcasted_iota(jnp.int32, shape, 0)
    lane = lax.broadcasted_iota(jnp.int32, shape, 1)
    kc = jnp.where(lane >= GRID_W, lane - GRID_W, lane)
    dc = kc - qc + (NA_COLS - 1)
    c_start = jnp.clip(qc - NA_COLS // 2, 0, GRID_W - NA_COLS)
    in_win = (kc >= c_start) & (kc < c_start + NA_COLS)
    low = lane < GRID_W
    tiles = [jnp.zeros(shape, F32) for _ in range(NA_DR)]
    for d in range(NA_DC):
        hit = dc == d
        for dr in range(NA_DR):
            tiles[dr] = jnp.where(hit, rpb_ref[(h * NA_DR + dr) * NA_DC + d], tiles[dr])
    for delta in range(NA_DELTAS):
        for j in range(NA_ROWS // 2):
            t = jnp.where(low, tiles[delta + 2 * j], tiles[delta + 2 * j + 1])
            o_ref[0, delta, :, j * 2 * GRID_W:(j + 1) * 2 * GRID_W] = jnp.where(in_win, t, MASKED)


def _na_bias(rpb):
    return pl.pallas_call(
        _na_bias_kernel,
        grid=(N_HEADS,),
        in_specs=[pl.BlockSpec(memory_space=pltpu.SMEM)],
        out_specs=pl.BlockSpec((1, NA_DELTAS, GRID_W, NA_KEYS), lambda h: (h, 0, 0, 0)),
        out_shape=jax.ShapeDtypeStruct((N_HEADS, NA_DELTAS, GRID_W, NA_KEYS), F32),
        compiler_params=_cparams("parallel"),
        name="na_bias",
    )(rpb.astype(F32).reshape(-1))


def _attn_na_kernel(q_ref, k_ref, v_ref, bias_ref, o_ref, v_scr, *bufs):
    _fill_v_ones(v_scr, v_ref[0, 0])

    def first_key_row(r):
        return jnp.clip(r - NA_ROWS // 2, 0, GRID_H - NA_ROWS)

    def prep(r):
        q0 = pl.multiple_of(r * GRID_W, GRID_W)
        return jnp.concatenate([q_ref[g, 0, pl.ds(q0, GRID_W), :] for g in range(GQA_GROUP)], axis=0)

    def scores(r, q4):
        k0 = pl.multiple_of(first_key_row(r) * GRID_W, GRID_W)
        return _dot_nt(q4, k_ref[0, 0, pl.ds(k0, NA_KEYS), :])

    def finish(r, s):
        r0 = first_key_row(r)
        delta = r0 - r + (NA_ROWS - 1)
        q0 = pl.multiple_of(r * GRID_W, GRID_W)
        k0 = pl.multiple_of(r0 * GRID_W, GRID_W)
        bias = jnp.concatenate([bias_ref[g, delta] for g in range(GQA_GROUP)], axis=0)
        o = _softmax_pv(s + bias, v_scr[pl.ds(k0, NA_KEYS), :])
        for g in range(GQA_GROUP):
            o_ref[g, 0, pl.ds(q0, GRID_W), :] = (
                o[g * GRID_W:(g + 1) * GRID_W].astype(o_ref.dtype))

    _waves(GRID_H, prep, scores, finish, bufs)


def _attn_na(qkv, bias):
    qkv_specs, out_spec = _gqa_specs()
    return pl.pallas_call(
        _attn_na_kernel,
        grid=(BATCH, N_KV_HEADS),
        in_specs=qkv_specs + [pl.BlockSpec((GQA_GROUP, NA_DELTAS, GRID_W, NA_KEYS), lambda b, h: (h, 0, 0, 0))],
        out_specs=out_spec,
        out_shape=_ATT_SHAPE,
        scratch_shapes=[pltpu.VMEM((SEQ, 2 * HEAD_DIM), BF16)] + _wave_bufs(GQA_GROUP * GRID_W, NA_KEYS),
        compiler_params=_cparams("parallel", "parallel"),
        name="attn_na",
    )(qkv, qkv, qkv, bias)


XA_TM = 256


def _xattn_kernel(h_ref, att_ref, wout_ref, g_ref, wq_ref, kv_ref, wo_ref, gf_ref, rw_ref, rb_ref,
                  o_ref, xn_ref, aff_ref):
    att = jnp.concatenate([att_ref[hh, 0] for hh in range(N_HEADS)], axis=1)
    x = h_ref[0] + _dot(att, wout_ref[...])
    xn = _rms(x, g_ref[...]).astype(BF16)
    q = _dot(xn, wq_ref[...]).astype(BF16)
    outs = []
    ones = jnp.ones((MEM_TOKENS, XA_HEAD_DIM), BF16)
    for hh in range(XA_HEADS):
        k = kv_ref[0, :, hh * XA_HEAD_DIM:(hh + 1) * XA_HEAD_DIM]
        v = kv_ref[0, :, XA_INNER + hh * XA_HEAD_DIM:XA_INNER + (hh + 1) * XA_HEAD_DIM]
        s = _dot_nt(q[:, hh * XA_HEAD_DIM:(hh + 1) * XA_HEAD_DIM], k)
        outs.append(_softmax_pv(s, jnp.concatenate([v, ones], axis=1)).astype(BF16))
    y = x + _dot(jnp.concatenate(outs, axis=1), wo_ref[...])
    o_ref[0] = y
    yn = _rms(y, gf_ref[...]).astype(BF16)
    xn_ref[0] = yn
    logits = _dot(yn, rw_ref[...]) + rb_ref[...]
    e = jnp.exp(logits - jnp.max(logits, axis=-1, keepdims=True))
    aff_ref[0] = e / jnp.sum(e, axis=-1, keepdims=True)


def _xattn(h, att, w_out, g, wq, kv, layer, wo, g_ffn, router_w):
    full = lambda b, i: (0, 0)
    tile = lambda b, i: (b, i, 0)
    rw = jnp.pad(router_w, ((0, 0), (0, LANES - N_EXPERTS))).astype(BF16)
    rb = jnp.where(jnp.arange(LANES) < N_EXPERTS, 0.0, MASKED).astype(F32).reshape(1, LANES)
    return pl.pallas_call(
        _xattn_kernel,
        grid=(BATCH, SEQ // XA_TM),
        in_specs=[
            pl.BlockSpec((1, XA_TM, D_MODEL), lambda b, i: (b, i, 0)),
            pl.BlockSpec((N_HEADS, 1, XA_TM, HEAD_DIM), lambda b, i: (0, b, i, 0)),
            pl.BlockSpec(w_out.shape, full),
            pl.BlockSpec((1, D_MODEL), full),
            pl.BlockSpec((D_MODEL, XA_INNER), full),
            pl.BlockSpec((1, MEM_TOKENS, 2 * XA_INNER), lambda b, i: (b, 0, layer)),
            pl.BlockSpec((XA_INNER, D_MODEL), full),
            pl.BlockSpec((1, D_MODEL), full),
            pl.BlockSpec((D_MODEL, LANES), full),
            pl.BlockSpec((1, LANES), full),
        ],
        out_specs=[
            pl.BlockSpec((1, XA_TM, D_MODEL), tile),
            pl.BlockSpec((1, XA_TM, D_MODEL), tile),
            pl.BlockSpec((1, XA_TM, LANES), tile),
        ],
        out_shape=[
            jax.ShapeDtypeStruct((BATCH, SEQ, D_MODEL), F32),
            jax.ShapeDtypeStruct((BATCH, SEQ, D_MODEL), BF16),
            jax.ShapeDtypeStruct((BATCH, SEQ, LANES), F32),
        ],
        compiler_params=_cparams("parallel", "parallel"),
        name="outproj_xattn",
    )(h, att, w_out, g.reshape(1, D_MODEL).astype(F32), wq, kv, wo, g_ffn.reshape(1, D_MODEL).astype(F32), rw, rb)


PREFIX_BLOCK = 256
TOPK_MAX_ITERS = 256


TOPK_ROWS = BATCH * N_EXPERTS


def _topk_kernel(aff_ref, selpos_ref, selpos_t_ref, gate_t_ref):
    a = jnp.concatenate([jnp.transpose(aff_ref[b])[:N_EXPERTS] for b in range(BATCH)], axis=0)
    gate_t_ref[...] = a
    kf = jnp.float32(CAP)

    def count(mask):
        return jnp.sum(jnp.where(mask, 1.0, 0.0), axis=1, keepdims=True)

    def cond(c):
        it, _, _, _, done = c
        return jnp.logical_and(it < TOPK_MAX_ITERS, done == 0)

    def body(c):
        it, lo, hi, _, _ = c
        mid = 0.5 * (lo + hi)
        take = count(a >= mid) >= kf
        lo = jnp.where(take, mid, lo)
        hi = jnp.where(take, hi, mid)
        top = jnp.max(jnp.where(a < hi, a, -1.0), axis=1, keepdims=True)
        bot = jnp.min(jnp.where(a >= lo, a, 3.0), axis=1, keepdims=True)
        done = jnp.min(jnp.where(top == bot, 1, 0))
        return it + 1, lo, hi, top, done

    init = (jnp.int32(0), jnp.zeros((TOPK_ROWS, 1), F32), jnp.full((TOPK_ROWS, 1), 2.0, F32),
            jnp.zeros((TOPK_ROWS, 1), F32), jnp.int32(0))
    _, _, _, kth, _ = lax.while_loop(cond, body, init)

    gt = a > kth
    eq = a == kth
    need = kf - count(gt)
    tok = lax.broadcasted_iota(jnp.int32, (TOPK_ROWS, SEQ), 1).astype(F32)
    cut = jnp.zeros((TOPK_ROWS, 1), F32)
    step = SEQ // 2
    while step >= 1:
        cand = cut + step
        cut = jnp.where(count(eq & (tok < cand)) < need, cand, cut)
        step //= 2
    sel = jnp.where(gt, 1.0, jnp.where(eq & (tok <= cut), 1.0, 0.0))

    r = lax.broadcasted_iota(jnp.int32, (PREFIX_BLOCK, PREFIX_BLOCK), 0)
    c = lax.broadcasted_iota(jnp.int32, (PREFIX_BLOCK, PREFIX_BLOCK), 1)
    utri = jnp.where(r < c, 1.0, 0.0).astype(BF16)
    off = jnp.zeros((TOPK_ROWS, 1), F32)
    for j in range(SEQ // PREFIX_BLOCK):
        blk = sel[:, j * PREFIX_BLOCK:(j + 1) * PREFIX_BLOCK]
        pos = _dot(blk.astype(BF16), utri) + off
        selpos_t_ref[:, j * PREFIX_BLOCK:(j + 1) * PREFIX_BLOCK] = jnp.where(blk > 0.5, pos, -1.0)
        off = off + jnp.sum(blk, axis=1, keepdims=True)

    for b in range(BATCH):
        padded = jnp.concatenate([selpos_t_ref[b * N_EXPERTS:(b + 1) * N_EXPERTS, :],
                                  jnp.zeros((LANES - N_EXPERTS, SEQ), F32)], axis=0)
        selpos_ref[b] = jnp.transpose(padded)[:, :N_EXPERTS]


def _topk(aff):
    whole = lambda i: (0, 0)
    return pl.pallas_call(
        _topk_kernel,
        grid=(1,),
        in_specs=[pl.BlockSpec((BATCH, SEQ, LANES), lambda i: (0, 0, 0))],
        out_specs=[
            pl.BlockSpec((BATCH, SEQ, N_EXPERTS), lambda i: (0, 0, 0)),
            pl.BlockSpec((TOPK_ROWS, SEQ), whole),
            pl.BlockSpec((TOPK_ROWS, SEQ), whole),
        ],
        out_shape=[
            jax.ShapeDtypeStruct((BATCH, SEQ, N_EXPERTS), F32),
            jax.ShapeDtypeStruct((TOPK_ROWS, SEQ), F32),
            jax.ShapeDtypeStruct((TOPK_ROWS, SEQ), F32),
        ],
        compiler_params=_cparams("arbitrary"),
        name="expert_topk",
    )(aff)


def _gather_kernel(xn_ref, selpos_t_ref, gate_t_ref, xg_ref, gs_ref):
    e = pl.program_id(1)
    slot_of_token = selpos_t_ref[pl.ds(e, 1), :]
    gate_row = gate_t_ref[pl.ds(e, 1), :]
    slot = lax.broadcasted_iota(jnp.int32, (CAP, SEQ), 0).astype(F32)
    hit = slot == slot_of_token
    xg_ref[0, 0] = _dot(jnp.where(hit, 1.0, 0.0).astype(BF16), xn_ref[0]).astype(BF16)
    gs_ref[0, 0] = jnp.sum(jnp.where(hit, gate_row, 0.0), axis=1, keepdims=True)


def _gather(xn, selpos_t, gate_t):
    return pl.pallas_call(
        _gather_kernel,
        grid=(BATCH, N_EXPERTS),
        in_specs=[
            pl.BlockSpec((1, SEQ, D_MODEL), lambda b, e: (b, 0, 0)),
            pl.BlockSpec((N_EXPERTS, SEQ), lambda b, e: (b, 0)),
            pl.BlockSpec((N_EXPERTS, SEQ), lambda b, e: (b, 0)),
        ],
        out_specs=[
            pl.BlockSpec((1, 1, CAP, D_MODEL), lambda b, e: (e, b, 0, 0)),
            pl.BlockSpec((1, 1, CAP, 1), lambda b, e: (e, b, 0, 0)),
        ],
        out_shape=[
            jax.ShapeDtypeStruct((N_EXPERTS, BATCH, CAP, D_MODEL), BF16),
            jax.ShapeDtypeStruct((N_EXPERTS, BATCH, CAP, 1), F32),
        ],
        compiler_params=_cparams("parallel", "arbitrary"),
        name="moe_gather",
    )(xn, selpos_t, gate_t)


FF_TF = 256


def _expert_ffn_kernel(xg_ref, wg_ref, wu_ref, wd_ref, gs_ref, y_ref, acc_ref):
    f = pl.program_id(1)

    @pl.when(f == 0)
    def _():
        acc_ref[...] = jnp.zeros_like(acc_ref)

    x = xg_ref[0].reshape(BATCH * CAP, D_MODEL)
    gate = _dot(x, wg_ref[0, 0].astype(BF16))
    up = _dot(x, wu_ref[0, 0].astype(BF16))
    act = (gate * (1.0 / (1.0 + jnp.exp(-gate))) * up).astype(BF16)
    acc_ref[...] += _dot(act, wd_ref[0, 0].astype(BF16))

    @pl.when(f == pl.num_programs(1) - 1)
    def _():
        y = acc_ref[...] * gs_ref[0].reshape(BATCH * CAP, 1)
        y_ref[0] = y.astype(BF16).reshape(BATCH, CAP, D_MODEL)


def _expert_ffn(xg, w_gate, w_up, w_down, gs, layer):
    return pl.pallas_call(
        _expert_ffn_kernel,
        grid=(N_EXPERTS, EXPERT_FF // FF_TF),
        in_specs=[
            pl.BlockSpec((1, BATCH, CAP, D_MODEL), lambda e, f: (e, 0, 0, 0)),
            pl.BlockSpec((1, 1, D_MODEL, FF_TF), lambda e, f: (layer, e, 0, f)),
            pl.BlockSpec((1, 1, D_MODEL, FF_TF), lambda e, f: (layer, e, 0, f)),
            pl.BlockSpec((1, 1, FF_TF, D_MODEL), lambda e, f: (layer, e, f, 0)),
            pl.BlockSpec((1, BATCH, CAP, 1), lambda e, f: (e, 0, 0, 0)),
        ],
        out_specs=pl.BlockSpec((1, BATCH, CAP, D_MODEL), lambda e, f: (e, 0, 0, 0)),
        out_shape=jax.ShapeDtypeStruct((N_EXPERTS, BATCH, CAP, D_MODEL), BF16),
        scratch_shapes=[pltpu.VMEM((BATCH * CAP, D_MODEL), F32)],
        compiler_params=_cparams("parallel", "arbitrary"),
        name="moe_ffn",
    )(xg, w_gate, w_up, w_down, gs)


CB_TM = 256
CB_EXPERTS = 16


def _combine_kernel(h_ref, selpos_ref, y_ref, *rest):
    o_ref = rest[-1]
    c = pl.program_id(2)

    @pl.when(c == 0)
    def _():
        o_ref[...] = h_ref[...]

    sp = selpos_ref[0]
    lane = lax.broadcasted_iota(jnp.int32, sp.shape, 1)
    slot = lax.broadcasted_iota(jnp.int32, (CB_TM, CAP), 1).astype(F32)
    pieces = []
    for k in range(CB_EXPERTS):
        col = jnp.sum(jnp.where(lane == c * CB_EXPERTS + k, sp, 0.0), axis=1, keepdims=True)
        pieces.append(jnp.where(col == slot, 1.0, 0.0).astype(BF16))
    onehot = jnp.concatenate(pieces, axis=1)
    o_ref[0] += _dot(onehot, y_ref[...].reshape(CB_EXPERTS * CAP, D_MODEL))

    if len(rest) == 2:
        @pl.when(c == pl.num_programs(2) - 1)
        def _():
            o_ref[0] = _rms(o_ref[0], rest[0][...])


def _combine(h, selpos, y, final_g=None):
    in_specs = [
        pl.BlockSpec((1, CB_TM, D_MODEL), lambda b, i, c: (b, i, 0)),
        pl.BlockSpec((1, CB_TM, N_EXPERTS), lambda b, i, c: (b, i, 0)),
        pl.BlockSpec((CB_EXPERTS, 1, CAP, D_MODEL), lambda b, i, c: (c, b, 0, 0)),
    ]
    args = [h, selpos, y]
    if final_g is not None:
        in_specs.append(pl.BlockSpec((1, D_MODEL), lambda b, i, c: (0, 0)))
        args.append(final_g.reshape(1, D_MODEL).astype(F32))
    return pl.pallas_call(
        _combine_kernel,
        grid=(BATCH, SEQ // CB_TM, N_EXPERTS // CB_EXPERTS),
        in_specs=in_specs,
        out_specs=pl.BlockSpec((1, CB_TM, D_MODEL), lambda b, i, c: (b, i, 0)),
        out_shape=jax.ShapeDtypeStruct((BATCH, SEQ, D_MODEL), F32),
        compiler_params=_cparams("parallel", "parallel", "arbitrary"),
        name="moe_combine",
    )(*args)


def _angles(pos, dim, theta):
    inv = jnp.power(jnp.float32(theta), -(jnp.arange(0, dim, 2, dtype=jnp.float32) / dim))
    ang = pos.astype(jnp.float32)[:, None] * inv[None, :]
    return jnp.cos(ang), jnp.sin(ang)


def _rot_tables(groups, width=HEAD_DIM):
    cos_parts, sneg_parts, spos_parts = [], [], []
    used = 0
    for c, s in groups:
        z = jnp.zeros_like(s)
        cos_parts += [c, c]
        sneg_parts += [-s, z]
        spos_parts += [z, s]
        used += 2 * c.shape[1]
    rest = width - used
    if rest:
        cos_parts.append(jnp.ones((SEQ, rest), F32))
        sneg_parts.append(jnp.zeros((SEQ, rest), F32))
        spos_parts.append(jnp.zeros((SEQ, rest), F32))
    return (jnp.concatenate(cos_parts, axis=1), jnp.concatenate(sneg_parts, axis=1),
            jnp.concatenate(spos_parts, axis=1))


def _gqa_w_in(w_in, scale_q):
    if scale_q:
        col = jnp.concatenate([jnp.full((N_HEADS * HEAD_DIM,), HEAD_DIM ** -0.5 * LOG2E, F32),
                               jnp.ones((2 * N_KV_HEADS * HEAD_DIM,), F32)])
        w_in = w_in * col[None, :]
    return w_in.astype(BF16)


def _mixer_gqa_in(h2, g, w_in, scale_q):
    qkv = _norm_matmul(h2, g, _gqa_w_in(w_in, scale_q), tm=1024, tn=1024, head_w=HEAD_DIM)
    return qkv.reshape(N_HEADS + 2 * N_KV_HEADS, BATCH, SEQ, HEAD_DIM)


def _mla_weights(w_in, w_uq):
    w_in = jnp.pad(w_in, ((0, 0), (0, MLA_LAT_PAD - w_in.shape[-1]))).astype(BF16)
    w_uq = jnp.pad((w_uq * (MLA_QK_DIM ** -0.5 * LOG2E)).reshape(MLA_Q_RANK, N_HEADS, MLA_QK_DIM),
                   ((0, 0), (0, 0), (0, MLA_Q_PAD - MLA_QK_DIM))).reshape(MLA_Q_RANK, N_HEADS * MLA_Q_PAD)
    return w_in, w_uq.astype(BF16)


def kernel(x, mem, norm_mix_g, norm_xa_g, norm_ffn_g, a_w_in, a_sink, a_w_out, b_w_in, b_q_norm_g, b_k_norm_g, b_w_out, c_w_in, c_q_lat_norm_g, c_kv_lat_norm_g, c_w_uq, c_w_ukv, c_w_out, d_w_in, d_rpb, d_w_out, mem_norm_g, xa_wq, xa_wkv, xa_wo, router_w, moe_w_gate, moe_w_up, moe_w_down, final_norm_g):
    t = BATCH * SEQ
    pos = jnp.arange(SEQ)
    tab_a = _rot_tables([_angles(pos, ROT_DIM, ROPE_THETA)])
    tab_b = _rot_tables([_angles(pos // GRID_W, HEAD_DIM // 2, AXIAL_THETA),
                         _angles(pos % GRID_W, HEAD_DIM // 2, AXIAL_THETA)])
    tab_c = _rot_tables([_angles(pos, MLA_ROPE_DIM, ROPE_THETA)])
    wkv_all = jnp.transpose(xa_wkv, (1, 0, 2)).reshape(D_MODEL, DEPTH * 2 * XA_INNER).astype(BF16)
    kv_all = _norm_matmul(mem.reshape(BATCH * MEM_TOKENS, D_MODEL), mem_norm_g, wkv_all, tm=BATCH * MEM_TOKENS,
                          tn=2 * XA_INNER).reshape(BATCH, MEM_TOKENS, DEPTH * 2 * XA_INNER)

    h = x
    for i in range(DEPTH):
        m, j = i % N_MIXERS, i // N_MIXERS
        h2 = h.reshape(t, D_MODEL)
        if m == 0:
            qkv = _mixer_gqa_in(h2, norm_mix_g[i], a_w_in[j], True)
            att = _attn_window(qkv, a_sink[j] * LOG2E, tab_a)
            w_out = a_w_out[j]
        elif m == 1:
            qkv = _mixer_gqa_in(h2, norm_mix_g[i], b_w_in[j], False)
            att = _attn_axial(qkv, b_q_norm_g[j] * (HEAD_DIM ** -0.5 * LOG2E), b_k_norm_g[j], tab_b)
            w_out = b_w_out[j]
        elif m == 2:
            w_in, w_uq = _mla_weights(c_w_in[j], c_w_uq[j])
            lat = _norm_matmul(h2, norm_mix_g[i], w_in, tm=1024, tn=MLA_LAT_PAD)
            qfull = _norm_matmul(lat, c_q_lat_norm_g[j], w_uq, col_block=0, tm=2048, tn=1024, head_w=MLA_Q_PAD)
            kvfull = _norm_matmul(lat, c_kv_lat_norm_g[j], c_w_ukv[j].astype(BF16), col_block=1, tm=2048, tn=1024,
                                  head_w=MLA_NOPE_DIM)
            k_rope = lat[:, MLA_Q_RANK + MLA_KV_RANK:].reshape(BATCH, SEQ, HEAD_DIM)
            att = _attn_mla(qfull.reshape(N_HEADS, BATCH, SEQ, MLA_Q_PAD),
                            kvfull.reshape(2 * N_HEADS, BATCH, SEQ, MLA_NOPE_DIM), k_rope, tab_c)
            w_out = c_w_out[j]
        else:
            qkv = _mixer_gqa_in(h2, norm_mix_g[i], d_w_in[j], True)
            att = _attn_na(qkv, _na_bias(d_rpb[j] * LOG2E))
            w_out = d_w_out[j]

        h, xn, aff = _xattn(h, att, w_out.astype(BF16), norm_xa_g[i],
                            (xa_wq[i] * (XA_HEAD_DIM ** -0.5 * LOG2E)).astype(BF16),
                            kv_all, i, xa_wo[i].astype(BF16), norm_ffn_g[i], router_w[i])

        selpos, selpos_t, gate_t = _topk(aff)
        xg, gs = _gather(xn, selpos_t, gate_t)
        y = _expert_ffn(xg, moe_w_gate, moe_w_up, moe_w_down, gs, i)
        h = _combine(h, selpos, y, final_norm_g if i == DEPTH - 1 else None)

    return h
```

```python
import jax
import jax.numpy as jnp
from jax import lax
from jax.experimental import pallas as pl
from jax.experimental.pallas import tpu as pltpu

F32 = jnp.float32
BF16 = jnp.bfloat16

D_MODEL = 2048
BATCH = 4
SEQ = 2048
DEPTH = 4
N_MIXERS = 4
HEAD_DIM = 128
N_HEADS = 16
N_KV_HEADS = 4
GQA_GROUP = N_HEADS // N_KV_HEADS
GQA_IN = (N_HEADS + 2 * N_KV_HEADS) * HEAD_DIM
Q_BLOCK = 128
WINDOW = 128
ROPE_THETA = 500000.0
ROT_DIM = HEAD_DIM // 4
AXIAL_THETA = 10000.0
GRID_W = 64
GRID_H = SEQ // GRID_W
MLA_Q_RANK = 512
MLA_KV_RANK = 512
MLA_NOPE_DIM = 128
MLA_ROPE_DIM = 64
MLA_V_DIM = 128
MLA_QK_DIM = MLA_NOPE_DIM + MLA_ROPE_DIM
MLA_Q_PAD = 256
MLA_LAT_PAD = 1152
NA_ROWS = 8
NA_COLS = 16
MEM_TOKENS = 256
XA_HEADS = 4
XA_HEAD_DIM = 128
XA_INNER = XA_HEADS * XA_HEAD_DIM
N_EXPERTS = 16
EC_CAPACITY = 2
EXPERT_FF = 1024
CAP = EC_CAPACITY * SEQ // N_EXPERTS
EPS = 1e-6
MASKED = -1e30
LOG2E = 1.4426950408889634
LANES = 128

VMEM_LIMIT_BYTES = 56 * 1024 * 1024


def _cparams(*sem):
    return pltpu.CompilerParams(dimension_semantics=sem, vmem_limit_bytes=VMEM_LIMIT_BYTES)


def _dot_nt(a, b):
    return lax.dot_general(a, b, (((1,), (1,)), ((), ())), preferred_element_type=F32)


def _dot(a, b):
    return jnp.dot(a, b, preferred_element_type=F32)


def _rms(x, g):
    ms = jnp.mean(x * x, axis=-1, keepdims=True)
    return x * lax.rsqrt(ms + EPS) * g


def _rot(x, cos, sneg, spos, shift):
    n = x.shape[-1]
    return x * cos + pltpu.roll(x, n - shift, 1) * sneg + pltpu.roll(x, shift, 1) * spos


PV_CHUNK = 512


def _softmax_pv(s, v_ones, sink=None):
    dv = v_ones.shape[1] // 2
    keys = s.shape[1]
    m = jnp.max(s, axis=-1, keepdims=True)
    if sink is not None:
        m = jnp.maximum(m, sink)
    r = None
    for j in range(0, keys, PV_CHUNK):
        part = _dot(jnp.exp2(s[:, j:j + PV_CHUNK] - m).astype(BF16), v_ones[j:j + PV_CHUNK])
        r = part if r is None else r + part
    den = r[:, dv:]
    if sink is not None:
        den = den + jnp.exp2(sink - m)
    return r[:, :dv] * (1.0 / den)


def _softmax_pv_t(s_t, v_t):
    m = jnp.max(s_t, axis=0, keepdims=True)
    e = jnp.exp2(s_t - m)
    den = jnp.sum(e, axis=0, keepdims=True)
    return _dot(v_t, e.astype(BF16)) * (1.0 / den)


def _fill_v_ones(v_scr, v):
    dv = v.shape[1]
    v_scr[:, :dv] = v
    v_scr[:, dv:] = jnp.ones_like(v)


def _pairs(n_blocks, prep, scores, finish, bufs):
    q_a, q_b, s_a, s_b = bufs
    last = n_blocks - 1
    q_b[...] = prep(0)
    s_b[...] = scores(0, q_b[...])
    q_a[...] = prep(1)

    def pair(i, carry):
        n = 2 * i
        s_a[...] = scores(n + 1, q_a[...])
        q_b[...] = prep(jnp.minimum(n + 2, last))
        finish(n, s_b[...])
        s_b[...] = scores(jnp.minimum(n + 2, last), q_b[...])
        q_a[...] = prep(jnp.minimum(n + 3, last))
        finish(n + 1, s_a[...])
        return carry

    lax.fori_loop(0, n_blocks // 2, pair, 0)


WAVE = 4


def _waves(n_blocks, prep, scores, finish, bufs):
    last = n_blocks - 1
    for k in range(WAVE):
        bufs[k][...] = scores(k, prep(k))

    def wave(i, carry):
        n = i * WAVE
        for k in range(WAVE):
            finish(n + k, bufs[k][...])
        for k in range(WAVE):
            nxt = jnp.minimum(n + WAVE + k, last)
            bufs[k][...] = scores(nxt, prep(nxt))
        return carry

    lax.fori_loop(0, n_blocks // WAVE, wave, 0)


def _wave_bufs(rows, keys):
    return [pltpu.VMEM((rows, keys), F32) for _ in range(WAVE)]


def _pipeline_bufs(rows, dq, keys, keys_major=False):
    score = (keys, rows) if keys_major else (rows, keys)
    return [pltpu.VMEM((rows, dq), BF16), pltpu.VMEM((rows, dq), BF16),
            pltpu.VMEM(score, F32), pltpu.VMEM(score, F32)]


def _norm_matmul_kernel(x_ref, g_ref, w_ref, o_ref, xn_ref):
    @pl.when(pl.program_id(1) == 0)
    def _():
        xn_ref[...] = _rms(x_ref[...].astype(F32), g_ref[...]).astype(BF16)

    res = _dot(xn_ref[...], w_ref[...])
    if len(o_ref.shape) == 3:
        head_w = o_ref.shape[2]
        for hh in range(o_ref.shape[0]):
            o_ref[hh] = res[:, hh * head_w:(hh + 1) * head_w].astype(o_ref.dtype)
    else:
        o_ref[...] = res.astype(o_ref.dtype)


def _norm_matmul(x, g, w, *, col_block=0, tm, tn, head_w=None):
    t = x.shape[0]
    k, n = w.shape
    if head_w is None:
        out_spec = pl.BlockSpec((tm, tn), lambda i, j: (i, j))
        out_shape = jax.ShapeDtypeStruct((t, n), BF16)
    else:
        out_spec = pl.BlockSpec((tn // head_w, tm, head_w), lambda i, j: (j, i, 0))
        out_shape = jax.ShapeDtypeStruct((n // head_w, t, head_w), BF16)
    return pl.pallas_call(
        _norm_matmul_kernel,
        grid=(t // tm, n // tn),
        in_specs=[
            pl.BlockSpec((tm, k), lambda i, j: (i, col_block)),
            pl.BlockSpec((1, k), lambda i, j: (0, 0)),
            pl.BlockSpec((k, tn), lambda i, j: (0, j)),
        ],
        out_specs=out_spec,
        out_shape=out_shape,
        scratch_shapes=[pltpu.VMEM((tm, k), BF16)],
        compiler_params=_cparams("parallel", "arbitrary"),
        name="norm_matmul",
    )(x, g.reshape(1, k).astype(F32), w)


def _attn_window_kernel(sink_ref, q_ref, k_ref, v_ref, cos_ref, sneg_ref, spos_ref, o_ref, k_scr, v_scr, *bufs):
    kvh = pl.program_id(1)
    half = ROT_DIM // 2
    nb = SEQ // Q_BLOCK
    rows, keys = GQA_GROUP * Q_BLOCK, 3 * Q_BLOCK
    k_scr[...] = _rot(k_ref[0, 0].astype(F32), cos_ref[...], sneg_ref[...], spos_ref[...], half).astype(BF16)
    _fill_v_ones(v_scr, v_ref[0, 0])
    r = lax.broadcasted_iota(jnp.int32, (rows, keys), 0)
    c = lax.broadcasted_iota(jnp.int32, (rows, keys), 1)
    base = (r & (Q_BLOCK - 1)) - c
    sink = jnp.concatenate(
        [jnp.full((Q_BLOCK, 1), sink_ref[kvh * GQA_GROUP + g], F32) for g in range(GQA_GROUP)], axis=0)

    def window_start(n):
        return pl.multiple_of(jnp.clip(n - 1, 0, nb - 3) * Q_BLOCK, Q_BLOCK)

    def prep(n):
        q0 = pl.multiple_of(n * Q_BLOCK, Q_BLOCK)
        cos = cos_ref[pl.ds(q0, Q_BLOCK), :]
        sneg = sneg_ref[pl.ds(q0, Q_BLOCK), :]
        spos = spos_ref[pl.ds(q0, Q_BLOCK), :]
        return jnp.concatenate(
            [_rot(q_ref[g, 0, pl.ds(q0, Q_BLOCK), :].astype(F32),
                  cos, sneg, spos, half).astype(BF16) for g in range(GQA_GROUP)], axis=0)

    def scores(n, q4):
        return _dot_nt(q4, k_scr[pl.ds(window_start(n), keys), :])

    def finish(n, s):
        q0 = pl.multiple_of(n * Q_BLOCK, Q_BLOCK)
        k0 = window_start(n)
        s = jnp.where(jnp.abs(base + (q0 - k0)) <= WINDOW, s, MASKED)
        o = _softmax_pv(s, v_scr[pl.ds(k0, keys), :], sink)
        for g in range(GQA_GROUP):
            o_ref[g, 0, pl.ds(q0, Q_BLOCK), :] = (
                o[g * Q_BLOCK:(g + 1) * Q_BLOCK].astype(o_ref.dtype))

    _waves(nb, prep, scores, finish, bufs)


def _gqa_specs():
    head = (1, 1, SEQ, HEAD_DIM)
    group = (GQA_GROUP, 1, SEQ, HEAD_DIM)
    return [
        pl.BlockSpec(group, lambda b, h: (h, b, 0, 0)),
        pl.BlockSpec(head, lambda b, h: (N_HEADS + h, b, 0, 0)),
        pl.BlockSpec(head, lambda b, h: (N_HEADS + N_KV_HEADS + h, b, 0, 0)),
    ], pl.BlockSpec(group, lambda b, h: (h, b, 0, 0))


_ATT_SHAPE = jax.ShapeDtypeStruct((N_HEADS, BATCH, SEQ, HEAD_DIM), BF16)


def _attn_window(qkv, sink, tables):
    cos, sneg, spos = tables
    qkv_specs, out_spec = _gqa_specs()
    tab = pl.BlockSpec((SEQ, HEAD_DIM), lambda b, h: (0, 0))
    return pl.pallas_call(
        _attn_window_kernel,
        grid=(BATCH, N_KV_HEADS),
        in_specs=[pl.BlockSpec(memory_space=pltpu.SMEM)] + qkv_specs + [tab, tab, tab],
        out_specs=out_spec,
        out_shape=_ATT_SHAPE,
        scratch_shapes=[pltpu.VMEM((SEQ, HEAD_DIM), BF16), pltpu.VMEM((SEQ, 2 * HEAD_DIM), BF16)]
        + _wave_bufs(GQA_GROUP * Q_BLOCK, 3 * Q_BLOCK),
        compiler_params=_cparams("parallel", "parallel"),
        name="attn_window",
    )(sink.astype(F32), qkv, qkv, qkv, cos, sneg, spos)


AX_TQ = 128


def _attn_axial_kernel(q_ref, k_ref, v_ref, qg_ref, kg_ref, cos_ref, sneg_ref, spos_ref, o_ref, k_scr, v_scr, *bufs):
    quarter = HEAD_DIM // 4
    kn = _rms(k_ref[0, 0].astype(F32), kg_ref[...])
    k_scr[...] = _rot(kn, cos_ref[...], sneg_ref[...], spos_ref[...], quarter).astype(BF16)
    v_scr[...] = jnp.transpose(v_ref[0, 0].astype(F32)).astype(BF16)

    def prep(n):
        q0 = pl.multiple_of(n * AX_TQ, AX_TQ)
        cos = cos_ref[pl.ds(q0, AX_TQ), :]
        sneg = sneg_ref[pl.ds(q0, AX_TQ), :]
        spos = spos_ref[pl.ds(q0, AX_TQ), :]
        return jnp.concatenate(
            [_rot(_rms(q_ref[g, 0, pl.ds(q0, AX_TQ), :].astype(F32), qg_ref[...]),
                  cos, sneg, spos, quarter).astype(BF16) for g in range(GQA_GROUP)], axis=0)

    def scores(n, q4):
        return _dot_nt(k_scr[...], q4)

    def finish(n, s_t):
        q0 = pl.multiple_of(n * AX_TQ, AX_TQ)
        o_t = _softmax_pv_t(s_t, v_scr[...])
        for g in range(GQA_GROUP):
            o_ref[g, 0, pl.ds(q0, AX_TQ), :] = jnp.transpose(o_t[:, g * AX_TQ:(g + 1) * AX_TQ]).astype(o_ref.dtype)

    _pairs(SEQ // AX_TQ, prep, scores, finish, bufs)


def _attn_axial(qkv, qg, kg, tables):
    cos, sneg, spos = tables
    qkv_specs, out_spec = _gqa_specs()
    tab = pl.BlockSpec((SEQ, HEAD_DIM), lambda b, h: (0, 0))
    gain = pl.BlockSpec((1, HEAD_DIM), lambda b, h: (0, 0))
    return pl.pallas_call(
        _attn_axial_kernel,
        grid=(BATCH, N_KV_HEADS),
        in_specs=qkv_specs + [gain, gain, tab, tab, tab],
        out_specs=out_spec,
        out_shape=_ATT_SHAPE,
        scratch_shapes=[pltpu.VMEM((SEQ, HEAD_DIM), BF16), pltpu.VMEM((HEAD_DIM, SEQ), BF16)]
        + _pipeline_bufs(GQA_GROUP * AX_TQ, HEAD_DIM, SEQ, keys_major=True),
        compiler_params=_cparams("parallel", "parallel"),
        name="attn_axial",
    )(qkv, qkv, qkv, qg.reshape(1, HEAD_DIM).astype(F32), kg.reshape(1, HEAD_DIM).astype(F32), cos, sneg, spos)


MLA_TQ = 512


def _attn_mla_kernel(q_ref, kn_ref, v_ref, kr_ref, cos_ref, sneg_ref, spos_ref, o_ref, k_scr, v_scr, *bufs):
    half = MLA_ROPE_DIM // 2
    k_scr[:, :MLA_NOPE_DIM] = kn_ref[0, 0]

    @pl.when(pl.program_id(1) == 0)
    def _():
        k_scr[:, MLA_NOPE_DIM:] = _rot(kr_ref[0].astype(F32), cos_ref[...], sneg_ref[...], spos_ref[...],
                                       half).astype(BF16)

    v_scr[...] = jnp.transpose(v_ref[0, 0].astype(F32)).astype(BF16)

    def prep(n):
        q0 = pl.multiple_of(n * MLA_TQ, MLA_TQ)
        cos = cos_ref[pl.ds(q0, MLA_TQ), :]
        sneg = sneg_ref[pl.ds(q0, MLA_TQ), :]
        spos = spos_ref[pl.ds(q0, MLA_TQ), :]
        qn = q_ref[0, 0, pl.ds(q0, MLA_TQ), :MLA_NOPE_DIM]
        qr = _rot(q_ref[0, 0, pl.ds(q0, MLA_TQ), MLA_NOPE_DIM:].astype(F32), cos, sneg, spos, half).astype(BF16)
        return jnp.concatenate([qn, qr], axis=1)

    def scores(n, q):
        return _dot_nt(k_scr[...], q)

    def finish(n, s_t):
        q0 = pl.multiple_of(n * MLA_TQ, MLA_TQ)
        o_ref[0, 0, pl.ds(q0, MLA_TQ), :] = jnp.transpose(_softmax_pv_t(s_t, v_scr[...])).astype(o_ref.dtype)

    _pairs(SEQ // MLA_TQ, prep, scores, finish, bufs)


def _attn_mla(qfull, kvfull, k_rope, tables):
    cos, sneg, spos = tables
    tab = pl.BlockSpec((SEQ, HEAD_DIM), lambda b, h: (0, 0))
    return pl.pallas_call(
        _attn_mla_kernel,
        grid=(BATCH, N_HEADS),
        in_specs=[
            pl.BlockSpec((1, 1, SEQ, MLA_Q_PAD), lambda b, h: (h, b, 0, 0)),
            pl.BlockSpec((1, 1, SEQ, MLA_NOPE_DIM), lambda b, h: (2 * h, b, 0, 0)),
            pl.BlockSpec((1, 1, SEQ, MLA_V_DIM), lambda b, h: (2 * h + 1, b, 0, 0)),
            pl.BlockSpec((1, SEQ, HEAD_DIM), lambda b, h: (b, 0, 0)),
            tab, tab, tab,
        ],
        out_specs=pl.BlockSpec((1, 1, SEQ, MLA_V_DIM), lambda b, h: (h, b, 0, 0)),
        out_shape=_ATT_SHAPE,
        scratch_shapes=[pltpu.VMEM((SEQ, MLA_Q_PAD), BF16), pltpu.VMEM((MLA_V_DIM, SEQ), BF16)]
        + _pipeline_bufs(MLA_TQ, MLA_Q_PAD, SEQ, keys_major=True),
        compiler_params=_cparams("parallel", "arbitrary"),
        name="attn_mla",
    )(qfull, kvfull, kvfull, k_rope, cos, sneg, spos)


NA_DELTAS = NA_ROWS
NA_KEYS = NA_ROWS * GRID_W
NA_DR = 2 * NA_ROWS - 1
NA_DC = 2 * NA_COLS - 1


def _na_bias_kernel(rpb_ref, o_ref):
    h = pl.program_id(0)
    shape = (GRID_W, 2 * GRID_W)
    qc = lax.broadcasted_iota(jnp.int32, shape, 0)
    lane = lax.broadcasted_iota(jnp.int32, shape, 1)
    kc = jnp.where(lane >= GRID_W, lane - GRID_W, lane)
    dc = kc - qc + (NA_COLS - 1)
    c_start = jnp.clip(qc - NA_COLS // 2, 0, GRID_W - NA_COLS)
    in_win = (kc >= c_start) & (kc < c_start + NA_COLS)
    low = lane < GRID_W
    tiles = [jnp.zeros(shape, F32) for _ in range(NA_DR)]
    for d in range(NA_DC):
        hit = dc == d
        for dr in range(NA_DR):
            tiles[dr] = jnp.where(hit, rpb_ref[(h * NA_DR + dr) * NA_DC + d], tiles[dr])
    for delta in range(NA_DELTAS):
        for j in range(NA_ROWS // 2):
            t = jnp.where(low, tiles[delta + 2 * j], tiles[delta + 2 * j + 1])
            o_ref[0, delta, :, j * 2 * GRID_W:(j + 1) * 2 * GRID_W] = jnp.where(in_win, t, MASKED)


def _na_bias(rpb):
    return pl.pallas_call(
        _na_bias_kernel,
        grid=(N_HEADS,),
        in_specs=[pl.BlockSpec(memory_space=pltpu.SMEM)],
        out_specs=pl.BlockSpec((1, NA_DELTAS, GRID_W, NA_KEYS), lambda h: (h, 0, 0, 0)),
        out_shape=jax.ShapeDtypeStruct((N_HEADS, NA_DELTAS, GRID_W, NA_KEYS), F32),
        compiler_params=_cparams("parallel"),
        name="na_bias",
    )(rpb.astype(F32).reshape(-1))


def _attn_na_kernel(q_ref, k_ref, v_ref, bias_ref, o_ref, v_scr, *bufs):
    _fill_v_ones(v_scr, v_ref[0, 0])

    def first_key_row(r):
        return jnp.clip(r - NA_ROWS // 2, 0, GRID_H - NA_ROWS)

    def prep(r):
        q0 = pl.multiple_of(r * GRID_W, GRID_W)
        return jnp.concatenate([q_ref[g, 0, pl.ds(q0, GRID_W), :] for g in range(GQA_GROUP)], axis=0)

    def scores(r, q4):
        k0 = pl.multiple_of(first_key_row(r) * GRID_W, GRID_W)
        return _dot_nt(q4, k_ref[0, 0, pl.ds(k0, NA_KEYS), :])

    def finish(r, s):
        r0 = first_key_row(r)
        delta = r0 - r + (NA_ROWS - 1)
        q0 = pl.multiple_of(r * GRID_W, GRID_W)
        k0 = pl.multiple_of(r0 * GRID_W, GRID_W)
        bias = jnp.concatenate([bias_ref[g, delta] for g in range(GQA_GROUP)], axis=0)
        o = _softmax_pv(s + bias, v_scr[pl.ds(k0, NA_KEYS), :])
        for g in range(GQA_GROUP):
            o_ref[g, 0, pl.ds(q0, GRID_W), :] = (
                o[g * GRID_W:(g + 1) * GRID_W].astype(o_ref.dtype))

    _waves(GRID_H, prep, scores, finish, bufs)


def _attn_na(qkv, bias):
    qkv_specs, out_spec = _gqa_specs()
    return pl.pallas_call(
        _attn_na_kernel,
        grid=(BATCH, N_KV_HEADS),
        in_specs=qkv_specs + [pl.BlockSpec((GQA_GROUP, NA_DELTAS, GRID_W, NA_KEYS), lambda b, h: (h, 0, 0, 0))],
        out_specs=out_spec,
        out_shape=_ATT_SHAPE,
        scratch_shapes=[pltpu.VMEM((SEQ, 2 * HEAD_DIM), BF16)] + _wave_bufs(GQA_GROUP * GRID_W, NA_KEYS),
        compiler_params=_cparams("parallel", "parallel"),
        name="attn_na",
    )(qkv, qkv, qkv, bias)


XA_TM = 256


def _xattn_kernel(h_ref, att_ref, wout_ref, g_ref, wq_ref, kv_ref, wo_ref, gf_ref, rw_ref, rb_ref,
                  o_ref, xn_ref, aff_ref):
    att = jnp.concatenate([att_ref[hh, 0] for hh in range(N_HEADS)], axis=1)
    x = h_ref[0] + _dot(att, wout_ref[...])
    xn = _rms(x, g_ref[...]).astype(BF16)
    q = _dot(xn, wq_ref[...]).astype(BF16)
    outs = []
    ones = jnp.ones((MEM_TOKENS, XA_HEAD_DIM), BF16)
    for hh in range(XA_HEADS):
        k = kv_ref[0, :, hh * XA_HEAD_DIM:(hh + 1) * XA_HEAD_DIM]
        v = kv_ref[0, :, XA_INNER + hh * XA_HEAD_DIM:XA_INNER + (hh + 1) * XA_HEAD_DIM]
        s = _dot_nt(q[:, hh * XA_HEAD_DIM:(hh + 1) * XA_HEAD_DIM], k)
        outs.append(_softmax_pv(s, jnp.concatenate([v, ones], axis=1)).astype(BF16))
    y = x + _dot(jnp.concatenate(outs, axis=1), wo_ref[...])
    o_ref[0] = y
    yn = _rms(y, gf_ref[...]).astype(BF16)
    xn_ref[0] = yn
    logits = _dot(yn, rw_ref[...]) + rb_ref[...]
    e = jnp.exp(logits - jnp.max(logits, axis=-1, keepdims=True))
    aff_ref[0] = e / jnp.sum(e, axis=-1, keepdims=True)


def _xattn(h, att, w_out, g, wq, kv, layer, wo, g_ffn, router_w):
    full = lambda b, i: (0, 0)
    tile = lambda b, i: (b, i, 0)
    rw = jnp.pad(router_w, ((0, 0), (0, LANES - N_EXPERTS))).astype(BF16)
    rb = jnp.where(jnp.arange(LANES) < N_EXPERTS, 0.0, MASKED).astype(F32).reshape(1, LANES)
    return pl.pallas_call(
        _xattn_kernel,
        grid=(BATCH, SEQ // XA_TM),
        in_specs=[
            pl.BlockSpec((1, XA_TM, D_MODEL), lambda b, i: (b, i, 0)),
            pl.BlockSpec((N_HEADS, 1, XA_TM, HEAD_DIM), lambda b, i: (0, b, i, 0)),
            pl.BlockSpec(w_out.shape, full),
            pl.BlockSpec((1, D_MODEL), full),
            pl.BlockSpec((D_MODEL, XA_INNER), full),
            pl.BlockSpec((1, MEM_TOKENS, 2 * XA_INNER), lambda b, i: (b, 0, layer)),
            pl.BlockSpec((XA_INNER, D_MODEL), full),
            pl.BlockSpec((1, D_MODEL), full),
            pl.BlockSpec((D_MODEL, LANES), full),
            pl.BlockSpec((1, LANES), full),
        ],
        out_specs=[
            pl.BlockSpec((1, XA_TM, D_MODEL), tile),
            pl.BlockSpec((1, XA_TM, D_MODEL), tile),
            pl.BlockSpec((1, XA_TM, LANES), tile),
        ],
        out_shape=[
            jax.ShapeDtypeStruct((BATCH, SEQ, D_MODEL), F32),
            jax.ShapeDtypeStruct((BATCH, SEQ, D_MODEL), BF16),
            jax.ShapeDtypeStruct((BATCH, SEQ, LANES), F32),
        ],
        compiler_params=_cparams("parallel", "parallel"),
        name="outproj_xattn",
    )(h, att, w_out, g.reshape(1, D_MODEL).astype(F32), wq, kv, wo, g_ffn.reshape(1, D_MODEL).astype(F32), rw, rb)


PREFIX_BLOCK = 256
TOPK_MAX_ITERS = 256


TOPK_ROWS = BATCH * N_EXPERTS


def _topk_kernel(aff_ref, selpos_ref, selpos_t_ref, gate_t_ref):
    a = jnp.concatenate([jnp.transpose(aff_ref[b])[:N_EXPERTS] for b in range(BATCH)], axis=0)
    gate_t_ref[...] = a
    kf = jnp.float32(CAP)

    def count(mask):
        return jnp.sum(jnp.where(mask, 1.0, 0.0), axis=1, keepdims=True)

    def cond(c):
        it, _, _, _, done = c
        return jnp.logical_and(it < TOPK_MAX_ITERS, done == 0)

    def body(c):
        it, lo, hi, _, _ = c
        mid = 0.5 * (lo + hi)
        take = count(a >= mid) >= kf
        lo = jnp.where(take, mid, lo)
        hi = jnp.where(take, hi, mid)
        top = jnp.max(jnp.where(a < hi, a, -1.0), axis=1, keepdims=True)
        bot = jnp.min(jnp.where(a >= lo, a, 3.0), axis=1, keepdims=True)
        done = jnp.min(jnp.where(top == bot, 1, 0))
        return it + 1, lo, hi, top, done

    init = (jnp.int32(0), jnp.zeros((TOPK_ROWS, 1), F32), jnp.full((TOPK_ROWS, 1), 2.0, F32),
            jnp.zeros((TOPK_ROWS, 1), F32), jnp.int32(0))
    _, _, _, kth, _ = lax.while_loop(cond, body, init)

    gt = a > kth
    eq = a == kth
    need = kf - count(gt)
    tok = lax.broadcasted_iota(jnp.int32, (TOPK_ROWS, SEQ), 1).astype(F32)
    cut = jnp.zeros((TOPK_ROWS, 1), F32)
    step = SEQ // 2
    while step >= 1:
        cand = cut + step
        cut = jnp.where(count(eq & (tok < cand)) < need, cand, cut)
        step //= 2
    sel = jnp.where(gt, 1.0, jnp.where(eq & (tok <= cut), 1.0, 0.0))

    r = lax.broadcasted_iota(jnp.int32, (PREFIX_BLOCK, PREFIX_BLOCK), 0)
    c = lax.broadcasted_iota(jnp.int32, (PREFIX_BLOCK, PREFIX_BLOCK), 1)
    utri = jnp.where(r < c, 1.0, 0.0).astype(BF16)
    off = jnp.zeros((TOPK_ROWS, 1), F32)
    for j in range(SEQ // PREFIX_BLOCK):
        blk = sel[:, j * PREFIX_BLOCK:(j + 1) * PREFIX_BLOCK]
        pos = _dot(blk.astype(BF16), utri) + off
        selpos_t_ref[:, j * PREFIX_BLOCK:(j + 1) * PREFIX_BLOCK] = jnp.where(blk > 0.5, pos, -1.0)
        off = off + jnp.sum(blk, axis=1, keepdims=True)

    for b in range(BATCH):
        padded = jnp.concatenate([selpos_t_ref[b * N_EXPERTS:(b + 1) * N_EXPERTS, :],
                                  jnp.zeros((LANES - N_EXPERTS, SEQ), F32)], axis=0)
        selpos_ref[b] = jnp.transpose(padded)[:, :N_EXPERTS]


def _topk(aff):
    whole = lambda i: (0, 0)
    return pl.pallas_call(
        _topk_kernel,
        grid=(1,),
        in_specs=[pl.BlockSpec((BATCH, SEQ, LANES), lambda i: (0, 0, 0))],
        out_specs=[
            pl.BlockSpec((BATCH, SEQ, N_EXPERTS), lambda i: (0, 0, 0)),
            pl.BlockSpec((TOPK_ROWS, SEQ), whole),
            pl.BlockSpec((TOPK_ROWS, SEQ), whole),
        ],
        out_shape=[
            jax.ShapeDtypeStruct((BATCH, SEQ, N_EXPERTS), F32),
            jax.ShapeDtypeStruct((TOPK_ROWS, SEQ), F32),
            jax.ShapeDtypeStruct((TOPK_ROWS, SEQ), F32),
        ],
        compiler_params=_cparams("arbitrary"),
        name="expert_topk",
    )(aff)


def _gather_kernel(xn_ref, selpos_t_ref, gate_t_ref, xg_ref, gs_ref):
    e = pl.program_id(1)
    slot_of_token = selpos_t_ref[pl.ds(e, 1), :]
    gate_row = gate_t_ref[pl.ds(e, 1), :]
    slot = lax.broadcasted_iota(jnp.int32, (CAP, SEQ), 0).astype(F32)
    hit = slot == slot_of_token
    xg_ref[0, 0] = _dot(jnp.where(hit, 1.0, 0.0).astype(BF16), xn_ref[0]).astype(BF16)
    gs_ref[0, 0] = jnp.sum(jnp.where(hit, gate_row, 0.0), axis=1, keepdims=True)


def _gather(xn, selpos_t, gate_t):
    return pl.pallas_call(
        _gather_kernel,
        grid=(BATCH, N_EXPERTS),
        in_specs=[
            pl.BlockSpec((1, SEQ, D_MODEL), lambda b, e: (b, 0, 0)),
            pl.BlockSpec((N_EXPERTS, SEQ), lambda b, e: (b, 0)),
            pl.BlockSpec((N_EXPERTS, SEQ), lambda b, e: (b, 0)),
        ],
        out_specs=[
            pl.BlockSpec((1, 1, CAP, D_MODEL), lambda b, e: (e, b, 0, 0)),
            pl.BlockSpec((1, 1, CAP, 1), lambda b, e: (e, b, 0, 0)),
        ],
        out_shape=[
            jax.ShapeDtypeStruct((N_EXPERTS, BATCH, CAP, D_MODEL), BF16),
            jax.ShapeDtypeStruct((N_EXPERTS, BATCH, CAP, 1), F32),
        ],
        compiler_params=_cparams("parallel", "arbitrary"),
        name="moe_gather",
    )(xn, selpos_t, gate_t)


FF_TF = 256


def _expert_ffn_kernel(xg_ref, wg_ref, wu_ref, wd_ref, gs_ref, y_ref, acc_ref):
    f = pl.program_id(1)

    @pl.when(f == 0)
    def _():
        acc_ref[...] = jnp.zeros_like(acc_ref)

    x = xg_ref[0].reshape(BATCH * CAP, D_MODEL)
    gate = _dot(x, wg_ref[0, 0].astype(BF16))
    up = _dot(x, wu_ref[0, 0].astype(BF16))
    act = (gate * (1.0 / (1.0 + jnp.exp(-gate))) * up).astype(BF16)
    acc_ref[...] += _dot(act, wd_ref[0, 0].astype(BF16))

    @pl.when(f == pl.num_programs(1) - 1)
    def _():
        y = acc_ref[...] * gs_ref[0].reshape(BATCH * CAP, 1)
        y_ref[0] = y.astype(BF16).reshape(BATCH, CAP, D_MODEL)


def _expert_ffn(xg, w_gate, w_up, w_down, gs, layer):
    return pl.pallas_call(
        _expert_ffn_kernel,
        grid=(N_EXPERTS, EXPERT_FF // FF_TF),
        in_specs=[
            pl.BlockSpec((1, BATCH, CAP, D_MODEL), lambda e, f: (e, 0, 0, 0)),
            pl.BlockSpec((1, 1, D_MODEL, FF_TF), lambda e, f: (layer, e, 0, f)),
            pl.BlockSpec((1, 1, D_MODEL, FF_TF), lambda e, f: (layer, e, 0, f)),
            pl.BlockSpec((1, 1, FF_TF, D_MODEL), lambda e, f: (layer, e, f, 0)),
            pl.BlockSpec((1, BATCH, CAP, 1), lambda e, f: (e, 0, 0, 0)),
        ],
        out_specs=pl.BlockSpec((1, BATCH, CAP, D_MODEL), lambda e, f: (e, 0, 0, 0)),
        out_shape=jax.ShapeDtypeStruct((N_EXPERTS, BATCH, CAP, D_MODEL), BF16),
        scratch_shapes=[pltpu.VMEM((BATCH * CAP, D_MODEL), F32)],
        compiler_params=_cparams("parallel", "arbitrary"),
        name="moe_ffn",
    )(xg, w_gate, w_up, w_down, gs)


CB_TM = 256
CB_EXPERTS = 16


def _combine_kernel(h_ref, selpos_ref, y_ref, *rest):
    o_ref = rest[-1]
    c = pl.program_id(2)

    @pl.when(c == 0)
    def _():
        o_ref[...] = h_ref[...]

    sp = selpos_ref[0]
    lane = lax.broadcasted_iota(jnp.int32, sp.shape, 1)
    slot = lax.broadcasted_iota(jnp.int32, (CB_TM, CAP), 1).astype(F32)
    pieces = []
    for k in range(CB_EXPERTS):
        col = jnp.sum(jnp.where(lane == c * CB_EXPERTS + k, sp, 0.0), axis=1, keepdims=True)
        pieces.append(jnp.where(col == slot, 1.0, 0.0).astype(BF16))
    onehot = jnp.concatenate(pieces, axis=1)
    o_ref[0] += _dot(onehot, y_ref[...].reshape(CB_EXPERTS * CAP, D_MODEL))

    if len(rest) == 2:
        @pl.when(c == pl.num_programs(2) - 1)
        def _():
            o_ref[0] = _rms(o_ref[0], rest[0][...])


def _combine(h, selpos, y, final_g=None):
    in_specs = [
        pl.BlockSpec((1, CB_TM, D_MODEL), lambda b, i, c: (b, i, 0)),
        pl.BlockSpec((1, CB_TM, N_EXPERTS), lambda b, i, c: (b, i, 0)),
        pl.BlockSpec((CB_EXPERTS, 1, CAP, D_MODEL), lambda b, i, c: (c, b, 0, 0)),
    ]
    args = [h, selpos, y]
    if final_g is not None:
        in_specs.append(pl.BlockSpec((1, D_MODEL), lambda b, i, c: (0, 0)))
        args.append(final_g.reshape(1, D_MODEL).astype(F32))
    return pl.pallas_call(
        _combine_kernel,
        grid=(BATCH, SEQ // CB_TM, N_EXPERTS // CB_EXPERTS),
        in_specs=in_specs,
        out_specs=pl.BlockSpec((1, CB_TM, D_MODEL), lambda b, i, c: (b, i, 0)),
        out_shape=jax.ShapeDtypeStruct((BATCH, SEQ, D_MODEL), F32),
        compiler_params=_cparams("parallel", "parallel", "arbitrary"),
        name="moe_combine",
    )(*args)


def _angles(pos, dim, theta):
    inv = jnp.power(jnp.float32(theta), -(jnp.arange(0, dim, 2, dtype=jnp.float32) / dim))
    ang = pos.astype(jnp.float32)[:, None] * inv[None, :]
    return jnp.cos(ang), jnp.sin(ang)


def _rot_tables(groups, width=HEAD_DIM):
    cos_parts, sneg_parts, spos_parts = [], [], []
    used = 0
    for c, s in groups:
        z = jnp.zeros_like(s)
        cos_parts += [c, c]
        sneg_parts += [-s, z]
        spos_parts += [z, s]
        used += 2 * c.shape[1]
    rest = width - used
    if rest:
        cos_parts.append(jnp.ones((SEQ, rest), F32))
        sneg_parts.append(jnp.zeros((SEQ, rest), F32))
        spos_parts.append(jnp.zeros((SEQ, rest), F32))
    return (jnp.concatenate(cos_parts, axis=1), jnp.concatenate(sneg_parts, axis=1),
            jnp.concatenate(spos_parts, axis=1))


def _gqa_w_in(w_in, scale_q):
    if scale_q:
        col = jnp.concatenate([jnp.full((N_HEADS * HEAD_DIM,), HEAD_DIM ** -0.5 * LOG2E, F32),
                               jnp.ones((2 * N_KV_HEADS * HEAD_DIM,), F32)])
        w_in = w_in * col[None, :]
    return w_in.astype(BF16)


def _mixer_gqa_in(h2, g, w_in, scale_q):
    qkv = _norm_matmul(h2, g, _gqa_w_in(w_in, scale_q), tm=1024, tn=1024, head_w=HEAD_DIM)
    return qkv.reshape(N_HEADS + 2 * N_KV_HEADS, BATCH, SEQ, HEAD_DIM)


def _mla_weights(w_in, w_uq):
    w_in = jnp.pad(w_in, ((0, 0), (0, MLA_LAT_PAD - w_in.shape[-1]))).astype(BF16)
    w_uq = jnp.pad((w_uq * (MLA_QK_DIM ** -0.5 * LOG2E)).reshape(MLA_Q_RANK, N_HEADS, MLA_QK_DIM),
                   ((0, 0), (0, 0), (0, MLA_Q_PAD - MLA_QK_DIM))).reshape(MLA_Q_RANK, N_HEADS * MLA_Q_PAD)
    return w_in, w_uq.astype(BF16)


def kernel(x, mem, norm_mix_g, norm_xa_g, norm_ffn_g, a_w_in, a_sink, a_w_out, b_w_in, b_q_norm_g, b_k_norm_g, b_w_out, c_w_in, c_q_lat_norm_g, c_kv_lat_norm_g, c_w_uq, c_w_ukv, c_w_out, d_w_in, d_rpb, d_w_out, mem_norm_g, xa_wq, xa_wkv, xa_wo, router_w, moe_w_gate, moe_w_up, moe_w_down, final_norm_g):
    t = BATCH * SEQ
    pos = jnp.arange(SEQ)
    tab_a = _rot_tables([_angles(pos, ROT_DIM, ROPE_THETA)])
    tab_b = _rot_tables([_angles(pos // GRID_W, HEAD_DIM // 2, AXIAL_THETA),
                         _angles(pos % GRID_W, HEAD_DIM // 2, AXIAL_THETA)])
    tab_c = _rot_tables([_angles(pos, MLA_ROPE_DIM, ROPE_THETA)])
    wkv_all = jnp.transpose(xa_wkv, (1, 0, 2)).reshape(D_MODEL, DEPTH * 2 * XA_INNER).astype(BF16)
    kv_all = _norm_matmul(mem.reshape(BATCH * MEM_TOKENS, D_MODEL), mem_norm_g, wkv_all, tm=BATCH * MEM_TOKENS,
                          tn=2 * XA_INNER).reshape(BATCH, MEM_TOKENS, DEPTH * 2 * XA_INNER)

    h = x
    for i in range(DEPTH):
        m, j = i % N_MIXERS, i // N_MIXERS
        h2 = h.reshape(t, D_MODEL)
        if m == 0:
            qkv = _mixer_gqa_in(h2, norm_mix_g[i], a_w_in[j], True)
            att = _attn_window(qkv, a_sink[j] * LOG2E, tab_a)
            w_out = a_w_out[j]
        elif m == 1:
            qkv = _mixer_gqa_in(h2, norm_mix_g[i], b_w_in[j], False)
            att = _attn_axial(qkv, b_q_norm_g[j] * (HEAD_DIM ** -0.5 * LOG2E), b_k_norm_g[j], tab_b)
            w_out = b_w_out[j]
        elif m == 2:
            w_in, w_uq = _mla_weights(c_w_in[j], c_w_uq[j])
            lat = _norm_matmul(h2, norm_mix_g[i], w_in, tm=1024, tn=MLA_LAT_PAD)
            qfull = _norm_matmul(lat, c_q_lat_norm_g[j], w_uq, col_block=0, tm=2048, tn=1024, head_w=MLA_Q_PAD)
            kvfull = _norm_matmul(lat, c_kv_lat_norm_g[j], c_w_ukv[j].astype(BF16), col_block=1, tm=2048, tn=1024,
                                  head_w=MLA_NOPE_DIM)
            k_rope = lat[:, MLA_Q_RANK + MLA_KV_RANK:].reshape(BATCH, SEQ, HEAD_DIM)
            att = _attn_mla(qfull.reshape(N_HEADS, BATCH, SEQ, MLA_Q_PAD),
                            kvfull.reshape(2 * N_HEADS, BATCH, SEQ, MLA_NOPE_DIM), k_rope, tab_c)
            w_out = c_w_out[j]
        else:
            qkv = _mixer_gqa_in(h2, norm_mix_g[i], d_w_in[j], True)
            att = _attn_na(qkv, _na_bias(d_rpb[j] * LOG2E))
            w_out = d_w_out[j]

        h, xn, aff = _xattn(h, att, w_out.astype(BF16), norm_xa_g[i],
                            (xa_wq[i] * (XA_HEAD_DIM ** -0.5 * LOG2E)).astype(BF16),
                            kv_all, i, xa_wo[i].astype(BF16), norm_ffn_g[i], router_w[i])

        selpos, selpos_t, gate_t = _topk(aff)
        xg, gs = _gather(xn, selpos_t, gate_t)
        y = _expert_ffn(xg, moe_w_gate, moe_w_up, moe_w_down, gs, i)
        h = _combine(h, selpos, y, final_norm_g if i == DEPTH - 1 else None)

    return h
```

```python
import jax
import jax.numpy as jnp
from jax import lax
from jax.experimental import pallas as pl
from jax.experimental.pallas import tpu as pltpu

F32 = jnp.float32
BF16 = jnp.bfloat16

D_MODEL = 2048
BATCH = 4
SEQ = 2048
DEPTH = 4
N_MIXERS = 4
HEAD_DIM = 128
N_HEADS = 16
N_KV_HEADS = 4
GQA_GROUP = N_HEADS // N_KV_HEADS
GQA_IN = (N_HEADS + 2 * N_KV_HEADS) * HEAD_DIM
Q_BLOCK = 128
WINDOW = 128
ROPE_THETA = 500000.0
ROT_DIM = HEAD_DIM // 4
AXIAL_THETA = 10000.0
GRID_W = 64
GRID_H = SEQ // GRID_W
MLA_Q_RANK = 512
MLA_KV_RANK = 512
MLA_NOPE_DIM = 128
MLA_ROPE_DIM = 64
MLA_V_DIM = 128
MLA_QK_DIM = MLA_NOPE_DIM + MLA_ROPE_DIM
MLA_Q_PAD = 256
MLA_LAT_PAD = 1152
NA_ROWS = 8
NA_COLS = 16
MEM_TOKENS = 256
XA_HEADS = 4
XA_HEAD_DIM = 128
XA_INNER = XA_HEADS * XA_HEAD_DIM
N_EXPERTS = 16
EC_CAPACITY = 2
EXPERT_FF = 1024
CAP = EC_CAPACITY * SEQ // N_EXPERTS
EPS = 1e-6
MASKED = -1e30
LOG2E = 1.4426950408889634
LANES = 128

VMEM_LIMIT_BYTES = 56 * 1024 * 1024


def _cparams(*sem):
    return pltpu.CompilerParams(dimension_semantics=sem, vmem_limit_bytes=VMEM_LIMIT_BYTES)


def _dot_nt(a, b):
    return lax.dot_general(a, b, (((1,), (1,)), ((), ())), preferred_element_type=F32)


def _dot(a, b):
    return jnp.dot(a, b, preferred_element_type=F32)


def _rms(x, g):
    ms = jnp.mean(x * x, axis=-1, keepdims=True)
    return x * lax.rsqrt(ms + EPS) * g


def _rot(x, cos, sneg, spos, shift):
    n = x.shape[-1]
    return x * cos + pltpu.roll(x, n - shift, 1) * sneg + pltpu.roll(x, shift, 1) * spos


PV_CHUNK = 512


def _softmax_pv(s, v_ones, sink=None):
    dv = v_ones.shape[1] // 2
    keys = s.shape[1]
    m = jnp.max(s, axis=-1, keepdims=True)
    if sink is not None:
        m = jnp.maximum(m, sink)
    r = None
    for j in range(0, keys, PV_CHUNK):
        part = _dot(jnp.exp2(s[:, j:j + PV_CHUNK] - m).astype(BF16), v_ones[j:j + PV_CHUNK])
        r = part if r is None else r + part
    den = r[:, dv:]
    if sink is not None:
        den = den + jnp.exp2(sink - m)
    return r[:, :dv] * (1.0 / den)


def _fill_v_ones(v_scr, v):
    dv = v.shape[1]
    v_scr[:, :dv] = v
    v_scr[:, dv:] = jnp.ones_like(v)


def _pairs(n_blocks, prep, scores, finish, bufs):
    q_a, q_b, s_a, s_b = bufs
    last = n_blocks - 1
    q_b[...] = prep(0)
    s_b[...] = scores(0, q_b[...])
    q_a[...] = prep(1)

    def pair(i, carry):
        n = 2 * i
        s_a[...] = scores(n + 1, q_a[...])
        q_b[...] = prep(jnp.minimum(n + 2, last))
        finish(n, s_b[...])
        s_b[...] = scores(jnp.minimum(n + 2, last), q_b[...])
        q_a[...] = prep(jnp.minimum(n + 3, last))
        finish(n + 1, s_a[...])
        return carry

    lax.fori_loop(0, n_blocks // 2, pair, 0)


WAVE = 4


def _waves(n_blocks, prep, scores, finish, bufs):
    last = n_blocks - 1
    for k in range(WAVE):
        bufs[k][...] = scores(k, prep(k))

    def wave(i, carry):
        n = i * WAVE
        for k in range(WAVE):
            finish(n + k, bufs[k][...])
        for k in range(WAVE):
            nxt = jnp.minimum(n + WAVE + k, last)
            bufs[k][...] = scores(nxt, prep(nxt))
        return carry

    lax.fori_loop(0, n_blocks // WAVE, wave, 0)


def _wave_bufs(rows, keys):
    return [pltpu.VMEM((rows, keys), F32) for _ in range(WAVE)]


def _pipeline_bufs(rows, dq, keys):
    return [pltpu.VMEM((rows, dq), BF16), pltpu.VMEM((rows, dq), BF16),
            pltpu.VMEM((rows, keys), F32), pltpu.VMEM((rows, keys), F32)]


def _norm_matmul_kernel(x_ref, g_ref, w_ref, *rest):
    o_ref, xn_ref = rest[-2:]

    @pl.when(pl.program_id(1) == 0)
    def _():
        xn_ref[...] = _rms(x_ref[...].astype(F32), g_ref[...]).astype(BF16)

    res = _dot(xn_ref[...], w_ref[...].astype(BF16))
    if len(rest) == 3:
        res = res * rest[0][...]
    if len(o_ref.shape) == 3:
        head_w = o_ref.shape[2]
        for hh in range(o_ref.shape[0]):
            o_ref[hh] = res[:, hh * head_w:(hh + 1) * head_w].astype(o_ref.dtype)
    else:
        o_ref[...] = res.astype(o_ref.dtype)


def _norm_matmul(x, g, w, *, col_block=0, tm, tn, head_w=None, col_scale=None):
    t = x.shape[0]
    k, n = w.shape
    in_specs = [
        pl.BlockSpec((tm, k), lambda i, j: (i, col_block)),
        pl.BlockSpec((1, k), lambda i, j: (0, 0)),
        pl.BlockSpec((k, tn), lambda i, j: (0, j)),
    ]
    args = [x, g.reshape(1, k).astype(F32), w]
    if col_scale is not None:
        in_specs.append(pl.BlockSpec((1, tn), lambda i, j: (0, j)))
        args.append(col_scale.reshape(1, n).astype(F32))
    if head_w is None:
        out_spec = pl.BlockSpec((tm, tn), lambda i, j: (i, j))
        out_shape = jax.ShapeDtypeStruct((t, n), BF16)
    else:
        out_spec = pl.BlockSpec((tn // head_w, tm, head_w), lambda i, j: (j, i, 0))
        out_shape = jax.ShapeDtypeStruct((n // head_w, t, head_w), BF16)
    return pl.pallas_call(
        _norm_matmul_kernel,
        grid=(t // tm, n // tn),
        in_specs=in_specs,
        out_specs=out_spec,
        out_shape=out_shape,
        scratch_shapes=[pltpu.VMEM((tm, k), BF16)],
        compiler_params=_cparams("parallel", "arbitrary"),
        name="norm_matmul",
    )(*args)


def _attn_window_kernel(sink_ref, q_ref, k_ref, v_ref, cos_ref, sneg_ref, spos_ref, o_ref, k_scr, v_scr, *bufs):
    kvh = pl.program_id(1)
    half = ROT_DIM // 2
    nb = SEQ // Q_BLOCK
    rows, keys = GQA_GROUP * Q_BLOCK, 3 * Q_BLOCK
    k_scr[...] = _rot(k_ref[0, 0].astype(F32), cos_ref[...], sneg_ref[...], spos_ref[...], half).astype(BF16)
    _fill_v_ones(v_scr, v_ref[0, 0])
    r = lax.broadcasted_iota(jnp.int32, (rows, keys), 0)
    c = lax.broadcasted_iota(jnp.int32, (rows, keys), 1)
    base = (r & (Q_BLOCK - 1)) - c
    sink = jnp.concatenate(
        [jnp.full((Q_BLOCK, 1), sink_ref[kvh * GQA_GROUP + g], F32) for g in range(GQA_GROUP)], axis=0)

    def window_start(n):
        return pl.multiple_of(jnp.clip(n - 1, 0, nb - 3) * Q_BLOCK, Q_BLOCK)

    def prep(n):
        q0 = pl.multiple_of(n * Q_BLOCK, Q_BLOCK)
        cos = cos_ref[pl.ds(q0, Q_BLOCK), :]
        sneg = sneg_ref[pl.ds(q0, Q_BLOCK), :]
        spos = spos_ref[pl.ds(q0, Q_BLOCK), :]
        return jnp.concatenate(
            [_rot(q_ref[g, 0, pl.ds(q0, Q_BLOCK), :].astype(F32),
                  cos, sneg, spos, half).astype(BF16) for g in range(GQA_GROUP)], axis=0)

    def scores(n, q4):
        return _dot_nt(q4, k_scr[pl.ds(window_start(n), keys), :])

    def finish(n, s):
        q0 = pl.multiple_of(n * Q_BLOCK, Q_BLOCK)
        k0 = window_start(n)
        s = jnp.where(jnp.abs(base + (q0 - k0)) <= WINDOW, s, MASKED)
        o = _softmax_pv(s, v_scr[pl.ds(k0, keys), :], sink)
        for g in range(GQA_GROUP):
            o_ref[g, 0, pl.ds(q0, Q_BLOCK), :] = (
                o[g * Q_BLOCK:(g + 1) * Q_BLOCK].astype(o_ref.dtype))

    _waves(nb, prep, scores, finish, bufs)


def _gqa_specs():
    head = (1, 1, SEQ, HEAD_DIM)
    group = (GQA_GROUP, 1, SEQ, HEAD_DIM)
    return [
        pl.BlockSpec(group, lambda b, h: (h, b, 0, 0)),
        pl.BlockSpec(head, lambda b, h: (N_HEADS + h, b, 0, 0)),
        pl.BlockSpec(head, lambda b, h: (N_HEADS + N_KV_HEADS + h, b, 0, 0)),
    ], pl.BlockSpec(group, lambda b, h: (h, b, 0, 0))


_ATT_SHAPE = jax.ShapeDtypeStruct((N_HEADS, BATCH, SEQ, HEAD_DIM), BF16)


def _attn_window(qkv, sink, tables):
    cos, sneg, spos = tables
    qkv_specs, out_spec = _gqa_specs()
    tab = pl.BlockSpec((SEQ, HEAD_DIM), lambda b, h: (0, 0))
    return pl.pallas_call(
        _attn_window_kernel,
        grid=(BATCH, N_KV_HEADS),
        in_specs=[pl.BlockSpec(memory_space=pltpu.SMEM)] + qkv_specs + [tab, tab, tab],
        out_specs=out_spec,
        out_shape=_ATT_SHAPE,
        scratch_shapes=[pltpu.VMEM((SEQ, HEAD_DIM), BF16), pltpu.VMEM((SEQ, 2 * HEAD_DIM), BF16)]
        + _wave_bufs(GQA_GROUP * Q_BLOCK, 3 * Q_BLOCK),
        compiler_params=_cparams("parallel", "parallel"),
        name="attn_window",
    )(sink.astype(F32), qkv, qkv, qkv, cos, sneg, spos)


AX_TQ = 128


def _attn_axial_kernel(q_ref, k_ref, v_ref, qg_ref, kg_ref, cos_ref, sneg_ref, spos_ref, o_ref, k_scr, v_scr, *bufs):
    quarter = HEAD_DIM // 4
    kn = _rms(k_ref[0, 0].astype(F32), kg_ref[...])
    k_scr[...] = _rot(kn, cos_ref[...], sneg_ref[...], spos_ref[...], quarter).astype(BF16)
    _fill_v_ones(v_scr, v_ref[0, 0])

    def prep(n):
        q0 = pl.multiple_of(n * AX_TQ, AX_TQ)
        cos = cos_ref[pl.ds(q0, AX_TQ), :]
        sneg = sneg_ref[pl.ds(q0, AX_TQ), :]
        spos = spos_ref[pl.ds(q0, AX_TQ), :]
        return jnp.concatenate(
            [_rot(_rms(q_ref[g, 0, pl.ds(q0, AX_TQ), :].astype(F32), qg_ref[...]),
                  cos, sneg, spos, quarter).astype(BF16) for g in range(GQA_GROUP)], axis=0)

    def scores(n, q4):
        return _dot_nt(q4, k_scr[...])

    def finish(n, s):
        q0 = pl.multiple_of(n * AX_TQ, AX_TQ)
        o = _softmax_pv(s, v_scr[...])
        for g in range(GQA_GROUP):
            o_ref[g, 0, pl.ds(q0, AX_TQ), :] = (
                o[g * AX_TQ:(g + 1) * AX_TQ].astype(o_ref.dtype))

    _pairs(SEQ // AX_TQ, prep, scores, finish, bufs)


def _attn_axial(qkv, qg, kg, tables):
    cos, sneg, spos = tables
    qkv_specs, out_spec = _gqa_specs()
    tab = pl.BlockSpec((SEQ, HEAD_DIM), lambda b, h: (0, 0))
    gain = pl.BlockSpec((1, HEAD_DIM), lambda b, h: (0, 0))
    return pl.pallas_call(
        _attn_axial_kernel,
        grid=(BATCH, N_KV_HEADS),
        in_specs=qkv_specs + [gain, gain, tab, tab, tab],
        out_specs=out_spec,
        out_shape=_ATT_SHAPE,
        scratch_shapes=[pltpu.VMEM((SEQ, HEAD_DIM), BF16), pltpu.VMEM((SEQ, 2 * HEAD_DIM), BF16)]
        + _pipeline_bufs(GQA_GROUP * AX_TQ, HEAD_DIM, SEQ),
        compiler_params=_cparams("parallel", "parallel"),
        name="attn_axial",
    )(qkv, qkv, qkv, qg.reshape(1, HEAD_DIM).astype(F32), kg.reshape(1, HEAD_DIM).astype(F32), cos, sneg, spos)


MLA_TQ = 512


def _attn_mla_kernel(q_ref, kn_ref, v_ref, kr_ref, cos_ref, sneg_ref, spos_ref, o_ref, k_scr, v_scr, *bufs):
    half = MLA_ROPE_DIM // 2
    k_scr[:, :MLA_NOPE_DIM] = kn_ref[0, 0]

    @pl.when(pl.program_id(1) == 0)
    def _():
        k_scr[:, MLA_NOPE_DIM:] = _rot(kr_ref[0].astype(F32), cos_ref[...], sneg_ref[...], spos_ref[...],
                                       half).astype(BF16)

    _fill_v_ones(v_scr, v_ref[0, 0])

    def prep(n):
        q0 = pl.multiple_of(n * MLA_TQ, MLA_TQ)
        cos = cos_ref[pl.ds(q0, MLA_TQ), :]
        sneg = sneg_ref[pl.ds(q0, MLA_TQ), :]
        spos = spos_ref[pl.ds(q0, MLA_TQ), :]
        qn = q_ref[0, 0, pl.ds(q0, MLA_TQ), :MLA_NOPE_DIM]
        qr = _rot(q_ref[0, 0, pl.ds(q0, MLA_TQ), MLA_NOPE_DIM:].astype(F32), cos, sneg, spos, half).astype(BF16)
        return jnp.concatenate([qn, qr], axis=1)

    def scores(n, q):
        return _dot_nt(q, k_scr[...])

    def finish(n, s):
        q0 = pl.multiple_of(n * MLA_TQ, MLA_TQ)
        o_ref[0, 0, pl.ds(q0, MLA_TQ), :] = _softmax_pv(s, v_scr[...]).astype(o_ref.dtype)

    _pairs(SEQ // MLA_TQ, prep, scores, finish, bufs)


def _attn_mla(qfull, kvfull, k_rope, tables):
    cos, sneg, spos = tables
    tab = pl.BlockSpec((SEQ, HEAD_DIM), lambda b, h: (0, 0))
    return pl.pallas_call(
        _attn_mla_kernel,
        grid=(BATCH, N_HEADS),
        in_specs=[
            pl.BlockSpec((1, 1, SEQ, MLA_Q_PAD), lambda b, h: (h, b, 0, 0)),
            pl.BlockSpec((1, 1, SEQ, MLA_NOPE_DIM), lambda b, h: (2 * h, b, 0, 0)),
            pl.BlockSpec((1, 1, SEQ, MLA_V_DIM), lambda b, h: (2 * h + 1, b, 0, 0)),
            pl.BlockSpec((1, SEQ, HEAD_DIM), lambda b, h: (b, 0, 0)),
            tab, tab, tab,
        ],
        out_specs=pl.BlockSpec((1, 1, SEQ, MLA_V_DIM), lambda b, h: (h, b, 0, 0)),
        out_shape=_ATT_SHAPE,
        scratch_shapes=[pltpu.VMEM((SEQ, MLA_Q_PAD), BF16), pltpu.VMEM((SEQ, 2 * MLA_V_DIM), BF16)]
        + _pipeline_bufs(MLA_TQ, MLA_Q_PAD, SEQ),
        compiler_params=_cparams("parallel", "arbitrary"),
        name="attn_mla",
    )(qfull, kvfull, kvfull, k_rope, cos, sneg, spos)


NA_DELTAS = NA_ROWS
NA_KEYS = NA_ROWS * GRID_W
NA_DR = 2 * NA_ROWS - 1
NA_DC = 2 * NA_COLS - 1


def _na_bias_kernel(rpb_ref, o_ref):
    h = pl.program_id(0)
    shape = (GRID_W, 2 * GRID_W)
    qc = lax.broadcasted_iota(jnp.int32, shape, 0)
    lane = lax.broadcasted_iota(jnp.int32, shape, 1)
    kc = jnp.where(lane >= GRID_W, lane - GRID_W, lane)
    dc = kc - qc + (NA_COLS - 1)
    c_start = jnp.clip(qc - NA_COLS // 2, 0, GRID_W - NA_COLS)
    in_win = (kc >= c_start) & (kc < c_start + NA_COLS)
    low = lane < GRID_W
    tiles = [jnp.zeros(shape, F32) for _ in range(NA_DR)]
    for d in range(NA_DC):
        hit = dc == d
        for dr in range(NA_DR):
            tiles[dr] = jnp.where(hit, rpb_ref[(h * NA_DR + dr) * NA_DC + d], tiles[dr])
    for delta in range(NA_DELTAS):
        for j in range(NA_ROWS // 2):
            t = jnp.where(low, tiles[delta + 2 * j], tiles[delta + 2 * j + 1])
            o_ref[0, delta, :, j * 2 * GRID_W:(j + 1) * 2 * GRID_W] = jnp.where(in_win, t, MASKED)


def _na_bias(rpb):
    return pl.pallas_call(
        _na_bias_kernel,
        grid=(N_HEADS,),
        in_specs=[pl.BlockSpec(memory_space=pltpu.SMEM)],
        out_specs=pl.BlockSpec((1, NA_DELTAS, GRID_W, NA_KEYS), lambda h: (h, 0, 0, 0)),
        out_shape=jax.ShapeDtypeStruct((N_HEADS, NA_DELTAS, GRID_W, NA_KEYS), F32),
        compiler_params=_cparams("parallel"),
        name="na_bias",
    )(rpb.astype(F32).reshape(-1))


def _attn_na_kernel(q_ref, k_ref, v_ref, bias_ref, o_ref, v_scr, *bufs):
    _fill_v_ones(v_scr, v_ref[0, 0])

    def first_key_row(r):
        return jnp.clip(r - NA_ROWS // 2, 0, GRID_H - NA_ROWS)

    def prep(r):
        q0 = pl.multiple_of(r * GRID_W, GRID_W)
        return jnp.concatenate([q_ref[g, 0, pl.ds(q0, GRID_W), :] for g in range(GQA_GROUP)], axis=0)

    def scores(r, q4):
        k0 = pl.multiple_of(first_key_row(r) * GRID_W, GRID_W)
        return _dot_nt(q4, k_ref[0, 0, pl.ds(k0, NA_KEYS), :])

    def finish(r, s):
        r0 = first_key_row(r)
        delta = r0 - r + (NA_ROWS - 1)
        q0 = pl.multiple_of(r * GRID_W, GRID_W)
        k0 = pl.multiple_of(r0 * GRID_W, GRID_W)
        bias = jnp.concatenate([bias_ref[g, delta] for g in range(GQA_GROUP)], axis=0)
        o = _softmax_pv(s + bias, v_scr[pl.ds(k0, NA_KEYS), :])
        for g in range(GQA_GROUP):
            o_ref[g, 0, pl.ds(q0, GRID_W), :] = (
                o[g * GRID_W:(g + 1) * GRID_W].astype(o_ref.dtype))

    _waves(GRID_H, prep, scores, finish, bufs)


def _attn_na(qkv, bias):
    qkv_specs, out_spec = _gqa_specs()
    return pl.pallas_call(
        _attn_na_kernel,
        grid=(BATCH, N_KV_HEADS),
        in_specs=qkv_specs + [pl.BlockSpec((GQA_GROUP, NA_DELTAS, GRID_W, NA_KEYS), lambda b, h: (h, 0, 0, 0))],
        out_specs=out_spec,
        out_shape=_ATT_SHAPE,
        scratch_shapes=[pltpu.VMEM((SEQ, 2 * HEAD_DIM), BF16)] + _wave_bufs(GQA_GROUP * GRID_W, NA_KEYS),
        compiler_params=_cparams("parallel", "parallel"),
        name="attn_na",
    )(qkv, qkv, qkv, bias)


XA_TM = 256


def _xattn_kernel(h_ref, att_ref, wout_ref, g_ref, wq_ref, kv_ref, wo_ref, gf_ref, rw_ref, rb_ref,
                  o_ref, xn_ref, aff_ref):
    att = jnp.concatenate([att_ref[hh, 0] for hh in range(N_HEADS)], axis=1)
    x = h_ref[0] + _dot(att, wout_ref[...])
    xn = _rms(x, g_ref[...]).astype(BF16)
    q = _dot(xn, wq_ref[...]).astype(BF16)
    outs = []
    ones = jnp.ones((MEM_TOKENS, XA_HEAD_DIM), BF16)
    for hh in range(XA_HEADS):
        k = kv_ref[0, :, hh * XA_HEAD_DIM:(hh + 1) * XA_HEAD_DIM]
        v = kv_ref[0, :, XA_INNER + hh * XA_HEAD_DIM:XA_INNER + (hh + 1) * XA_HEAD_DIM]
        s = _dot_nt(q[:, hh * XA_HEAD_DIM:(hh + 1) * XA_HEAD_DIM], k)
        outs.append(_softmax_pv(s, jnp.concatenate([v, ones], axis=1)).astype(BF16))
    y = x + _dot(jnp.concatenate(outs, axis=1), wo_ref[...])
    o_ref[0] = y
    yn = _rms(y, gf_ref[...]).astype(BF16)
    xn_ref[0] = yn
    logits = _dot(yn, rw_ref[...]) + rb_ref[...]
    e = jnp.exp(logits - jnp.max(logits, axis=-1, keepdims=True))
    aff_ref[0] = e / jnp.sum(e, axis=-1, keepdims=True)


def _xattn(h, att, w_out, g, wq, kv, layer, wo, g_ffn, router_w):
    full = lambda b, i: (0, 0)
    tile = lambda b, i: (b, i, 0)
    rw = jnp.pad(router_w, ((0, 0), (0, LANES - N_EXPERTS))).astype(BF16)
    rb = jnp.where(jnp.arange(LANES) < N_EXPERTS, 0.0, MASKED).astype(F32).reshape(1, LANES)
    return pl.pallas_call(
        _xattn_kernel,
        grid=(BATCH, SEQ // XA_TM),
        in_specs=[
            pl.BlockSpec((1, XA_TM, D_MODEL), lambda b, i: (b, i, 0)),
            pl.BlockSpec((N_HEADS, 1, XA_TM, HEAD_DIM), lambda b, i: (0, b, i, 0)),
            pl.BlockSpec(w_out.shape, full),
            pl.BlockSpec((1, D_MODEL), full),
            pl.BlockSpec((D_MODEL, XA_INNER), full),
            pl.BlockSpec((1, MEM_TOKENS, 2 * XA_INNER), lambda b, i: (b, 0, layer)),
            pl.BlockSpec((XA_INNER, D_MODEL), full),
            pl.BlockSpec((1, D_MODEL), full),
            pl.BlockSpec((D_MODEL, LANES), full),
            pl.BlockSpec((1, LANES), full),
        ],
        out_specs=[
            pl.BlockSpec((1, XA_TM, D_MODEL), tile),
            pl.BlockSpec((1, XA_TM, D_MODEL), tile),
            pl.BlockSpec((1, XA_TM, LANES), tile),
        ],
        out_shape=[
            jax.ShapeDtypeStruct((BATCH, SEQ, D_MODEL), F32),
            jax.ShapeDtypeStruct((BATCH, SEQ, D_MODEL), BF16),
            jax.ShapeDtypeStruct((BATCH, SEQ, LANES), F32),
        ],
        compiler_params=_cparams("parallel", "parallel"),
        name="outproj_xattn",
    )(h, att, w_out, g.reshape(1, D_MODEL).astype(F32), wq, kv, wo, g_ffn.reshape(1, D_MODEL).astype(F32), rw, rb)


PREFIX_BLOCK = 256
TOPK_MAX_ITERS = 256


TOPK_ROWS = BATCH * N_EXPERTS


def _topk_kernel(aff_ref, selpos_ref, selpos_t_ref, gate_t_ref):
    a = jnp.concatenate([jnp.transpose(aff_ref[b])[:N_EXPERTS] for b in range(BATCH)], axis=0)
    gate_t_ref[...] = a
    kf = jnp.float32(CAP)

    def count(mask):
        return jnp.sum(jnp.where(mask, 1.0, 0.0), axis=1, keepdims=True)

    def cond(c):
        it, _, _, _, done = c
        return jnp.logical_and(it < TOPK_MAX_ITERS, done == 0)

    def body(c):
        it, lo, hi, _, _ = c
        mid = 0.5 * (lo + hi)
        take = count(a >= mid) >= kf
        lo = jnp.where(take, mid, lo)
        hi = jnp.where(take, hi, mid)
        top = jnp.max(jnp.where(a < hi, a, -1.0), axis=1, keepdims=True)
        bot = jnp.min(jnp.where(a >= lo, a, 3.0), axis=1, keepdims=True)
        done = jnp.min(jnp.where(top == bot, 1, 0))
        return it + 1, lo, hi, top, done

    init = (jnp.int32(0), jnp.zeros((TOPK_ROWS, 1), F32), jnp.full((TOPK_ROWS, 1), 2.0, F32),
            jnp.zeros((TOPK_ROWS, 1), F32), jnp.int32(0))
    _, _, _, kth, _ = lax.while_loop(cond, body, init)

    gt = a > kth
    eq = a == kth
    need = kf - count(gt)
    tok = lax.broadcasted_iota(jnp.int32, (TOPK_ROWS, SEQ), 1).astype(F32)
    cut = jnp.zeros((TOPK_ROWS, 1), F32)
    step = SEQ // 2
    while step >= 1:
        cand = cut + step
        cut = jnp.where(count(eq & (tok < cand)) < need, cand, cut)
        step //= 2
    sel = jnp.where(gt, 1.0, jnp.where(eq & (tok <= cut), 1.0, 0.0))

    r = lax.broadcasted_iota(jnp.int32, (PREFIX_BLOCK, PREFIX_BLOCK), 0)
    c = lax.broadcasted_iota(jnp.int32, (PREFIX_BLOCK, PREFIX_BLOCK), 1)
    utri = jnp.where(r < c, 1.0, 0.0).astype(BF16)
    off = jnp.zeros((TOPK_ROWS, 1), F32)
    for j in range(SEQ // PREFIX_BLOCK):
        blk = sel[:, j * PREFIX_BLOCK:(j + 1) * PREFIX_BLOCK]
        pos = _dot(blk.astype(BF16), utri) + off
        selpos_t_ref[:, j * PREFIX_BLOCK:(j + 1) * PREFIX_BLOCK] = jnp.where(blk > 0.5, pos, -1.0)
        off = off + jnp.sum(blk, axis=1, keepdims=True)

    for b in range(BATCH):
        padded = jnp.concatenate([selpos_t_ref[b * N_EXPERTS:(b + 1) * N_EXPERTS, :],
                                  jnp.zeros((LANES - N_EXPERTS, SEQ), F32)], axis=0)
        selpos_ref[b] = jnp.transpose(padded)[:, :N_EXPERTS]


def _topk(aff):
    whole = lambda i: (0, 0)
    return pl.pallas_call(
        _topk_kernel,
        grid=(1,),
        in_specs=[pl.BlockSpec((BATCH, SEQ, LANES), lambda i: (0, 0, 0))],
        out_specs=[
            pl.BlockSpec((BATCH, SEQ, N_EXPERTS), lambda i: (0, 0, 0)),
            pl.BlockSpec((TOPK_ROWS, SEQ), whole),
            pl.BlockSpec((TOPK_ROWS, SEQ), whole),
        ],
        out_shape=[
            jax.ShapeDtypeStruct((BATCH, SEQ, N_EXPERTS), F32),
            jax.ShapeDtypeStruct((TOPK_ROWS, SEQ), F32),
            jax.ShapeDtypeStruct((TOPK_ROWS, SEQ), F32),
        ],
        compiler_params=_cparams("arbitrary"),
        name="expert_topk",
    )(aff)


def _gather_kernel(xn_ref, selpos_t_ref, gate_t_ref, xg_ref, gs_ref):
    e = pl.program_id(1)
    slot_of_token = selpos_t_ref[pl.ds(e, 1), :]
    gate_row = gate_t_ref[pl.ds(e, 1), :]
    slot = lax.broadcasted_iota(jnp.int32, (CAP, SEQ), 0).astype(F32)
    hit = slot == slot_of_token
    xg_ref[0, 0] = _dot(jnp.where(hit, 1.0, 0.0).astype(BF16), xn_ref[0]).astype(BF16)
    gs_ref[0, 0] = jnp.sum(jnp.where(hit, gate_row, 0.0), axis=1, keepdims=True)


def _gather(xn, selpos_t, gate_t):
    return pl.pallas_call(
        _gather_kernel,
        grid=(BATCH, N_EXPERTS),
        in_specs=[
            pl.BlockSpec((1, SEQ, D_MODEL), lambda b, e: (b, 0, 0)),
            pl.BlockSpec((N_EXPERTS, SEQ), lambda b, e: (b, 0)),
            pl.BlockSpec((N_EXPERTS, SEQ), lambda b, e: (b, 0)),
        ],
        out_specs=[
            pl.BlockSpec((1, 1, CAP, D_MODEL), lambda b, e: (e, b, 0, 0)),
            pl.BlockSpec((1, 1, CAP, 1), lambda b, e: (e, b, 0, 0)),
        ],
        out_shape=[
            jax.ShapeDtypeStruct((N_EXPERTS, BATCH, CAP, D_MODEL), BF16),
            jax.ShapeDtypeStruct((N_EXPERTS, BATCH, CAP, 1), F32),
        ],
        compiler_params=_cparams("parallel", "arbitrary"),
        name="moe_gather",
    )(xn, selpos_t, gate_t)


FF_TF = 256


def _expert_ffn_kernel(xg_ref, wg_ref, wu_ref, wd_ref, gs_ref, y_ref, acc_ref):
    f = pl.program_id(1)
    last = pl.num_programs(1) - 1

    def partial():
        x = xg_ref[0].reshape(BATCH * CAP, D_MODEL)
        gate = _dot(x, wg_ref[0, 0].astype(BF16))
        up = _dot(x, wu_ref[0, 0].astype(BF16))
        act = (gate * (1.0 / (1.0 + jnp.exp(-gate))) * up).astype(BF16)
        return _dot(act, wd_ref[0, 0].astype(BF16))

    @pl.when(f == 0)
    def _():
        acc_ref[...] = partial()

    @pl.when(jnp.logical_and(f > 0, f < last))
    def _():
        acc_ref[...] += partial()

    @pl.when(f == last)
    def _():
        y = (acc_ref[...] + partial()) * gs_ref[0].reshape(BATCH * CAP, 1)
        y_ref[0] = y.astype(BF16).reshape(BATCH, CAP, D_MODEL)


def _expert_ffn(xg, w_gate, w_up, w_down, gs, layer):
    return pl.pallas_call(
        _expert_ffn_kernel,
        grid=(N_EXPERTS, EXPERT_FF // FF_TF),
        in_specs=[
            pl.BlockSpec((1, BATCH, CAP, D_MODEL), lambda e, f: (e, 0, 0, 0)),
            pl.BlockSpec((1, 1, D_MODEL, FF_TF), lambda e, f: (layer, e, 0, f)),
            pl.BlockSpec((1, 1, D_MODEL, FF_TF), lambda e, f: (layer, e, 0, f)),
            pl.BlockSpec((1, 1, FF_TF, D_MODEL), lambda e, f: (layer, e, f, 0)),
            pl.BlockSpec((1, BATCH, CAP, 1), lambda e, f: (e, 0, 0, 0)),
        ],
        out_specs=pl.BlockSpec((1, BATCH, CAP, D_MODEL), lambda e, f: (e, 0, 0, 0)),
        out_shape=jax.ShapeDtypeStruct((N_EXPERTS, BATCH, CAP, D_MODEL), BF16),
        scratch_shapes=[pltpu.VMEM((BATCH * CAP, D_MODEL), F32)],
        compiler_params=_cparams("parallel", "arbitrary"),
        name="moe_ffn",
    )(xg, w_gate, w_up, w_down, gs)


CB_TM = 256


def _combine_kernel(h_ref, selpos_ref, y_ref, *rest):
    o_ref = rest[-1]
    sp = selpos_ref[0]
    slot = lax.broadcasted_iota(jnp.int32, (CB_TM, CAP), 1).astype(F32)
    onehot = jnp.concatenate(
        [jnp.where(sp[:, e:e + 1] == slot, 1.0, 0.0).astype(BF16) for e in range(N_EXPERTS)], axis=1)
    out = h_ref[0] + _dot(onehot, y_ref[...].reshape(N_EXPERTS * CAP, D_MODEL))
    if len(rest) == 2:
        out = _rms(out, rest[0][...])
    o_ref[0] = out


def _combine(h, selpos, y, final_g=None):
    in_specs = [
        pl.BlockSpec((1, CB_TM, D_MODEL), lambda b, i: (b, i, 0)),
        pl.BlockSpec((1, CB_TM, N_EXPERTS), lambda b, i: (b, i, 0)),
        pl.BlockSpec((N_EXPERTS, 1, CAP, D_MODEL), lambda b, i: (0, b, 0, 0)),
    ]
    args = [h, selpos, y]
    if final_g is not None:
        in_specs.append(pl.BlockSpec((1, D_MODEL), lambda b, i: (0, 0)))
        args.append(final_g.reshape(1, D_MODEL).astype(F32))
    return pl.pallas_call(
        _combine_kernel,
        grid=(BATCH, SEQ // CB_TM),
        in_specs=in_specs,
        out_specs=pl.BlockSpec((1, CB_TM, D_MODEL), lambda b, i: (b, i, 0)),
        out_shape=jax.ShapeDtypeStruct((BATCH, SEQ, D_MODEL), F32),
        compiler_params=_cparams("parallel", "parallel"),
        name="moe_combine",
    )(*args)


def _angles(pos, dim, theta):
    inv = jnp.power(jnp.float32(theta), -(jnp.arange(0, dim, 2, dtype=jnp.float32) / dim))
    ang = pos.astype(jnp.float32)[:, None] * inv[None, :]
    return jnp.cos(ang), jnp.sin(ang)


def _rot_tables(groups, width=HEAD_DIM):
    cos_parts, sneg_parts, spos_parts = [], [], []
    used = 0
    for c, s in groups:
        z = jnp.zeros_like(s)
        cos_parts += [c, c]
        sneg_parts += [-s, z]
        spos_parts += [z, s]
        used += 2 * c.shape[1]
    rest = width - used
    if rest:
        cos_parts.append(jnp.ones((SEQ, rest), F32))
        sneg_parts.append(jnp.zeros((SEQ, rest), F32))
        spos_parts.append(jnp.zeros((SEQ, rest), F32))
    return (jnp.concatenate(cos_parts, axis=1), jnp.concatenate(sneg_parts, axis=1),
            jnp.concatenate(spos_parts, axis=1))


def _mixer_gqa_in(h2, g, w_in, scale_q):
    col = None
    if scale_q:
        col = jnp.concatenate([jnp.full((N_HEADS * HEAD_DIM,), HEAD_DIM ** -0.5 * LOG2E, F32),
                               jnp.ones((2 * N_KV_HEADS * HEAD_DIM,), F32)])
    qkv = _norm_matmul(h2, g, w_in, tm=1024, tn=1024, head_w=HEAD_DIM, col_scale=col)
    return qkv.reshape(N_HEADS + 2 * N_KV_HEADS, BATCH, SEQ, HEAD_DIM)


def _mla_weights(w_in, w_uq):
    w_in = jnp.pad(w_in, ((0, 0), (0, MLA_LAT_PAD - w_in.shape[-1]))).astype(BF16)
    w_uq = jnp.pad((w_uq * (MLA_QK_DIM ** -0.5 * LOG2E)).reshape(MLA_Q_RANK, N_HEADS, MLA_QK_DIM),
                   ((0, 0), (0, 0), (0, MLA_Q_PAD - MLA_QK_DIM))).reshape(MLA_Q_RANK, N_HEADS * MLA_Q_PAD)
    return w_in, w_uq.astype(BF16)


def kernel(x, mem, norm_mix_g, norm_xa_g, norm_ffn_g, a_w_in, a_sink, a_w_out, b_w_in, b_q_norm_g, b_k_norm_g, b_w_out, c_w_in, c_q_lat_norm_g, c_kv_lat_norm_g, c_w_uq, c_w_ukv, c_w_out, d_w_in, d_rpb, d_w_out, mem_norm_g, xa_wq, xa_wkv, xa_wo, router_w, moe_w_gate, moe_w_up, moe_w_down, final_norm_g):
    t = BATCH * SEQ
    pos = jnp.arange(SEQ)
    tab_a = _rot_tables([_angles(pos, ROT_DIM, ROPE_THETA)])
    tab_b = _rot_tables([_angles(pos // GRID_W, HEAD_DIM // 2, AXIAL_THETA),
                         _angles(pos % GRID_W, HEAD_DIM // 2, AXIAL_THETA)])
    tab_c = _rot_tables([_angles(pos, MLA_ROPE_DIM, ROPE_THETA)])
    wkv_all = jnp.transpose(xa_wkv, (1, 0, 2)).reshape(D_MODEL, DEPTH * 2 * XA_INNER).astype(BF16)
    kv_all = _norm_matmul(mem.reshape(BATCH * MEM_TOKENS, D_MODEL), mem_norm_g, wkv_all, tm=BATCH * MEM_TOKENS,
                          tn=2 * XA_INNER).reshape(BATCH, MEM_TOKENS, DEPTH * 2 * XA_INNER)

    h = x
    for i in range(DEPTH):
        m, j = i % N_MIXERS, i // N_MIXERS
        h2 = h.reshape(t, D_MODEL)
        if m == 0:
            qkv = _mixer_gqa_in(h2, norm_mix_g[i], a_w_in[j], True)
            att = _attn_window(qkv, a_sink[j] * LOG2E, tab_a)
            w_out = a_w_out[j]
        elif m == 1:
            qkv = _mixer_gqa_in(h2, norm_mix_g[i], b_w_in[j], False)
            att = _attn_axial(qkv, b_q_norm_g[j] * (HEAD_DIM ** -0.5 * LOG2E), b_k_norm_g[j], tab_b)
            w_out = b_w_out[j]
        elif m == 2:
            w_in, w_uq = _mla_weights(c_w_in[j], c_w_uq[j])
            lat = _norm_matmul(h2, norm_mix_g[i], w_in, tm=1024, tn=MLA_LAT_PAD)
            qfull = _norm_matmul(lat, c_q_lat_norm_g[j], w_uq, col_block=0, tm=2048, tn=1024, head_w=MLA_Q_PAD)
            kvfull = _norm_matmul(lat, c_kv_lat_norm_g[j], c_w_ukv[j].astype(BF16), col_block=1, tm=2048, tn=1024,
                                  head_w=MLA_NOPE_DIM)
            k_rope = lat[:, MLA_Q_RANK + MLA_KV_RANK:].reshape(BATCH, SEQ, HEAD_DIM)
            att = _attn_mla(qfull.reshape(N_HEADS, BATCH, SEQ, MLA_Q_PAD),
                            kvfull.reshape(2 * N_HEADS, BATCH, SEQ, MLA_NOPE_DIM), k_rope, tab_c)
            w_out = c_w_out[j]
        else:
            qkv = _mixer_gqa_in(h2, norm_mix_g[i], d_w_in[j], True)
            att = _attn_na(qkv, _na_bias(d_rpb[j] * LOG2E))
            w_out = d_w_out[j]

        h, xn, aff = _xattn(h, att, w_out.astype(BF16), norm_xa_g[i],
                            (xa_wq[i] * (XA_HEAD_DIM ** -0.5 * LOG2E)).astype(BF16),
                            kv_all, i, xa_wo[i].astype(BF16), norm_ffn_g[i], router_w[i])

        selpos, selpos_t, gate_t = _topk(aff)
        xg, gs = _gather(xn, selpos_t, gate_t)
        y = _expert_ffn(xg, moe_w_gate, moe_w_up, moe_w_down, gs, i)
        h = _combine(h, selpos, y, final_norm_g if i == DEPTH - 1 else None)

    return h
```

```python
import jax
import jax.numpy as jnp
from jax import lax
from jax.experimental import pallas as pl
from jax.experimental.pallas import tpu as pltpu

F32 = jnp.float32
BF16 = jnp.bfloat16

D_MODEL = 2048
BATCH = 4
SEQ = 2048
DEPTH = 4
N_MIXERS = 4
HEAD_DIM = 128
N_HEADS = 16
N_KV_HEADS = 4
GQA_GROUP = N_HEADS // N_KV_HEADS
GQA_IN = (N_HEADS + 2 * N_KV_HEADS) * HEAD_DIM
Q_BLOCK = 128
WINDOW = 128
ROPE_THETA = 500000.0
ROT_DIM = HEAD_DIM // 4
AXIAL_THETA = 10000.0
GRID_W = 64
GRID_H = SEQ // GRID_W
MLA_Q_RANK = 512
MLA_KV_RANK = 512
MLA_NOPE_DIM = 128
MLA_ROPE_DIM = 64
MLA_V_DIM = 128
MLA_QK_DIM = MLA_NOPE_DIM + MLA_ROPE_DIM
MLA_Q_PAD = 256
MLA_LAT_PAD = 1152
NA_ROWS = 8
NA_COLS = 16
MEM_TOKENS = 256
XA_HEADS = 4
XA_HEAD_DIM = 128
XA_INNER = XA_HEADS * XA_HEAD_DIM
N_EXPERTS = 16
EC_CAPACITY = 2
EXPERT_FF = 1024
CAP = EC_CAPACITY * SEQ // N_EXPERTS
EPS = 1e-6
MASKED = -1e30
LOG2E = 1.4426950408889634
LANES = 128

VMEM_LIMIT_BYTES = 56 * 1024 * 1024


def _cparams(*sem):
    return pltpu.CompilerParams(dimension_semantics=sem, vmem_limit_bytes=VMEM_LIMIT_BYTES)


def _dot_nt(a, b):
    return lax.dot_general(a, b, (((1,), (1,)), ((), ())), preferred_element_type=F32)


def _dot(a, b):
    return jnp.dot(a, b, preferred_element_type=F32)


def _rms(x, g):
    ms = jnp.mean(x * x, axis=-1, keepdims=True)
    return x * lax.rsqrt(ms + EPS) * g


def _rot(x, cos, sneg, spos, shift):
    n = x.shape[-1]
    return x * cos + pltpu.roll(x, n - shift, 1) * sneg + pltpu.roll(x, shift, 1) * spos


PV_CHUNK = 512


def _softmax_pv(s, v_ones, sink=None):
    dv = v_ones.shape[1] // 2
    keys = s.shape[1]
    m = jnp.max(s, axis=-1, keepdims=True)
    if sink is not None:
        m = jnp.maximum(m, sink)
    r = None
    for j in range(0, keys, PV_CHUNK):
        part = _dot(jnp.exp2(s[:, j:j + PV_CHUNK] - m).astype(BF16), v_ones[j:j + PV_CHUNK])
        r = part if r is None else r + part
    den = r[:, dv:]
    if sink is not None:
        den = den + jnp.exp2(sink - m)
    return r[:, :dv] * (1.0 / den)


def _fill_v_ones(v_scr, v):
    dv = v.shape[1]
    v_scr[:, :dv] = v
    v_scr[:, dv:] = jnp.ones_like(v)


def _pairs(n_blocks, prep, scores, finish, bufs, peel):
    q_a, q_b, s_a, s_b = bufs
    last = jnp.int32(n_blocks - 1)
    q_b[...] = prep(0)
    s_b[...] = scores(0, q_b[...])
    q_a[...] = prep(1)

    def pair(i, carry):
        n = 2 * i
        s_a[...] = scores(n + 1, q_a[...])
        q_b[...] = prep(jnp.minimum(n + 2, last))
        finish(n, s_b[...])
        s_b[...] = scores(jnp.minimum(n + 2, last), q_b[...])
        q_a[...] = prep(jnp.minimum(n + 3, last))
        finish(n + 1, s_a[...])
        return carry

    if not peel:
        lax.fori_loop(0, n_blocks // 2, pair, 0)
        return
    lax.fori_loop(0, n_blocks // 2 - 1, pair, 0)
    s_a[...] = scores(last, q_a[...])
    finish(last - 1, s_b[...])
    finish(last, s_a[...])


WAVE = 4


def _waves(n_blocks, prep, scores, finish, bufs):
    last = n_blocks - 1
    for k in range(WAVE):
        bufs[k][...] = scores(k, prep(k))

    def wave(i, carry):
        n = i * WAVE
        for k in range(WAVE):
            finish(n + k, bufs[k][...])
        for k in range(WAVE):
            nxt = jnp.minimum(n + WAVE + k, last)
            bufs[k][...] = scores(nxt, prep(nxt))
        return carry

    lax.fori_loop(0, n_blocks // WAVE, wave, 0)


def _wave_bufs(rows, keys):
    return [pltpu.VMEM((rows, keys), F32) for _ in range(WAVE)]


def _pipeline_bufs(rows, dq, keys):
    return [pltpu.VMEM((rows, dq), BF16), pltpu.VMEM((rows, dq), BF16),
            pltpu.VMEM((rows, keys), F32), pltpu.VMEM((rows, keys), F32)]


def _norm_matmul_kernel(x_ref, g_ref, w_ref, *rest):
    o_ref, xn_ref = rest[-2:]

    @pl.when(pl.program_id(1) == 0)
    def _():
        xn_ref[...] = _rms(x_ref[...].astype(F32), g_ref[...]).astype(BF16)

    res = _dot(xn_ref[...], w_ref[...].astype(BF16))
    if len(rest) == 3:
        res = res * rest[0][...]
    if len(o_ref.shape) == 3:
        head_w = o_ref.shape[2]
        for hh in range(o_ref.shape[0]):
            o_ref[hh] = res[:, hh * head_w:(hh + 1) * head_w].astype(o_ref.dtype)
    else:
        o_ref[...] = res.astype(o_ref.dtype)


def _norm_matmul(x, g, w, *, col_block=0, tm, tn, head_w=None, col_scale=None):
    t = x.shape[0]
    k, n = w.shape
    in_specs = [
        pl.BlockSpec((tm, k), lambda i, j: (i, col_block)),
        pl.BlockSpec((1, k), lambda i, j: (0, 0)),
        pl.BlockSpec((k, tn), lambda i, j: (0, j)),
    ]
    args = [x, g.reshape(1, k).astype(F32), w]
    if col_scale is not None:
        in_specs.append(pl.BlockSpec((1, tn), lambda i, j: (0, j)))
        args.append(col_scale.reshape(1, n).astype(F32))
    if head_w is None:
        out_spec = pl.BlockSpec((tm, tn), lambda i, j: (i, j))
        out_shape = jax.ShapeDtypeStruct((t, n), BF16)
    else:
        out_spec = pl.BlockSpec((tn // head_w, tm, head_w), lambda i, j: (j, i, 0))
        out_shape = jax.ShapeDtypeStruct((n // head_w, t, head_w), BF16)
    return pl.pallas_call(
        _norm_matmul_kernel,
        grid=(t // tm, n // tn),
        in_specs=in_specs,
        out_specs=out_spec,
        out_shape=out_shape,
        scratch_shapes=[pltpu.VMEM((tm, k), BF16)],
        compiler_params=_cparams("parallel", "arbitrary"),
        name="norm_matmul",
    )(*args)


def _attn_window_kernel(sink_ref, q_ref, k_ref, v_ref, cos_ref, sneg_ref, spos_ref, o_ref, k_scr, v_scr, *bufs):
    kvh = pl.program_id(1)
    half = ROT_DIM // 2
    nb = SEQ // Q_BLOCK
    rows, keys = GQA_GROUP * Q_BLOCK, 3 * Q_BLOCK
    k_scr[...] = _rot(k_ref[0, 0].astype(F32), cos_ref[...], sneg_ref[...], spos_ref[...], half).astype(BF16)
    _fill_v_ones(v_scr, v_ref[0, 0])
    r = lax.broadcasted_iota(jnp.int32, (rows, keys), 0)
    c = lax.broadcasted_iota(jnp.int32, (rows, keys), 1)
    base = (r & (Q_BLOCK - 1)) - c
    sink = jnp.concatenate(
        [jnp.full((Q_BLOCK, 1), sink_ref[kvh * GQA_GROUP + g], F32) for g in range(GQA_GROUP)], axis=0)

    def window_start(n):
        return pl.multiple_of(jnp.clip(n - 1, 0, nb - 3) * Q_BLOCK, Q_BLOCK)

    def prep(n):
        q0 = pl.multiple_of(n * Q_BLOCK, Q_BLOCK)
        cos = cos_ref[pl.ds(q0, Q_BLOCK), :]
        sneg = sneg_ref[pl.ds(q0, Q_BLOCK), :]
        spos = spos_ref[pl.ds(q0, Q_BLOCK), :]
        return jnp.concatenate(
            [_rot(q_ref[g, 0, pl.ds(q0, Q_BLOCK), :].astype(F32),
                  cos, sneg, spos, half).astype(BF16) for g in range(GQA_GROUP)], axis=0)

    def scores(n, q4):
        return _dot_nt(q4, k_scr[pl.ds(window_start(n), keys), :])

    def finish(n, s):
        q0 = pl.multiple_of(n * Q_BLOCK, Q_BLOCK)
        k0 = window_start(n)
        s = jnp.where(jnp.abs(base + (q0 - k0)) <= WINDOW, s, MASKED)
        o = _softmax_pv(s, v_scr[pl.ds(k0, keys), :], sink)
        for g in range(GQA_GROUP):
            o_ref[g, 0, pl.ds(q0, Q_BLOCK), :] = (
                o[g * Q_BLOCK:(g + 1) * Q_BLOCK].astype(o_ref.dtype))

    _waves(nb, prep, scores, finish, bufs)


def _gqa_specs():
    head = (1, 1, SEQ, HEAD_DIM)
    group = (GQA_GROUP, 1, SEQ, HEAD_DIM)
    return [
        pl.BlockSpec(group, lambda b, h: (h, b, 0, 0)),
        pl.BlockSpec(head, lambda b, h: (N_HEADS + h, b, 0, 0)),
        pl.BlockSpec(head, lambda b, h: (N_HEADS + N_KV_HEADS + h, b, 0, 0)),
    ], pl.BlockSpec(group, lambda b, h: (h, b, 0, 0))


_ATT_SHAPE = jax.ShapeDtypeStruct((N_HEADS, BATCH, SEQ, HEAD_DIM), BF16)


def _attn_window(qkv, sink, tables):
    cos, sneg, spos = tables
    qkv_specs, out_spec = _gqa_specs()
    tab = pl.BlockSpec((SEQ, HEAD_DIM), lambda b, h: (0, 0))
    return pl.pallas_call(
        _attn_window_kernel,
        grid=(BATCH, N_KV_HEADS),
        in_specs=[pl.BlockSpec(memory_space=pltpu.SMEM)] + qkv_specs + [tab, tab, tab],
        out_specs=out_spec,
        out_shape=_ATT_SHAPE,
        scratch_shapes=[pltpu.VMEM((SEQ, HEAD_DIM), BF16), pltpu.VMEM((SEQ, 2 * HEAD_DIM), BF16)]
        + _wave_bufs(GQA_GROUP * Q_BLOCK, 3 * Q_BLOCK),
        compiler_params=_cparams("parallel", "parallel"),
        name="attn_window",
    )(sink.astype(F32), qkv, qkv, qkv, cos, sneg, spos)


AX_TQ = 128


def _attn_axial_kernel(q_ref, k_ref, v_ref, qg_ref, kg_ref, cos_ref, sneg_ref, spos_ref, o_ref, k_scr, v_scr, *bufs):
    quarter = HEAD_DIM // 4
    kn = _rms(k_ref[0, 0].astype(F32), kg_ref[...])
    k_scr[...] = _rot(kn, cos_ref[...], sneg_ref[...], spos_ref[...], quarter).astype(BF16)
    _fill_v_ones(v_scr, v_ref[0, 0])

    def prep(n):
        q0 = pl.multiple_of(n * AX_TQ, AX_TQ)
        cos = cos_ref[pl.ds(q0, AX_TQ), :]
        sneg = sneg_ref[pl.ds(q0, AX_TQ), :]
        spos = spos_ref[pl.ds(q0, AX_TQ), :]
        return jnp.concatenate(
            [_rot(_rms(q_ref[g, 0, pl.ds(q0, AX_TQ), :].astype(F32), qg_ref[...]),
                  cos, sneg, spos, quarter).astype(BF16) for g in range(GQA_GROUP)], axis=0)

    def scores(n, q4):
        return _dot_nt(q4, k_scr[...])

    def finish(n, s):
        q0 = pl.multiple_of(n * AX_TQ, AX_TQ)
        o = _softmax_pv(s, v_scr[...])
        for g in range(GQA_GROUP):
            o_ref[g, 0, pl.ds(q0, AX_TQ), :] = (
                o[g * AX_TQ:(g + 1) * AX_TQ].astype(o_ref.dtype))

    _pairs(SEQ // AX_TQ, prep, scores, finish, bufs, peel=False)


def _attn_axial(qkv, qg, kg, tables):
    cos, sneg, spos = tables
    qkv_specs, out_spec = _gqa_specs()
    tab = pl.BlockSpec((SEQ, HEAD_DIM), lambda b, h: (0, 0))
    gain = pl.BlockSpec((1, HEAD_DIM), lambda b, h: (0, 0))
    return pl.pallas_call(
        _attn_axial_kernel,
        grid=(BATCH, N_KV_HEADS),
        in_specs=qkv_specs + [gain, gain, tab, tab, tab],
        out_specs=out_spec,
        out_shape=_ATT_SHAPE,
        scratch_shapes=[pltpu.VMEM((SEQ, HEAD_DIM), BF16), pltpu.VMEM((SEQ, 2 * HEAD_DIM), BF16)]
        + _pipeline_bufs(GQA_GROUP * AX_TQ, HEAD_DIM, SEQ),
        compiler_params=_cparams("parallel", "parallel"),
        name="attn_axial",
    )(qkv, qkv, qkv, qg.reshape(1, HEAD_DIM).astype(F32), kg.reshape(1, HEAD_DIM).astype(F32), cos, sneg, spos)


MLA_TQ = 512


def _attn_mla_kernel(q_ref, kn_ref, v_ref, kr_ref, cos_ref, sneg_ref, spos_ref, o_ref, k_scr, v_scr, *bufs):
    half = MLA_ROPE_DIM // 2
    k_scr[:, :MLA_NOPE_DIM] = kn_ref[0, 0]

    @pl.when(pl.program_id(1) == 0)
    def _():
        k_scr[:, MLA_NOPE_DIM:] = _rot(kr_ref[0].astype(F32), cos_ref[...], sneg_ref[...], spos_ref[...],
                                       half).astype(BF16)

    _fill_v_ones(v_scr, v_ref[0, 0])

    def prep(n):
        q0 = pl.multiple_of(n * MLA_TQ, MLA_TQ)
        cos = cos_ref[pl.ds(q0, MLA_TQ), :]
        sneg = sneg_ref[pl.ds(q0, MLA_TQ), :]
        spos = spos_ref[pl.ds(q0, MLA_TQ), :]
        qn = q_ref[0, 0, pl.ds(q0, MLA_TQ), :MLA_NOPE_DIM]
        qr = _rot(q_ref[0, 0, pl.ds(q0, MLA_TQ), MLA_NOPE_DIM:].astype(F32), cos, sneg, spos, half).astype(BF16)
        return jnp.concatenate([qn, qr], axis=1)

    def scores(n, q):
        return _dot_nt(q, k_scr[...])

    def finish(n, s):
        q0 = pl.multiple_of(n * MLA_TQ, MLA_TQ)
        o_ref[0, 0, pl.ds(q0, MLA_TQ), :] = _softmax_pv(s, v_scr[...]).astype(o_ref.dtype)

    _pairs(SEQ // MLA_TQ, prep, scores, finish, bufs, peel=True)


def _attn_mla(qfull, kvfull, k_rope, tables):
    cos, sneg, spos = tables
    tab = pl.BlockSpec((SEQ, HEAD_DIM), lambda b, h: (0, 0))
    return pl.pallas_call(
        _attn_mla_kernel,
        grid=(BATCH, N_HEADS),
        in_specs=[
            pl.BlockSpec((1, 1, SEQ, MLA_Q_PAD), lambda b, h: (h, b, 0, 0)),
            pl.BlockSpec((1, 1, SEQ, MLA_NOPE_DIM), lambda b, h: (2 * h, b, 0, 0)),
            pl.BlockSpec((1, 1, SEQ, MLA_V_DIM), lambda b, h: (2 * h + 1, b, 0, 0)),
            pl.BlockSpec((1, SEQ, HEAD_DIM), lambda b, h: (b, 0, 0)),
            tab, tab, tab,
        ],
        out_specs=pl.BlockSpec((1, 1, SEQ, MLA_V_DIM), lambda b, h: (h, b, 0, 0)),
        out_shape=_ATT_SHAPE,
        scratch_shapes=[pltpu.VMEM((SEQ, MLA_Q_PAD), BF16), pltpu.VMEM((SEQ, 2 * MLA_V_DIM), BF16)]
        + _pipeline_bufs(MLA_TQ, MLA_Q_PAD, SEQ),
        compiler_params=_cparams("parallel", "arbitrary"),
        name="attn_mla",
    )(qfull, kvfull, kvfull, k_rope, cos, sneg, spos)


NA_DELTAS = NA_ROWS
NA_KEYS = NA_ROWS * GRID_W
NA_DR = 2 * NA_ROWS - 1
NA_DC = 2 * NA_COLS - 1


def _na_bias_kernel(rpb_ref, o_ref):
    h = pl.program_id(0)
    shape = (GRID_W, 2 * GRID_W)
    qc = lax.broadcasted_iota(jnp.int32, shape, 0)
    lane = lax.broadcasted_iota(jnp.int32, shape, 1)
    kc = jnp.where(lane >= GRID_W, lane - GRID_W, lane)
    dc = kc - qc + (NA_COLS - 1)
    c_start = jnp.clip(qc - NA_COLS // 2, 0, GRID_W - NA_COLS)
    in_win = (kc >= c_start) & (kc < c_start + NA_COLS)
    low = lane < GRID_W
    tiles = [jnp.zeros(shape, F32) for _ in range(NA_DR)]
    for d in range(NA_DC):
        hit = dc == d
        for dr in range(NA_DR):
            tiles[dr] = jnp.where(hit, rpb_ref[(h * NA_DR + dr) * NA_DC + d], tiles[dr])
    for delta in range(NA_DELTAS):
        for j in range(NA_ROWS // 2):
            t = jnp.where(low, tiles[delta + 2 * j], tiles[delta + 2 * j + 1])
            o_ref[0, delta, :, j * 2 * GRID_W:(j + 1) * 2 * GRID_W] = jnp.where(in_win, t, MASKED)


def _na_bias(rpb):
    return pl.pallas_call(
        _na_bias_kernel,
        grid=(N_HEADS,),
        in_specs=[pl.BlockSpec(memory_space=pltpu.SMEM)],
        out_specs=pl.BlockSpec((1, NA_DELTAS, GRID_W, NA_KEYS), lambda h: (h, 0, 0, 0)),
        out_shape=jax.ShapeDtypeStruct((N_HEADS, NA_DELTAS, GRID_W, NA_KEYS), F32),
        compiler_params=_cparams("parallel"),
        name="na_bias",
    )(rpb.astype(F32).reshape(-1))


def _attn_na_kernel(q_ref, k_ref, v_ref, bias_ref, o_ref, v_scr, *bufs):
    _fill_v_ones(v_scr, v_ref[0, 0])

    def first_key_row(r):
        return jnp.clip(r - NA_ROWS // 2, 0, GRID_H - NA_ROWS)

    def prep(r):
        q0 = pl.multiple_of(r * GRID_W, GRID_W)
        return jnp.concatenate([q_ref[g, 0, pl.ds(q0, GRID_W), :] for g in range(GQA_GROUP)], axis=0)

    def scores(r, q4):
        k0 = pl.multiple_of(first_key_row(r) * GRID_W, GRID_W)
        return _dot_nt(q4, k_ref[0, 0, pl.ds(k0, NA_KEYS), :])

    def finish(r, s):
        r0 = first_key_row(r)
        delta = r0 - r + (NA_ROWS - 1)
        q0 = pl.multiple_of(r * GRID_W, GRID_W)
        k0 = pl.multiple_of(r0 * GRID_W, GRID_W)
        bias = jnp.concatenate([bias_ref[g, delta] for g in range(GQA_GROUP)], axis=0)
        o = _softmax_pv(s + bias, v_scr[pl.ds(k0, NA_KEYS), :])
        for g in range(GQA_GROUP):
            o_ref[g, 0, pl.ds(q0, GRID_W), :] = (
                o[g * GRID_W:(g + 1) * GRID_W].astype(o_ref.dtype))

    _waves(GRID_H, prep, scores, finish, bufs)


def _attn_na(qkv, bias):
    qkv_specs, out_spec = _gqa_specs()
    return pl.pallas_call(
        _attn_na_kernel,
        grid=(BATCH, N_KV_HEADS),
        in_specs=qkv_specs + [pl.BlockSpec((GQA_GROUP, NA_DELTAS, GRID_W, NA_KEYS), lambda b, h: (h, 0, 0, 0))],
        out_specs=out_spec,
        out_shape=_ATT_SHAPE,
        scratch_shapes=[pltpu.VMEM((SEQ, 2 * HEAD_DIM), BF16)] + _wave_bufs(GQA_GROUP * GRID_W, NA_KEYS),
        compiler_params=_cparams("parallel", "parallel"),
        name="attn_na",
    )(qkv, qkv, qkv, bias)


XA_TM = 256


def _xattn_kernel(h_ref, att_ref, wout_ref, g_ref, wq_ref, kv_ref, wo_ref, gf_ref, rw_ref, rb_ref,
                  o_ref, xn_ref, aff_ref):
    att = jnp.concatenate([att_ref[hh, 0] for hh in range(N_HEADS)], axis=1)
    x = h_ref[0] + _dot(att, wout_ref[...])
    xn = _rms(x, g_ref[...]).astype(BF16)
    q = _dot(xn, wq_ref[...]).astype(BF16)
    outs = []
    ones = jnp.ones((MEM_TOKENS, XA_HEAD_DIM), BF16)
    for hh in range(XA_HEADS):
        k = kv_ref[0, :, hh * XA_HEAD_DIM:(hh + 1) * XA_HEAD_DIM]
        v = kv_ref[0, :, XA_INNER + hh * XA_HEAD_DIM:XA_INNER + (hh + 1) * XA_HEAD_DIM]
        s = _dot_nt(q[:, hh * XA_HEAD_DIM:(hh + 1) * XA_HEAD_DIM], k)
        outs.append(_softmax_pv(s, jnp.concatenate([v, ones], axis=1)).astype(BF16))
    y = x + _dot(jnp.concatenate(outs, axis=1), wo_ref[...])
    o_ref[0] = y
    yn = _rms(y, gf_ref[...]).astype(BF16)
    xn_ref[0] = yn
    logits = _dot(yn, rw_ref[...]) + rb_ref[...]
    e = jnp.exp(logits - jnp.max(logits, axis=-1, keepdims=True))
    aff_ref[0] = e / jnp.sum(e, axis=-1, keepdims=True)


def _xattn(h, att, w_out, g, wq, kv, layer, wo, g_ffn, router_w):
    full = lambda b, i: (0, 0)
    tile = lambda b, i: (b, i, 0)
    rw = jnp.pad(router_w, ((0, 0), (0, LANES - N_EXPERTS))).astype(BF16)
    rb = jnp.where(jnp.arange(LANES) < N_EXPERTS, 0.0, MASKED).astype(F32).reshape(1, LANES)
    return pl.pallas_call(
        _xattn_kernel,
        grid=(BATCH, SEQ // XA_TM),
        in_specs=[
            pl.BlockSpec((1, XA_TM, D_MODEL), lambda b, i: (b, i, 0)),
            pl.BlockSpec((N_HEADS, 1, XA_TM, HEAD_DIM), lambda b, i: (0, b, i, 0)),
            pl.BlockSpec(w_out.shape, full),
            pl.BlockSpec((1, D_MODEL), full),
            pl.BlockSpec((D_MODEL, XA_INNER), full),
            pl.BlockSpec((1, MEM_TOKENS, 2 * XA_INNER), lambda b, i: (b, 0, layer)),
            pl.BlockSpec((XA_INNER, D_MODEL), full),
            pl.BlockSpec((1, D_MODEL), full),
            pl.BlockSpec((D_MODEL, LANES), full),
            pl.BlockSpec((1, LANES), full),
        ],
        out_specs=[
            pl.BlockSpec((1, XA_TM, D_MODEL), tile),
            pl.BlockSpec((1, XA_TM, D_MODEL), tile),
            pl.BlockSpec((1, XA_TM, LANES), tile),
        ],
        out_shape=[
            jax.ShapeDtypeStruct((BATCH, SEQ, D_MODEL), F32),
            jax.ShapeDtypeStruct((BATCH, SEQ, D_MODEL), BF16),
            jax.ShapeDtypeStruct((BATCH, SEQ, LANES), F32),
        ],
        compiler_params=_cparams("parallel", "parallel"),
        name="outproj_xattn",
    )(h, att, w_out, g.reshape(1, D_MODEL).astype(F32), wq, kv, wo, g_ffn.reshape(1, D_MODEL).astype(F32), rw, rb)


PREFIX_BLOCK = 256
TOPK_MAX_ITERS = 256


TOPK_ROWS = BATCH * N_EXPERTS


def _topk_kernel(aff_ref, selpos_ref, selpos_t_ref, gate_t_ref):
    a = jnp.concatenate([jnp.transpose(aff_ref[b])[:N_EXPERTS] for b in range(BATCH)], axis=0)
    gate_t_ref[...] = a
    kf = jnp.float32(CAP)

    def count(mask):
        return jnp.sum(jnp.where(mask, 1.0, 0.0), axis=1, keepdims=True)

    def cond(c):
        it, _, _, _, done = c
        return jnp.logical_and(it < TOPK_MAX_ITERS, done == 0)

    def body(c):
        it, lo, hi, _, _ = c
        mid = 0.5 * (lo + hi)
        take = count(a >= mid) >= kf
        lo = jnp.where(take, mid, lo)
        hi = jnp.where(take, hi, mid)
        top = jnp.max(jnp.where(a < hi, a, -1.0), axis=1, keepdims=True)
        bot = jnp.min(jnp.where(a >= lo, a, 3.0), axis=1, keepdims=True)
        done = jnp.min(jnp.where(top == bot, 1, 0))
        return it + 1, lo, hi, top, done

    init = (jnp.int32(0), jnp.zeros((TOPK_ROWS, 1), F32), jnp.full((TOPK_ROWS, 1), 2.0, F32),
            jnp.zeros((TOPK_ROWS, 1), F32), jnp.int32(0))
    _, _, _, kth, _ = lax.while_loop(cond, body, init)

    gt = a > kth
    eq = a == kth
    need = kf - count(gt)
    tok = lax.broadcasted_iota(jnp.int32, (TOPK_ROWS, SEQ), 1).astype(F32)
    cut = jnp.zeros((TOPK_ROWS, 1), F32)
    step = SEQ // 2
    while step >= 1:
        cand = cut + step
        cut = jnp.where(count(eq & (tok < cand)) < need, cand, cut)
        step //= 2
    sel = jnp.where(gt, 1.0, jnp.where(eq & (tok <= cut), 1.0, 0.0))

    r = lax.broadcasted_iota(jnp.int32, (PREFIX_BLOCK, PREFIX_BLOCK), 0)
    c = lax.broadcasted_iota(jnp.int32, (PREFIX_BLOCK, PREFIX_BLOCK), 1)
    utri = jnp.where(r < c, 1.0, 0.0).astype(BF16)
    off = jnp.zeros((TOPK_ROWS, 1), F32)
    for j in range(SEQ // PREFIX_BLOCK):
        blk = sel[:, j * PREFIX_BLOCK:(j + 1) * PREFIX_BLOCK]
        pos = _dot(blk.astype(BF16), utri) + off
        selpos_t_ref[:, j * PREFIX_BLOCK:(j + 1) * PREFIX_BLOCK] = jnp.where(blk > 0.5, pos, -1.0)
        off = off + jnp.sum(blk, axis=1, keepdims=True)

    for b in range(BATCH):
        padded = jnp.concatenate([selpos_t_ref[b * N_EXPERTS:(b + 1) * N_EXPERTS, :],
                                  jnp.zeros((LANES - N_EXPERTS, SEQ), F32)], axis=0)
        selpos_ref[b] = jnp.transpose(padded)[:, :N_EXPERTS]


def _topk(aff):
    whole = lambda i: (0, 0)
    return pl.pallas_call(
        _topk_kernel,
        grid=(1,),
        in_specs=[pl.BlockSpec((BATCH, SEQ, LANES), lambda i: (0, 0, 0))],
        out_specs=[
            pl.BlockSpec((BATCH, SEQ, N_EXPERTS), lambda i: (0, 0, 0)),
            pl.BlockSpec((TOPK_ROWS, SEQ), whole),
            pl.BlockSpec((TOPK_ROWS, SEQ), whole),
        ],
        out_shape=[
            jax.ShapeDtypeStruct((BATCH, SEQ, N_EXPERTS), F32),
            jax.ShapeDtypeStruct((TOPK_ROWS, SEQ), F32),
            jax.ShapeDtypeStruct((TOPK_ROWS, SEQ), F32),
        ],
        compiler_params=_cparams("arbitrary"),
        name="expert_topk",
    )(aff)


def _gather_kernel(xn_ref, selpos_t_ref, gate_t_ref, xg_ref, gs_ref):
    e = pl.program_id(1)
    slot_of_token = selpos_t_ref[pl.ds(e, 1), :]
    gate_row = gate_t_ref[pl.ds(e, 1), :]
    slot = lax.broadcasted_iota(jnp.int32, (CAP, SEQ), 0).astype(F32)
    hit = slot == slot_of_token
    xg_ref[0, 0] = _dot(jnp.where(hit, 1.0, 0.0).astype(BF16), xn_ref[0]).astype(BF16)
    gs_ref[0, 0] = jnp.sum(jnp.where(hit, gate_row, 0.0), axis=1, keepdims=True)


def _gather(xn, selpos_t, gate_t):
    return pl.pallas_call(
        _gather_kernel,
        grid=(BATCH, N_EXPERTS),
        in_specs=[
            pl.BlockSpec((1, SEQ, D_MODEL), lambda b, e: (b, 0, 0)),
            pl.BlockSpec((N_EXPERTS, SEQ), lambda b, e: (b, 0)),
            pl.BlockSpec((N_EXPERTS, SEQ), lambda b, e: (b, 0)),
        ],
        out_specs=[
            pl.BlockSpec((1, 1, CAP, D_MODEL), lambda b, e: (e, b, 0, 0)),
            pl.BlockSpec((1, 1, CAP, 1), lambda b, e: (e, b, 0, 0)),
        ],
        out_shape=[
            jax.ShapeDtypeStruct((N_EXPERTS, BATCH, CAP, D_MODEL), BF16),
            jax.ShapeDtypeStruct((N_EXPERTS, BATCH, CAP, 1), F32),
        ],
        compiler_params=_cparams("parallel", "arbitrary"),
        name="moe_gather",
    )(xn, selpos_t, gate_t)


FF_TF = 256


def _expert_ffn_kernel(xg_ref, wg_ref, wu_ref, wd_ref, gs_ref, y_ref, acc_ref):
    f = pl.program_id(1)
    last = pl.num_programs(1) - 1

    def partial():
        x = xg_ref[0].reshape(BATCH * CAP, D_MODEL)
        gate = _dot(x, wg_ref[0, 0].astype(BF16))
        up = _dot(x, wu_ref[0, 0].astype(BF16))
        act = (gate * (1.0 / (1.0 + jnp.exp(-gate))) * up).astype(BF16)
        return _dot(act, wd_ref[0, 0].astype(BF16))

    @pl.when(f == 0)
    def _():
        acc_ref[...] = partial()

    @pl.when(jnp.logical_and(f > 0, f < last))
    def _():
        acc_ref[...] += partial()

    @pl.when(f == last)
    def _():
        y = (acc_ref[...] + partial()) * gs_ref[0].reshape(BATCH * CAP, 1)
        y_ref[0] = y.astype(BF16).reshape(BATCH, CAP, D_MODEL)


def _expert_ffn(xg, w_gate, w_up, w_down, gs, layer):
    return pl.pallas_call(
        _expert_ffn_kernel,
        grid=(N_EXPERTS, EXPERT_FF // FF_TF),
        in_specs=[
            pl.BlockSpec((1, BATCH, CAP, D_MODEL), lambda e, f: (e, 0, 0, 0)),
            pl.BlockSpec((1, 1, D_MODEL, FF_TF), lambda e, f: (layer, e, 0, f)),
            pl.BlockSpec((1, 1, D_MODEL, FF_TF), lambda e, f: (layer, e, 0, f)),
            pl.BlockSpec((1, 1, FF_TF, D_MODEL), lambda e, f: (layer, e, f, 0)),
            pl.BlockSpec((1, BATCH, CAP, 1), lambda e, f: (e, 0, 0, 0)),
        ],
        out_specs=pl.BlockSpec((1, BATCH, CAP, D_MODEL), lambda e, f: (e, 0, 0, 0)),
        out_shape=jax.ShapeDtypeStruct((N_EXPERTS, BATCH, CAP, D_MODEL), BF16),
        scratch_shapes=[pltpu.VMEM((BATCH * CAP, D_MODEL), F32)],
        compiler_params=_cparams("parallel", "arbitrary"),
        name="moe_ffn",
    )(xg, w_gate, w_up, w_down, gs)


CB_TM = 256


def _combine_kernel(h_ref, selpos_ref, y_ref, *rest):
    o_ref = rest[-1]
    sp = selpos_ref[0]
    slot = lax.broadcasted_iota(jnp.int32, (CB_TM, CAP), 1).astype(F32)
    onehot = jnp.concatenate(
        [jnp.where(sp[:, e:e + 1] == slot, 1.0, 0.0).astype(BF16) for e in range(N_EXPERTS)], axis=1)
    out = h_ref[0] + _dot(onehot, y_ref[...].reshape(N_EXPERTS * CAP, D_MODEL))
    if len(rest) == 2:
        out = _rms(out, rest[0][...])
    o_ref[0] = out


def _combine(h, selpos, y, final_g=None):
    in_specs = [
        pl.BlockSpec((1, CB_TM, D_MODEL), lambda b, i: (b, i, 0)),
        pl.BlockSpec((1, CB_TM, N_EXPERTS), lambda b, i: (b, i, 0)),
        pl.BlockSpec((N_EXPERTS, 1, CAP, D_MODEL), lambda b, i: (0, b, 0, 0)),
    ]
    args = [h, selpos, y]
    if final_g is not None:
        in_specs.append(pl.BlockSpec((1, D_MODEL), lambda b, i: (0, 0)))
        args.append(final_g.reshape(1, D_MODEL).astype(F32))
    return pl.pallas_call(
        _combine_kernel,
        grid=(BATCH, SEQ // CB_TM),
        in_specs=in_specs,
        out_specs=pl.BlockSpec((1, CB_TM, D_MODEL), lambda b, i: (b, i, 0)),
        out_shape=jax.ShapeDtypeStruct((BATCH, SEQ, D_MODEL), F32),
        compiler_params=_cparams("parallel", "parallel"),
        name="moe_combine",
    )(*args)


def _angles(pos, dim, theta):
    inv = jnp.power(jnp.float32(theta), -(jnp.arange(0, dim, 2, dtype=jnp.float32) / dim))
    ang = pos.astype(jnp.float32)[:, None] * inv[None, :]
    return jnp.cos(ang), jnp.sin(ang)


def _rot_tables(groups, width=HEAD_DIM):
    cos_parts, sneg_parts, spos_parts = [], [], []
    used = 0
    for c, s in groups:
        z = jnp.zeros_like(s)
        cos_parts += [c, c]
        sneg_parts += [-s, z]
        spos_parts += [z, s]
        used += 2 * c.shape[1]
    rest = width - used
    if rest:
        cos_parts.append(jnp.ones((SEQ, rest), F32))
        sneg_parts.append(jnp.zeros((SEQ, rest), F32))
        spos_parts.append(jnp.zeros((SEQ, rest), F32))
    return (jnp.concatenate(cos_parts, axis=1), jnp.concatenate(sneg_parts, axis=1),
            jnp.concatenate(spos_parts, axis=1))


def _mixer_gqa_in(h2, g, w_in, scale_q):
    col = None
    if scale_q:
        col = jnp.concatenate([jnp.full((N_HEADS * HEAD_DIM,), HEAD_DIM ** -0.5 * LOG2E, F32),
                               jnp.ones((2 * N_KV_HEADS * HEAD_DIM,), F32)])
    qkv = _norm_matmul(h2, g, w_in, tm=1024, tn=1024, head_w=HEAD_DIM, col_scale=col)
    return qkv.reshape(N_HEADS + 2 * N_KV_HEADS, BATCH, SEQ, HEAD_DIM)


def _mla_weights(w_in, w_uq):
    w_in = jnp.pad(w_in, ((0, 0), (0, MLA_LAT_PAD - w_in.shape[-1]))).astype(BF16)
    w_uq = jnp.pad((w_uq * (MLA_QK_DIM ** -0.5 * LOG2E)).reshape(MLA_Q_RANK, N_HEADS, MLA_QK_DIM),
                   ((0, 0), (0, 0), (0, MLA_Q_PAD - MLA_QK_DIM))).reshape(MLA_Q_RANK, N_HEADS * MLA_Q_PAD)
    return w_in, w_uq.astype(BF16)


def kernel(x, mem, norm_mix_g, norm_xa_g, norm_ffn_g, a_w_in, a_sink, a_w_out, b_w_in, b_q_norm_g, b_k_norm_g, b_w_out, c_w_in, c_q_lat_norm_g, c_kv_lat_norm_g, c_w_uq, c_w_ukv, c_w_out, d_w_in, d_rpb, d_w_out, mem_norm_g, xa_wq, xa_wkv, xa_wo, router_w, moe_w_gate, moe_w_up, moe_w_down, final_norm_g):
    t = BATCH * SEQ
    pos = jnp.arange(SEQ)
    tab_a = _rot_tables([_angles(pos, ROT_DIM, ROPE_THETA)])
    tab_b = _rot_tables([_angles(pos // GRID_W, HEAD_DIM // 2, AXIAL_THETA),
                         _angles(pos % GRID_W, HEAD_DIM // 2, AXIAL_THETA)])
    tab_c = _rot_tables([_angles(pos, MLA_ROPE_DIM, ROPE_THETA)])
    wkv_all = jnp.transpose(xa_wkv, (1, 0, 2)).reshape(D_MODEL, DEPTH * 2 * XA_INNER).astype(BF16)
    kv_all = _norm_matmul(mem.reshape(BATCH * MEM_TOKENS, D_MODEL), mem_norm_g, wkv_all, tm=BATCH * MEM_TOKENS,
                          tn=2 * XA_INNER).reshape(BATCH, MEM_TOKENS, DEPTH * 2 * XA_INNER)

    h = x
    for i in range(DEPTH):
        m, j = i % N_MIXERS, i // N_MIXERS
        h2 = h.reshape(t, D_MODEL)
        if m == 0:
            qkv = _mixer_gqa_in(h2, norm_mix_g[i], a_w_in[j], True)
            att = _attn_window(qkv, a_sink[j] * LOG2E, tab_a)
            w_out = a_w_out[j]
        elif m == 1:
            qkv = _mixer_gqa_in(h2, norm_mix_g[i], b_w_in[j], False)
            att = _attn_axial(qkv, b_q_norm_g[j] * (HEAD_DIM ** -0.5 * LOG2E), b_k_norm_g[j], tab_b)
            w_out = b_w_out[j]
        elif m == 2:
            w_in, w_uq = _mla_weights(c_w_in[j], c_w_uq[j])
            lat = _norm_matmul(h2, norm_mix_g[i], w_in, tm=1024, tn=MLA_LAT_PAD)
            qfull = _norm_matmul(lat, c_q_lat_norm_g[j], w_uq, col_block=0, tm=2048, tn=1024, head_w=MLA_Q_PAD)
            kvfull = _norm_matmul(lat, c_kv_lat_norm_g[j], c_w_ukv[j].astype(BF16), col_block=1, tm=2048, tn=1024,
                                  head_w=MLA_NOPE_DIM)
            k_rope = lat[:, MLA_Q_RANK + MLA_KV_RANK:].reshape(BATCH, SEQ, HEAD_DIM)
            att = _attn_mla(qfull.reshape(N_HEADS, BATCH, SEQ, MLA_Q_PAD),
                            kvfull.reshape(2 * N_HEADS, BATCH, SEQ, MLA_NOPE_DIM), k_rope, tab_c)
            w_out = c_w_out[j]
        else:
            qkv = _mixer_gqa_in(h2, norm_mix_g[i], d_w_in[j], True)
            att = _attn_na(qkv, _na_bias(d_rpb[j] * LOG2E))
            w_out = d_w_out[j]

        h, xn, aff = _xattn(h, att, w_out.astype(BF16), norm_xa_g[i],
                            (xa_wq[i] * (XA_HEAD_DIM ** -0.5 * LOG2E)).astype(BF16),
                            kv_all, i, xa_wo[i].astype(BF16), norm_ffn_g[i], router_w[i])

        selpos, selpos_t, gate_t = _topk(aff)
        xg, gs = _gather(xn, selpos_t, gate_t)
        y = _expert_ffn(xg, moe_w_gate, moe_w_up, moe_w_down, gs, i)
        h = _combine(h, selpos, y, final_norm_g if i == DEPTH - 1 else None)

    return h
```

```python
import jax
import jax.numpy as jnp
from jax import lax
from jax.experimental import pallas as pl
from jax.experimental.pallas import tpu as pltpu

F32 = jnp.float32
BF16 = jnp.bfloat16

D_MODEL = 2048
BATCH = 4
SEQ = 2048
DEPTH = 4
N_MIXERS = 4
HEAD_DIM = 128
N_HEADS = 16
N_KV_HEADS = 4
GQA_GROUP = N_HEADS // N_KV_HEADS
GQA_IN = (N_HEADS + 2 * N_KV_HEADS) * HEAD_DIM
Q_BLOCK = 128
WINDOW = 128
ROPE_THETA = 500000.0
ROT_DIM = HEAD_DIM // 4
AXIAL_THETA = 10000.0
GRID_W = 64
GRID_H = SEQ // GRID_W
MLA_Q_RANK = 512
MLA_KV_RANK = 512
MLA_NOPE_DIM = 128
MLA_ROPE_DIM = 64
MLA_V_DIM = 128
MLA_QK_DIM = MLA_NOPE_DIM + MLA_ROPE_DIM
MLA_Q_PAD = 256
MLA_LAT_PAD = 1152
NA_ROWS = 8
NA_COLS = 16
MEM_TOKENS = 256
XA_HEADS = 4
XA_HEAD_DIM = 128
XA_INNER = XA_HEADS * XA_HEAD_DIM
N_EXPERTS = 16
EC_CAPACITY = 2
EXPERT_FF = 1024
CAP = EC_CAPACITY * SEQ // N_EXPERTS
EPS = 1e-6
MASKED = -1e30
LOG2E = 1.4426950408889634
LANES = 128

VMEM_LIMIT_BYTES = 56 * 1024 * 1024


def _cparams(*sem):
    return pltpu.CompilerParams(dimension_semantics=sem, vmem_limit_bytes=VMEM_LIMIT_BYTES)


def _dot_nt(a, b):
    return lax.dot_general(a, b, (((1,), (1,)), ((), ())), preferred_element_type=F32)


def _dot(a, b):
    return jnp.dot(a, b, preferred_element_type=F32)


def _rms(x, g):
    ms = jnp.mean(x * x, axis=-1, keepdims=True)
    return x * lax.rsqrt(ms + EPS) * g


def _rot(x, cos, sneg, spos, shift):
    n = x.shape[-1]
    return x * cos + pltpu.roll(x, n - shift, 1) * sneg + pltpu.roll(x, shift, 1) * spos


PV_CHUNK = 512


def _softmax_pv(s, v_ones, sink=None):
    dv = v_ones.shape[1] // 2
    keys = s.shape[1]
    m = jnp.max(s, axis=-1, keepdims=True)
    if sink is not None:
        m = jnp.maximum(m, sink)
    r = None
    for j in range(0, keys, PV_CHUNK):
        part = _dot(jnp.exp2(s[:, j:j + PV_CHUNK] - m).astype(BF16), v_ones[j:j + PV_CHUNK])
        r = part if r is None else r + part
    den = r[:, dv:]
    if sink is not None:
        den = den + jnp.exp2(sink - m)
    return r[:, :dv] * (1.0 / den)


def _fill_v_ones(v_scr, v):
    dv = v.shape[1]
    v_scr[:, :dv] = v
    v_scr[:, dv:] = jnp.ones_like(v)


def _pairs(n_blocks, prep, scores, finish, bufs, peel):
    q_a, q_b, s_a, s_b = bufs
    last = jnp.int32(n_blocks - 1)
    q_b[...] = prep(0)
    s_b[...] = scores(0, q_b[...])
    q_a[...] = prep(1)

    def pair(i, carry):
        n = 2 * i
        s_a[...] = scores(n + 1, q_a[...])
        q_b[...] = prep(jnp.minimum(n + 2, last))
        finish(n, s_b[...])
        s_b[...] = scores(jnp.minimum(n + 2, last), q_b[...])
        q_a[...] = prep(jnp.minimum(n + 3, last))
        finish(n + 1, s_a[...])
        return carry

    if not peel:
        lax.fori_loop(0, n_blocks // 2, pair, 0)
        return
    lax.fori_loop(0, n_blocks // 2 - 1, pair, 0)
    s_a[...] = scores(last, q_a[...])
    finish(last - 1, s_b[...])
    finish(last, s_a[...])


WAVE = 4


def _waves(n_blocks, prep, scores, finish, bufs):
    for k in range(WAVE):
        bufs[k][...] = scores(k, prep(k))

    def wave(i, carry):
        n = i * WAVE
        for k in range(WAVE):
            finish(n + k, bufs[k][...])
        for k in range(WAVE):
            bufs[k][...] = scores(n + WAVE + k, prep(n + WAVE + k))
        return carry

    lax.fori_loop(0, n_blocks // WAVE - 1, wave, 0)
    for k in range(WAVE):
        finish(jnp.int32(n_blocks - WAVE + k), bufs[k][...])


def _wave_bufs(rows, keys):
    return [pltpu.VMEM((rows, keys), F32) for _ in range(WAVE)]


def _pipeline_bufs(rows, dq, keys):
    return [pltpu.VMEM((rows, dq), BF16), pltpu.VMEM((rows, dq), BF16),
            pltpu.VMEM((rows, keys), F32), pltpu.VMEM((rows, keys), F32)]


def _norm_matmul_kernel(x_ref, g_ref, w_ref, *rest):
    o_ref, xn_ref = rest[-2:]

    @pl.when(pl.program_id(1) == 0)
    def _():
        xn_ref[...] = _rms(x_ref[...].astype(F32), g_ref[...]).astype(BF16)

    res = _dot(xn_ref[...], w_ref[...].astype(BF16))
    if len(rest) == 3:
        res = res * rest[0][...]
    if len(o_ref.shape) == 3:
        head_w = o_ref.shape[2]
        for hh in range(o_ref.shape[0]):
            o_ref[hh] = res[:, hh * head_w:(hh + 1) * head_w].astype(o_ref.dtype)
    else:
        o_ref[...] = res.astype(o_ref.dtype)


def _norm_matmul(x, g, w, *, col_block=0, tm, tn, head_w=None, col_scale=None):
    t = x.shape[0]
    k, n = w.shape
    in_specs = [
        pl.BlockSpec((tm, k), lambda i, j: (i, col_block)),
        pl.BlockSpec((1, k), lambda i, j: (0, 0)),
        pl.BlockSpec((k, tn), lambda i, j: (0, j)),
    ]
    args = [x, g.reshape(1, k).astype(F32), w]
    if col_scale is not None:
        in_specs.append(pl.BlockSpec((1, tn), lambda i, j: (0, j)))
        args.append(col_scale.reshape(1, n).astype(F32))
    if head_w is None:
        out_spec = pl.BlockSpec((tm, tn), lambda i, j: (i, j))
        out_shape = jax.ShapeDtypeStruct((t, n), BF16)
    else:
        out_spec = pl.BlockSpec((tn // head_w, tm, head_w), lambda i, j: (j, i, 0))
        out_shape = jax.ShapeDtypeStruct((n // head_w, t, head_w), BF16)
    return pl.pallas_call(
        _norm_matmul_kernel,
        grid=(t // tm, n // tn),
        in_specs=in_specs,
        out_specs=out_spec,
        out_shape=out_shape,
        scratch_shapes=[pltpu.VMEM((tm, k), BF16)],
        compiler_params=_cparams("parallel", "arbitrary"),
        name="norm_matmul",
    )(*args)


def _attn_window_kernel(sink_ref, q_ref, k_ref, v_ref, cos_ref, sneg_ref, spos_ref, o_ref, k_scr, v_scr, *bufs):
    kvh = pl.program_id(1)
    half = ROT_DIM // 2
    nb = SEQ // Q_BLOCK
    rows, keys = GQA_GROUP * Q_BLOCK, 3 * Q_BLOCK
    k_scr[...] = _rot(k_ref[0, 0].astype(F32), cos_ref[...], sneg_ref[...], spos_ref[...], half).astype(BF16)
    _fill_v_ones(v_scr, v_ref[0, 0])
    r = lax.broadcasted_iota(jnp.int32, (rows, keys), 0)
    c = lax.broadcasted_iota(jnp.int32, (rows, keys), 1)
    base = (r & (Q_BLOCK - 1)) - c
    sink = jnp.concatenate(
        [jnp.full((Q_BLOCK, 1), sink_ref[kvh * GQA_GROUP + g], F32) for g in range(GQA_GROUP)], axis=0)

    def window_start(n):
        return pl.multiple_of(jnp.clip(n - 1, 0, nb - 3) * Q_BLOCK, Q_BLOCK)

    def prep(n):
        q0 = pl.multiple_of(n * Q_BLOCK, Q_BLOCK)
        cos = cos_ref[pl.ds(q0, Q_BLOCK), :]
        sneg = sneg_ref[pl.ds(q0, Q_BLOCK), :]
        spos = spos_ref[pl.ds(q0, Q_BLOCK), :]
        return jnp.concatenate(
            [_rot(q_ref[g, 0, pl.ds(q0, Q_BLOCK), :].astype(F32),
                  cos, sneg, spos, half).astype(BF16) for g in range(GQA_GROUP)], axis=0)

    def scores(n, q4):
        return _dot_nt(q4, k_scr[pl.ds(window_start(n), keys), :])

    def finish(n, s):
        q0 = pl.multiple_of(n * Q_BLOCK, Q_BLOCK)
        k0 = window_start(n)
        s = jnp.where(jnp.abs(base + (q0 - k0)) <= WINDOW, s, MASKED)
        o = _softmax_pv(s, v_scr[pl.ds(k0, keys), :], sink)
        for g in range(GQA_GROUP):
            o_ref[g, 0, pl.ds(q0, Q_BLOCK), :] = (
                o[g * Q_BLOCK:(g + 1) * Q_BLOCK].astype(o_ref.dtype))

    _waves(nb, prep, scores, finish, bufs)


def _gqa_specs():
    head = (1, 1, SEQ, HEAD_DIM)
    group = (GQA_GROUP, 1, SEQ, HEAD_DIM)
    return [
        pl.BlockSpec(group, lambda b, h: (h, b, 0, 0)),
        pl.BlockSpec(head, lambda b, h: (N_HEADS + h, b, 0, 0)),
        pl.BlockSpec(head, lambda b, h: (N_HEADS + N_KV_HEADS + h, b, 0, 0)),
    ], pl.BlockSpec(group, lambda b, h: (h, b, 0, 0))


_ATT_SHAPE = jax.ShapeDtypeStruct((N_HEADS, BATCH, SEQ, HEAD_DIM), BF16)


def _attn_window(qkv, sink, tables):
    cos, sneg, spos = tables
    qkv_specs, out_spec = _gqa_specs()
    tab = pl.BlockSpec((SEQ, HEAD_DIM), lambda b, h: (0, 0))
    return pl.pallas_call(
        _attn_window_kernel,
        grid=(BATCH, N_KV_HEADS),
        in_specs=[pl.BlockSpec(memory_space=pltpu.SMEM)] + qkv_specs + [tab, tab, tab],
        out_specs=out_spec,
        out_shape=_ATT_SHAPE,
        scratch_shapes=[pltpu.VMEM((SEQ, HEAD_DIM), BF16), pltpu.VMEM((SEQ, 2 * HEAD_DIM), BF16)]
        + _wave_bufs(GQA_GROUP * Q_BLOCK, 3 * Q_BLOCK),
        compiler_params=_cparams("parallel", "parallel"),
        name="attn_window",
    )(sink.astype(F32), qkv, qkv, qkv, cos, sneg, spos)


AX_TQ = 128


def _attn_axial_kernel(q_ref, k_ref, v_ref, qg_ref, kg_ref, cos_ref, sneg_ref, spos_ref, o_ref, k_scr, v_scr, *bufs):
    quarter = HEAD_DIM // 4
    kn = _rms(k_ref[0, 0].astype(F32), kg_ref[...])
    k_scr[...] = _rot(kn, cos_ref[...], sneg_ref[...], spos_ref[...], quarter).astype(BF16)
    _fill_v_ones(v_scr, v_ref[0, 0])

    def prep(n):
        q0 = pl.multiple_of(n * AX_TQ, AX_TQ)
        cos = cos_ref[pl.ds(q0, AX_TQ), :]
        sneg = sneg_ref[pl.ds(q0, AX_TQ), :]
        spos = spos_ref[pl.ds(q0, AX_TQ), :]
        return jnp.concatenate(
            [_rot(_rms(q_ref[g, 0, pl.ds(q0, AX_TQ), :].astype(F32), qg_ref[...]),
                  cos, sneg, spos, quarter).astype(BF16) for g in range(GQA_GROUP)], axis=0)

    def scores(n, q4):
        return _dot_nt(q4, k_scr[...])

    def finish(n, s):
        q0 = pl.multiple_of(n * AX_TQ, AX_TQ)
        o = _softmax_pv(s, v_scr[...])
        for g in range(GQA_GROUP):
            o_ref[g, 0, pl.ds(q0, AX_TQ), :] = (
                o[g * AX_TQ:(g + 1) * AX_TQ].astype(o_ref.dtype))

    _pairs(SEQ // AX_TQ, prep, scores, finish, bufs, peel=False)


def _attn_axial(qkv, qg, kg, tables):
    cos, sneg, spos = tables
    qkv_specs, out_spec = _gqa_specs()
    tab = pl.BlockSpec((SEQ, HEAD_DIM), lambda b, h: (0, 0))
    gain = pl.BlockSpec((1, HEAD_DIM), lambda b, h: (0, 0))
    return pl.pallas_call(
        _attn_axial_kernel,
        grid=(BATCH, N_KV_HEADS),
        in_specs=qkv_specs + [gain, gain, tab, tab, tab],
        out_specs=out_spec,
        out_shape=_ATT_SHAPE,
        scratch_shapes=[pltpu.VMEM((SEQ, HEAD_DIM), BF16), pltpu.VMEM((SEQ, 2 * HEAD_DIM), BF16)]
        + _pipeline_bufs(GQA_GROUP * AX_TQ, HEAD_DIM, SEQ),
        compiler_params=_cparams("parallel", "parallel"),
        name="attn_axial",
    )(qkv, qkv, qkv, qg.reshape(1, HEAD_DIM).astype(F32), kg.reshape(1, HEAD_DIM).astype(F32), cos, sneg, spos)


MLA_TQ = 512


def _attn_mla_kernel(q_ref, kn_ref, v_ref, kr_ref, cos_ref, sneg_ref, spos_ref, o_ref, k_scr, v_scr, *bufs):
    half = MLA_ROPE_DIM // 2
    k_scr[:, :MLA_NOPE_DIM] = kn_ref[0, 0]

    @pl.when(pl.program_id(1) == 0)
    def _():
        k_scr[:, MLA_NOPE_DIM:] = _rot(kr_ref[0].astype(F32), cos_ref[...], sneg_ref[...], spos_ref[...],
                                       half).astype(BF16)

    _fill_v_ones(v_scr, v_ref[0, 0])

    def prep(n):
        q0 = pl.multiple_of(n * MLA_TQ, MLA_TQ)
        cos = cos_ref[pl.ds(q0, MLA_TQ), :]
        sneg = sneg_ref[pl.ds(q0, MLA_TQ), :]
        spos = spos_ref[pl.ds(q0, MLA_TQ), :]
        qn = q_ref[0, 0, pl.ds(q0, MLA_TQ), :MLA_NOPE_DIM]
        qr = _rot(q_ref[0, 0, pl.ds(q0, MLA_TQ), MLA_NOPE_DIM:].astype(F32), cos, sneg, spos, half).astype(BF16)
        return jnp.concatenate([qn, qr], axis=1)

    def scores(n, q):
        return _dot_nt(q, k_scr[...])

    def finish(n, s):
        q0 = pl.multiple_of(n * MLA_TQ, MLA_TQ)
        o_ref[0, 0, pl.ds(q0, MLA_TQ), :] = _softmax_pv(s, v_scr[...]).astype(o_ref.dtype)

    _pairs(SEQ // MLA_TQ, prep, scores, finish, bufs, peel=True)


def _attn_mla(qfull, kvfull, k_rope, tables):
    cos, sneg, spos = tables
    tab = pl.BlockSpec((SEQ, HEAD_DIM), lambda b, h: (0, 0))
    return pl.pallas_call(
        _attn_mla_kernel,
        grid=(BATCH, N_HEADS),
        in_specs=[
            pl.BlockSpec((1, 1, SEQ, MLA_Q_PAD), lambda b, h: (h, b, 0, 0)),
            pl.BlockSpec((1, 1, SEQ, MLA_NOPE_DIM), lambda b, h: (2 * h, b, 0, 0)),
            pl.BlockSpec((1, 1, SEQ, MLA_V_DIM), lambda b, h: (2 * h + 1, b, 0, 0)),
            pl.BlockSpec((1, SEQ, HEAD_DIM), lambda b, h: (b, 0, 0)),
            tab, tab, tab,
        ],
        out_specs=pl.BlockSpec((1, 1, SEQ, MLA_V_DIM), lambda b, h: (h, b, 0, 0)),
        out_shape=_ATT_SHAPE,
        scratch_shapes=[pltpu.VMEM((SEQ, MLA_Q_PAD), BF16), pltpu.VMEM((SEQ, 2 * MLA_V_DIM), BF16)]
        + _pipeline_bufs(MLA_TQ, MLA_Q_PAD, SEQ),
        compiler_params=_cparams("parallel", "arbitrary"),
        name="attn_mla",
    )(qfull, kvfull, kvfull, k_rope, cos, sneg, spos)


NA_DELTAS = NA_ROWS
NA_KEYS = NA_ROWS * GRID_W
NA_DR = 2 * NA_ROWS - 1
NA_DC = 2 * NA_COLS - 1


def _na_bias_kernel(rpb_ref, o_ref):
    h = pl.program_id(0)
    shape = (GRID_W, 2 * GRID_W)
    qc = lax.broadcasted_iota(jnp.int32, shape, 0)
    lane = lax.broadcasted_iota(jnp.int32, shape, 1)
    kc = jnp.where(lane >= GRID_W, lane - GRID_W, lane)
    dc = kc - qc + (NA_COLS - 1)
    c_start = jnp.clip(qc - NA_COLS // 2, 0, GRID_W - NA_COLS)
    in_win = (kc >= c_start) & (kc < c_start + NA_COLS)
    low = lane < GRID_W
    tiles = [jnp.zeros(shape, F32) for _ in range(NA_DR)]
    for d in range(NA_DC):
        hit = dc == d
        for dr in range(NA_DR):
            tiles[dr] = jnp.where(hit, rpb_ref[(h * NA_DR + dr) * NA_DC + d], tiles[dr])
    for delta in range(NA_DELTAS):
        for j in range(NA_ROWS // 2):
            t = jnp.where(low, tiles[delta + 2 * j], tiles[delta + 2 * j + 1])
            o_ref[0, delta, :, j * 2 * GRID_W:(j + 1) * 2 * GRID_W] = jnp.where(in_win, t, MASKED)


def _na_bias(rpb):
    return pl.pallas_call(
        _na_bias_kernel,
        grid=(N_HEADS,),
        in_specs=[pl.BlockSpec(memory_space=pltpu.SMEM)],
        out_specs=pl.BlockSpec((1, NA_DELTAS, GRID_W, NA_KEYS), lambda h: (h, 0, 0, 0)),
        out_shape=jax.ShapeDtypeStruct((N_HEADS, NA_DELTAS, GRID_W, NA_KEYS), F32),
        compiler_params=_cparams("parallel"),
        name="na_bias",
    )(rpb.astype(F32).reshape(-1))


def _attn_na_kernel(q_ref, k_ref, v_ref, bias_ref, o_ref, v_scr, *bufs):
    _fill_v_ones(v_scr, v_ref[0, 0])

    def first_key_row(r):
        return jnp.clip(r - NA_ROWS // 2, 0, GRID_H - NA_ROWS)

    def prep(r):
        q0 = pl.multiple_of(r * GRID_W, GRID_W)
        return jnp.concatenate([q_ref[g, 0, pl.ds(q0, GRID_W), :] for g in range(GQA_GROUP)], axis=0)

    def scores(r, q4):
        k0 = pl.multiple_of(first_key_row(r) * GRID_W, GRID_W)
        return _dot_nt(q4, k_ref[0, 0, pl.ds(k0, NA_KEYS), :])

    def finish(r, s):
        r0 = first_key_row(r)
        delta = r0 - r + (NA_ROWS - 1)
        q0 = pl.multiple_of(r * GRID_W, GRID_W)
        k0 = pl.multiple_of(r0 * GRID_W, GRID_W)
        bias = jnp.concatenate([bias_ref[g, delta] for g in range(GQA_GROUP)], axis=0)
        o = _softmax_pv(s + bias, v_scr[pl.ds(k0, NA_KEYS), :])
        for g in range(GQA_GROUP):
            o_ref[g, 0, pl.ds(q0, GRID_W), :] = (
                o[g * GRID_W:(g + 1) * GRID_W].astype(o_ref.dtype))

    _waves(GRID_H, prep, scores, finish, bufs)


def _attn_na(qkv, bias):
    qkv_specs, out_spec = _gqa_specs()
    return pl.pallas_call(
        _attn_na_kernel,
        grid=(BATCH, N_KV_HEADS),
        in_specs=qkv_specs + [pl.BlockSpec((GQA_GROUP, NA_DELTAS, GRID_W, NA_KEYS), lambda b, h: (h, 0, 0, 0))],
        out_specs=out_spec,
        out_shape=_ATT_SHAPE,
        scratch_shapes=[pltpu.VMEM((SEQ, 2 * HEAD_DIM), BF16)] + _wave_bufs(GQA_GROUP * GRID_W, NA_KEYS),
        compiler_params=_cparams("parallel", "parallel"),
        name="attn_na",
    )(qkv, qkv, qkv, bias)


XA_TM = 256


def _xattn_kernel(h_ref, att_ref, wout_ref, g_ref, wq_ref, kv_ref, wo_ref, gf_ref, rw_ref, rb_ref,
                  o_ref, xn_ref, aff_ref):
    att = jnp.concatenate([att_ref[hh, 0] for hh in range(N_HEADS)], axis=1)
    x = h_ref[0] + _dot(att, wout_ref[...])
    xn = _rms(x, g_ref[...]).astype(BF16)
    q = _dot(xn, wq_ref[...]).astype(BF16)
    outs = []
    ones = jnp.ones((MEM_TOKENS, XA_HEAD_DIM), BF16)
    for hh in range(XA_HEADS):
        k = kv_ref[0, :, hh * XA_HEAD_DIM:(hh + 1) * XA_HEAD_DIM]
        v = kv_ref[0, :, XA_INNER + hh * XA_HEAD_DIM:XA_INNER + (hh + 1) * XA_HEAD_DIM]
        s = _dot_nt(q[:, hh * XA_HEAD_DIM:(hh + 1) * XA_HEAD_DIM], k)
        outs.append(_softmax_pv(s, jnp.concatenate([v, ones], axis=1)).astype(BF16))
    y = x + _dot(jnp.concatenate(outs, axis=1), wo_ref[...])
    o_ref[0] = y
    yn = _rms(y, gf_ref[...]).astype(BF16)
    xn_ref[0] = yn
    logits = _dot(yn, rw_ref[...]) + rb_ref[...]
    e = jnp.exp(logits - jnp.max(logits, axis=-1, keepdims=True))
    aff_ref[0] = e / jnp.sum(e, axis=-1, keepdims=True)


def _xattn(h, att, w_out, g, wq, kv, layer, wo, g_ffn, router_w):
    full = lambda b, i: (0, 0)
    tile = lambda b, i: (b, i, 0)
    rw = jnp.pad(router_w, ((0, 0), (0, LANES - N_EXPERTS))).astype(BF16)
    rb = jnp.where(jnp.arange(LANES) < N_EXPERTS, 0.0, MASKED).astype(F32).reshape(1, LANES)
    return pl.pallas_call(
        _xattn_kernel,
        grid=(BATCH, SEQ // XA_TM),
        in_specs=[
            pl.BlockSpec((1, XA_TM, D_MODEL), lambda b, i: (b, i, 0)),
            pl.BlockSpec((N_HEADS, 1, XA_TM, HEAD_DIM), lambda b, i: (0, b, i, 0)),
            pl.BlockSpec(w_out.shape, full),
            pl.BlockSpec((1, D_MODEL), full),
            pl.BlockSpec((D_MODEL, XA_INNER), full),
            pl.BlockSpec((1, MEM_TOKENS, 2 * XA_INNER), lambda b, i: (b, 0, layer)),
            pl.BlockSpec((XA_INNER, D_MODEL), full),
            pl.BlockSpec((1, D_MODEL), full),
            pl.BlockSpec((D_MODEL, LANES), full),
            pl.BlockSpec((1, LANES), full),
        ],
        out_specs=[
            pl.BlockSpec((1, XA_TM, D_MODEL), tile),
            pl.BlockSpec((1, XA_TM, D_MODEL), tile),
            pl.BlockSpec((1, XA_TM, LANES), tile),
        ],
        out_shape=[
            jax.ShapeDtypeStruct((BATCH, SEQ, D_MODEL), F32),
            jax.ShapeDtypeStruct((BATCH, SEQ, D_MODEL), BF16),
            jax.ShapeDtypeStruct((BATCH, SEQ, LANES), F32),
        ],
        compiler_params=_cparams("parallel", "parallel"),
        name="outproj_xattn",
    )(h, att, w_out, g.reshape(1, D_MODEL).astype(F32), wq, kv, wo, g_ffn.reshape(1, D_MODEL).astype(F32), rw, rb)


PREFIX_BLOCK = 256
TOPK_MAX_ITERS = 256


TOPK_ROWS = BATCH * N_EXPERTS


def _topk_kernel(aff_ref, selpos_ref, selpos_t_ref, gate_t_ref):
    a = jnp.concatenate([jnp.transpose(aff_ref[b])[:N_EXPERTS] for b in range(BATCH)], axis=0)
    gate_t_ref[...] = a
    kf = jnp.float32(CAP)

    def count(mask):
        return jnp.sum(jnp.where(mask, 1.0, 0.0), axis=1, keepdims=True)

    def cond(c):
        it, _, _, _, done = c
        return jnp.logical_and(it < TOPK_MAX_ITERS, done == 0)

    def body(c):
        it, lo, hi, _, _ = c
        mid = 0.5 * (lo + hi)
        take = count(a >= mid) >= kf
        lo = jnp.where(take, mid, lo)
        hi = jnp.where(take, hi, mid)
        top = jnp.max(jnp.where(a < hi, a, -1.0), axis=1, keepdims=True)
        bot = jnp.min(jnp.where(a >= lo, a, 3.0), axis=1, keepdims=True)
        done = jnp.min(jnp.where(top == bot, 1, 0))
        return it + 1, lo, hi, top, done

    init = (jnp.int32(0), jnp.zeros((TOPK_ROWS, 1), F32), jnp.full((TOPK_ROWS, 1), 2.0, F32),
            jnp.zeros((TOPK_ROWS, 1), F32), jnp.int32(0))
    _, _, _, kth, _ = lax.while_loop(cond, body, init)

    gt = a > kth
    eq = a == kth
    need = kf - count(gt)
    tok = lax.broadcasted_iota(jnp.int32, (TOPK_ROWS, SEQ), 1).astype(F32)
    cut = jnp.zeros((TOPK_ROWS, 1), F32)
    step = SEQ // 2
    while step >= 1:
        cand = cut + step
        cut = jnp.where(count(eq & (tok < cand)) < need, cand, cut)
        step //= 2
    sel = jnp.where(gt, 1.0, jnp.where(eq & (tok <= cut), 1.0, 0.0))

    r = lax.broadcasted_iota(jnp.int32, (PREFIX_BLOCK, PREFIX_BLOCK), 0)
    c = lax.broadcasted_iota(jnp.int32, (PREFIX_BLOCK, PREFIX_BLOCK), 1)
    utri = jnp.where(r < c, 1.0, 0.0).astype(BF16)
    off = jnp.zeros((TOPK_ROWS, 1), F32)
    for j in range(SEQ // PREFIX_BLOCK):
        blk = sel[:, j * PREFIX_BLOCK:(j + 1) * PREFIX_BLOCK]
        pos = _dot(blk.astype(BF16), utri) + off
        selpos_t_ref[:, j * PREFIX_BLOCK:(j + 1) * PREFIX_BLOCK] = jnp.where(blk > 0.5, pos, -1.0)
        off = off + jnp.sum(blk, axis=1, keepdims=True)

    for b in range(BATCH):
        padded = jnp.concatenate([selpos_t_ref[b * N_EXPERTS:(b + 1) * N_EXPERTS, :],
                                  jnp.zeros((LANES - N_EXPERTS, SEQ), F32)], axis=0)
        selpos_ref[b] = jnp.transpose(padded)[:, :N_EXPERTS]


def _topk(aff):
    whole = lambda i: (0, 0)
    return pl.pallas_call(
        _topk_kernel,
        grid=(1,),
        in_specs=[pl.BlockSpec((BATCH, SEQ, LANES), lambda i: (0, 0, 0))],
        out_specs=[
            pl.BlockSpec((BATCH, SEQ, N_EXPERTS), lambda i: (0, 0, 0)),
            pl.BlockSpec((TOPK_ROWS, SEQ), whole),
            pl.BlockSpec((TOPK_ROWS, SEQ), whole),
        ],
        out_shape=[
            jax.ShapeDtypeStruct((BATCH, SEQ, N_EXPERTS), F32),
            jax.ShapeDtypeStruct((TOPK_ROWS, SEQ), F32),
            jax.ShapeDtypeStruct((TOPK_ROWS, SEQ), F32),
        ],
        compiler_params=_cparams("arbitrary"),
        name="expert_topk",
    )(aff)


GATHER_EXPERTS = 4


def _gather_kernel(xn_ref, selpos_t_ref, gate_t_ref, xg_ref, gs_ref):
    e0 = pl.multiple_of(pl.program_id(1) * GATHER_EXPERTS, GATHER_EXPERTS)
    slot = lax.broadcasted_iota(jnp.int32, (CAP, SEQ), 0).astype(F32)
    hits = []
    for k in range(GATHER_EXPERTS):
        hit = slot == selpos_t_ref[pl.ds(e0 + k, 1), :]
        hits.append(jnp.where(hit, 1.0, 0.0).astype(BF16))
        gs_ref[k, 0] = jnp.sum(jnp.where(hit, gate_t_ref[pl.ds(e0 + k, 1), :], 0.0), axis=1, keepdims=True)
    xg = _dot(jnp.concatenate(hits, axis=0), xn_ref[0]).astype(BF16)
    for k in range(GATHER_EXPERTS):
        xg_ref[k, 0] = xg[k * CAP:(k + 1) * CAP]


def _gather(xn, selpos_t, gate_t):
    return pl.pallas_call(
        _gather_kernel,
        grid=(BATCH, N_EXPERTS // GATHER_EXPERTS),
        in_specs=[
            pl.BlockSpec((1, SEQ, D_MODEL), lambda b, e: (b, 0, 0)),
            pl.BlockSpec((N_EXPERTS, SEQ), lambda b, e: (b, 0)),
            pl.BlockSpec((N_EXPERTS, SEQ), lambda b, e: (b, 0)),
        ],
        out_specs=[
            pl.BlockSpec((GATHER_EXPERTS, 1, CAP, D_MODEL), lambda b, e: (e, b, 0, 0)),
            pl.BlockSpec((GATHER_EXPERTS, 1, CAP, 1), lambda b, e: (e, b, 0, 0)),
        ],
        out_shape=[
            jax.ShapeDtypeStruct((N_EXPERTS, BATCH, CAP, D_MODEL), BF16),
            jax.ShapeDtypeStruct((N_EXPERTS, BATCH, CAP, 1), F32),
        ],
        compiler_params=_cparams("parallel", "arbitrary"),
        name="moe_gather",
    )(xn, selpos_t, gate_t)


FF_TF = 256


def _expert_ffn_kernel(xg_ref, wg_ref, wu_ref, wd_ref, gs_ref, y_ref, acc_ref):
    f = pl.program_id(1)
    last = pl.num_programs(1) - 1

    def partial():
        x = xg_ref[0].reshape(BATCH * CAP, D_MODEL)
        gate = _dot(x, wg_ref[0, 0].astype(BF16))
        up = _dot(x, wu_ref[0, 0].astype(BF16))
        act = (gate * (1.0 / (1.0 + jnp.exp(-gate))) * up).astype(BF16)
        return _dot(act, wd_ref[0, 0].astype(BF16))

    @pl.when(f == 0)
    def _():
        acc_ref[...] = partial()

    @pl.when(jnp.logical_and(f > 0, f < last))
    def _():
        acc_ref[...] += partial()

    @pl.when(f == last)
    def _():
        y = (acc_ref[...] + partial()) * gs_ref[0].reshape(BATCH * CAP, 1)
        y_ref[0] = y.astype(BF16).reshape(BATCH, CAP, D_MODEL)


def _expert_ffn(xg, w_gate, w_up, w_down, gs, layer):
    return pl.pallas_call(
        _expert_ffn_kernel,
        grid=(N_EXPERTS, EXPERT_FF // FF_TF),
        in_specs=[
            pl.BlockSpec((1, BATCH, CAP, D_MODEL), lambda e, f: (e, 0, 0, 0)),
            pl.BlockSpec((1, 1, D_MODEL, FF_TF), lambda e, f: (layer, e, 0, f)),
            pl.BlockSpec((1, 1, D_MODEL, FF_TF), lambda e, f: (layer, e, 0, f)),
            pl.BlockSpec((1, 1, FF_TF, D_MODEL), lambda e, f: (layer, e, f, 0)),
            pl.BlockSpec((1, BATCH, CAP, 1), lambda e, f: (e, 0, 0, 0)),
        ],
        out_specs=pl.BlockSpec((1, BATCH, CAP, D_MODEL), lambda e, f: (e, 0, 0, 0)),
        out_shape=jax.ShapeDtypeStruct((N_EXPERTS, BATCH, CAP, D_MODEL), BF16),
        scratch_shapes=[pltpu.VMEM((BATCH * CAP, D_MODEL), F32)],
        compiler_params=_cparams("parallel", "arbitrary"),
        name="moe_ffn",
    )(xg, w_gate, w_up, w_down, gs)


CB_TM = 256


def _combine_kernel(h_ref, selpos_ref, y_ref, *rest):
    o_ref = rest[-1]
    sp = selpos_ref[0]
    slot = lax.broadcasted_iota(jnp.int32, (CB_TM, CAP), 1).astype(F32)
    onehot = jnp.concatenate(
        [jnp.where(sp[:, e:e + 1] == slot, 1.0, 0.0).astype(BF16) for e in range(N_EXPERTS)], axis=1)
    out = h_ref[0] + _dot(onehot, y_ref[...].reshape(N_EXPERTS * CAP, D_MODEL))
    if len(rest) == 2:
        out = _rms(out, rest[0][...])
    o_ref[0] = out


def _combine(h, selpos, y, final_g=None):
    in_specs = [
        pl.BlockSpec((1, CB_TM, D_MODEL), lambda b, i: (b, i, 0)),
        pl.BlockSpec((1, CB_TM, N_EXPERTS), lambda b, i: (b, i, 0)),
        pl.BlockSpec((N_EXPERTS, 1, CAP, D_MODEL), lambda b, i: (0, b, 0, 0)),
    ]
    args = [h, selpos, y]
    if final_g is not None:
        in_specs.append(pl.BlockSpec((1, D_MODEL), lambda b, i: (0, 0)))
        args.append(final_g.reshape(1, D_MODEL).astype(F32))
    return pl.pallas_call(
        _combine_kernel,
        grid=(BATCH, SEQ // CB_TM),
        in_specs=in_specs,
        out_specs=pl.BlockSpec((1, CB_TM, D_MODEL), lambda b, i: (b, i, 0)),
        out_shape=jax.ShapeDtypeStruct((BATCH, SEQ, D_MODEL), F32),
        compiler_params=_cparams("parallel", "parallel"),
        name="moe_combine",
    )(*args)


def _angles(pos, dim, theta):
    inv = jnp.power(jnp.float32(theta), -(jnp.arange(0, dim, 2, dtype=jnp.float32) / dim))
    ang = pos.astype(jnp.float32)[:, None] * inv[None, :]
    return jnp.cos(ang), jnp.sin(ang)


def _rot_tables(groups, width=HEAD_DIM):
    cos_parts, sneg_parts, spos_parts = [], [], []
    used = 0
    for c, s in groups:
        z = jnp.zeros_like(s)
        cos_parts += [c, c]
        sneg_parts += [-s, z]
        spos_parts += [z, s]
        used += 2 * c.shape[1]
    rest = width - used
    if rest:
        cos_parts.append(jnp.ones((SEQ, rest), F32))
        sneg_parts.append(jnp.zeros((SEQ, rest), F32))
        spos_parts.append(jnp.zeros((SEQ, rest), F32))
    return (jnp.concatenate(cos_parts, axis=1), jnp.concatenate(sneg_parts, axis=1),
            jnp.concatenate(spos_parts, axis=1))


def _mixer_gqa_in(h2, g, w_in, scale_q):
    col = None
    if scale_q:
        col = jnp.concatenate([jnp.full((N_HEADS * HEAD_DIM,), HEAD_DIM ** -0.5 * LOG2E, F32),
                               jnp.ones((2 * N_KV_HEADS * HEAD_DIM,), F32)])
    qkv = _norm_matmul(h2, g, w_in, tm=1024, tn=1024, head_w=HEAD_DIM, col_scale=col)
    return qkv.reshape(N_HEADS + 2 * N_KV_HEADS, BATCH, SEQ, HEAD_DIM)


def _mla_weights(w_in, w_uq):
    w_in = jnp.pad(w_in, ((0, 0), (0, MLA_LAT_PAD - w_in.shape[-1]))).astype(BF16)
    w_uq = jnp.pad((w_uq * (MLA_QK_DIM ** -0.5 * LOG2E)).reshape(MLA_Q_RANK, N_HEADS, MLA_QK_DIM),
                   ((0, 0), (0, 0), (0, MLA_Q_PAD - MLA_QK_DIM))).reshape(MLA_Q_RANK, N_HEADS * MLA_Q_PAD)
    return w_in, w_uq.astype(BF16)


def kernel(x, mem, norm_mix_g, norm_xa_g, norm_ffn_g, a_w_in, a_sink, a_w_out, b_w_in, b_q_norm_g, b_k_norm_g, b_w_out, c_w_in, c_q_lat_norm_g, c_kv_lat_norm_g, c_w_uq, c_w_ukv, c_w_out, d_w_in, d_rpb, d_w_out, mem_norm_g, xa_wq, xa_wkv, xa_wo, router_w, moe_w_gate, moe_w_up, moe_w_down, final_norm_g):
    t = BATCH * SEQ
    pos = jnp.arange(SEQ)
    tab_a = _rot_tables([_angles(pos, ROT_DIM, ROPE_THETA)])
    tab_b = _rot_tables([_angles(pos // GRID_W, HEAD_DIM // 2, AXIAL_THETA),
                         _angles(pos % GRID_W, HEAD_DIM // 2, AXIAL_THETA)])
    tab_c = _rot_tables([_angles(pos, MLA_ROPE_DIM, ROPE_THETA)])
    wkv_all = jnp.transpose(xa_wkv, (1, 0, 2)).reshape(D_MODEL, DEPTH * 2 * XA_INNER).astype(BF16)
    kv_all = _norm_matmul(mem.reshape(BATCH * MEM_TOKENS, D_MODEL), mem_norm_g, wkv_all, tm=BATCH * MEM_TOKENS,
                          tn=2 * XA_INNER).reshape(BATCH, MEM_TOKENS, DEPTH * 2 * XA_INNER)

    h = x
    for i in range(DEPTH):
        m, j = i % N_MIXERS, i // N_MIXERS
        h2 = h.reshape(t, D_MODEL)
        if m == 0:
            qkv = _mixer_gqa_in(h2, norm_mix_g[i], a_w_in[j], True)
            att = _attn_window(qkv, a_sink[j] * LOG2E, tab_a)
            w_out = a_w_out[j]
        elif m == 1:
            qkv = _mixer_gqa_in(h2, norm_mix_g[i], b_w_in[j], False)
            att = _attn_axial(qkv, b_q_norm_g[j] * (HEAD_DIM ** -0.5 * LOG2E), b_k_norm_g[j], tab_b)
            w_out = b_w_out[j]
        elif m == 2:
            w_in, w_uq = _mla_weights(c_w_in[j], c_w_uq[j])
            lat = _norm_matmul(h2, norm_mix_g[i], w_in, tm=1024, tn=MLA_LAT_PAD)
            qfull = _norm_matmul(lat, c_q_lat_norm_g[j], w_uq, col_block=0, tm=2048, tn=1024, head_w=MLA_Q_PAD)
            kvfull = _norm_matmul(lat, c_kv_lat_norm_g[j], c_w_ukv[j].astype(BF16), col_block=1, tm=2048, tn=1024,
                                  head_w=MLA_NOPE_DIM)
            k_rope = lat[:, MLA_Q_RANK + MLA_KV_RANK:].reshape(BATCH, SEQ, HEAD_DIM)
            att = _attn_mla(qfull.reshape(N_HEADS, BATCH, SEQ, MLA_Q_PAD),
                            kvfull.reshape(2 * N_HEADS, BATCH, SEQ, MLA_NOPE_DIM), k_rope, tab_c)
            w_out = c_w_out[j]
        else:
            qkv = _mixer_gqa_in(h2, norm_mix_g[i], d_w_in[j], True)
            att = _attn_na(qkv, _na_bias(d_rpb[j] * LOG2E))
            w_out = d_w_out[j]

        h, xn, aff = _xattn(h, att, w_out.astype(BF16), norm_xa_g[i],
                            (xa_wq[i] * (XA_HEAD_DIM ** -0.5 * LOG2E)).astype(BF16),
                            kv_all, i, xa_wo[i].astype(BF16), norm_ffn_g[i], router_w[i])

        selpos, selpos_t, gate_t = _topk(aff)
        xg, gs = _gather(xn, selpos_t, gate_t)
        y = _expert_ffn(xg, moe_w_gate, moe_w_up, moe_w_down, gs, i)
        h = _combine(h, selpos, y, final_norm_g if i == DEPTH - 1 else None)

    return h
```

```python
import jax
import jax.numpy as jnp
from jax import lax
from jax.experimental import pallas as pl
from jax.experimental.pallas import tpu as pltpu

F32 = jnp.float32
BF16 = jnp.bfloat16

D_MODEL = 2048
BATCH = 4
SEQ = 2048
DEPTH = 4
N_MIXERS = 4
HEAD_DIM = 128
N_HEADS = 16
N_KV_HEADS = 4
GQA_GROUP = N_HEADS // N_KV_HEADS
GQA_IN = (N_HEADS + 2 * N_KV_HEADS) * HEAD_DIM
Q_BLOCK = 128
WINDOW = 128
ROPE_THETA = 500000.0
ROT_DIM = HEAD_DIM // 4
AXIAL_THETA = 10000.0
GRID_W = 64
GRID_H = SEQ // GRID_W
MLA_Q_RANK = 512
MLA_KV_RANK = 512
MLA_NOPE_DIM = 128
MLA_ROPE_DIM = 64
MLA_V_DIM = 128
MLA_QK_DIM = MLA_NOPE_DIM + MLA_ROPE_DIM
MLA_Q_PAD = 256
MLA_LAT_PAD = 1152
NA_ROWS = 8
NA_COLS = 16
MEM_TOKENS = 256
XA_HEADS = 4
XA_HEAD_DIM = 128
XA_INNER = XA_HEADS * XA_HEAD_DIM
N_EXPERTS = 16
EC_CAPACITY = 2
EXPERT_FF = 1024
CAP = EC_CAPACITY * SEQ // N_EXPERTS
EPS = 1e-6
MASKED = -1e30
LOG2E = 1.4426950408889634
LANES = 128

VMEM_LIMIT_BYTES = 56 * 1024 * 1024


def _cparams(*sem):
    return pltpu.CompilerParams(dimension_semantics=sem, vmem_limit_bytes=VMEM_LIMIT_BYTES)


def _dot_nt(a, b):
    return lax.dot_general(a, b, (((1,), (1,)), ((), ())), preferred_element_type=F32)


def _dot(a, b):
    return jnp.dot(a, b, preferred_element_type=F32)


def _rms(x, g):
    ms = jnp.mean(x * x, axis=-1, keepdims=True)
    return x * lax.rsqrt(ms + EPS) * g


def _rot(x, cos, sneg, spos, shift):
    n = x.shape[-1]
    return x * cos + pltpu.roll(x, n - shift, 1) * sneg + pltpu.roll(x, shift, 1) * spos


PV_CHUNK = 512


def _softmax_pv(s, v_ones, sink=None):
    dv = v_ones.shape[1] // 2
    keys = s.shape[1]
    m = jnp.max(s, axis=-1, keepdims=True)
    if sink is not None:
        m = jnp.maximum(m, sink)
    r = None
    for j in range(0, keys, PV_CHUNK):
        part = _dot(jnp.exp2(s[:, j:j + PV_CHUNK] - m).astype(BF16), v_ones[j:j + PV_CHUNK])
        r = part if r is None else r + part
    den = r[:, dv:]
    if sink is not None:
        den = den + jnp.exp2(sink - m)
    return r[:, :dv] * (1.0 / den)


def _fill_v_ones(v_scr, v):
    dv = v.shape[1]
    v_scr[:, :dv] = v
    v_scr[:, dv:] = jnp.ones_like(v)


def _pairs(n_blocks, prep, scores, finish, bufs, peel):
    q_a, q_b, s_a, s_b = bufs
    last = jnp.int32(n_blocks - 1)
    q_b[...] = prep(0)
    s_b[...] = scores(0, q_b[...])
    q_a[...] = prep(1)

    def pair(i, carry):
        n = 2 * i
        s_a[...] = scores(n + 1, q_a[...])
        q_b[...] = prep(jnp.minimum(n + 2, last))
        finish(n, s_b[...])
        s_b[...] = scores(jnp.minimum(n + 2, last), q_b[...])
        q_a[...] = prep(jnp.minimum(n + 3, last))
        finish(n + 1, s_a[...])
        return carry

    if not peel:
        lax.fori_loop(0, n_blocks // 2, pair, 0)
        return
    lax.fori_loop(0, n_blocks // 2 - 1, pair, 0)
    s_a[...] = scores(last, q_a[...])
    finish(last - 1, s_b[...])
    finish(last, s_a[...])


WAVE = 4


def _waves(n_blocks, prep, scores, finish, bufs):
    for k in range(WAVE):
        bufs[k][...] = scores(k, prep(k))

    def wave(i, carry):
        n = i * WAVE
        for k in range(WAVE):
            finish(n + k, bufs[k][...])
        for k in range(WAVE):
            bufs[k][...] = scores(n + WAVE + k, prep(n + WAVE + k))
        return carry

    lax.fori_loop(0, n_blocks // WAVE - 1, wave, 0)
    for k in range(WAVE):
        finish(jnp.int32(n_blocks - WAVE + k), bufs[k][...])


def _wave_bufs(rows, keys):
    return [pltpu.VMEM((rows, keys), F32) for _ in range(WAVE)]


def _pipeline_bufs(rows, dq, keys):
    return [pltpu.VMEM((rows, dq), BF16), pltpu.VMEM((rows, dq), BF16),
            pltpu.VMEM((rows, keys), F32), pltpu.VMEM((rows, keys), F32)]


def _norm_matmul_kernel(x_ref, g_ref, w_ref, *rest):
    o_ref, xn_ref = rest[-2:]

    @pl.when(pl.program_id(1) == 0)
    def _():
        xn_ref[...] = _rms(x_ref[...].astype(F32), g_ref[...]).astype(BF16)

    res = _dot(xn_ref[...], w_ref[...].astype(BF16))
    if len(rest) == 3:
        res = res * rest[0][...]
    if len(o_ref.shape) == 3:
        head_w = o_ref.shape[2]
        for hh in range(o_ref.shape[0]):
            o_ref[hh] = res[:, hh * head_w:(hh + 1) * head_w].astype(o_ref.dtype)
    else:
        o_ref[...] = res.astype(o_ref.dtype)


def _norm_matmul(x, g, w, *, col_block=0, tm, tn, head_w=None, col_scale=None):
    t = x.shape[0]
    k, n = w.shape
    in_specs = [
        pl.BlockSpec((tm, k), lambda i, j: (i, col_block)),
        pl.BlockSpec((1, k), lambda i, j: (0, 0)),
        pl.BlockSpec((k, tn), lambda i, j: (0, j)),
    ]
    args = [x, g.reshape(1, k).astype(F32), w]
    if col_scale is not None:
        in_specs.append(pl.BlockSpec((1, tn), lambda i, j: (0, j)))
        args.append(col_scale.reshape(1, n).astype(F32))
    if head_w is None:
        out_spec = pl.BlockSpec((tm, tn), lambda i, j: (i, j))
        out_shape = jax.ShapeDtypeStruct((t, n), BF16)
    else:
        out_spec = pl.BlockSpec((tn // head_w, tm, head_w), lambda i, j: (j, i, 0))
        out_shape = jax.ShapeDtypeStruct((n // head_w, t, head_w), BF16)
    return pl.pallas_call(
        _norm_matmul_kernel,
        grid=(t // tm, n // tn),
        in_specs=in_specs,
        out_specs=out_spec,
        out_shape=out_shape,
        scratch_shapes=[pltpu.VMEM((tm, k), BF16)],
        compiler_params=_cparams("parallel", "arbitrary"),
        name="norm_matmul",
    )(*args)


def _attn_window_kernel(sink_ref, q_ref, k_ref, v_ref, cos_ref, sneg_ref, spos_ref, o_ref, k_scr, v_scr, mask_scr,
                        *bufs):
    kvh = pl.program_id(1)
    half = ROT_DIM // 2
    nb = SEQ // Q_BLOCK
    rows, keys = GQA_GROUP * Q_BLOCK, 3 * Q_BLOCK
    k_scr[...] = _rot(k_ref[0, 0].astype(F32), cos_ref[...], sneg_ref[...], spos_ref[...], half).astype(BF16)
    _fill_v_ones(v_scr, v_ref[0, 0])

    @pl.when(jnp.logical_and(pl.program_id(0) == 0, kvh == 0))
    def _():
        r = lax.broadcasted_iota(jnp.int32, (rows, keys), 0)
        c = lax.broadcasted_iota(jnp.int32, (rows, keys), 1)
        base = (r & (Q_BLOCK - 1)) - c
        for lead in range(3):
            mask_scr[lead] = jnp.where(jnp.abs(base + lead * Q_BLOCK) <= WINDOW, 0.0, MASKED)

    sink = jnp.concatenate(
        [jnp.full((Q_BLOCK, 1), sink_ref[kvh * GQA_GROUP + g], F32) for g in range(GQA_GROUP)], axis=0)

    def window_start(n):
        return pl.multiple_of(jnp.clip(n - 1, 0, nb - 3) * Q_BLOCK, Q_BLOCK)

    def prep(n):
        q0 = pl.multiple_of(n * Q_BLOCK, Q_BLOCK)
        cos = cos_ref[pl.ds(q0, Q_BLOCK), :]
        sneg = sneg_ref[pl.ds(q0, Q_BLOCK), :]
        spos = spos_ref[pl.ds(q0, Q_BLOCK), :]
        return jnp.concatenate(
            [_rot(q_ref[g, 0, pl.ds(q0, Q_BLOCK), :].astype(F32),
                  cos, sneg, spos, half).astype(BF16) for g in range(GQA_GROUP)], axis=0)

    def scores(n, q4):
        return _dot_nt(q4, k_scr[pl.ds(window_start(n), keys), :])

    def finish(n, s):
        q0 = pl.multiple_of(n * Q_BLOCK, Q_BLOCK)
        k0 = window_start(n)
        lead = n - jnp.clip(n - 1, 0, nb - 3)
        o = _softmax_pv(s + mask_scr[lead], v_scr[pl.ds(k0, keys), :], sink)
        for g in range(GQA_GROUP):
            o_ref[g, 0, pl.ds(q0, Q_BLOCK), :] = (
                o[g * Q_BLOCK:(g + 1) * Q_BLOCK].astype(o_ref.dtype))

    _waves(nb, prep, scores, finish, bufs)


def _gqa_specs():
    head = (1, 1, SEQ, HEAD_DIM)
    group = (GQA_GROUP, 1, SEQ, HEAD_DIM)
    return [
        pl.BlockSpec(group, lambda b, h: (h, b, 0, 0)),
        pl.BlockSpec(head, lambda b, h: (N_HEADS + h, b, 0, 0)),
        pl.BlockSpec(head, lambda b, h: (N_HEADS + N_KV_HEADS + h, b, 0, 0)),
    ], pl.BlockSpec(group, lambda b, h: (h, b, 0, 0))


_ATT_SHAPE = jax.ShapeDtypeStruct((N_HEADS, BATCH, SEQ, HEAD_DIM), BF16)


def _attn_window(qkv, sink, tables):
    cos, sneg, spos = tables
    qkv_specs, out_spec = _gqa_specs()
    tab = pl.BlockSpec((SEQ, HEAD_DIM), lambda b, h: (0, 0))
    return pl.pallas_call(
        _attn_window_kernel,
        grid=(BATCH, N_KV_HEADS),
        in_specs=[pl.BlockSpec(memory_space=pltpu.SMEM)] + qkv_specs + [tab, tab, tab],
        out_specs=out_spec,
        out_shape=_ATT_SHAPE,
        scratch_shapes=[pltpu.VMEM((SEQ, HEAD_DIM), BF16), pltpu.VMEM((SEQ, 2 * HEAD_DIM), BF16),
                        pltpu.VMEM((3, GQA_GROUP * Q_BLOCK, 3 * Q_BLOCK), F32)]
        + _wave_bufs(GQA_GROUP * Q_BLOCK, 3 * Q_BLOCK),
        compiler_params=_cparams("arbitrary", "arbitrary"),
        name="attn_window",
    )(sink.astype(F32), qkv, qkv, qkv, cos, sneg, spos)


AX_TQ = 128


def _attn_axial_kernel(q_ref, k_ref, v_ref, qg_ref, kg_ref, cos_ref, sneg_ref, spos_ref, o_ref, k_scr, v_scr, *bufs):
    quarter = HEAD_DIM // 4
    kn = _rms(k_ref[0, 0].astype(F32), kg_ref[...])
    k_scr[...] = _rot(kn, cos_ref[...], sneg_ref[...], spos_ref[...], quarter).astype(BF16)
    _fill_v_ones(v_scr, v_ref[0, 0])

    def prep(n):
        q0 = pl.multiple_of(n * AX_TQ, AX_TQ)
        cos = cos_ref[pl.ds(q0, AX_TQ), :]
        sneg = sneg_ref[pl.ds(q0, AX_TQ), :]
        spos = spos_ref[pl.ds(q0, AX_TQ), :]
        return jnp.concatenate(
            [_rot(_rms(q_ref[g, 0, pl.ds(q0, AX_TQ), :].astype(F32), qg_ref[...]),
                  cos, sneg, spos, quarter).astype(BF16) for g in range(GQA_GROUP)], axis=0)

    def scores(n, q4):
        return _dot_nt(q4, k_scr[...])

    def finish(n, s):
        q0 = pl.multiple_of(n * AX_TQ, AX_TQ)
        o = _softmax_pv(s, v_scr[...])
        for g in range(GQA_GROUP):
            o_ref[g, 0, pl.ds(q0, AX_TQ), :] = (
                o[g * AX_TQ:(g + 1) * AX_TQ].astype(o_ref.dtype))

    _pairs(SEQ // AX_TQ, prep, scores, finish, bufs, peel=False)


def _attn_axial(qkv, qg, kg, tables):
    cos, sneg, spos = tables
    qkv_specs, out_spec = _gqa_specs()
    tab = pl.BlockSpec((SEQ, HEAD_DIM), lambda b, h: (0, 0))
    gain = pl.BlockSpec((1, HEAD_DIM), lambda b, h: (0, 0))
    return pl.pallas_call(
        _attn_axial_kernel,
        grid=(BATCH, N_KV_HEADS),
        in_specs=qkv_specs + [gain, gain, tab, tab, tab],
        out_specs=out_spec,
        out_shape=_ATT_SHAPE,
        scratch_shapes=[pltpu.VMEM((SEQ, HEAD_DIM), BF16), pltpu.VMEM((SEQ, 2 * HEAD_DIM), BF16)]
        + _pipeline_bufs(GQA_GROUP * AX_TQ, HEAD_DIM, SEQ),
        compiler_params=_cparams("parallel", "parallel"),
        name="attn_axial",
    )(qkv, qkv, qkv, qg.reshape(1, HEAD_DIM).astype(F32), kg.reshape(1, HEAD_DIM).astype(F32), cos, sneg, spos)


MLA_TQ = 1024


def _attn_mla_kernel(q_ref, kn_ref, v_ref, kr_ref, cos_ref, sneg_ref, spos_ref, o_ref, k_scr, v_scr, *bufs):
    half = MLA_ROPE_DIM // 2
    k_scr[:, :MLA_NOPE_DIM] = kn_ref[0, 0]

    @pl.when(pl.program_id(1) == 0)
    def _():
        k_scr[:, MLA_NOPE_DIM:] = _rot(kr_ref[0].astype(F32), cos_ref[...], sneg_ref[...], spos_ref[...],
                                       half).astype(BF16)

    _fill_v_ones(v_scr, v_ref[0, 0])

    def prep(n):
        q0 = pl.multiple_of(n * MLA_TQ, MLA_TQ)
        cos = cos_ref[pl.ds(q0, MLA_TQ), :]
        sneg = sneg_ref[pl.ds(q0, MLA_TQ), :]
        spos = spos_ref[pl.ds(q0, MLA_TQ), :]
        qn = q_ref[0, 0, pl.ds(q0, MLA_TQ), :MLA_NOPE_DIM]
        qr = _rot(q_ref[0, 0, pl.ds(q0, MLA_TQ), MLA_NOPE_DIM:].astype(F32), cos, sneg, spos, half).astype(BF16)
        return jnp.concatenate([qn, qr], axis=1)

    def scores(n, q):
        return _dot_nt(q, k_scr[...])

    def finish(n, s):
        q0 = pl.multiple_of(n * MLA_TQ, MLA_TQ)
        o_ref[0, 0, pl.ds(q0, MLA_TQ), :] = _softmax_pv(s, v_scr[...]).astype(o_ref.dtype)

    _pairs(SEQ // MLA_TQ, prep, scores, finish, bufs, peel=True)


def _attn_mla(qfull, kvfull, k_rope, tables):
    cos, sneg, spos = tables
    tab = pl.BlockSpec((SEQ, HEAD_DIM), lambda b, h: (0, 0))
    return pl.pallas_call(
        _attn_mla_kernel,
        grid=(BATCH, N_HEADS),
        in_specs=[
            pl.BlockSpec((1, 1, SEQ, MLA_Q_PAD), lambda b, h: (h, b, 0, 0)),
            pl.BlockSpec((1, 1, SEQ, MLA_NOPE_DIM), lambda b, h: (2 * h, b, 0, 0)),
            pl.BlockSpec((1, 1, SEQ, MLA_V_DIM), lambda b, h: (2 * h + 1, b, 0, 0)),
            pl.BlockSpec((1, SEQ, HEAD_DIM), lambda b, h: (b, 0, 0)),
            tab, tab, tab,
        ],
        out_specs=pl.BlockSpec((1, 1, SEQ, MLA_V_DIM), lambda b, h: (h, b, 0, 0)),
        out_shape=_ATT_SHAPE,
        scratch_shapes=[pltpu.VMEM((SEQ, MLA_Q_PAD), BF16), pltpu.VMEM((SEQ, 2 * MLA_V_DIM), BF16)]
        + _pipeline_bufs(MLA_TQ, MLA_Q_PAD, SEQ),
        compiler_params=_cparams("parallel", "arbitrary"),
        name="attn_mla",
    )(qfull, kvfull, kvfull, k_rope, cos, sneg, spos)


NA_DELTAS = NA_ROWS
NA_KEYS = NA_ROWS * GRID_W
NA_DR = 2 * NA_ROWS - 1
NA_DC = 2 * NA_COLS - 1


def _na_bias_kernel(rpb_ref, o_ref):
    h = pl.program_id(0)
    shape = (GRID_W, 2 * GRID_W)
    qc = lax.broadcasted_iota(jnp.int32, shape, 0)
    lane = lax.broadcasted_iota(jnp.int32, shape, 1)
    kc = jnp.where(lane >= GRID_W, lane - GRID_W, lane)
    dc = kc - qc + (NA_COLS - 1)
    c_start = jnp.clip(qc - NA_COLS // 2, 0, GRID_W - NA_COLS)
    in_win = (kc >= c_start) & (kc < c_start + NA_COLS)
    low = lane < GRID_W
    tiles = [jnp.zeros(shape, F32) for _ in range(NA_DR)]
    for d in range(NA_DC):
        hit = dc == d
        for dr in range(NA_DR):
            tiles[dr] = jnp.where(hit, rpb_ref[(h * NA_DR + dr) * NA_DC + d], tiles[dr])
    for delta in range(NA_DELTAS):
        for j in range(NA_ROWS // 2):
            t = jnp.where(low, tiles[delta + 2 * j], tiles[delta + 2 * j + 1])
            o_ref[0, delta, :, j * 2 * GRID_W:(j + 1) * 2 * GRID_W] = jnp.where(in_win, t, MASKED)


def _na_bias(rpb):
    return pl.pallas_call(
        _na_bias_kernel,
        grid=(N_HEADS,),
        in_specs=[pl.BlockSpec(memory_space=pltpu.SMEM)],
        out_specs=pl.BlockSpec((1, NA_DELTAS, GRID_W, NA_KEYS), lambda h: (h, 0, 0, 0)),
        out_shape=jax.ShapeDtypeStruct((N_HEADS, NA_DELTAS, GRID_W, NA_KEYS), F32),
        compiler_params=_cparams("parallel"),
        name="na_bias",
    )(rpb.astype(F32).reshape(-1))


def _attn_na_kernel(q_ref, k_ref, v_ref, bias_ref, o_ref, v_scr, *bufs):
    _fill_v_ones(v_scr, v_ref[0, 0])

    def first_key_row(r):
        return jnp.clip(r - NA_ROWS // 2, 0, GRID_H - NA_ROWS)

    def prep(r):
        q0 = pl.multiple_of(r * GRID_W, GRID_W)
        return jnp.concatenate([q_ref[g, 0, pl.ds(q0, GRID_W), :] for g in range(GQA_GROUP)], axis=0)

    def scores(r, q4):
        k0 = pl.multiple_of(first_key_row(r) * GRID_W, GRID_W)
        return _dot_nt(q4, k_ref[0, 0, pl.ds(k0, NA_KEYS), :])

    def finish(r, s):
        r0 = first_key_row(r)
        delta = r0 - r + (NA_ROWS - 1)
        q0 = pl.multiple_of(r * GRID_W, GRID_W)
        k0 = pl.multiple_of(r0 * GRID_W, GRID_W)
        bias = jnp.concatenate([bias_ref[g, delta] for g in range(GQA_GROUP)], axis=0)
        o = _softmax_pv(s + bias, v_scr[pl.ds(k0, NA_KEYS), :])
        for g in range(GQA_GROUP):
            o_ref[g, 0, pl.ds(q0, GRID_W), :] = (
                o[g * GRID_W:(g + 1) * GRID_W].astype(o_ref.dtype))

    _waves(GRID_H, prep, scores, finish, bufs)


def _attn_na(qkv, bias):
    qkv_specs, out_spec = _gqa_specs()
    return pl.pallas_call(
        _attn_na_kernel,
        grid=(BATCH, N_KV_HEADS),
        in_specs=qkv_specs + [pl.BlockSpec((GQA_GROUP, NA_DELTAS, GRID_W, NA_KEYS), lambda b, h: (h, 0, 0, 0))],
        out_specs=out_spec,
        out_shape=_ATT_SHAPE,
        scratch_shapes=[pltpu.VMEM((SEQ, 2 * HEAD_DIM), BF16)] + _wave_bufs(GQA_GROUP * GRID_W, NA_KEYS),
        compiler_params=_cparams("parallel", "parallel"),
        name="attn_na",
    )(qkv, qkv, qkv, bias)


XA_TM = 256


def _xattn_kernel(h_ref, att_ref, wout_ref, g_ref, wq_ref, kv_ref, wo_ref, gf_ref, rw_ref, rb_ref,
                  o_ref, xn_ref, aff_ref):
    att = jnp.concatenate([att_ref[hh, 0] for hh in range(N_HEADS)], axis=1)
    x = h_ref[0] + _dot(att, wout_ref[...])
    xn = _rms(x, g_ref[...]).astype(BF16)
    q = _dot(xn, wq_ref[...]).astype(BF16)
    outs = []
    ones = jnp.ones((MEM_TOKENS, XA_HEAD_DIM), BF16)
    for hh in range(XA_HEADS):
        k = kv_ref[0, :, hh * XA_HEAD_DIM:(hh + 1) * XA_HEAD_DIM]
        v = kv_ref[0, :, XA_INNER + hh * XA_HEAD_DIM:XA_INNER + (hh + 1) * XA_HEAD_DIM]
        s = _dot_nt(q[:, hh * XA_HEAD_DIM:(hh + 1) * XA_HEAD_DIM], k)
        outs.append(_softmax_pv(s, jnp.concatenate([v, ones], axis=1)).astype(BF16))
    y = x + _dot(jnp.concatenate(outs, axis=1), wo_ref[...])
    o_ref[0] = y
    yn = _rms(y, gf_ref[...]).astype(BF16)
    xn_ref[0] = yn
    logits = _dot(yn, rw_ref[...]) + rb_ref[...]
    e = jnp.exp(logits - jnp.max(logits, axis=-1, keepdims=True))
    aff_ref[0] = e / jnp.sum(e, axis=-1, keepdims=True)


def _xattn(h, att, w_out, g, wq, kv, layer, wo, g_ffn, router_w):
    full = lambda b, i: (0, 0)
    tile = lambda b, i: (b, i, 0)
    rw = jnp.pad(router_w, ((0, 0), (0, LANES - N_EXPERTS))).astype(BF16)
    rb = jnp.where(jnp.arange(LANES) < N_EXPERTS, 0.0, MASKED).astype(F32).reshape(1, LANES)
    return pl.pallas_call(
        _xattn_kernel,
        grid=(BATCH, SEQ // XA_TM),
        in_specs=[
            pl.BlockSpec((1, XA_TM, D_MODEL), lambda b, i: (b, i, 0)),
            pl.BlockSpec((N_HEADS, 1, XA_TM, HEAD_DIM), lambda b, i: (0, b, i, 0)),
            pl.BlockSpec(w_out.shape, full),
            pl.BlockSpec((1, D_MODEL), full),
            pl.BlockSpec((D_MODEL, XA_INNER), full),
            pl.BlockSpec((1, MEM_TOKENS, 2 * XA_INNER), lambda b, i: (b, 0, layer)),
            pl.BlockSpec((XA_INNER, D_MODEL), full),
            pl.BlockSpec((1, D_MODEL), full),
            pl.BlockSpec((D_MODEL, LANES), full),
            pl.BlockSpec((1, LANES), full),
        ],
        out_specs=[
            pl.BlockSpec((1, XA_TM, D_MODEL), tile),
            pl.BlockSpec((1, XA_TM, D_MODEL), tile),
            pl.BlockSpec((1, XA_TM, LANES), tile),
        ],
        out_shape=[
            jax.ShapeDtypeStruct((BATCH, SEQ, D_MODEL), F32),
            jax.ShapeDtypeStruct((BATCH, SEQ, D_MODEL), BF16),
            jax.ShapeDtypeStruct((BATCH, SEQ, LANES), F32),
        ],
        compiler_params=_cparams("parallel", "parallel"),
        name="outproj_xattn",
    )(h, att, w_out, g.reshape(1, D_MODEL).astype(F32), wq, kv, wo, g_ffn.reshape(1, D_MODEL).astype(F32), rw, rb)


PREFIX_BLOCK = 256
TOPK_MAX_ITERS = 256


TOPK_ROWS = BATCH * N_EXPERTS


def _topk_kernel(aff_ref, selpos_ref, selpos_t_ref, gate_t_ref):
    a = jnp.concatenate([jnp.transpose(aff_ref[b])[:N_EXPERTS] for b in range(BATCH)], axis=0)
    gate_t_ref[...] = a
    kf = jnp.float32(CAP)

    def count(mask):
        return jnp.sum(jnp.where(mask, 1.0, 0.0), axis=1, keepdims=True)

    def cond(c):
        it, _, _, _, done = c
        return jnp.logical_and(it < TOPK_MAX_ITERS, done == 0)

    def body(c):
        it, lo, hi, _, _ = c
        mid = 0.5 * (lo + hi)
        take = count(a >= mid) >= kf
        lo = jnp.where(take, mid, lo)
        hi = jnp.where(take, hi, mid)
        top = jnp.max(jnp.where(a < hi, a, -1.0), axis=1, keepdims=True)
        bot = jnp.min(jnp.where(a >= lo, a, 3.0), axis=1, keepdims=True)
        done = jnp.min(jnp.where(top == bot, 1, 0))
        return it + 1, lo, hi, top, done

    init = (jnp.int32(0), jnp.zeros((TOPK_ROWS, 1), F32), jnp.full((TOPK_ROWS, 1), 2.0, F32),
            jnp.zeros((TOPK_ROWS, 1), F32), jnp.int32(0))
    _, _, _, kth, _ = lax.while_loop(cond, body, init)

    gt = a > kth
    eq = a == kth
    need = kf - count(gt)
    tok = lax.broadcasted_iota(jnp.int32, (TOPK_ROWS, SEQ), 1).astype(F32)
    cut = jnp.zeros((TOPK_ROWS, 1), F32)
    step = SEQ // 2
    while step >= 1:
        cand = cut + step
        cut = jnp.where(count(eq & (tok < cand)) < need, cand, cut)
        step //= 2
    sel = jnp.where(gt, 1.0, jnp.where(eq & (tok <= cut), 1.0, 0.0))

    r = lax.broadcasted_iota(jnp.int32, (PREFIX_BLOCK, PREFIX_BLOCK), 0)
    c = lax.broadcasted_iota(jnp.int32, (PREFIX_BLOCK, PREFIX_BLOCK), 1)
    utri = jnp.where(r < c, 1.0, 0.0).astype(BF16)
    off = jnp.zeros((TOPK_ROWS, 1), F32)
    for j in range(SEQ // PREFIX_BLOCK):
        blk = sel[:, j * PREFIX_BLOCK:(j + 1) * PREFIX_BLOCK]
        pos = _dot(blk.astype(BF16), utri) + off
        selpos_t_ref[:, j * PREFIX_BLOCK:(j + 1) * PREFIX_BLOCK] = jnp.where(blk > 0.5, pos, -1.0)
        off = off + jnp.sum(blk, axis=1, keepdims=True)

    for b in range(BATCH):
        padded = jnp.concatenate([selpos_t_ref[b * N_EXPERTS:(b + 1) * N_EXPERTS, :],
                                  jnp.zeros((LANES - N_EXPERTS, SEQ), F32)], axis=0)
        selpos_ref[b] = jnp.transpose(padded)[:, :N_EXPERTS]


def _topk(aff):
    whole = lambda i: (0, 0)
    return pl.pallas_call(
        _topk_kernel,
        grid=(1,),
        in_specs=[pl.BlockSpec((BATCH, SEQ, LANES), lambda i: (0, 0, 0))],
        out_specs=[
            pl.BlockSpec((BATCH, SEQ, N_EXPERTS), lambda i: (0, 0, 0)),
            pl.BlockSpec((TOPK_ROWS, SEQ), whole),
            pl.BlockSpec((TOPK_ROWS, SEQ), whole),
        ],
        out_shape=[
            jax.ShapeDtypeStruct((BATCH, SEQ, N_EXPERTS), F32),
            jax.ShapeDtypeStruct((TOPK_ROWS, SEQ), F32),
            jax.ShapeDtypeStruct((TOPK_ROWS, SEQ), F32),
        ],
        compiler_params=_cparams("arbitrary"),
        name="expert_topk",
    )(aff)


GATHER_EXPERTS = 4


def _gather_kernel(xn_ref, selpos_t_ref, gate_t_ref, xg_ref, gs_ref):
    e0 = pl.multiple_of(pl.program_id(1) * GATHER_EXPERTS, GATHER_EXPERTS)
    slot = lax.broadcasted_iota(jnp.int32, (CAP, SEQ), 0).astype(F32)
    hits = []
    for k in range(GATHER_EXPERTS):
        hit = slot == selpos_t_ref[pl.ds(e0 + k, 1), :]
        hits.append(jnp.where(hit, 1.0, 0.0).astype(BF16))
        gs_ref[k, 0] = jnp.sum(jnp.where(hit, gate_t_ref[pl.ds(e0 + k, 1), :], 0.0), axis=1, keepdims=True)
    xg = _dot(jnp.concatenate(hits, axis=0), xn_ref[0]).astype(BF16)
    for k in range(GATHER_EXPERTS):
        xg_ref[k, 0] = xg[k * CAP:(k + 1) * CAP]


def _gather(xn, selpos_t, gate_t):
    return pl.pallas_call(
        _gather_kernel,
        grid=(BATCH, N_EXPERTS // GATHER_EXPERTS),
        in_specs=[
            pl.BlockSpec((1, SEQ, D_MODEL), lambda b, e: (b, 0, 0)),
            pl.BlockSpec((N_EXPERTS, SEQ), lambda b, e: (b, 0)),
            pl.BlockSpec((N_EXPERTS, SEQ), lambda b, e: (b, 0)),
        ],
        out_specs=[
            pl.BlockSpec((GATHER_EXPERTS, 1, CAP, D_MODEL), lambda b, e: (e, b, 0, 0)),
            pl.BlockSpec((GATHER_EXPERTS, 1, CAP, 1), lambda b, e: (e, b, 0, 0)),
        ],
        out_shape=[
            jax.ShapeDtypeStruct((N_EXPERTS, BATCH, CAP, D_MODEL), BF16),
            jax.ShapeDtypeStruct((N_EXPERTS, BATCH, CAP, 1), F32),
        ],
        compiler_params=_cparams("parallel", "arbitrary"),
        name="moe_gather",
    )(xn, selpos_t, gate_t)


FF_TF = 256


def _expert_ffn_kernel(xg_ref, wg_ref, wu_ref, wd_ref, gs_ref, y_ref, acc_ref):
    f = pl.program_id(1)
    last = pl.num_programs(1) - 1

    def partial():
        x = xg_ref[0].reshape(BATCH * CAP, D_MODEL)
        gate = _dot(x, wg_ref[0, 0].astype(BF16))
        up = _dot(x, wu_ref[0, 0].astype(BF16))
        act = (gate * (1.0 / (1.0 + jnp.exp(-gate))) * up).astype(BF16)
        return _dot(act, wd_ref[0, 0].astype(BF16))

    @pl.when(f == 0)
    def _():
        acc_ref[...] = partial()

    @pl.when(jnp.logical_and(f > 0, f < last))
    def _():
        acc_ref[...] += partial()

    @pl.when(f == last)
    def _():
        y = (acc_ref[...] + partial()) * gs_ref[0].reshape(BATCH * CAP, 1)
        y_ref[0] = y.astype(BF16).reshape(BATCH, CAP, D_MODEL)


def _expert_ffn(xg, w_gate, w_up, w_down, gs, layer):
    return pl.pallas_call(
        _expert_ffn_kernel,
        grid=(N_EXPERTS, EXPERT_FF // FF_TF),
        in_specs=[
            pl.BlockSpec((1, BATCH, CAP, D_MODEL), lambda e, f: (e, 0, 0, 0)),
            pl.BlockSpec((1, 1, D_MODEL, FF_TF), lambda e, f: (layer, e, 0, f)),
            pl.BlockSpec((1, 1, D_MODEL, FF_TF), lambda e, f: (layer, e, 0, f)),
            pl.BlockSpec((1, 1, FF_TF, D_MODEL), lambda e, f: (layer, e, f, 0)),
            pl.BlockSpec((1, BATCH, CAP, 1), lambda e, f: (e, 0, 0, 0)),
        ],
        out_specs=pl.BlockSpec((1, BATCH, CAP, D_MODEL), lambda e, f: (e, 0, 0, 0)),
        out_shape=jax.ShapeDtypeStruct((N_EXPERTS, BATCH, CAP, D_MODEL), BF16),
        scratch_shapes=[pltpu.VMEM((BATCH * CAP, D_MODEL), F32)],
        compiler_params=_cparams("parallel", "arbitrary"),
        name="moe_ffn",
    )(xg, w_gate, w_up, w_down, gs)


CB_TM = 256


def _combine_kernel(h_ref, selpos_ref, y_ref, *rest):
    o_ref = rest[-1]
    sp = selpos_ref[0]
    slot = lax.broadcasted_iota(jnp.int32, (CB_TM, CAP), 1).astype(F32)
    onehot = jnp.concatenate(
        [jnp.where(sp[:, e:e + 1] == slot, 1.0, 0.0).astype(BF16) for e in range(N_EXPERTS)], axis=1)
    out = h_ref[0] + _dot(onehot, y_ref[...].reshape(N_EXPERTS * CAP, D_MODEL))
    if len(rest) == 2:
        out = _rms(out, rest[0][...])
    o_ref[0] = out


def _combine(h, selpos, y, final_g=None):
    in_specs = [
        pl.BlockSpec((1, CB_TM, D_MODEL), lambda b, i: (b, i, 0)),
        pl.BlockSpec((1, CB_TM, N_EXPERTS), lambda b, i: (b, i, 0)),
        pl.BlockSpec((N_EXPERTS, 1, CAP, D_MODEL), lambda b, i: (0, b, 0, 0)),
    ]
    args = [h, selpos, y]
    if final_g is not None:
        in_specs.append(pl.BlockSpec((1, D_MODEL), lambda b, i: (0, 0)))
        args.append(final_g.reshape(1, D_MODEL).astype(F32))
    return pl.pallas_call(
        _combine_kernel,
        grid=(BATCH, SEQ // CB_TM),
        in_specs=in_specs,
        out_specs=pl.BlockSpec((1, CB_TM, D_MODEL), lambda b, i: (b, i, 0)),
        out_shape=jax.ShapeDtypeStruct((BATCH, SEQ, D_MODEL), F32),
        compiler_params=_cparams("parallel", "parallel"),
        name="moe_combine",
    )(*args)


def _angles(pos, dim, theta):
    inv = jnp.power(jnp.float32(theta), -(jnp.arange(0, dim, 2, dtype=jnp.float32) / dim))
    ang = pos.astype(jnp.float32)[:, None] * inv[None, :]
    return jnp.cos(ang), jnp.sin(ang)


def _rot_tables(groups, width=HEAD_DIM):
    cos_parts, sneg_parts, spos_parts = [], [], []
    used = 0
    for c, s in groups:
        z = jnp.zeros_like(s)
        cos_parts += [c, c]
        sneg_parts += [-s, z]
        spos_parts += [z, s]
        used += 2 * c.shape[1]
    rest = width - used
    if rest:
        cos_parts.append(jnp.ones((SEQ, rest), F32))
        sneg_parts.append(jnp.zeros((SEQ, rest), F32))
        spos_parts.append(jnp.zeros((SEQ, rest), F32))
    return (jnp.concatenate(cos_parts, axis=1), jnp.concatenate(sneg_parts, axis=1),
            jnp.concatenate(spos_parts, axis=1))


def _mixer_gqa_in(h2, g, w_in, scale_q):
    col = None
    if scale_q:
        col = jnp.concatenate([jnp.full((N_HEADS * HEAD_DIM,), HEAD_DIM ** -0.5 * LOG2E, F32),
                               jnp.ones((2 * N_KV_HEADS * HEAD_DIM,), F32)])
    qkv = _norm_matmul(h2, g, w_in, tm=1024, tn=1024, head_w=HEAD_DIM, col_scale=col)
    return qkv.reshape(N_HEADS + 2 * N_KV_HEADS, BATCH, SEQ, HEAD_DIM)


def _mla_weights(w_in, w_uq):
    w_in = jnp.pad(w_in, ((0, 0), (0, MLA_LAT_PAD - w_in.shape[-1]))).astype(BF16)
    w_uq = jnp.pad((w_uq * (MLA_QK_DIM ** -0.5 * LOG2E)).reshape(MLA_Q_RANK, N_HEADS, MLA_QK_DIM),
                   ((0, 0), (0, 0), (0, MLA_Q_PAD - MLA_QK_DIM))).reshape(MLA_Q_RANK, N_HEADS * MLA_Q_PAD)
    return w_in, w_uq.astype(BF16)


def kernel(x, mem, norm_mix_g, norm_xa_g, norm_ffn_g, a_w_in, a_sink, a_w_out, b_w_in, b_q_norm_g, b_k_norm_g, b_w_out, c_w_in, c_q_lat_norm_g, c_kv_lat_norm_g, c_w_uq, c_w_ukv, c_w_out, d_w_in, d_rpb, d_w_out, mem_norm_g, xa_wq, xa_wkv, xa_wo, router_w, moe_w_gate, moe_w_up, moe_w_down, final_norm_g):
    t = BATCH * SEQ
    pos = jnp.arange(SEQ)
    tab_a = _rot_tables([_angles(pos, ROT_DIM, ROPE_THETA)])
    tab_b = _rot_tables([_angles(pos // GRID_W, HEAD_DIM // 2, AXIAL_THETA),
                         _angles(pos % GRID_W, HEAD_DIM // 2, AXIAL_THETA)])
    tab_c = _rot_tables([_angles(pos, MLA_ROPE_DIM, ROPE_THETA)])
    wkv_all = jnp.transpose(xa_wkv, (1, 0, 2)).reshape(D_MODEL, DEPTH * 2 * XA_INNER).astype(BF16)
    kv_all = _norm_matmul(mem.reshape(BATCH * MEM_TOKENS, D_MODEL), mem_norm_g, wkv_all, tm=BATCH * MEM_TOKENS,
                          tn=2 * XA_INNER).reshape(BATCH, MEM_TOKENS, DEPTH * 2 * XA_INNER)

    h = x
    for i in range(DEPTH):
        m, j = i % N_MIXERS, i // N_MIXERS
        h2 = h.reshape(t, D_MODEL)
        if m == 0:
            qkv = _mixer_gqa_in(h2, norm_mix_g[i], a_w_in[j], True)
            att = _attn_window(qkv, a_sink[j] * LOG2E, tab_a)
            w_out = a_w_out[j]
        elif m == 1:
            qkv = _mixer_gqa_in(h2, norm_mix_g[i], b_w_in[j], False)
            att = _attn_axial(qkv, b_q_norm_g[j] * (HEAD_DIM ** -0.5 * LOG2E), b_k_norm_g[j], tab_b)
            w_out = b_w_out[j]
        elif m == 2:
            w_in, w_uq = _mla_weights(c_w_in[j], c_w_uq[j])
            lat = _norm_matmul(h2, norm_mix_g[i], w_in, tm=1024, tn=MLA_LAT_PAD)
            qfull = _norm_matmul(lat, c_q_lat_norm_g[j], w_uq, col_block=0, tm=2048, tn=1024, head_w=MLA_Q_PAD)
            kvfull = _norm_matmul(lat, c_kv_lat_norm_g[j], c_w_ukv[j].astype(BF16), col_block=1, tm=2048, tn=1024,
                                  head_w=MLA_NOPE_DIM)
            k_rope = lat[:, MLA_Q_RANK + MLA_KV_RANK:].reshape(BATCH, SEQ, HEAD_DIM)
            att = _attn_mla(qfull.reshape(N_HEADS, BATCH, SEQ, MLA_Q_PAD),
                            kvfull.reshape(2 * N_HEADS, BATCH, SEQ, MLA_NOPE_DIM), k_rope, tab_c)
            w_out = c_w_out[j]
        else:
            qkv = _mixer_gqa_in(h2, norm_mix_g[i], d_w_in[j], True)
            att = _attn_na(qkv, _na_bias(d_rpb[j] * LOG2E))
            w_out = d_w_out[j]

        h, xn, aff = _xattn(h, att, w_out.astype(BF16), norm_xa_g[i],
                            (xa_wq[i] * (XA_HEAD_DIM ** -0.5 * LOG2E)).astype(BF16),
                            kv_all, i, xa_wo[i].astype(BF16), norm_ffn_g[i], router_w[i])

        selpos, selpos_t, gate_t = _topk(aff)
        xg, gs = _gather(xn, selpos_t, gate_t)
        y = _expert_ffn(xg, moe_w_gate, moe_w_up, moe_w_down, gs, i)
        h = _combine(h, selpos, y, final_norm_g if i == DEPTH - 1 else None)

    return h
```

```python
import jax
import jax.numpy as jnp
from jax import lax
from jax.experimental import pallas as pl
from jax.experimental.pallas import tpu as pltpu

F32 = jnp.float32
BF16 = jnp.bfloat16

D_MODEL = 2048
BATCH = 4
SEQ = 2048
DEPTH = 4
N_MIXERS = 4
HEAD_DIM = 128
N_HEADS = 16
N_KV_HEADS = 4
GQA_GROUP = N_HEADS // N_KV_HEADS
GQA_IN = (N_HEADS + 2 * N_KV_HEADS) * HEAD_DIM
Q_BLOCK = 128
WINDOW = 128
ROPE_THETA = 500000.0
ROT_DIM = HEAD_DIM // 4
AXIAL_THETA = 10000.0
GRID_W = 64
GRID_H = SEQ // GRID_W
MLA_Q_RANK = 512
MLA_KV_RANK = 512
MLA_NOPE_DIM = 128
MLA_ROPE_DIM = 64
MLA_V_DIM = 128
MLA_QK_DIM = MLA_NOPE_DIM + MLA_ROPE_DIM
MLA_Q_PAD = 256
MLA_LAT_PAD = 1152
NA_ROWS = 8
NA_COLS = 16
MEM_TOKENS = 256
XA_HEADS = 4
XA_HEAD_DIM = 128
XA_INNER = XA_HEADS * XA_HEAD_DIM
N_EXPERTS = 16
EC_CAPACITY = 2
EXPERT_FF = 1024
CAP = EC_CAPACITY * SEQ // N_EXPERTS
EPS = 1e-6
MASKED = -1e30
LOG2E = 1.4426950408889634
LANES = 128

VMEM_LIMIT_BYTES = 56 * 1024 * 1024


def _cparams(*sem):
    return pltpu.CompilerParams(dimension_semantics=sem, vmem_limit_bytes=VMEM_LIMIT_BYTES)


def _dot_nt(a, b):
    return lax.dot_general(a, b, (((1,), (1,)), ((), ())), preferred_element_type=F32)


def _dot(a, b):
    return jnp.dot(a, b, preferred_element_type=F32)


def _rms(x, g):
    ms = jnp.mean(x * x, axis=-1, keepdims=True)
    return x * lax.rsqrt(ms + EPS) * g


def _rot(x, cos, sneg, spos, shift):
    n = x.shape[-1]
    return x * cos + pltpu.roll(x, n - shift, 1) * sneg + pltpu.roll(x, shift, 1) * spos


PV_CHUNK = 512


def _softmax_pv(s, v_ones, sink=None):
    dv = v_ones.shape[1] // 2
    keys = s.shape[1]
    m = jnp.max(s, axis=-1, keepdims=True)
    if sink is not None:
        m = jnp.maximum(m, sink)
    r = None
    for j in range(0, keys, PV_CHUNK):
        part = _dot(jnp.exp2(s[:, j:j + PV_CHUNK] - m).astype(BF16), v_ones[j:j + PV_CHUNK])
        r = part if r is None else r + part
    den = r[:, dv:]
    if sink is not None:
        den = den + jnp.exp2(sink - m)
    return r[:, :dv] * (1.0 / den)


def _fill_v_ones(v_scr, v):
    dv = v.shape[1]
    v_scr[:, :dv] = v
    v_scr[:, dv:] = jnp.ones_like(v)


def _pairs(n_blocks, prep, scores, finish, bufs, peel):
    q_a, q_b, s_a, s_b = bufs
    last = jnp.int32(n_blocks - 1)
    q_b[...] = prep(0)
    s_b[...] = scores(0, q_b[...])
    q_a[...] = prep(1)

    def pair(i, carry):
        n = 2 * i
        s_a[...] = scores(n + 1, q_a[...])
        q_b[...] = prep(jnp.minimum(n + 2, last))
        finish(n, s_b[...])
        s_b[...] = scores(jnp.minimum(n + 2, last), q_b[...])
        q_a[...] = prep(jnp.minimum(n + 3, last))
        finish(n + 1, s_a[...])
        return carry

    if not peel:
        lax.fori_loop(0, n_blocks // 2, pair, 0)
        return
    lax.fori_loop(0, n_blocks // 2 - 1, pair, 0)
    s_a[...] = scores(last, q_a[...])
    finish(last - 1, s_b[...])
    finish(last, s_a[...])


WAVE = 8


def _waves(n_blocks, prep, scores, finish, bufs):
    for k in range(WAVE):
        bufs[k][...] = scores(k, prep(k))

    def wave(i, carry):
        n = i * WAVE
        for k in range(WAVE):
            finish(n + k, bufs[k][...])
        for k in range(WAVE):
            bufs[k][...] = scores(n + WAVE + k, prep(n + WAVE + k))
        return carry

    lax.fori_loop(0, n_blocks // WAVE - 1, wave, 0)
    for k in range(WAVE):
        finish(jnp.int32(n_blocks - WAVE + k), bufs[k][...])


def _wave_bufs(rows, keys):
    return [pltpu.VMEM((rows, keys), F32) for _ in range(WAVE)]


def _pipeline_bufs(rows, dq, keys):
    return [pltpu.VMEM((rows, dq), BF16), pltpu.VMEM((rows, dq), BF16),
            pltpu.VMEM((rows, keys), F32), pltpu.VMEM((rows, keys), F32)]


def _norm_matmul_kernel(x_ref, g_ref, w_ref, *rest):
    o_ref, xn_ref = rest[-2:]

    @pl.when(pl.program_id(1) == 0)
    def _():
        xn_ref[...] = _rms(x_ref[...].astype(F32), g_ref[...]).astype(BF16)

    res = _dot(xn_ref[...], w_ref[...].astype(BF16))
    if len(rest) == 3:
        res = res * rest[0][...]
    if len(o_ref.shape) == 3:
        head_w = o_ref.shape[2]
        for hh in range(o_ref.shape[0]):
            o_ref[hh] = res[:, hh * head_w:(hh + 1) * head_w].astype(o_ref.dtype)
    else:
        o_ref[...] = res.astype(o_ref.dtype)


def _norm_matmul(x, g, w, *, col_block=0, tm, tn, head_w=None, col_scale=None):
    t = x.shape[0]
    k, n = w.shape
    in_specs = [
        pl.BlockSpec((tm, k), lambda i, j: (i, col_block)),
        pl.BlockSpec((1, k), lambda i, j: (0, 0)),
        pl.BlockSpec((k, tn), lambda i, j: (0, j)),
    ]
    args = [x, g.reshape(1, k).astype(F32), w]
    if col_scale is not None:
        in_specs.append(pl.BlockSpec((1, tn), lambda i, j: (0, j)))
        args.append(col_scale.reshape(1, n).astype(F32))
    if head_w is None:
        out_spec = pl.BlockSpec((tm, tn), lambda i, j: (i, j))
        out_shape = jax.ShapeDtypeStruct((t, n), BF16)
    else:
        out_spec = pl.BlockSpec((tn // head_w, tm, head_w), lambda i, j: (j, i, 0))
        out_shape = jax.ShapeDtypeStruct((n // head_w, t, head_w), BF16)
    return pl.pallas_call(
        _norm_matmul_kernel,
        grid=(t // tm, n // tn),
        in_specs=in_specs,
        out_specs=out_spec,
        out_shape=out_shape,
        scratch_shapes=[pltpu.VMEM((tm, k), BF16)],
        compiler_params=_cparams("parallel", "arbitrary"),
        name="norm_matmul",
    )(*args)


def _attn_window_kernel(sink_ref, q_ref, k_ref, v_ref, cos_ref, sneg_ref, spos_ref, o_ref, k_scr, v_scr, mask_scr,
                        *bufs):
    kvh = pl.program_id(1)
    half = ROT_DIM // 2
    nb = SEQ // Q_BLOCK
    rows, keys = GQA_GROUP * Q_BLOCK, 3 * Q_BLOCK
    k_scr[...] = _rot(k_ref[0, 0].astype(F32), cos_ref[...], sneg_ref[...], spos_ref[...], half).astype(BF16)
    _fill_v_ones(v_scr, v_ref[0, 0])

    @pl.when(jnp.logical_and(pl.program_id(0) == 0, kvh == 0))
    def _():
        r = lax.broadcasted_iota(jnp.int32, (rows, keys), 0)
        c = lax.broadcasted_iota(jnp.int32, (rows, keys), 1)
        base = (r & (Q_BLOCK - 1)) - c
        for lead in range(3):
            mask_scr[lead] = jnp.where(jnp.abs(base + lead * Q_BLOCK) <= WINDOW, 0.0, MASKED)

    sink = jnp.concatenate(
        [jnp.full((Q_BLOCK, 1), sink_ref[kvh * GQA_GROUP + g], F32) for g in range(GQA_GROUP)], axis=0)

    def window_start(n):
        return pl.multiple_of(jnp.clip(n - 1, 0, nb - 3) * Q_BLOCK, Q_BLOCK)

    def prep(n):
        q0 = pl.multiple_of(n * Q_BLOCK, Q_BLOCK)
        cos = cos_ref[pl.ds(q0, Q_BLOCK), :]
        sneg = sneg_ref[pl.ds(q0, Q_BLOCK), :]
        spos = spos_ref[pl.ds(q0, Q_BLOCK), :]
        return jnp.concatenate(
            [_rot(q_ref[g, 0, pl.ds(q0, Q_BLOCK), :].astype(F32),
                  cos, sneg, spos, half).astype(BF16) for g in range(GQA_GROUP)], axis=0)

    def scores(n, q4):
        return _dot_nt(q4, k_scr[pl.ds(window_start(n), keys), :])

    def finish(n, s):
        q0 = pl.multiple_of(n * Q_BLOCK, Q_BLOCK)
        k0 = window_start(n)
        lead = n - jnp.clip(n - 1, 0, nb - 3)
        o = _softmax_pv(s + mask_scr[lead], v_scr[pl.ds(k0, keys), :], sink)
        for g in range(GQA_GROUP):
            o_ref[g, 0, pl.ds(q0, Q_BLOCK), :] = (
                o[g * Q_BLOCK:(g + 1) * Q_BLOCK].astype(o_ref.dtype))

    _waves(nb, prep, scores, finish, bufs)


def _gqa_specs():
    head = (1, 1, SEQ, HEAD_DIM)
    group = (GQA_GROUP, 1, SEQ, HEAD_DIM)
    return [
        pl.BlockSpec(group, lambda b, h: (h, b, 0, 0)),
        pl.BlockSpec(head, lambda b, h: (N_HEADS + h, b, 0, 0)),
        pl.BlockSpec(head, lambda b, h: (N_HEADS + N_KV_HEADS + h, b, 0, 0)),
    ], pl.BlockSpec(group, lambda b, h: (h, b, 0, 0))


_ATT_SHAPE = jax.ShapeDtypeStruct((N_HEADS, BATCH, SEQ, HEAD_DIM), BF16)


def _attn_window(qkv, sink, tables):
    cos, sneg, spos = tables
    qkv_specs, out_spec = _gqa_specs()
    tab = pl.BlockSpec((SEQ, HEAD_DIM), lambda b, h: (0, 0))
    return pl.pallas_call(
        _attn_window_kernel,
        grid=(BATCH, N_KV_HEADS),
        in_specs=[pl.BlockSpec(memory_space=pltpu.SMEM)] + qkv_specs + [tab, tab, tab],
        out_specs=out_spec,
        out_shape=_ATT_SHAPE,
        scratch_shapes=[pltpu.VMEM((SEQ, HEAD_DIM), BF16), pltpu.VMEM((SEQ, 2 * HEAD_DIM), BF16),
                        pltpu.VMEM((3, GQA_GROUP * Q_BLOCK, 3 * Q_BLOCK), F32)]
        + _wave_bufs(GQA_GROUP * Q_BLOCK, 3 * Q_BLOCK),
        compiler_params=_cparams("arbitrary", "arbitrary"),
        name="attn_window",
    )(sink.astype(F32), qkv, qkv, qkv, cos, sneg, spos)


AX_TQ = 128


def _attn_axial_kernel(q_ref, k_ref, v_ref, qg_ref, kg_ref, cos_ref, sneg_ref, spos_ref, o_ref, k_scr, v_scr, *bufs):
    quarter = HEAD_DIM // 4
    kn = _rms(k_ref[0, 0].astype(F32), kg_ref[...])
    k_scr[...] = _rot(kn, cos_ref[...], sneg_ref[...], spos_ref[...], quarter).astype(BF16)
    _fill_v_ones(v_scr, v_ref[0, 0])

    def prep(n):
        q0 = pl.multiple_of(n * AX_TQ, AX_TQ)
        cos = cos_ref[pl.ds(q0, AX_TQ), :]
        sneg = sneg_ref[pl.ds(q0, AX_TQ), :]
        spos = spos_ref[pl.ds(q0, AX_TQ), :]
        return jnp.concatenate(
            [_rot(_rms(q_ref[g, 0, pl.ds(q0, AX_TQ), :].astype(F32), qg_ref[...]),
                  cos, sneg, spos, quarter).astype(BF16) for g in range(GQA_GROUP)], axis=0)

    def scores(n, q4):
        return _dot_nt(q4, k_scr[...])

    def finish(n, s):
        q0 = pl.multiple_of(n * AX_TQ, AX_TQ)
        o = _softmax_pv(s, v_scr[...])
        for g in range(GQA_GROUP):
            o_ref[g, 0, pl.ds(q0, AX_TQ), :] = (
                o[g * AX_TQ:(g + 1) * AX_TQ].astype(o_ref.dtype))

    _pairs(SEQ // AX_TQ, prep, scores, finish, bufs, peel=False)


def _attn_axial(qkv, qg, kg, tables):
    cos, sneg, spos = tables
    qkv_specs, out_spec = _gqa_specs()
    tab = pl.BlockSpec((SEQ, HEAD_DIM), lambda b, h: (0, 0))
    gain = pl.BlockSpec((1, HEAD_DIM), lambda b, h: (0, 0))
    return pl.pallas_call(
        _attn_axial_kernel,
        grid=(BATCH, N_KV_HEADS),
        in_specs=qkv_specs + [gain, gain, tab, tab, tab],
        out_specs=out_spec,
        out_shape=_ATT_SHAPE,
        scratch_shapes=[pltpu.VMEM((SEQ, HEAD_DIM), BF16), pltpu.VMEM((SEQ, 2 * HEAD_DIM), BF16)]
        + _pipeline_bufs(GQA_GROUP * AX_TQ, HEAD_DIM, SEQ),
        compiler_params=_cparams("parallel", "parallel"),
        name="attn_axial",
    )(qkv, qkv, qkv, qg.reshape(1, HEAD_DIM).astype(F32), kg.reshape(1, HEAD_DIM).astype(F32), cos, sneg, spos)


MLA_TQ = 1024


def _attn_mla_kernel(q_ref, kn_ref, v_ref, kr_ref, cos_ref, sneg_ref, spos_ref, o_ref, k_scr, v_scr, *bufs):
    half = MLA_ROPE_DIM // 2
    k_scr[:, :MLA_NOPE_DIM] = kn_ref[0, 0]

    @pl.when(pl.program_id(1) == 0)
    def _():
        k_scr[:, MLA_NOPE_DIM:] = _rot(kr_ref[0].astype(F32), cos_ref[...], sneg_ref[...], spos_ref[...],
                                       half).astype(BF16)

    _fill_v_ones(v_scr, v_ref[0, 0])

    def prep(n):
        q0 = pl.multiple_of(n * MLA_TQ, MLA_TQ)
        cos = cos_ref[pl.ds(q0, MLA_TQ), :]
        sneg = sneg_ref[pl.ds(q0, MLA_TQ), :]
        spos = spos_ref[pl.ds(q0, MLA_TQ), :]
        qn = q_ref[0, 0, pl.ds(q0, MLA_TQ), :MLA_NOPE_DIM]
        qr = _rot(q_ref[0, 0, pl.ds(q0, MLA_TQ), MLA_NOPE_DIM:].astype(F32), cos, sneg, spos, half).astype(BF16)
        return jnp.concatenate([qn, qr], axis=1)

    def scores(n, q):
        return _dot_nt(q, k_scr[...])

    def finish(n, s):
        q0 = pl.multiple_of(n * MLA_TQ, MLA_TQ)
        o_ref[0, 0, pl.ds(q0, MLA_TQ), :] = _softmax_pv(s, v_scr[...]).astype(o_ref.dtype)

    _pairs(SEQ // MLA_TQ, prep, scores, finish, bufs, peel=True)


def _attn_mla(qfull, kvfull, k_rope, tables):
    cos, sneg, spos = tables
    tab = pl.BlockSpec((SEQ, HEAD_DIM), lambda b, h: (0, 0))
    return pl.pallas_call(
        _attn_mla_kernel,
        grid=(BATCH, N_HEADS),
        in_specs=[
            pl.BlockSpec((1, 1, SEQ, MLA_Q_PAD), lambda b, h: (h, b, 0, 0)),
            pl.BlockSpec((1, 1, SEQ, MLA_NOPE_DIM), lambda b, h: (2 * h, b, 0, 0)),
            pl.BlockSpec((1, 1, SEQ, MLA_V_DIM), lambda b, h: (2 * h + 1, b, 0, 0)),
            pl.BlockSpec((1, SEQ, HEAD_DIM), lambda b, h: (b, 0, 0)),
            tab, tab, tab,
        ],
        out_specs=pl.BlockSpec((1, 1, SEQ, MLA_V_DIM), lambda b, h: (h, b, 0, 0)),
        out_shape=_ATT_SHAPE,
        scratch_shapes=[pltpu.VMEM((SEQ, MLA_Q_PAD), BF16), pltpu.VMEM((SEQ, 2 * MLA_V_DIM), BF16)]
        + _pipeline_bufs(MLA_TQ, MLA_Q_PAD, SEQ),
        compiler_params=_cparams("parallel", "arbitrary"),
        name="attn_mla",
    )(qfull, kvfull, kvfull, k_rope, cos, sneg, spos)


NA_DELTAS = NA_ROWS
NA_KEYS = NA_ROWS * GRID_W
NA_DR = 2 * NA_ROWS - 1
NA_DC = 2 * NA_COLS - 1


def _na_bias_kernel(rpb_ref, o_ref):
    h = pl.program_id(0)
    shape = (GRID_W, 2 * GRID_W)
    qc = lax.broadcasted_iota(jnp.int32, shape, 0)
    lane = lax.broadcasted_iota(jnp.int32, shape, 1)
    kc = jnp.where(lane >= GRID_W, lane - GRID_W, lane)
    dc = kc - qc + (NA_COLS - 1)
    c_start = jnp.clip(qc - NA_COLS // 2, 0, GRID_W - NA_COLS)
    in_win = (kc >= c_start) & (kc < c_start + NA_COLS)
    low = lane < GRID_W
    tiles = [jnp.zeros(shape, F32) for _ in range(NA_DR)]
    for d in range(NA_DC):
        hit = dc == d
        for dr in range(NA_DR):
            tiles[dr] = jnp.where(hit, rpb_ref[(h * NA_DR + dr) * NA_DC + d], tiles[dr])
    for delta in range(NA_DELTAS):
        for j in range(NA_ROWS // 2):
            t = jnp.where(low, tiles[delta + 2 * j], tiles[delta + 2 * j + 1])
            o_ref[0, delta, :, j * 2 * GRID_W:(j + 1) * 2 * GRID_W] = jnp.where(in_win, t, MASKED)


def _na_bias(rpb):
    return pl.pallas_call(
        _na_bias_kernel,
        grid=(N_HEADS,),
        in_specs=[pl.BlockSpec(memory_space=pltpu.SMEM)],
        out_specs=pl.BlockSpec((1, NA_DELTAS, GRID_W, NA_KEYS), lambda h: (h, 0, 0, 0)),
        out_shape=jax.ShapeDtypeStruct((N_HEADS, NA_DELTAS, GRID_W, NA_KEYS), F32),
        compiler_params=_cparams("parallel"),
        name="na_bias",
    )(rpb.astype(F32).reshape(-1))


def _attn_na_kernel(q_ref, k_ref, v_ref, bias_ref, o_ref, v_scr, *bufs):
    _fill_v_ones(v_scr, v_ref[0, 0])

    def first_key_row(r):
        return jnp.clip(r - NA_ROWS // 2, 0, GRID_H - NA_ROWS)

    def prep(r):
        q0 = pl.multiple_of(r * GRID_W, GRID_W)
        return jnp.concatenate([q_ref[g, 0, pl.ds(q0, GRID_W), :] for g in range(GQA_GROUP)], axis=0)

    def scores(r, q4):
        k0 = pl.multiple_of(first_key_row(r) * GRID_W, GRID_W)
        return _dot_nt(q4, k_ref[0, 0, pl.ds(k0, NA_KEYS), :])

    def finish(r, s):
        r0 = first_key_row(r)
        delta = r0 - r + (NA_ROWS - 1)
        q0 = pl.multiple_of(r * GRID_W, GRID_W)
        k0 = pl.multiple_of(r0 * GRID_W, GRID_W)
        bias = jnp.concatenate([bias_ref[g, delta] for g in range(GQA_GROUP)], axis=0)
        o = _softmax_pv(s + bias, v_scr[pl.ds(k0, NA_KEYS), :])
        for g in range(GQA_GROUP):
            o_ref[g, 0, pl.ds(q0, GRID_W), :] = (
                o[g * GRID_W:(g + 1) * GRID_W].astype(o_ref.dtype))

    _waves(GRID_H, prep, scores, finish, bufs)


def _attn_na(qkv, bias):
    qkv_specs, out_spec = _gqa_specs()
    return pl.pallas_call(
        _attn_na_kernel,
        grid=(BATCH, N_KV_HEADS),
        in_specs=qkv_specs + [pl.BlockSpec((GQA_GROUP, NA_DELTAS, GRID_W, NA_KEYS), lambda b, h: (h, 0, 0, 0))],
        out_specs=out_spec,
        out_shape=_ATT_SHAPE,
        scratch_shapes=[pltpu.VMEM((SEQ, 2 * HEAD_DIM), BF16)] + _wave_bufs(GQA_GROUP * GRID_W, NA_KEYS),
        compiler_params=_cparams("parallel", "parallel"),
        name="attn_na",
    )(qkv, qkv, qkv, bias)


XA_TM = 256


def _xattn_kernel(h_ref, att_ref, wout_ref, g_ref, wq_ref, kv_ref, wo_ref, gf_ref, rw_ref, rb_ref,
                  o_ref, xn_ref, aff_ref):
    att = jnp.concatenate([att_ref[hh, 0] for hh in range(N_HEADS)], axis=1)
    x = h_ref[0] + _dot(att, wout_ref[...])
    xn = _rms(x, g_ref[...]).astype(BF16)
    q = _dot(xn, wq_ref[...]).astype(BF16)
    outs = []
    ones = jnp.ones((MEM_TOKENS, XA_HEAD_DIM), BF16)
    for hh in range(XA_HEADS):
        k = kv_ref[0, :, hh * XA_HEAD_DIM:(hh + 1) * XA_HEAD_DIM]
        v = kv_ref[0, :, XA_INNER + hh * XA_HEAD_DIM:XA_INNER + (hh + 1) * XA_HEAD_DIM]
        s = _dot_nt(q[:, hh * XA_HEAD_DIM:(hh + 1) * XA_HEAD_DIM], k)
        outs.append(_softmax_pv(s, jnp.concatenate([v, ones], axis=1)).astype(BF16))
    y = x + _dot(jnp.concatenate(outs, axis=1), wo_ref[...])
    o_ref[0] = y
    yn = _rms(y, gf_ref[...]).astype(BF16)
    xn_ref[0] = yn
    logits = _dot(yn, rw_ref[...]) + rb_ref[...]
    e = jnp.exp(logits - jnp.max(logits, axis=-1, keepdims=True))
    aff_ref[0] = e / jnp.sum(e, axis=-1, keepdims=True)


def _xattn(h, att, w_out, g, wq, kv, layer, wo, g_ffn, router_w):
    full = lambda b, i: (0, 0)
    tile = lambda b, i: (b, i, 0)
    rw = jnp.pad(router_w, ((0, 0), (0, LANES - N_EXPERTS))).astype(BF16)
    rb = jnp.where(jnp.arange(LANES) < N_EXPERTS, 0.0, MASKED).astype(F32).reshape(1, LANES)
    return pl.pallas_call(
        _xattn_kernel,
        grid=(BATCH, SEQ // XA_TM),
        in_specs=[
            pl.BlockSpec((1, XA_TM, D_MODEL), lambda b, i: (b, i, 0)),
            pl.BlockSpec((N_HEADS, 1, XA_TM, HEAD_DIM), lambda b, i: (0, b, i, 0)),
            pl.BlockSpec(w_out.shape, full),
            pl.BlockSpec((1, D_MODEL), full),
            pl.BlockSpec((D_MODEL, XA_INNER), full),
            pl.BlockSpec((1, MEM_TOKENS, 2 * XA_INNER), lambda b, i: (b, 0, layer)),
            pl.BlockSpec((XA_INNER, D_MODEL), full),
            pl.BlockSpec((1, D_MODEL), full),
            pl.BlockSpec((D_MODEL, LANES), full),
            pl.BlockSpec((1, LANES), full),
        ],
        out_specs=[
            pl.BlockSpec((1, XA_TM, D_MODEL), tile),
            pl.BlockSpec((1, XA_TM, D_MODEL), tile),
            pl.BlockSpec((1, XA_TM, LANES), tile),
        ],
        out_shape=[
            jax.ShapeDtypeStruct((BATCH, SEQ, D_MODEL), F32),
            jax.ShapeDtypeStruct((BATCH, SEQ, D_MODEL), BF16),
            jax.ShapeDtypeStruct((BATCH, SEQ, LANES), F32),
        ],
        compiler_params=_cparams("parallel", "parallel"),
        name="outproj_xattn",
    )(h, att, w_out, g.reshape(1, D_MODEL).astype(F32), wq, kv, wo, g_ffn.reshape(1, D_MODEL).astype(F32), rw, rb)


PREFIX_BLOCK = 256
TOPK_MAX_ITERS = 256


TOPK_ROWS = BATCH * N_EXPERTS


def _topk_kernel(aff_ref, selpos_ref, selpos_t_ref, gate_t_ref):
    a = jnp.concatenate([jnp.transpose(aff_ref[b])[:N_EXPERTS] for b in range(BATCH)], axis=0)
    gate_t_ref[...] = a
    kf = jnp.float32(CAP)

    def count(mask):
        return jnp.sum(jnp.where(mask, 1.0, 0.0), axis=1, keepdims=True)

    def cond(c):
        it, _, _, _, done = c
        return jnp.logical_and(it < TOPK_MAX_ITERS, done == 0)

    def body(c):
        it, lo, hi, _, _ = c
        mid = 0.5 * (lo + hi)
        take = count(a >= mid) >= kf
        lo = jnp.where(take, mid, lo)
        hi = jnp.where(take, hi, mid)
        top = jnp.max(jnp.where(a < hi, a, -1.0), axis=1, keepdims=True)
        bot = jnp.min(jnp.where(a >= lo, a, 3.0), axis=1, keepdims=True)
        done = jnp.min(jnp.where(top == bot, 1, 0))
        return it + 1, lo, hi, top, done

    init = (jnp.int32(0), jnp.zeros((TOPK_ROWS, 1), F32), jnp.full((TOPK_ROWS, 1), 2.0, F32),
            jnp.zeros((TOPK_ROWS, 1), F32), jnp.int32(0))
    _, _, _, kth, _ = lax.while_loop(cond, body, init)

    gt = a > kth
    eq = a == kth
    need = kf - count(gt)
    tok = lax.broadcasted_iota(jnp.int32, (TOPK_ROWS, SEQ), 1).astype(F32)
    cut = jnp.zeros((TOPK_ROWS, 1), F32)
    step = SEQ // 2
    while step >= 1:
        cand = cut + step
        cut = jnp.where(count(eq & (tok < cand)) < need, cand, cut)
        step //= 2
    sel = jnp.where(gt, 1.0, jnp.where(eq & (tok <= cut), 1.0, 0.0))

    r = lax.broadcasted_iota(jnp.int32, (PREFIX_BLOCK, PREFIX_BLOCK), 0)
    c = lax.broadcasted_iota(jnp.int32, (PREFIX_BLOCK, PREFIX_BLOCK), 1)
    utri = jnp.where(r < c, 1.0, 0.0).astype(BF16)
    off = jnp.zeros((TOPK_ROWS, 1), F32)
    for j in range(SEQ // PREFIX_BLOCK):
        blk = sel[:, j * PREFIX_BLOCK:(j + 1) * PREFIX_BLOCK]
        pos = _dot(blk.astype(BF16), utri) + off
        selpos_t_ref[:, j * PREFIX_BLOCK:(j + 1) * PREFIX_BLOCK] = jnp.where(blk > 0.5, pos, -1.0)
        off = off + jnp.sum(blk, axis=1, keepdims=True)

    for b in range(BATCH):
        padded = jnp.concatenate([selpos_t_ref[b * N_EXPERTS:(b + 1) * N_EXPERTS, :],
                                  jnp.zeros((LANES - N_EXPERTS, SEQ), F32)], axis=0)
        selpos_ref[b] = jnp.transpose(padded)[:, :N_EXPERTS]


def _topk(aff):
    whole = lambda i: (0, 0)
    return pl.pallas_call(
        _topk_kernel,
        grid=(1,),
        in_specs=[pl.BlockSpec((BATCH, SEQ, LANES), lambda i: (0, 0, 0))],
        out_specs=[
            pl.BlockSpec((BATCH, SEQ, N_EXPERTS), lambda i: (0, 0, 0)),
            pl.BlockSpec((TOPK_ROWS, SEQ), whole),
            pl.BlockSpec((TOPK_ROWS, SEQ), whole),
        ],
        out_shape=[
            jax.ShapeDtypeStruct((BATCH, SEQ, N_EXPERTS), F32),
            jax.ShapeDtypeStruct((TOPK_ROWS, SEQ), F32),
            jax.ShapeDtypeStruct((TOPK_ROWS, SEQ), F32),
        ],
        compiler_params=_cparams("arbitrary"),
        name="expert_topk",
    )(aff)


GATHER_EXPERTS = 4


def _gather_kernel(xn_ref, selpos_t_ref, gate_t_ref, xg_ref, gs_ref):
    e0 = pl.multiple_of(pl.program_id(1) * GATHER_EXPERTS, GATHER_EXPERTS)
    slot = lax.broadcasted_iota(jnp.int32, (CAP, SEQ), 0).astype(F32)
    hits = []
    for k in range(GATHER_EXPERTS):
        hit = slot == selpos_t_ref[pl.ds(e0 + k, 1), :]
        hits.append(jnp.where(hit, 1.0, 0.0).astype(BF16))
        gs_ref[k, 0] = jnp.sum(jnp.where(hit, gate_t_ref[pl.ds(e0 + k, 1), :], 0.0), axis=1, keepdims=True)
    xg = _dot(jnp.concatenate(hits, axis=0), xn_ref[0]).astype(BF16)
    for k in range(GATHER_EXPERTS):
        xg_ref[k, 0] = xg[k * CAP:(k + 1) * CAP]


def _gather(xn, selpos_t, gate_t):
    return pl.pallas_call(
        _gather_kernel,
        grid=(BATCH, N_EXPERTS // GATHER_EXPERTS),
        in_specs=[
            pl.BlockSpec((1, SEQ, D_MODEL), lambda b, e: (b, 0, 0)),
            pl.BlockSpec((N_EXPERTS, SEQ), lambda b, e: (b, 0)),
            pl.BlockSpec((N_EXPERTS, SEQ), lambda b, e: (b, 0)),
        ],
        out_specs=[
            pl.BlockSpec((GATHER_EXPERTS, 1, CAP, D_MODEL), lambda b, e: (e, b, 0, 0)),
            pl.BlockSpec((GATHER_EXPERTS, 1, CAP, 1), lambda b, e: (e, b, 0, 0)),
        ],
        out_shape=[
            jax.ShapeDtypeStruct((N_EXPERTS, BATCH, CAP, D_MODEL), BF16),
            jax.ShapeDtypeStruct((N_EXPERTS, BATCH, CAP, 1), F32),
        ],
        compiler_params=_cparams("parallel", "arbitrary"),
        name="moe_gather",
    )(xn, selpos_t, gate_t)


FF_TF = 256


def _expert_ffn_kernel(xg_ref, wg_ref, wu_ref, wd_ref, gs_ref, y_ref, acc_ref):
    f = pl.program_id(1)
    last = pl.num_programs(1) - 1

    def partial():
        x = xg_ref[0].reshape(BATCH * CAP, D_MODEL)
        gate = _dot(x, wg_ref[0, 0].astype(BF16))
        up = _dot(x, wu_ref[0, 0].astype(BF16))
        act = (gate * (1.0 / (1.0 + jnp.exp(-gate))) * up).astype(BF16)
        return _dot(act, wd_ref[0, 0].astype(BF16))

    @pl.when(f == 0)
    def _():
        acc_ref[...] = partial()

    @pl.when(jnp.logical_and(f > 0, f < last))
    def _():
        acc_ref[...] += partial()

    @pl.when(f == last)
    def _():
        y = (acc_ref[...] + partial()) * gs_ref[0].reshape(BATCH * CAP, 1)
        y_ref[0] = y.astype(BF16).reshape(BATCH, CAP, D_MODEL)


def _expert_ffn(xg, w_gate, w_up, w_down, gs, layer):
    return pl.pallas_call(
        _expert_ffn_kernel,
        grid=(N_EXPERTS, EXPERT_FF // FF_TF),
        in_specs=[
            pl.BlockSpec((1, BATCH, CAP, D_MODEL), lambda e, f: (e, 0, 0, 0)),
            pl.BlockSpec((1, 1, D_MODEL, FF_TF), lambda e, f: (layer, e, 0, f)),
            pl.BlockSpec((1, 1, D_MODEL, FF_TF), lambda e, f: (layer, e, 0, f)),
            pl.BlockSpec((1, 1, FF_TF, D_MODEL), lambda e, f: (layer, e, f, 0)),
            pl.BlockSpec((1, BATCH, CAP, 1), lambda e, f: (e, 0, 0, 0)),
        ],
        out_specs=pl.BlockSpec((1, BATCH, CAP, D_MODEL), lambda e, f: (e, 0, 0, 0)),
        out_shape=jax.ShapeDtypeStruct((N_EXPERTS, BATCH, CAP, D_MODEL), BF16),
        scratch_shapes=[pltpu.VMEM((BATCH * CAP, D_MODEL), F32)],
        compiler_params=_cparams("parallel", "arbitrary"),
        name="moe_ffn",
    )(xg, w_gate, w_up, w_down, gs)


CB_TM = 256


def _combine_kernel(h_ref, selpos_ref, y_ref, *rest):
    o_ref = rest[-1]
    sp = selpos_ref[0]
    slot = lax.broadcasted_iota(jnp.int32, (CB_TM, CAP), 1).astype(F32)
    onehot = jnp.concatenate(
        [jnp.where(sp[:, e:e + 1] == slot, 1.0, 0.0).astype(BF16) for e in range(N_EXPERTS)], axis=1)
    out = h_ref[0] + _dot(onehot, y_ref[...].reshape(N_EXPERTS * CAP, D_MODEL))
    if len(rest) == 2:
        out = _rms(out, rest[0][...])
    o_ref[0] = out


def _combine(h, selpos, y, final_g=None):
    in_specs = [
        pl.BlockSpec((1, CB_TM, D_MODEL), lambda b, i: (b, i, 0)),
        pl.BlockSpec((1, CB_TM, N_EXPERTS), lambda b, i: (b, i, 0)),
        pl.BlockSpec((N_EXPERTS, 1, CAP, D_MODEL), lambda b, i: (0, b, 0, 0)),
    ]
    args = [h, selpos, y]
    if final_g is not None:
        in_specs.append(pl.BlockSpec((1, D_MODEL), lambda b, i: (0, 0)))
        args.append(final_g.reshape(1, D_MODEL).astype(F32))
    return pl.pallas_call(
        _combine_kernel,
        grid=(BATCH, SEQ // CB_TM),
        in_specs=in_specs,
        out_specs=pl.BlockSpec((1, CB_TM, D_MODEL), lambda b, i: (b, i, 0)),
        out_shape=jax.ShapeDtypeStruct((BATCH, SEQ, D_MODEL), F32),
        compiler_params=_cparams("parallel", "parallel"),
        name="moe_combine",
    )(*args)


def _angles(pos, dim, theta):
    inv = jnp.power(jnp.float32(theta), -(jnp.arange(0, dim, 2, dtype=jnp.float32) / dim))
    ang = pos.astype(jnp.float32)[:, None] * inv[None, :]
    return jnp.cos(ang), jnp.sin(ang)


def _rot_tables(groups, width=HEAD_DIM):
    cos_parts, sneg_parts, spos_parts = [], [], []
    used = 0
    for c, s in groups:
        z = jnp.zeros_like(s)
        cos_parts += [c, c]
        sneg_parts += [-s, z]
        spos_parts += [z, s]
        used += 2 * c.shape[1]
    rest = width - used
    if rest:
        cos_parts.append(jnp.ones((SEQ, rest), F32))
        sneg_parts.append(jnp.zeros((SEQ, rest), F32))
        spos_parts.append(jnp.zeros((SEQ, rest), F32))
    return (jnp.concatenate(cos_parts, axis=1), jnp.concatenate(sneg_parts, axis=1),
            jnp.concatenate(spos_parts, axis=1))


def _mixer_gqa_in(h2, g, w_in, scale_q):
    col = None
    if scale_q:
        col = jnp.concatenate([jnp.full((N_HEADS * HEAD_DIM,), HEAD_DIM ** -0.5 * LOG2E, F32),
                               jnp.ones((2 * N_KV_HEADS * HEAD_DIM,), F32)])
    qkv = _norm_matmul(h2, g, w_in, tm=1024, tn=1024, head_w=HEAD_DIM, col_scale=col)
    return qkv.reshape(N_HEADS + 2 * N_KV_HEADS, BATCH, SEQ, HEAD_DIM)


def _mla_weights(w_in, w_uq):
    w_in = jnp.pad(w_in, ((0, 0), (0, MLA_LAT_PAD - w_in.shape[-1]))).astype(BF16)
    w_uq = jnp.pad((w_uq * (MLA_QK_DIM ** -0.5 * LOG2E)).reshape(MLA_Q_RANK, N_HEADS, MLA_QK_DIM),
                   ((0, 0), (0, 0), (0, MLA_Q_PAD - MLA_QK_DIM))).reshape(MLA_Q_RANK, N_HEADS * MLA_Q_PAD)
    return w_in, w_uq.astype(BF16)


def kernel(x, mem, norm_mix_g, norm_xa_g, norm_ffn_g, a_w_in, a_sink, a_w_out, b_w_in, b_q_norm_g, b_k_norm_g, b_w_out, c_w_in, c_q_lat_norm_g, c_kv_lat_norm_g, c_w_uq, c_w_ukv, c_w_out, d_w_in, d_rpb, d_w_out, mem_norm_g, xa_wq, xa_wkv, xa_wo, router_w, moe_w_gate, moe_w_up, moe_w_down, final_norm_g):
    t = BATCH * SEQ
    pos = jnp.arange(SEQ)
    tab_a = _rot_tables([_angles(pos, ROT_DIM, ROPE_THETA)])
    tab_b = _rot_tables([_angles(pos // GRID_W, HEAD_DIM // 2, AXIAL_THETA),
                         _angles(pos % GRID_W, HEAD_DIM // 2, AXIAL_THETA)])
    tab_c = _rot_tables([_angles(pos, MLA_ROPE_DIM, ROPE_THETA)])
    wkv_all = jnp.transpose(xa_wkv, (1, 0, 2)).reshape(D_MODEL, DEPTH * 2 * XA_INNER).astype(BF16)
    kv_all = _norm_matmul(mem.reshape(BATCH * MEM_TOKENS, D_MODEL), mem_norm_g, wkv_all, tm=BATCH * MEM_TOKENS,
                          tn=2 * XA_INNER).reshape(BATCH, MEM_TOKENS, DEPTH * 2 * XA_INNER)

    h = x
    for i in range(DEPTH):
        m, j = i % N_MIXERS, i // N_MIXERS
        h2 = h.reshape(t, D_MODEL)
        if m == 0:
            qkv = _mixer_gqa_in(h2, norm_mix_g[i], a_w_in[j], True)
            att = _attn_window(qkv, a_sink[j] * LOG2E, tab_a)
            w_out = a_w_out[j]
        elif m == 1:
            qkv = _mixer_gqa_in(h2, norm_mix_g[i], b_w_in[j], False)
            att = _attn_axial(qkv, b_q_norm_g[j] * (HEAD_DIM ** -0.5 * LOG2E), b_k_norm_g[j], tab_b)
            w_out = b_w_out[j]
        elif m == 2:
            w_in, w_uq = _mla_weights(c_w_in[j], c_w_uq[j])
            lat = _norm_matmul(h2, norm_mix_g[i], w_in, tm=1024, tn=MLA_LAT_PAD)
            qfull = _norm_matmul(lat, c_q_lat_norm_g[j], w_uq, col_block=0, tm=2048, tn=1024, head_w=MLA_Q_PAD)
            kvfull = _norm_matmul(lat, c_kv_lat_norm_g[j], c_w_ukv[j].astype(BF16), col_block=1, tm=2048, tn=1024,
                                  head_w=MLA_NOPE_DIM)
            k_rope = lat[:, MLA_Q_RANK + MLA_KV_RANK:].reshape(BATCH, SEQ, HEAD_DIM)
            att = _attn_mla(qfull.reshape(N_HEADS, BATCH, SEQ, MLA_Q_PAD),
                            kvfull.reshape(2 * N_HEADS, BATCH, SEQ, MLA_NOPE_DIM), k_rope, tab_c)
            w_out = c_w_out[j]
        else:
            qkv = _mixer_gqa_in(h2, norm_mix_g[i], d_w_in[j], True)
            att = _attn_na(qkv, _na_bias(d_rpb[j] * LOG2E))
            w_out = d_w_out[j]

        h, xn, aff = _xattn(h, att, w_out.astype(BF16), norm_xa_g[i],
                            (xa_wq[i] * (XA_HEAD_DIM ** -0.5 * LOG2E)).astype(BF16),
                            kv_all, i, xa_wo[i].astype(BF16), norm_ffn_g[i], router_w[i])

        selpos, selpos_t, gate_t = _topk(aff)
        xg, gs = _gather(xn, selpos_t, gate_t)
        y = _expert_ffn(xg, moe_w_gate, moe_w_up, moe_w_down, gs, i)
        h = _combine(h, selpos, y, final_norm_g if i == DEPTH - 1 else None)

    return h
```

```python
import jax
import jax.numpy as jnp
from jax import lax
from jax.experimental import pallas as pl
from jax.experimental.pallas import tpu as pltpu

F32 = jnp.float32
BF16 = jnp.bfloat16

D_MODEL = 2048
BATCH = 4
SEQ = 2048
DEPTH = 4
N_MIXERS = 4
HEAD_DIM = 128
N_HEADS = 16
N_KV_HEADS = 4
GQA_GROUP = N_HEADS // N_KV_HEADS
GQA_IN = (N_HEADS + 2 * N_KV_HEADS) * HEAD_DIM
Q_BLOCK = 128
WINDOW = 128
ROPE_THETA = 500000.0
ROT_DIM = HEAD_DIM // 4
AXIAL_THETA = 10000.0
GRID_W = 64
GRID_H = SEQ // GRID_W
MLA_Q_RANK = 512
MLA_KV_RANK = 512
MLA_NOPE_DIM = 128
MLA_ROPE_DIM = 64
MLA_V_DIM = 128
MLA_QK_DIM = MLA_NOPE_DIM + MLA_ROPE_DIM
MLA_Q_PAD = 256
MLA_LAT_PAD = 1152
NA_ROWS = 8
NA_COLS = 16
MEM_TOKENS = 256
XA_HEADS = 4
XA_HEAD_DIM = 128
XA_INNER = XA_HEADS * XA_HEAD_DIM
N_EXPERTS = 16
EC_CAPACITY = 2
EXPERT_FF = 1024
CAP = EC_CAPACITY * SEQ // N_EXPERTS
EPS = 1e-6
MASKED = -1e30
LOG2E = 1.4426950408889634
LANES = 128

VMEM_LIMIT_BYTES = 56 * 1024 * 1024


def _cparams(*sem, fuse_inputs=None):
    return pltpu.CompilerParams(dimension_semantics=sem, vmem_limit_bytes=VMEM_LIMIT_BYTES,
                                allow_input_fusion=fuse_inputs)


def _dot_nt(a, b):
    return lax.dot_general(a, b, (((1,), (1,)), ((), ())), preferred_element_type=F32)


def _dot(a, b):
    return jnp.dot(a, b, preferred_element_type=F32)


def _rms(x, g):
    ms = jnp.mean(x * x, axis=-1, keepdims=True)
    return x * lax.rsqrt(ms + EPS) * g


def _rot(x, cos, sneg, spos, shift):
    n = x.shape[-1]
    return x * cos + pltpu.roll(x, n - shift, 1) * sneg + pltpu.roll(x, shift, 1) * spos


PV_CHUNK = 512


def _softmax_pv(s, v_ones, sink=None):
    dv = v_ones.shape[1] // 2
    keys = s.shape[1]
    m = jnp.max(s, axis=-1, keepdims=True)
    if sink is not None:
        m = jnp.maximum(m, sink)
    r = None
    for j in range(0, keys, PV_CHUNK):
        part = _dot(jnp.exp2(s[:, j:j + PV_CHUNK] - m).astype(BF16), v_ones[j:j + PV_CHUNK])
        r = part if r is None else r + part
    den = r[:, dv:]
    if sink is not None:
        den = den + jnp.exp2(sink - m)
    return r[:, :dv] * (1.0 / den)


def _fill_v_ones(v_scr, v):
    dv = v.shape[1]
    v_scr[:, :dv] = v
    v_scr[:, dv:] = jnp.ones_like(v)


def _pairs(n_blocks, prep, scores, finish, bufs, peel):
    q_a, q_b, s_a, s_b = bufs
    last = jnp.int32(n_blocks - 1)
    q_b[...] = prep(0)
    s_b[...] = scores(0, q_b[...])
    q_a[...] = prep(1)

    def pair(i, carry):
        n = 2 * i
        s_a[...] = scores(n + 1, q_a[...])
        q_b[...] = prep(jnp.minimum(n + 2, last))
        finish(n, s_b[...])
        s_b[...] = scores(jnp.minimum(n + 2, last), q_b[...])
        q_a[...] = prep(jnp.minimum(n + 3, last))
        finish(n + 1, s_a[...])
        return carry

    if not peel:
        lax.fori_loop(0, n_blocks // 2, pair, 0)
        return
    lax.fori_loop(0, n_blocks // 2 - 1, pair, 0)
    s_a[...] = scores(last, q_a[...])
    finish(last - 1, s_b[...])
    finish(last, s_a[...])


WAVE = 8


def _waves(n_blocks, prep, scores, finish, bufs):
    for k in range(WAVE):
        bufs[k][...] = scores(k, prep(k))

    def wave(i, carry):
        n = i * WAVE
        for k in range(WAVE):
            finish(n + k, bufs[k][...])
        for k in range(WAVE):
            bufs[k][...] = scores(n + WAVE + k, prep(n + WAVE + k))
        return carry

    lax.fori_loop(0, n_blocks // WAVE - 1, wave, 0)
    for k in range(WAVE):
        finish(jnp.int32(n_blocks - WAVE + k), bufs[k][...])


def _wave_bufs(rows, keys):
    return [pltpu.VMEM((rows, keys), F32) for _ in range(WAVE)]


def _pipeline_bufs(rows, dq, keys):
    return [pltpu.VMEM((rows, dq), BF16), pltpu.VMEM((rows, dq), BF16),
            pltpu.VMEM((rows, keys), F32), pltpu.VMEM((rows, keys), F32)]


def _norm_matmul_kernel(x_ref, g_ref, w_ref, *rest):
    o_ref, xn_ref = rest[-2:]

    @pl.when(pl.program_id(1) == 0)
    def _():
        xn_ref[...] = _rms(x_ref[...].astype(F32), g_ref[...]).astype(BF16)

    res = _dot(xn_ref[...], w_ref[...].astype(BF16))
    if len(rest) == 3:
        res = res * rest[0][...]
    if len(o_ref.shape) == 3:
        head_w = o_ref.shape[2]
        for hh in range(o_ref.shape[0]):
            o_ref[hh] = res[:, hh * head_w:(hh + 1) * head_w].astype(o_ref.dtype)
    else:
        o_ref[...] = res.astype(o_ref.dtype)


def _norm_matmul(x, g, w, *, col_block=0, tm, tn, head_w=None, col_scale=None):
    t = x.shape[0]
    k, n = w.shape
    in_specs = [
        pl.BlockSpec((tm, k), lambda i, j: (i, col_block)),
        pl.BlockSpec((1, k), lambda i, j: (0, 0)),
        pl.BlockSpec((k, tn), lambda i, j: (0, j)),
    ]
    args = [x, g.reshape(1, k).astype(F32), w]
    if col_scale is not None:
        in_specs.append(pl.BlockSpec((1, tn), lambda i, j: (0, j)))
        args.append(col_scale.reshape(1, n).astype(F32))
    if head_w is None:
        out_spec = pl.BlockSpec((tm, tn), lambda i, j: (i, j))
        out_shape = jax.ShapeDtypeStruct((t, n), BF16)
    else:
        out_spec = pl.BlockSpec((tn // head_w, tm, head_w), lambda i, j: (j, i, 0))
        out_shape = jax.ShapeDtypeStruct((n // head_w, t, head_w), BF16)
    return pl.pallas_call(
        _norm_matmul_kernel,
        grid=(t // tm, n // tn),
        in_specs=in_specs,
        out_specs=out_spec,
        out_shape=out_shape,
        scratch_shapes=[pltpu.VMEM((tm, k), BF16)],
        compiler_params=_cparams("parallel", "arbitrary"),
        name="norm_matmul",
    )(*args)


def _attn_window_kernel(sink_ref, q_ref, k_ref, v_ref, cos_ref, sneg_ref, spos_ref, o_ref, k_scr, v_scr, mask_scr,
                        *bufs):
    kvh = pl.program_id(1)
    half = ROT_DIM // 2
    nb = SEQ // Q_BLOCK
    rows, keys = GQA_GROUP * Q_BLOCK, 3 * Q_BLOCK
    k_scr[...] = _rot(k_ref[0, 0].astype(F32), cos_ref[...], sneg_ref[...], spos_ref[...], half).astype(BF16)
    _fill_v_ones(v_scr, v_ref[0, 0])

    @pl.when(jnp.logical_and(pl.program_id(0) == 0, kvh == 0))
    def _():
        r = lax.broadcasted_iota(jnp.int32, (rows, keys), 0)
        c = lax.broadcasted_iota(jnp.int32, (rows, keys), 1)
        base = (r & (Q_BLOCK - 1)) - c
        for lead in range(3):
            mask_scr[lead] = jnp.where(jnp.abs(base + lead * Q_BLOCK) <= WINDOW, 0.0, MASKED)

    sink = jnp.concatenate(
        [jnp.full((Q_BLOCK, 1), sink_ref[kvh * GQA_GROUP + g], F32) for g in range(GQA_GROUP)], axis=0)

    def window_start(n):
        return pl.multiple_of(jnp.clip(n - 1, 0, nb - 3) * Q_BLOCK, Q_BLOCK)

    def prep(n):
        q0 = pl.multiple_of(n * Q_BLOCK, Q_BLOCK)
        cos = cos_ref[pl.ds(q0, Q_BLOCK), :]
        sneg = sneg_ref[pl.ds(q0, Q_BLOCK), :]
        spos = spos_ref[pl.ds(q0, Q_BLOCK), :]
        return jnp.concatenate(
            [_rot(q_ref[g, 0, pl.ds(q0, Q_BLOCK), :].astype(F32),
                  cos, sneg, spos, half).astype(BF16) for g in range(GQA_GROUP)], axis=0)

    def scores(n, q4):
        return _dot_nt(q4, k_scr[pl.ds(window_start(n), keys), :])

    def finish(n, s):
        q0 = pl.multiple_of(n * Q_BLOCK, Q_BLOCK)
        k0 = window_start(n)
        lead = n - jnp.clip(n - 1, 0, nb - 3)
        o = _softmax_pv(s + mask_scr[lead], v_scr[pl.ds(k0, keys), :], sink)
        for g in range(GQA_GROUP):
            o_ref[g, 0, pl.ds(q0, Q_BLOCK), :] = (
                o[g * Q_BLOCK:(g + 1) * Q_BLOCK].astype(o_ref.dtype))

    _waves(nb, prep, scores, finish, bufs)


def _gqa_specs():
    head = (1, 1, SEQ, HEAD_DIM)
    group = (GQA_GROUP, 1, SEQ, HEAD_DIM)
    return [
        pl.BlockSpec(group, lambda b, h: (h, b, 0, 0)),
        pl.BlockSpec(head, lambda b, h: (N_HEADS + h, b, 0, 0)),
        pl.BlockSpec(head, lambda b, h: (N_HEADS + N_KV_HEADS + h, b, 0, 0)),
    ], pl.BlockSpec(group, lambda b, h: (h, b, 0, 0))


_ATT_SHAPE = jax.ShapeDtypeStruct((N_HEADS, BATCH, SEQ, HEAD_DIM), BF16)


def _attn_window(qkv, sink, tables):
    cos, sneg, spos = tables
    qkv_specs, out_spec = _gqa_specs()
    tab = pl.BlockSpec((SEQ, HEAD_DIM), lambda b, h: (0, 0))
    return pl.pallas_call(
        _attn_window_kernel,
        grid=(BATCH, N_KV_HEADS),
        in_specs=[pl.BlockSpec(memory_space=pltpu.SMEM)] + qkv_specs + [tab, tab, tab],
        out_specs=out_spec,
        out_shape=_ATT_SHAPE,
        scratch_shapes=[pltpu.VMEM((SEQ, HEAD_DIM), BF16), pltpu.VMEM((SEQ, 2 * HEAD_DIM), BF16),
                        pltpu.VMEM((3, GQA_GROUP * Q_BLOCK, 3 * Q_BLOCK), F32)]
        + _wave_bufs(GQA_GROUP * Q_BLOCK, 3 * Q_BLOCK),
        compiler_params=_cparams("arbitrary", "arbitrary"),
        name="attn_window",
    )(sink.astype(F32), qkv, qkv, qkv, cos, sneg, spos)


AX_TQ = 128


def _attn_axial_kernel(q_ref, k_ref, v_ref, qg_ref, kg_ref, cos_ref, sneg_ref, spos_ref, o_ref, k_scr, v_scr, *bufs):
    quarter = HEAD_DIM // 4
    kn = _rms(k_ref[0, 0].astype(F32), kg_ref[...])
    k_scr[...] = _rot(kn, cos_ref[...], sneg_ref[...], spos_ref[...], quarter).astype(BF16)
    _fill_v_ones(v_scr, v_ref[0, 0])

    def prep(n):
        q0 = pl.multiple_of(n * AX_TQ, AX_TQ)
        cos = cos_ref[pl.ds(q0, AX_TQ), :]
        sneg = sneg_ref[pl.ds(q0, AX_TQ), :]
        spos = spos_ref[pl.ds(q0, AX_TQ), :]
        return jnp.concatenate(
            [_rot(_rms(q_ref[g, 0, pl.ds(q0, AX_TQ), :].astype(F32), qg_ref[...]),
                  cos, sneg, spos, quarter).astype(BF16) for g in range(GQA_GROUP)], axis=0)

    def scores(n, q4):
        return _dot_nt(q4, k_scr[...])

    def finish(n, s):
        q0 = pl.multiple_of(n * AX_TQ, AX_TQ)
        o = _softmax_pv(s, v_scr[...])
        for g in range(GQA_GROUP):
            o_ref[g, 0, pl.ds(q0, AX_TQ), :] = (
                o[g * AX_TQ:(g + 1) * AX_TQ].astype(o_ref.dtype))

    _pairs(SEQ // AX_TQ, prep, scores, finish, bufs, peel=False)


def _attn_axial(qkv, qg, kg, tables):
    cos, sneg, spos = tables
    qkv_specs, out_spec = _gqa_specs()
    tab = pl.BlockSpec((SEQ, HEAD_DIM), lambda b, h: (0, 0))
    gain = pl.BlockSpec((1, HEAD_DIM), lambda b, h: (0, 0))
    return pl.pallas_call(
        _attn_axial_kernel,
        grid=(BATCH, N_KV_HEADS),
        in_specs=qkv_specs + [gain, gain, tab, tab, tab],
        out_specs=out_spec,
        out_shape=_ATT_SHAPE,
        scratch_shapes=[pltpu.VMEM((SEQ, HEAD_DIM), BF16), pltpu.VMEM((SEQ, 2 * HEAD_DIM), BF16)]
        + _pipeline_bufs(GQA_GROUP * AX_TQ, HEAD_DIM, SEQ),
        compiler_params=_cparams("parallel", "parallel"),
        name="attn_axial",
    )(qkv, qkv, qkv, qg.reshape(1, HEAD_DIM).astype(F32), kg.reshape(1, HEAD_DIM).astype(F32), cos, sneg, spos)


MLA_TQ = 1024


def _attn_mla_kernel(q_ref, kn_ref, v_ref, kr_ref, cos_ref, sneg_ref, spos_ref, o_ref, k_scr, v_scr, *bufs):
    half = MLA_ROPE_DIM // 2
    k_scr[:, :MLA_NOPE_DIM] = kn_ref[0, 0]

    @pl.when(pl.program_id(1) == 0)
    def _():
        k_scr[:, MLA_NOPE_DIM:] = _rot(kr_ref[0].astype(F32), cos_ref[...], sneg_ref[...], spos_ref[...],
                                       half).astype(BF16)

    _fill_v_ones(v_scr, v_ref[0, 0])

    def prep(n):
        q0 = pl.multiple_of(n * MLA_TQ, MLA_TQ)
        cos = cos_ref[pl.ds(q0, MLA_TQ), :]
        sneg = sneg_ref[pl.ds(q0, MLA_TQ), :]
        spos = spos_ref[pl.ds(q0, MLA_TQ), :]
        qn = q_ref[0, 0, pl.ds(q0, MLA_TQ), :MLA_NOPE_DIM]
        qr = _rot(q_ref[0, 0, pl.ds(q0, MLA_TQ), MLA_NOPE_DIM:].astype(F32), cos, sneg, spos, half).astype(BF16)
        return jnp.concatenate([qn, qr], axis=1)

    def scores(n, q):
        return _dot_nt(q, k_scr[...])

    def finish(n, s):
        q0 = pl.multiple_of(n * MLA_TQ, MLA_TQ)
        o_ref[0, 0, pl.ds(q0, MLA_TQ), :] = _softmax_pv(s, v_scr[...]).astype(o_ref.dtype)

    _pairs(SEQ // MLA_TQ, prep, scores, finish, bufs, peel=True)


def _attn_mla(qfull, kvfull, k_rope, tables):
    cos, sneg, spos = tables
    tab = pl.BlockSpec((SEQ, HEAD_DIM), lambda b, h: (0, 0))
    return pl.pallas_call(
        _attn_mla_kernel,
        grid=(BATCH, N_HEADS),
        in_specs=[
            pl.BlockSpec((1, 1, SEQ, MLA_Q_PAD), lambda b, h: (h, b, 0, 0)),
            pl.BlockSpec((1, 1, SEQ, MLA_NOPE_DIM), lambda b, h: (2 * h, b, 0, 0)),
            pl.BlockSpec((1, 1, SEQ, MLA_V_DIM), lambda b, h: (2 * h + 1, b, 0, 0)),
            pl.BlockSpec((1, SEQ, HEAD_DIM), lambda b, h: (b, 0, 0)),
            tab, tab, tab,
        ],
        out_specs=pl.BlockSpec((1, 1, SEQ, MLA_V_DIM), lambda b, h: (h, b, 0, 0)),
        out_shape=_ATT_SHAPE,
        scratch_shapes=[pltpu.VMEM((SEQ, MLA_Q_PAD), BF16), pltpu.VMEM((SEQ, 2 * MLA_V_DIM), BF16)]
        + _pipeline_bufs(MLA_TQ, MLA_Q_PAD, SEQ),
        compiler_params=_cparams("parallel", "arbitrary"),
        name="attn_mla",
    )(qfull, kvfull, kvfull, k_rope, cos, sneg, spos)


NA_DELTAS = NA_ROWS
NA_KEYS = NA_ROWS * GRID_W
NA_DR = 2 * NA_ROWS - 1
NA_DC = 2 * NA_COLS - 1


def _na_bias_kernel(rpb_ref, o_ref):
    h = pl.program_id(0)
    shape = (GRID_W, 2 * GRID_W)
    qc = lax.broadcasted_iota(jnp.int32, shape, 0)
    lane = lax.broadcasted_iota(jnp.int32, shape, 1)
    kc = jnp.where(lane >= GRID_W, lane - GRID_W, lane)
    dc = kc - qc + (NA_COLS - 1)
    c_start = jnp.clip(qc - NA_COLS // 2, 0, GRID_W - NA_COLS)
    in_win = (kc >= c_start) & (kc < c_start + NA_COLS)
    low = lane < GRID_W
    tiles = [jnp.zeros(shape, F32) for _ in range(NA_DR)]
    for d in range(NA_DC):
        hit = dc == d
        for dr in range(NA_DR):
            tiles[dr] = jnp.where(hit, rpb_ref[(h * NA_DR + dr) * NA_DC + d], tiles[dr])
    for delta in range(NA_DELTAS):
        for j in range(NA_ROWS // 2):
            t = jnp.where(low, tiles[delta + 2 * j], tiles[delta + 2 * j + 1])
            o_ref[0, delta, :, j * 2 * GRID_W:(j + 1) * 2 * GRID_W] = jnp.where(in_win, t, MASKED)


def _na_bias(rpb):
    return pl.pallas_call(
        _na_bias_kernel,
        grid=(N_HEADS,),
        in_specs=[pl.BlockSpec(memory_space=pltpu.SMEM)],
        out_specs=pl.BlockSpec((1, NA_DELTAS, GRID_W, NA_KEYS), lambda h: (h, 0, 0, 0)),
        out_shape=jax.ShapeDtypeStruct((N_HEADS, NA_DELTAS, GRID_W, NA_KEYS), F32),
        compiler_params=_cparams("parallel"),
        name="na_bias",
    )(rpb.astype(F32).reshape(-1))


def _attn_na_kernel(q_ref, k_ref, v_ref, bias_ref, o_ref, v_scr, *bufs):
    _fill_v_ones(v_scr, v_ref[0, 0])

    def first_key_row(r):
        return jnp.clip(r - NA_ROWS // 2, 0, GRID_H - NA_ROWS)

    def prep(r):
        q0 = pl.multiple_of(r * GRID_W, GRID_W)
        return jnp.concatenate([q_ref[g, 0, pl.ds(q0, GRID_W), :] for g in range(GQA_GROUP)], axis=0)

    def scores(r, q4):
        k0 = pl.multiple_of(first_key_row(r) * GRID_W, GRID_W)
        return _dot_nt(q4, k_ref[0, 0, pl.ds(k0, NA_KEYS), :])

    def finish(r, s):
        r0 = first_key_row(r)
        delta = r0 - r + (NA_ROWS - 1)
        q0 = pl.multiple_of(r * GRID_W, GRID_W)
        k0 = pl.multiple_of(r0 * GRID_W, GRID_W)
        bias = jnp.concatenate([bias_ref[g, delta] for g in range(GQA_GROUP)], axis=0)
        o = _softmax_pv(s + bias, v_scr[pl.ds(k0, NA_KEYS), :])
        for g in range(GQA_GROUP):
            o_ref[g, 0, pl.ds(q0, GRID_W), :] = (
                o[g * GRID_W:(g + 1) * GRID_W].astype(o_ref.dtype))

    _waves(GRID_H, prep, scores, finish, bufs)


def _attn_na(qkv, bias):
    qkv_specs, out_spec = _gqa_specs()
    return pl.pallas_call(
        _attn_na_kernel,
        grid=(BATCH, N_KV_HEADS),
        in_specs=qkv_specs + [pl.BlockSpec((GQA_GROUP, NA_DELTAS, GRID_W, NA_KEYS), lambda b, h: (h, 0, 0, 0))],
        out_specs=out_spec,
        out_shape=_ATT_SHAPE,
        scratch_shapes=[pltpu.VMEM((SEQ, 2 * HEAD_DIM), BF16)] + _wave_bufs(GQA_GROUP * GRID_W, NA_KEYS),
        compiler_params=_cparams("parallel", "parallel"),
        name="attn_na",
    )(qkv, qkv, qkv, bias)


XA_TM = 256


def _xattn_kernel(h_ref, att_ref, wout_ref, g_ref, wq_ref, kv_ref, wo_ref, gf_ref, rw_ref, rb_ref,
                  o_ref, xn_ref, aff_ref):
    att = jnp.concatenate([att_ref[hh, 0] for hh in range(N_HEADS)], axis=1)
    x = h_ref[0] + _dot(att, wout_ref[...])
    xn = _rms(x, g_ref[...]).astype(BF16)
    q = _dot(xn, wq_ref[...]).astype(BF16)
    outs = []
    ones = jnp.ones((MEM_TOKENS, XA_HEAD_DIM), BF16)
    for hh in range(XA_HEADS):
        k = kv_ref[0, :, hh * XA_HEAD_DIM:(hh + 1) * XA_HEAD_DIM]
        v = kv_ref[0, :, XA_INNER + hh * XA_HEAD_DIM:XA_INNER + (hh + 1) * XA_HEAD_DIM]
        s = _dot_nt(q[:, hh * XA_HEAD_DIM:(hh + 1) * XA_HEAD_DIM], k)
        outs.append(_softmax_pv(s, jnp.concatenate([v, ones], axis=1)).astype(BF16))
    y = x + _dot(jnp.concatenate(outs, axis=1), wo_ref[...])
    o_ref[0] = y
    yn = _rms(y, gf_ref[...]).astype(BF16)
    xn_ref[0] = yn
    logits = _dot(yn, rw_ref[...]) + rb_ref[...]
    e = jnp.exp(logits - jnp.max(logits, axis=-1, keepdims=True))
    aff_ref[0] = e / jnp.sum(e, axis=-1, keepdims=True)


def _xattn(h, att, w_out, g, wq, kv, layer, wo, g_ffn, router_w):
    full = lambda b, i: (0, 0)
    tile = lambda b, i: (b, i, 0)
    rw = jnp.pad(router_w, ((0, 0), (0, LANES - N_EXPERTS))).astype(BF16)
    rb = jnp.where(jnp.arange(LANES) < N_EXPERTS, 0.0, MASKED).astype(F32).reshape(1, LANES)
    return pl.pallas_call(
        _xattn_kernel,
        grid=(BATCH, SEQ // XA_TM),
        in_specs=[
            pl.BlockSpec((1, XA_TM, D_MODEL), lambda b, i: (b, i, 0)),
            pl.BlockSpec((N_HEADS, 1, XA_TM, HEAD_DIM), lambda b, i: (0, b, i, 0)),
            pl.BlockSpec(w_out.shape, full),
            pl.BlockSpec((1, D_MODEL), full),
            pl.BlockSpec((D_MODEL, XA_INNER), full),
            pl.BlockSpec((1, MEM_TOKENS, 2 * XA_INNER), lambda b, i: (b, 0, layer)),
            pl.BlockSpec((XA_INNER, D_MODEL), full),
            pl.BlockSpec((1, D_MODEL), full),
            pl.BlockSpec((D_MODEL, LANES), full),
            pl.BlockSpec((1, LANES), full),
        ],
        out_specs=[
            pl.BlockSpec((1, XA_TM, D_MODEL), tile),
            pl.BlockSpec((1, XA_TM, D_MODEL), tile),
            pl.BlockSpec((1, XA_TM, LANES), tile),
        ],
        out_shape=[
            jax.ShapeDtypeStruct((BATCH, SEQ, D_MODEL), F32),
            jax.ShapeDtypeStruct((BATCH, SEQ, D_MODEL), BF16),
            jax.ShapeDtypeStruct((BATCH, SEQ, LANES), F32),
        ],
        compiler_params=_cparams("parallel", "parallel", fuse_inputs=[
            False, False, True, False, True, False, True, False, True, False]),
        name="outproj_xattn",
    )(h, att, w_out, g.reshape(1, D_MODEL).astype(F32), wq, kv, wo, g_ffn.reshape(1, D_MODEL).astype(F32), rw, rb)


PREFIX_BLOCK = 256
TOPK_MAX_ITERS = 256


TOPK_ROWS = BATCH * N_EXPERTS


def _topk_kernel(aff_ref, selpos_ref, selpos_t_ref, gate_t_ref):
    a = jnp.concatenate([jnp.transpose(aff_ref[b])[:N_EXPERTS] for b in range(BATCH)], axis=0)
    gate_t_ref[...] = a
    kf = jnp.float32(CAP)

    def count(mask):
        return jnp.sum(jnp.where(mask, 1.0, 0.0), axis=1, keepdims=True)

    def cond(c):
        it, _, _, _, done = c
        return jnp.logical_and(it < TOPK_MAX_ITERS, done == 0)

    def body(c):
        it, lo, hi, _, _ = c
        mid = 0.5 * (lo + hi)
        take = count(a >= mid) >= kf
        lo = jnp.where(take, mid, lo)
        hi = jnp.where(take, hi, mid)
        top = jnp.max(jnp.where(a < hi, a, -1.0), axis=1, keepdims=True)
        bot = jnp.min(jnp.where(a >= lo, a, 3.0), axis=1, keepdims=True)
        done = jnp.min(jnp.where(top == bot, 1, 0))
        return it + 1, lo, hi, top, done

    init = (jnp.int32(0), jnp.zeros((TOPK_ROWS, 1), F32), jnp.full((TOPK_ROWS, 1), 2.0, F32),
            jnp.zeros((TOPK_ROWS, 1), F32), jnp.int32(0))
    _, _, _, kth, _ = lax.while_loop(cond, body, init)

    gt = a > kth
    eq = a == kth
    need = kf - count(gt)
    tok = lax.broadcasted_iota(jnp.int32, (TOPK_ROWS, SEQ), 1).astype(F32)
    cut = jnp.zeros((TOPK_ROWS, 1), F32)
    step = SEQ // 2
    while step >= 1:
        cand = cut + step
        cut = jnp.where(count(eq & (tok < cand)) < need, cand, cut)
        step //= 2
    sel = jnp.where(gt, 1.0, jnp.where(eq & (tok <= cut), 1.0, 0.0))

    r = lax.broadcasted_iota(jnp.int32, (PREFIX_BLOCK, PREFIX_BLOCK), 0)
    c = lax.broadcasted_iota(jnp.int32, (PREFIX_BLOCK, PREFIX_BLOCK), 1)
    utri = jnp.where(r < c, 1.0, 0.0).astype(BF16)
    off = jnp.zeros((TOPK_ROWS, 1), F32)
    for j in range(SEQ // PREFIX_BLOCK):
        blk = sel[:, j * PREFIX_BLOCK:(j + 1) * PREFIX_BLOCK]
        pos = _dot(blk.astype(BF16), utri) + off
        selpos_t_ref[:, j * PREFIX_BLOCK:(j + 1) * PREFIX_BLOCK] = jnp.where(blk > 0.5, pos, -1.0)
        off = off + jnp.sum(blk, axis=1, keepdims=True)

    for b in range(BATCH):
        padded = jnp.concatenate([selpos_t_ref[b * N_EXPERTS:(b + 1) * N_EXPERTS, :],
                                  jnp.zeros((LANES - N_EXPERTS, SEQ), F32)], axis=0)
        selpos_ref[b] = jnp.transpose(padded)[:, :N_EXPERTS]


def _topk(aff):
    whole = lambda i: (0, 0)
    return pl.pallas_call(
        _topk_kernel,
        grid=(1,),
        in_specs=[pl.BlockSpec((BATCH, SEQ, LANES), lambda i: (0, 0, 0))],
        out_specs=[
            pl.BlockSpec((BATCH, SEQ, N_EXPERTS), lambda i: (0, 0, 0)),
            pl.BlockSpec((TOPK_ROWS, SEQ), whole),
            pl.BlockSpec((TOPK_ROWS, SEQ), whole),
        ],
        out_shape=[
            jax.ShapeDtypeStruct((BATCH, SEQ, N_EXPERTS), F32),
            jax.ShapeDtypeStruct((TOPK_ROWS, SEQ), F32),
            jax.ShapeDtypeStruct((TOPK_ROWS, SEQ), F32),
        ],
        compiler_params=_cparams("arbitrary"),
        name="expert_topk",
    )(aff)


GATHER_EXPERTS = 4


def _gather_kernel(xn_ref, selpos_t_ref, gate_t_ref, xg_ref, gs_ref):
    e0 = pl.multiple_of(pl.program_id(1) * GATHER_EXPERTS, GATHER_EXPERTS)
    slot = lax.broadcasted_iota(jnp.int32, (CAP, SEQ), 0).astype(F32)
    hits = []
    for k in range(GATHER_EXPERTS):
        hit = slot == selpos_t_ref[pl.ds(e0 + k, 1), :]
        hits.append(jnp.where(hit, 1.0, 0.0).astype(BF16))
        gs_ref[k, 0] = jnp.sum(jnp.where(hit, gate_t_ref[pl.ds(e0 + k, 1), :], 0.0), axis=1, keepdims=True)
    xg = _dot(jnp.concatenate(hits, axis=0), xn_ref[0]).astype(BF16)
    for k in range(GATHER_EXPERTS):
        xg_ref[k, 0] = xg[k * CAP:(k + 1) * CAP]


def _gather(xn, selpos_t, gate_t):
    return pl.pallas_call(
        _gather_kernel,
        grid=(BATCH, N_EXPERTS // GATHER_EXPERTS),
        in_specs=[
            pl.BlockSpec((1, SEQ, D_MODEL), lambda b, e: (b, 0, 0)),
            pl.BlockSpec((N_EXPERTS, SEQ), lambda b, e: (b, 0)),
            pl.BlockSpec((N_EXPERTS, SEQ), lambda b, e: (b, 0)),
        ],
        out_specs=[
            pl.BlockSpec((GATHER_EXPERTS, 1, CAP, D_MODEL), lambda b, e: (e, b, 0, 0)),
            pl.BlockSpec((GATHER_EXPERTS, 1, CAP, 1), lambda b, e: (e, b, 0, 0)),
        ],
        out_shape=[
            jax.ShapeDtypeStruct((N_EXPERTS, BATCH, CAP, D_MODEL), BF16),
            jax.ShapeDtypeStruct((N_EXPERTS, BATCH, CAP, 1), F32),
        ],
        compiler_params=_cparams("parallel", "arbitrary"),
        name="moe_gather",
    )(xn, selpos_t, gate_t)


FF_TF = 256


def _expert_ffn_kernel(xg_ref, wg_ref, wu_ref, wd_ref, gs_ref, y_ref, acc_ref):
    f = pl.program_id(1)
    last = pl.num_programs(1) - 1

    def partial():
        x = xg_ref[0].reshape(BATCH * CAP, D_MODEL)
        gate = _dot(x, wg_ref[0, 0].astype(BF16))
        up = _dot(x, wu_ref[0, 0].astype(BF16))
        act = (gate * (1.0 / (1.0 + jnp.exp(-gate))) * up).astype(BF16)
        return _dot(act, wd_ref[0, 0].astype(BF16))

    @pl.when(f == 0)
    def _():
        acc_ref[...] = partial()

    @pl.when(jnp.logical_and(f > 0, f < last))
    def _():
        acc_ref[...] += partial()

    @pl.when(f == last)
    def _():
        y = (acc_ref[...] + partial()) * gs_ref[0].reshape(BATCH * CAP, 1)
        y_ref[0] = y.astype(BF16).reshape(BATCH, CAP, D_MODEL)


def _expert_ffn(xg, w_gate, w_up, w_down, gs, layer):
    return pl.pallas_call(
        _expert_ffn_kernel,
        grid=(N_EXPERTS, EXPERT_FF // FF_TF),
        in_specs=[
            pl.BlockSpec((1, BATCH, CAP, D_MODEL), lambda e, f: (e, 0, 0, 0)),
            pl.BlockSpec((1, 1, D_MODEL, FF_TF), lambda e, f: (layer, e, 0, f)),
            pl.BlockSpec((1, 1, D_MODEL, FF_TF), lambda e, f: (layer, e, 0, f)),
            pl.BlockSpec((1, 1, FF_TF, D_MODEL), lambda e, f: (layer, e, f, 0)),
            pl.BlockSpec((1, BATCH, CAP, 1), lambda e, f: (e, 0, 0, 0)),
        ],
        out_specs=pl.BlockSpec((1, BATCH, CAP, D_MODEL), lambda e, f: (e, 0, 0, 0)),
        out_shape=jax.ShapeDtypeStruct((N_EXPERTS, BATCH, CAP, D_MODEL), BF16),
        scratch_shapes=[pltpu.VMEM((BATCH * CAP, D_MODEL), F32)],
        compiler_params=_cparams("parallel", "arbitrary"),
        name="moe_ffn",
    )(xg, w_gate, w_up, w_down, gs)


CB_TM = 256


def _combine_kernel(h_ref, selpos_ref, y_ref, *rest):
    o_ref = rest[-1]
    sp = selpos_ref[0]
    slot = lax.broadcasted_iota(jnp.int32, (CB_TM, CAP), 1).astype(F32)
    onehot = jnp.concatenate(
        [jnp.where(sp[:, e:e + 1] == slot, 1.0, 0.0).astype(BF16) for e in range(N_EXPERTS)], axis=1)
    out = h_ref[0] + _dot(onehot, y_ref[...].reshape(N_EXPERTS * CAP, D_MODEL))
    if len(rest) == 2:
        out = _rms(out, rest[0][...])
    o_ref[0] = out


def _combine(h, selpos, y, final_g=None):
    in_specs = [
        pl.BlockSpec((1, CB_TM, D_MODEL), lambda b, i: (b, i, 0)),
        pl.BlockSpec((1, CB_TM, N_EXPERTS), lambda b, i: (b, i, 0)),
        pl.BlockSpec((N_EXPERTS, 1, CAP, D_MODEL), lambda b, i: (0, b, 0, 0)),
    ]
    args = [h, selpos, y]
    if final_g is not None:
        in_specs.append(pl.BlockSpec((1, D_MODEL), lambda b, i: (0, 0)))
        args.append(final_g.reshape(1, D_MODEL).astype(F32))
    return pl.pallas_call(
        _combine_kernel,
        grid=(BATCH, SEQ // CB_TM),
        in_specs=in_specs,
        out_specs=pl.BlockSpec((1, CB_TM, D_MODEL), lambda b, i: (b, i, 0)),
        out_shape=jax.ShapeDtypeStruct((BATCH, SEQ, D_MODEL), F32),
        compiler_params=_cparams("parallel", "parallel"),
        name="moe_combine",
    )(*args)


def _angles(pos, dim, theta):
    inv = jnp.power(jnp.float32(theta), -(jnp.arange(0, dim, 2, dtype=jnp.float32) / dim))
    ang = pos.astype(jnp.float32)[:, None] * inv[None, :]
    return jnp.cos(ang), jnp.sin(ang)


def _rot_tables(groups, width=HEAD_DIM):
    cos_parts, sneg_parts, spos_parts = [], [], []
    used = 0
    for c, s in groups:
        z = jnp.zeros_like(s)
        cos_parts += [c, c]
        sneg_parts += [-s, z]
        spos_parts += [z, s]
        used += 2 * c.shape[1]
    rest = width - used
    if rest:
        cos_parts.append(jnp.ones((SEQ, rest), F32))
        sneg_parts.append(jnp.zeros((SEQ, rest), F32))
        spos_parts.append(jnp.zeros((SEQ, rest), F32))
    return (jnp.concatenate(cos_parts, axis=1), jnp.concatenate(sneg_parts, axis=1),
            jnp.concatenate(spos_parts, axis=1))


def _mixer_gqa_in(h2, g, w_in, scale_q):
    col = None
    if scale_q:
        col = jnp.concatenate([jnp.full((N_HEADS * HEAD_DIM,), HEAD_DIM ** -0.5 * LOG2E, F32),
                               jnp.ones((2 * N_KV_HEADS * HEAD_DIM,), F32)])
    qkv = _norm_matmul(h2, g, w_in, tm=1024, tn=1024, head_w=HEAD_DIM, col_scale=col)
    return qkv.reshape(N_HEADS + 2 * N_KV_HEADS, BATCH, SEQ, HEAD_DIM)


def _mla_weights(w_in, w_uq):
    w_in = jnp.pad(w_in, ((0, 0), (0, MLA_LAT_PAD - w_in.shape[-1]))).astype(BF16)
    w_uq = jnp.pad((w_uq * (MLA_QK_DIM ** -0.5 * LOG2E)).reshape(MLA_Q_RANK, N_HEADS, MLA_QK_DIM),
                   ((0, 0), (0, 0), (0, MLA_Q_PAD - MLA_QK_DIM))).reshape(MLA_Q_RANK, N_HEADS * MLA_Q_PAD)
    return w_in, w_uq.astype(BF16)


def kernel(x, mem, norm_mix_g, norm_xa_g, norm_ffn_g, a_w_in, a_sink, a_w_out, b_w_in, b_q_norm_g, b_k_norm_g, b_w_out, c_w_in, c_q_lat_norm_g, c_kv_lat_norm_g, c_w_uq, c_w_ukv, c_w_out, d_w_in, d_rpb, d_w_out, mem_norm_g, xa_wq, xa_wkv, xa_wo, router_w, moe_w_gate, moe_w_up, moe_w_down, final_norm_g):
    t = BATCH * SEQ
    pos = jnp.arange(SEQ)
    tab_a = _rot_tables([_angles(pos, ROT_DIM, ROPE_THETA)])
    tab_b = _rot_tables([_angles(pos // GRID_W, HEAD_DIM // 2, AXIAL_THETA),
                         _angles(pos % GRID_W, HEAD_DIM // 2, AXIAL_THETA)])
    tab_c = _rot_tables([_angles(pos, MLA_ROPE_DIM, ROPE_THETA)])
    wkv_all = jnp.transpose(xa_wkv, (1, 0, 2)).reshape(D_MODEL, DEPTH * 2 * XA_INNER).astype(BF16)
    kv_all = _norm_matmul(mem.reshape(BATCH * MEM_TOKENS, D_MODEL), mem_norm_g, wkv_all, tm=BATCH * MEM_TOKENS,
                          tn=2 * XA_INNER).reshape(BATCH, MEM_TOKENS, DEPTH * 2 * XA_INNER)

    h = x
    for i in range(DEPTH):
        m, j = i % N_MIXERS, i // N_MIXERS
        h2 = h.reshape(t, D_MODEL)
        if m == 0:
            qkv = _mixer_gqa_in(h2, norm_mix_g[i], a_w_in[j], True)
            att = _attn_window(qkv, a_sink[j] * LOG2E, tab_a)
            w_out = a_w_out[j]
        elif m == 1:
            qkv = _mixer_gqa_in(h2, norm_mix_g[i], b_w_in[j], False)
            att = _attn_axial(qkv, b_q_norm_g[j] * (HEAD_DIM ** -0.5 * LOG2E), b_k_norm_g[j], tab_b)
            w_out = b_w_out[j]
        elif m == 2:
            w_in, w_uq = _mla_weights(c_w_in[j], c_w_uq[j])
            lat = _norm_matmul(h2, norm_mix_g[i], w_in, tm=1024, tn=MLA_LAT_PAD)
            qfull = _norm_matmul(lat, c_q_lat_norm_g[j], w_uq, col_block=0, tm=2048, tn=1024, head_w=MLA_Q_PAD)
            kvfull = _norm_matmul(lat, c_kv_lat_norm_g[j], c_w_ukv[j].astype(BF16), col_block=1, tm=2048, tn=1024,
                                  head_w=MLA_NOPE_DIM)
            k_rope = lat[:, MLA_Q_RANK + MLA_KV_RANK:].reshape(BATCH, SEQ, HEAD_DIM)
            att = _attn_mla(qfull.reshape(N_HEADS, BATCH, SEQ, MLA_Q_PAD),
                            kvfull.reshape(2 * N_HEADS, BATCH, SEQ, MLA_NOPE_DIM), k_rope, tab_c)
            w_out = c_w_out[j]
        else:
            qkv = _mixer_gqa_in(h2, norm_mix_g[i], d_w_in[j], True)
            att = _attn_na(qkv, _na_bias(d_rpb[j] * LOG2E))
            w_out = d_w_out[j]

        h, xn, aff = _xattn(h, att, w_out.astype(BF16), norm_xa_g[i],
                            (xa_wq[i] * (XA_HEAD_DIM ** -0.5 * LOG2E)).astype(BF16),
                            kv_all, i, xa_wo[i].astype(BF16), norm_ffn_g[i], router_w[i])

        selpos, selpos_t, gate_t = _topk(aff)
        xg, gs = _gather(xn, selpos_t, gate_t)
        y = _expert_ffn(xg, moe_w_gate, moe_w_up, moe_w_down, gs, i)
        h = _combine(h, selpos, y, final_norm_g if i == DEPTH - 1 else None)

    return h
```
